```python
import jax, jax.numpy as jnp
from jax import lax
import numpy as np

D_MODEL = 1024
BATCH = 4
SEQ = 4096
DEPTH = 2
DEC_BATCH = 128
DEC_SEQ = 8
PAST_LEN = 16384
PAGE_SIZE = 128

N_META = 16
WINDOW = 128
BLOCK = 128
PAD_LEN = BLOCK - N_META
N_HEADS = 8
N_KV_HEADS = 2
Q_PER_KV = N_HEADS // N_KV_HEADS
HEAD_DIM = 64
ATT_SCALE = HEAD_DIM ** -0.5
ATT_Q_W = N_HEADS * HEAD_DIM
ATT_KV_W = N_KV_HEADS * HEAD_DIM
GLA_HEADS = 4
GLA_DK = 64
GLA_DV = 128
GLA_K_W = GLA_HEADS * GLA_DK
GLA_V_W = GLA_HEADS * GLA_DV
GATE_RANK = 16
GATE_NORMALIZER = 16.0
GLA_CHUNK = 128
D_MIX = ATT_Q_W + GLA_V_W
IN_W = ATT_Q_W + 2 * ATT_KV_W + 2 * GLA_K_W + GLA_V_W + GATE_RANK + GLA_V_W
D_FFN = ((8 * D_MODEL + 3 * 256 - 1) // (3 * 256)) * 256
RMS_EPS = 1e-6
MASK_VALUE = -1e30

kernel_name = 'hymba_swa_sink_gla_decode_step'


def rms_norm(x, g, eps=RMS_EPS):
    xf = x.astype(jnp.float32)
    y = xf * lax.rsqrt(jnp.mean(xf * xf, axis=-1, keepdims=True) + eps)
    return (y * g.astype(jnp.float32)).astype(x.dtype)


def project(h, w_in, q_norm, k_norm, w_g2, b_g):
    b, t, _ = h.shape
    z = jnp.einsum('btd,de->bte', h, w_in)
    sizes = (ATT_Q_W, ATT_KV_W, ATT_KV_W, GLA_K_W, GLA_K_W, GLA_V_W, GATE_RANK, GLA_V_W)
    offs = []
    acc = 0
    for s in sizes[:-1]:
        acc += s
        offs.append(acc)
    q, k, v, gq, gk, gv, g_low, og = jnp.split(z, offs, axis=-1)
    q = rms_norm(q.reshape(b, t, N_HEADS, HEAD_DIM), q_norm)
    k = rms_norm(k.reshape(b, t, N_KV_HEADS, HEAD_DIM), k_norm)
    v = v.reshape(b, t, N_KV_HEADS, HEAD_DIM)
    gq = gq.reshape(b, t, GLA_HEADS, GLA_DK) * (GLA_DK ** -0.5)
    gk = gk.reshape(b, t, GLA_HEADS, GLA_DK)
    gv = gv.reshape(b, t, GLA_HEADS, GLA_DV)
    logit = jnp.einsum('btr,re->bte', g_low, w_g2) + b_g
    log_decay = (jax.nn.log_sigmoid(logit.astype(jnp.float32)) / GATE_NORMALIZER).reshape(b, t, GLA_HEADS, GLA_DK)
    return q, k, v, gq, gk, gv, log_decay, og


def sink_softmax(scores, mask, sinks):
    sk = sinks.astype(jnp.float32).reshape(N_KV_HEADS, Q_PER_KV)
    s = jnp.where(mask, scores, MASK_VALUE)
    sink_col = jnp.broadcast_to(sk[:, :, None, None], s.shape[:-1] + (1,))
    return jax.nn.softmax(jnp.concatenate([s, sink_col], axis=-1), axis=-1)[..., :-1]


def window_attention_prompt(q, k, v, pos, sinks):
    b, L = q.shape[:2]
    nb = L // BLOCK
    qb = q.reshape(b, nb, BLOCK, N_KV_HEADS, Q_PER_KV, HEAD_DIM)
    kb = k.reshape(b, nb, BLOCK, N_KV_HEADS, HEAD_DIM)
    vb = v.reshape(b, nb, BLOCK, N_KV_HEADS, HEAD_DIM)
    shift = lambda a: jnp.concatenate([jnp.zeros_like(a[:, :1]), a[:, :-1]], axis=1)
    kk = jnp.concatenate([shift(kb), kb], axis=2)
    vv = jnp.concatenate([shift(vb), vb], axis=2)
    pb = pos.reshape(nb, BLOCK)
    pk = jnp.concatenate([pb - BLOCK, pb], axis=1)
    qp = pb[:, :, None]
    kp = pk[:, None, :]
    mask = (kp <= qp) & (qp - kp < WINDOW) & (kp >= 0)
    scores = jnp.einsum('bnqhgd,bnkhd->bnhgqk', qb, kk, preferred_element_type=jnp.float32) * ATT_SCALE
    p = sink_softmax(scores, mask[None, :, None, None], sinks)
    o = jnp.einsum('bnhgqk,bnkhd->bnqhgd', p.astype(v.dtype), vv)
    return o.reshape(b, L, ATT_Q_W)


def window_attention_sample(q, k, v, buf_k, buf_v, sinks):
    b, t = q.shape[:2]
    nbuf = buf_k.shape[1]
    kk = jnp.concatenate([buf_k.astype(k.dtype), k], axis=1)
    vv = jnp.concatenate([buf_v.astype(v.dtype), v], axis=1)
    qp = PAST_LEN + jnp.arange(t)
    kp = PAST_LEN - nbuf + jnp.arange(nbuf + t)
    mask = (kp[None, :] <= qp[:, None]) & (qp[:, None] - kp[None, :] < WINDOW)
    qg = q.reshape(b, t, N_KV_HEADS, Q_PER_KV, HEAD_DIM)
    scores = jnp.einsum('bqhgd,bkhd->bhgqk', qg, kk, preferred_element_type=jnp.float32) * ATT_SCALE
    p = sink_softmax(scores, mask, sinks)
    o = jnp.einsum('bhgqk,bkhd->bqhgd', p.astype(v.dtype), vv).reshape(b, t, ATT_Q_W)
    return o, kk[:, -nbuf:], vv[:, -nbuf:]


def gla_chunked(q, k, v, log_decay, s0, chunk):
    b, t, hg, dk = q.shape
    dv = v.shape[-1]
    nc = t // chunk
    to_chunks = lambda a: a.reshape((b, nc, chunk) + a.shape[2:]).swapaxes(0, 1).astype(jnp.float32)
    tri = jnp.tril(jnp.ones((chunk, chunk), dtype=bool))

    def step(S, xs):
        qc, kc, vc, gc = xs
        G = jnp.cumsum(gc, axis=1)
        o_inter = jnp.einsum('bthk,bhkv->bthv', qc * jnp.exp(G), S)
        diff = jnp.where(tri[None, :, :, None, None], G[:, :, None] - G[:, None, :], -jnp.inf)
        A = jnp.einsum('bthk,bshk,btshk->bths', qc, kc, jnp.exp(diff))
        o_intra = jnp.einsum('bths,bshv->bthv', A, vc)
        g_last = G[:, -1]
        S = jnp.exp(g_last)[..., None] * S + jnp.einsum('bshk,bshv->bhkv', kc * jnp.exp(g_last[:, None] - G), vc)
        return S, o_inter + o_intra

    S, o = lax.scan(step, s0.astype(jnp.float32), (to_chunks(q), to_chunks(k), to_chunks(v), to_chunks(log_decay)))
    return o.swapaxes(0, 1).reshape(b, t, hg, dv), S


def merge_heads(a, o_gla, og, gla_norm, w_o):
    b, t = a.shape[:2]
    g = rms_norm(o_gla.astype(a.dtype), gla_norm) * jax.nn.silu(og.reshape(b, t, GLA_HEADS, GLA_DV))
    merged = jnp.concatenate([a, g.reshape(b, t, GLA_V_W)], axis=-1)
    return jnp.einsum('bte,ed->btd', merged, w_o)


def swiglu(h, w_gate, w_up, w_down):
    u = jax.nn.silu(jnp.einsum('btd,df->btf', h, w_gate)) * jnp.einsum('btd,df->btf', h, w_up)
    return jnp.einsum('btf,fd->btd', u, w_down)


def setup_inputs(seed: int = 0) -> dict:
    key = jax.random.key(seed)
    ks = jax.random.split(key, 20)
    f32 = jnp.float32
    nrm = lambda k, shape, scale: jax.random.normal(k, shape, f32) * scale
    return {
        'x_prompt': nrm(ks[0], (BATCH, SEQ, D_MODEL), 1.0),
        'x_sample': nrm(ks[1], (DEC_BATCH, DEC_SEQ, D_MODEL), 1.0),
        'cache_k': nrm(ks[2], (DEPTH, DEC_BATCH, WINDOW, N_KV_HEADS, HEAD_DIM), 1.0),
        'cache_v': nrm(ks[3], (DEPTH, DEC_BATCH, WINDOW, N_KV_HEADS, HEAD_DIM), 1.0),
        'state_gla': nrm(ks[4], (DEPTH, DEC_BATCH, GLA_HEADS, GLA_DK, GLA_DV), 0.5),
        'meta': nrm(ks[5], (N_META, D_MODEL), 1.0),
        'norm1': 1.0 + nrm(ks[6], (DEPTH, D_MODEL), 0.02),
        'w_in': nrm(ks[7], (DEPTH, D_MODEL, IN_W), D_MODEL ** -0.5),
        'q_norm': 1.0 + nrm(ks[8], (DEPTH, HEAD_DIM), 0.02),
        'k_norm': 1.0 + nrm(ks[9], (DEPTH, HEAD_DIM), 0.02),
        'sinks': nrm(ks[10], (DEPTH, N_HEADS), 0.5),
        'w_g2': nrm(ks[11], (DEPTH, GATE_RANK, GLA_K_W), GATE_RANK ** -0.5),
        'b_g': nrm(ks[12], (DEPTH, GLA_K_W), 0.1),
        'gla_norm': 1.0 + nrm(ks[13], (DEPTH, GLA_DV), 0.02),
        'w_o': nrm(ks[14], (DEPTH, D_MIX, D_MODEL), D_MIX ** -0.5),
        'norm2': 1.0 + nrm(ks[15], (DEPTH, D_MODEL), 0.02),
        'w_gate': nrm(ks[16], (DEPTH, D_MODEL, D_FFN), D_MODEL ** -0.5),
        'w_up': nrm(ks[17], (DEPTH, D_MODEL, D_FFN), D_MODEL ** -0.5),
        'w_down': nrm(ks[18], (DEPTH, D_FFN, D_MODEL), D_FFN ** -0.5),
    }


def reference(x_prompt, x_sample, cache_k, cache_v, state_gla, meta, norm1, w_in, q_norm, k_norm, sinks,
              w_g2, b_g, gla_norm, w_o, norm2, w_gate, w_up, w_down):
    b = x_prompt.shape[0]
    dt = x_prompt.dtype
    x = jnp.concatenate([jnp.zeros((b, PAD_LEN, D_MODEL), dt),
                         jnp.broadcast_to(meta.astype(dt)[None], (b, N_META, D_MODEL)),
                         x_prompt], axis=1)
    pos = jnp.arange(x.shape[1]) - PAD_LEN
    valid = (pos >= 0).astype(dt)
    xs = x_sample
    pk, pv, ps, sk, sv, ss = [], [], [], [], [], []
    for l in range(DEPTH):
        h = rms_norm(x, norm1[l])
        q, k, v, gq, gk, gv, ld, og = project(h, w_in[l], q_norm[l], k_norm[l], w_g2[l], b_g[l])
        a = window_attention_prompt(q, k, v, pos, sinks[l])
        gk = gk * valid[None, :, None, None]
        o, s_fin = gla_chunked(gq, gk, gv, ld, jnp.zeros((b, GLA_HEADS, GLA_DK, GLA_DV), jnp.float32), GLA_CHUNK)
        x = x + merge_heads(a, o, og, gla_norm[l], w_o[l])
        x = x + swiglu(rms_norm(x, norm2[l]), w_gate[l], w_up[l], w_down[l])
        pk.append(k[:, -WINDOW:])
        pv.append(v[:, -WINDOW:])
        ps.append(s_fin)
        h = rms_norm(xs, norm1[l])
        q, k, v, gq, gk, gv, ld, og = project(h, w_in[l], q_norm[l], k_norm[l], w_g2[l], b_g[l])
        a, kbuf, vbuf = window_attention_sample(q, k, v, cache_k[l], cache_v[l], sinks[l])
        o, s_fin = gla_chunked(gq, gk, gv, ld, state_gla[l], xs.shape[1])
        xs = xs + merge_heads(a, o, og, gla_norm[l], w_o[l])
        xs = xs + swiglu(rms_norm(xs, norm2[l]), w_gate[l], w_up[l], w_down[l])
        sk.append(kbuf)
        sv.append(vbuf)
        ss.append(s_fin)
    y_prompt = x[:, PAD_LEN + N_META:]
    return (y_prompt, xs, jnp.stack(pk), jnp.stack(pv), jnp.stack(ps), jnp.stack(sk), jnp.stack(sv), jnp.stack(ss))
```

```python
import functools

import jax
import jax.numpy as jnp
import numpy as np
from jax import lax
from jax.experimental import pallas as pl
from jax.experimental.pallas import tpu as pltpu

F32 = jnp.float32
BF16 = jnp.bfloat16

D_MODEL = 1024
BATCH = 4
SEQ = 4096
DEPTH = 2
DEC_BATCH = 128
DEC_SEQ = 8
N_META = 16
WINDOW = 128
BLOCK = 128
PAD_LEN = BLOCK - N_META
N_HEADS = 8
N_KV_HEADS = 2
HEAD_DIM = 64
ATT_SCALE = HEAD_DIM ** -0.5
ATT_Q_W = N_HEADS * HEAD_DIM
ATT_KV_W = N_KV_HEADS * HEAD_DIM
GLA_HEADS = 4
GLA_DK = 64
GLA_DV = 128
GLA_K_W = GLA_HEADS * GLA_DK
GLA_V_W = GLA_HEADS * GLA_DV
GATE_RANK = 16
GATE_NORMALIZER = 16.0
D_MIX = ATT_Q_W + GLA_V_W
D_FFN = 2816
RMS_EPS = 1e-6
MASK_VALUE = -1e30

LANES = 128
PROMPT_LEN = PAD_LEN + N_META + SEQ
N_BLOCKS = PROMPT_LEN // BLOCK
PROMPT_ROWS = BATCH * PROMPT_LEN
SAMPLE_ROWS = DEC_BATCH * DEC_SEQ
TOTAL_ROWS = PROMPT_ROWS + SAMPLE_ROWS
ROW_TILE = 512
SEQ_GROUP = BLOCK // DEC_SEQ
N_LEVELS = 7
LOW_LEVELS = 3
LOG_DK = 6

OFF_Q, OFF_K, OFF_V = 0, 512, 640
OFF_GQ, OFF_GK, OFF_GV, OFF_OG, OFF_LOW = 768, 1024, 1280, 1792, 2304
PROJ_W = OFF_LOW + LANES
FFN_CHUNK = 256
VMEM_LIMIT = 56 * 1024 * 1024


def _dot(a, b):
    return jnp.dot(a, b, preferred_element_type=F32)


def _dot_nt(a, b):
    return lax.dot_general(a, b, (((1,), (1,)), ((), ())), preferred_element_type=F32)


def _dot_tn(a, b):
    return lax.dot_general(a, b, (((0,), (0,)), ((), ())), preferred_element_type=F32)


def _split(x):
    hi = x.astype(BF16)
    lo = (x - hi.astype(F32)).astype(BF16)
    return hi, lo


def _sigmoid(x):
    return 1.0 / (1.0 + jnp.exp(-x))


def _level_matrix(levels, n=BLOCK):
    out = np.zeros((len(levels) * n, n), np.float32)
    for i, l in enumerate(levels):
        size = 2 << l
        for t in range(n):
            mid = (t // size) * size + size // 2 - 1
            if (t >> l) & 1:
                out[i * n + t, mid + 1:t + 1] = 1.0
            else:
                out[i * n + t, t + 1:mid + 1] = 1.0
    return out


def _constants():
    r = np.arange(BLOCK)
    tri = (r[None, :] <= r[:, None]).astype(np.float32)
    same_seq = (r[None, :] // DEC_SEQ) == (r[:, None] // DEC_SEQ)
    seq_tri = tri * same_seq
    seq_ones = same_seq.astype(np.float32)
    c = np.arange(2 * LANES)
    bd256 = ((c[None, :] // HEAD_DIM) == (c[:, None] // HEAD_DIM)).astype(np.float32)
    return dict(
        tri=jnp.asarray(tri, BF16),
        seq_tri=jnp.asarray(seq_tri, BF16),
        seq_ones=jnp.asarray(seq_ones, BF16),
        lev_all=jnp.asarray(_level_matrix(range(N_LEVELS)), BF16),
        lev_low=jnp.asarray(_level_matrix(range(LOW_LEVELS)), BF16),
        bd256=jnp.asarray(bd256, BF16),
        bd128=jnp.asarray(bd256[:LANES, :LANES], BF16),
    )


def _inproj_kernel(x_ref, n1_ref, w_ref, qg_ref, kg_ref, wg2_ref, bg_ref, bd256_ref, bd128_ref,
                   q_ref, k_ref, v_ref, gq_ref, gk_ref, ld_ref, gv_ref, og_ref):
    x = x_ref[...]
    ms = jnp.mean(x * x, axis=-1, keepdims=True)
    h = (x * lax.rsqrt(ms + RMS_EPS) * n1_ref[...]).astype(BF16)
    z = _dot(h, w_ref[...])

    q = z[:, OFF_Q:OFF_K]
    q2 = (q * q).astype(BF16)
    bd = bd256_ref[...]
    ssq = jnp.concatenate([_dot(q2[:, :256], bd), _dot(q2[:, 256:], bd)], axis=1)
    q_ref[...] = (q * lax.rsqrt(ssq * (1.0 / HEAD_DIM) + RMS_EPS) * qg_ref[...]).astype(BF16)
    k = z[:, OFF_K:OFF_V]
    ssk = _dot((k * k).astype(BF16), bd128_ref[...])
    k_ref[...] = k * lax.rsqrt(ssk * (1.0 / HEAD_DIM) + RMS_EPS) * kg_ref[...]
    v_ref[...] = z[:, OFF_V:OFF_GQ]
    gq_ref[...] = z[:, OFF_GQ:OFF_GK] * (GLA_DK ** -0.5)
    gk_ref[...] = z[:, OFF_GK:OFF_GV]
    gv_ref[...] = z[:, OFF_GV:OFF_OG].astype(BF16)
    og_ref[...] = z[:, OFF_OG:OFF_LOW]
    logit = _dot(z[:, OFF_LOW:PROJ_W].astype(BF16), wg2_ref[...]) + bg_ref[...]
    log_sig = jnp.minimum(logit, 0.0) - jnp.log1p(jnp.exp(-jnp.abs(logit)))
    ld_ref[...] = log_sig * (1.0 / GATE_NORMALIZER)


def _inproj(x_all, n1, w, qg, kg, wg2, bg, consts):
    tile = lambda w_: pl.BlockSpec((ROW_TILE, w_), lambda i: (i, 0))
    full = lambda a: pl.BlockSpec(a.shape, lambda i: (0, 0))
    outs = [(ATT_Q_W, BF16), (ATT_KV_W, F32), (ATT_KV_W, F32), (GLA_K_W, F32), (GLA_K_W, F32),
            (GLA_K_W, F32), (GLA_V_W, BF16), (GLA_V_W, F32)]
    args = (x_all, n1, w, qg, kg, wg2, bg, consts["bd256"], consts["bd128"])
    return pl.pallas_call(
        _inproj_kernel,
        grid=(TOTAL_ROWS // ROW_TILE,),
        in_specs=[tile(D_MODEL)] + [full(a) for a in args[1:]],
        out_specs=[tile(w_) for w_, _ in outs],
        out_shape=[jax.ShapeDtypeStruct((TOTAL_ROWS, w_), dt) for w_, dt in outs],
        compiler_params=pltpu.CompilerParams(dimension_semantics=("arbitrary",),
                                             vmem_limit_bytes=VMEM_LIMIT),
        name="inproj",
    )(*args)


def _head_masks(rows):
    lane = lax.broadcasted_iota(jnp.int32, (rows, GLA_K_W), 1)
    return [(lane >> LOG_DK) == h for h in range(GLA_HEADS)]


def _gla_intra(gq, gk, decay_levels, levels):
    rows = gq.shape[0]
    hm = _head_masks(rows)
    r = lax.broadcasted_iota(jnp.int32, (rows, GLA_HEADS * rows), 0)
    c = lax.broadcasted_iota(jnp.int32, (rows, GLA_HEADS * rows), 1) & (rows - 1)
    rbit = lax.broadcasted_iota(jnp.int32, (rows, 1), 0)

    def pair_products(qh, kh):
        kh = kh.astype(BF16)
        stacked = jnp.concatenate([jnp.where(m, kh, jnp.zeros_like(kh)) for m in hm], axis=0)
        return _dot_nt(qh.astype(BF16), stacked)

    a = jnp.where(r == c, pair_products(gq, gk), 0.0)
    for i, l in enumerate(levels):
        e = decay_levels[i * rows:(i + 1) * rows]
        upper = ((rbit >> l) & 1) == 1
        p = pair_products(jnp.where(upper, gq * e, 0.0), jnp.where(upper, 0.0, gk * e))
        a = jnp.where((((r ^ c) >> l) == 1) & (((r >> l) & 1) == 1), p, a)
    return a


def _gla_merge(o, og, gn):
    outs = []
    for h in range(GLA_HEADS):
        oh = o[:, h * GLA_DV:(h + 1) * GLA_DV]
        gh = og[:, h * GLA_DV:(h + 1) * GLA_DV]
        ms = jnp.mean(oh * oh, axis=-1, keepdims=True)
        outs.append(oh * lax.rsqrt(ms + RMS_EPS) * gn * (gh * _sigmoid(gh)))
    return outs


def _softmax_pv(s, mask, sink, vv):
    s = jnp.where(mask, s, MASK_VALUE)
    m = jnp.maximum(jnp.max(s, axis=-1, keepdims=True), sink)
    p = jnp.exp(s - m)
    denom = jnp.sum(p, axis=-1, keepdims=True) + jnp.exp(sink - m)
    return _dot(p.astype(BF16), vv) / denom


def _mix_prompt_kernel(sinks_ref, q_ref, k_ref, v_ref, gq_ref, gk_ref, ld_ref, gv_ref, og_ref,
                       gn_ref, tri_ref, lev_ref, ones_ref,
                       m_ref, s_out_ref, kprev_ref, vprev_ref, state_ref):
    blk = pl.program_id(1)

    @pl.when(blk == 0)
    def _():
        kprev_ref[...] = jnp.zeros_like(kprev_ref)
        vprev_ref[...] = jnp.zeros_like(vprev_ref)
        state_ref[...] = jnp.zeros_like(state_ref)

    kcur = k_ref[...].astype(BF16)
    vcur = v_ref[...].astype(BF16)
    kk = jnp.concatenate([kprev_ref[...], kcur], axis=0)
    vv = jnp.concatenate([vprev_ref[...], vcur], axis=0)
    row = lax.broadcasted_iota(jnp.int32, (BLOCK, 2 * BLOCK), 0)
    col = lax.broadcasted_iota(jnp.int32, (BLOCK, 2 * BLOCK), 1)
    qpos = blk * BLOCK + row - PAD_LEN
    kpos = (blk - 1) * BLOCK + col - PAD_LEN
    mask = (kpos <= qpos) & (qpos - kpos < WINDOW) & (kpos >= 0)
    low_half = lax.broadcasted_iota(jnp.int32, (BLOCK, LANES), 1) < HEAD_DIM
    for j in range(N_HEADS // 2):
        qt = q_ref[:, j * LANES:(j + 1) * LANES]
        zero = jnp.zeros_like(qt)
        o_lo = _softmax_pv(_dot_nt(jnp.where(low_half, qt, zero), kk), mask, sinks_ref[j], vv)
        o_hi = _softmax_pv(_dot_nt(jnp.where(low_half, zero, qt), kk), mask, sinks_ref[j + 4], vv)
        m_ref[:, j * LANES:(j + 1) * LANES] = jnp.where(low_half, o_lo, o_hi).astype(BF16)
    kprev_ref[...] = kcur
    vprev_ref[...] = vcur

    rpos = blk * BLOCK + lax.broadcasted_iota(jnp.int32, (BLOCK, 1), 0) - PAD_LEN
    valid = (rpos >= 0).astype(F32)
    gq = gq_ref[...]
    gk = gk_ref[...] * valid
    gv = gv_ref[...]
    hi, lo = _split(ld_ref[...])
    tri = tri_ref[...]
    g = _dot(tri, hi) + _dot(tri, lo)
    g_last = g[BLOCK - 1:BLOCK, :]
    lev = lev_ref[...]
    decay_levels = jnp.exp(_dot(lev, hi) + _dot(lev, lo))
    a = _gla_intra(gq, gk, decay_levels, range(N_LEVELS)).astype(BF16)

    state = state_ref[...]
    sb = state.astype(BF16)
    hrow = lax.broadcasted_iota(jnp.int32, (GLA_K_W, GLA_DV), 0) >> LOG_DK
    s_bd = jnp.concatenate([jnp.where(hrow == h, sb, jnp.zeros_like(sb)) for h in range(GLA_HEADS)],
                           axis=1)
    o = _dot((gq * jnp.exp(g)).astype(BF16), s_bd)
    o = o + jnp.concatenate(
        [_dot(a[:, h * BLOCK:(h + 1) * BLOCK], gv[:, h * GLA_DV:(h + 1) * GLA_DV])
         for h in range(GLA_HEADS)], axis=1)
    for h, gh in enumerate(_gla_merge(o, og_ref[...], gn_ref[...])):
        m_ref[:, ATT_Q_W + h * GLA_DV:ATT_Q_W + (h + 1) * GLA_DV] = gh.astype(BF16)

    ones = ones_ref[...]
    decay_col = jnp.exp(_dot_tn(hi, ones) + _dot_tn(lo, ones))
    kv = _dot_tn((gk * jnp.exp(g_last - g)).astype(BF16), gv)
    new_state = decay_col * state + jnp.concatenate(
        [kv[h * GLA_DK:(h + 1) * GLA_DK, h * GLA_DV:(h + 1) * GLA_DV] for h in range(GLA_HEADS)],
        axis=0)
    state_ref[...] = new_state

    @pl.when(blk == N_BLOCKS - 1)
    def _():
        s_out_ref[0] = new_state


def _mix_prompt(sinks, proj, gn, consts):
    q, k, v, gq, gk, ld, gv, og = proj
    tok = lambda w_: pl.BlockSpec((BLOCK, w_), lambda b, i, *_: (b * N_BLOCKS + i, 0))
    full = lambda a: pl.BlockSpec(a.shape, lambda b, i, *_: (0,) * a.ndim)
    ones = jnp.ones((BLOCK, LANES), BF16)
    const_args = (gn, consts["tri"], consts["lev_all"], ones)
    grid_spec = pltpu.PrefetchScalarGridSpec(
        num_scalar_prefetch=1,
        grid=(BATCH, N_BLOCKS),
        in_specs=[tok(a.shape[1]) for a in proj] + [full(a) for a in const_args],
        out_specs=[tok(D_MIX),
                   pl.BlockSpec((1, GLA_K_W, GLA_DV), lambda b, i, *_: (b, 0, 0))],
        scratch_shapes=[pltpu.VMEM((BLOCK, ATT_KV_W), BF16), pltpu.VMEM((BLOCK, ATT_KV_W), BF16),
                        pltpu.VMEM((GLA_K_W, GLA_DV), F32)],
    )
    return pl.pallas_call(
        _mix_prompt_kernel,
        grid_spec=grid_spec,
        out_shape=[jax.ShapeDtypeStruct((TOTAL_ROWS, D_MIX), BF16),
                   jax.ShapeDtypeStruct((BATCH, GLA_K_W, GLA_DV), F32)],
        compiler_params=pltpu.CompilerParams(dimension_semantics=("arbitrary", "arbitrary"),
                                             vmem_limit_bytes=VMEM_LIMIT),
        name="mix_prompt",
    )(sinks, *proj, *const_args)


def _mix_sample_kernel(sinks_ref, m_in_ref, q_ref, k_ref, v_ref, gq_ref, gk_ref, ld_ref, gv_ref, og_ref,
                       ck_ref, cv_ref, st_ref, gn_ref, tri_ref, sones_ref, lev_ref, ones_ref,
                       m_ref, ck_out_ref, cv_out_ref, st_out_ref):
    del m_in_ref
    gq = gq_ref[...]
    gk = gk_ref[...]
    gvf = gv_ref[...].astype(F32)
    hi, lo = _split(ld_ref[...])
    tri = tri_ref[...]
    g = _dot(tri, hi) + _dot(tri, lo)
    sones = sones_ref[...]
    g_tot = _dot(sones, hi) + _dot(sones, lo)
    lev = lev_ref[...]
    decay_levels = jnp.exp(_dot(lev, hi) + _dot(lev, lo))
    a = _gla_intra(gq, gk, decay_levels, range(LOW_LEVELS)).astype(BF16)
    gv = gv_ref[...]
    o_intra = jnp.concatenate(
        [_dot(a[:, h * BLOCK:(h + 1) * BLOCK], gv[:, h * GLA_DV:(h + 1) * GLA_DV])
         for h in range(GLA_HEADS)], axis=1)
    q_dec = gq * jnp.exp(g)
    k_dec = gk * jnp.exp(g_tot - g)
    hm8 = _head_masks(DEC_SEQ)
    ones8 = ones_ref[...]
    hi_f, lo_f = hi.astype(F32), lo.astype(F32)

    qf = q_ref[...].astype(F32)
    kf = k_ref[...]
    vf = v_ref[...]
    low8 = lax.broadcasted_iota(jnp.int32, (DEC_SEQ, LANES), 1) < HEAD_DIM
    nkeys = WINDOW + DEC_SEQ
    srow = lax.broadcasted_iota(jnp.int32, (N_HEADS * DEC_SEQ, nkeys), 0)
    scol = lax.broadcasted_iota(jnp.int32, (N_HEADS * DEC_SEQ, nkeys), 1)
    t_of_row = srow & (DEC_SEQ - 1)
    amask = ((scol < WINDOW) & (scol > t_of_row)) | ((scol >= WINDOW) & (scol - WINDOW <= t_of_row))
    rid = lax.broadcasted_iota(jnp.int32, (N_HEADS * DEC_SEQ, 1), 0) >> LOW_LEVELS
    sink_col = jnp.zeros((N_HEADS * DEC_SEQ, 1), F32)
    for i in range(N_HEADS):
        sink_col = jnp.where(rid == i, sinks_ref[(i // 2) + 4 * (i % 2)], sink_col)

    att_rows, inter_rows = [], []
    for b in range(SEQ_GROUP):
        rows = slice(b * DEC_SEQ, (b + 1) * DEC_SEQ)
        pieces = []
        for j in range(N_HEADS // 2):
            qt = qf[rows, j * LANES:(j + 1) * LANES]
            pieces += [jnp.where(low8, qt, 0.0), jnp.where(low8, 0.0, qt)]
        qp = jnp.concatenate(pieces, axis=0).astype(BF16)
        kk = jnp.concatenate([ck_ref[b], kf[rows]], axis=0)
        vv = jnp.concatenate([cv_ref[b], vf[rows]], axis=0)
        ob = _softmax_pv(_dot_nt(qp, kk.astype(BF16)), amask, sink_col, vv.astype(BF16))
        att_rows.append(jnp.concatenate(
            [jnp.where(low8, ob[(2 * j) * DEC_SEQ:(2 * j + 1) * DEC_SEQ],
                       ob[(2 * j + 1) * DEC_SEQ:(2 * j + 2) * DEC_SEQ])
             for j in range(N_HEADS // 2)], axis=1))
        ck_out_ref[b] = kk[DEC_SEQ:]
        cv_out_ref[b] = vv[DEC_SEQ:]

        state = st_ref[b]
        qb = q_dec[rows]
        qstack = jnp.concatenate([jnp.where(m, qb, 0.0) for m in hm8], axis=0).astype(BF16)
        oi = _dot(qstack, state.astype(BF16))
        inter_rows.append(jnp.concatenate(
            [oi[h * DEC_SEQ:(h + 1) * DEC_SEQ] for h in range(GLA_HEADS)], axis=1))
        kb = k_dec[rows]
        kstack = jnp.concatenate([jnp.where(m, kb, 0.0) for m in hm8], axis=0).astype(BF16)
        vb = gvf[rows]
        vstack = jnp.concatenate([vb[:, h * GLA_DV:(h + 1) * GLA_DV] for h in range(GLA_HEADS)],
                                 axis=0).astype(BF16)
        decay_col = jnp.exp(_dot_tn(hi_f[rows], ones8) + _dot_tn(lo_f[rows], ones8))
        st_out_ref[b] = decay_col * state + _dot_tn(kstack, vstack)

    m_ref[:, :ATT_Q_W] = jnp.concatenate(att_rows, axis=0).astype(BF16)
    o = o_intra + jnp.concatenate(inter_rows, axis=0)
    for h, gh in enumerate(_gla_merge(o, og_ref[...], gn_ref[...])):
        m_ref[:, ATT_Q_W + h * GLA_DV:ATT_Q_W + (h + 1) * GLA_DV] = gh.astype(BF16)


def _mix_sample(sinks, merged, proj, cache_k, cache_v, state, gn, consts):
    base = PROMPT_ROWS // BLOCK
    tok = lambda w_: pl.BlockSpec((BLOCK, w_), lambda i, *_: (base + i, 0))
    full = lambda a: pl.BlockSpec(a.shape, lambda i, *_: (0,) * a.ndim)
    seq = lambda a: pl.BlockSpec((SEQ_GROUP,) + a.shape[1:], lambda i, *_: (i, 0, 0))
    ones8 = jnp.ones((DEC_SEQ, LANES), F32)
    const_args = (gn, consts["seq_tri"], consts["seq_ones"], consts["lev_low"], ones8)
    seq_args = (cache_k, cache_v, state)
    grid_spec = pltpu.PrefetchScalarGridSpec(
        num_scalar_prefetch=1,
        grid=(DEC_BATCH // SEQ_GROUP,),
        in_specs=([pl.BlockSpec(memory_space=pl.ANY)] + [tok(a.shape[1]) for a in proj]
                  + [seq(a) for a in seq_args] + [full(a) for a in const_args]),
        out_specs=[tok(D_MIX)] + [seq(a) for a in seq_args],
    )
    return pl.pallas_call(
        _mix_sample_kernel,
        grid_spec=grid_spec,
        out_shape=[jax.ShapeDtypeStruct(merged.shape, merged.dtype)]
        + [jax.ShapeDtypeStruct(a.shape, a.dtype) for a in seq_args],
        input_output_aliases={1: 0},
        compiler_params=pltpu.CompilerParams(dimension_semantics=("arbitrary",),
                                             vmem_limit_bytes=VMEM_LIMIT),
        name="mix_sample",
    )(sinks, merged, *proj, *seq_args, *const_args)


def _out_ffn_kernel(x_ref, m_ref, wo_ref, n2_ref, wg_ref, wu_ref, wd_ref, y_ref, act_ref):
    x1 = x_ref[...] + _dot(m_ref[...], wo_ref[...])
    ms = jnp.mean(x1 * x1, axis=-1, keepdims=True)
    h = (x1 * lax.rsqrt(ms + RMS_EPS) * n2_ref[...]).astype(BF16)
    for c in range(D_FFN // FFN_CHUNK):
        cols = slice(c * FFN_CHUNK, (c + 1) * FFN_CHUNK)
        gate = _dot(h, wg_ref[:, cols])
        up = _dot(h, wu_ref[:, cols])
        act_ref[:, cols] = (gate * _sigmoid(gate) * up).astype(BF16)
    y_ref[...] = x1 + _dot(act_ref[...], wd_ref[...])


def _out_ffn(x_all, merged, wo, n2, wg, wu, wd):
    tile = lambda w_: pl.BlockSpec((ROW_TILE, w_), lambda i: (i, 0))
    resident = lambda a: pl.BlockSpec(a.shape, lambda i: (0, 0), pipeline_mode=pl.Buffered(1))
    return pl.pallas_call(
        _out_ffn_kernel,
        grid=(TOTAL_ROWS // ROW_TILE,),
        in_specs=[tile(D_MODEL), tile(D_MIX)] + [resident(a) for a in (wo, n2, wg, wu, wd)],
        out_specs=tile(D_MODEL),
        out_shape=jax.ShapeDtypeStruct((TOTAL_ROWS, D_MODEL), F32),
        scratch_shapes=[pltpu.VMEM((ROW_TILE, D_FFN), BF16)],
        compiler_params=pltpu.CompilerParams(dimension_semantics=("arbitrary",),
                                             vmem_limit_bytes=VMEM_LIMIT),
        name="out_ffn",
    )(x_all, merged, wo, n2, wg, wu, wd)


def _head_perm():
    cols = []
    for j in range(N_HEADS // 2):
        cols += list(range(j * HEAD_DIM, (j + 1) * HEAD_DIM))
        cols += list(range((j + 4) * HEAD_DIM, (j + 5) * HEAD_DIM))
    return np.asarray(cols, np.int32)


def kernel(x_prompt, x_sample, cache_k, cache_v, state_gla, meta, norm1, w_in, q_norm, k_norm, sinks,
           w_g2, b_g, gla_norm, w_o, norm2, w_gate, w_up, w_down):
    consts = _constants()
    perm = _head_perm()
    dt = x_prompt.dtype
    lead = jnp.concatenate([jnp.zeros((PAD_LEN, D_MODEL), dt), meta.astype(dt)], axis=0)
    xp = jnp.concatenate([jnp.broadcast_to(lead[None], (BATCH, BLOCK, D_MODEL)), x_prompt], axis=1)
    x_all = jnp.concatenate([xp.reshape(PROMPT_ROWS, D_MODEL), x_sample.reshape(SAMPLE_ROWS, D_MODEL)],
                            axis=0)
    row = lambda a: a.reshape(1, -1).astype(F32)
    o_q, o_k, o_v = 0, ATT_Q_W, ATT_Q_W + ATT_KV_W
    o_gq = o_v + ATT_KV_W
    o_gk, o_gv = o_gq + GLA_K_W, o_gq + 2 * GLA_K_W
    o_low = o_gv + GLA_V_W
    o_og = o_low + GATE_RANK

    pk, pv, ps, sk, sv, ss = [], [], [], [], [], []
    for l in range(DEPTH):
        wl = w_in[l]
        w = jnp.concatenate(
            [wl[:, o_q:o_k][:, perm], wl[:, o_k:o_gv + GLA_V_W], wl[:, o_og:],
             wl[:, o_low:o_og], jnp.zeros((D_MODEL, LANES - GATE_RANK), wl.dtype)], axis=1).astype(BF16)
        qg = row(jnp.tile(q_norm[l], N_HEADS) * ATT_SCALE)
        kg = row(jnp.tile(k_norm[l], N_KV_HEADS))
        wg2 = jnp.concatenate([w_g2[l], jnp.zeros((LANES - GATE_RANK, GLA_K_W), w_g2.dtype)],
                              axis=0).astype(BF16)
        sinks_l = sinks[l].astype(F32)
        gn = row(gla_norm[l])
        wo = jnp.concatenate([w_o[l][:ATT_Q_W][perm], w_o[l][ATT_Q_W:]], axis=0).astype(BF16)

        proj = _inproj(x_all, row(norm1[l]), w, qg, kg, wg2, row(b_g[l]), consts)
        merged, s_prompt = _mix_prompt(sinks_l, proj, gn, consts)
        merged, ck, cv, s_sample = _mix_sample(
            sinks_l, merged, proj,
            cache_k[l].reshape(DEC_BATCH, WINDOW, ATT_KV_W), cache_v[l].reshape(DEC_BATCH, WINDOW, ATT_KV_W),
            state_gla[l].reshape(DEC_BATCH, GLA_K_W, GLA_DV), gn, consts)
        x_all = _out_ffn(x_all, merged, wo, row(norm2[l]), w_gate[l].astype(BF16), w_up[l].astype(BF16),
                         w_down[l].astype(BF16))

        k_p = proj[1][:PROMPT_ROWS].reshape(BATCH, PROMPT_LEN, N_KV_HEADS, HEAD_DIM)
        v_p = proj[2][:PROMPT_ROWS].reshape(BATCH, PROMPT_LEN, N_KV_HEADS, HEAD_DIM)
        pk.append(k_p[:, -WINDOW:])
        pv.append(v_p[:, -WINDOW:])
        ps.append(s_prompt.reshape(BATCH, GLA_HEADS, GLA_DK, GLA_DV))
        sk.append(ck.reshape(DEC_BATCH, WINDOW, N_KV_HEADS, HEAD_DIM))
        sv.append(cv.reshape(DEC_BATCH, WINDOW, N_KV_HEADS, HEAD_DIM))
        ss.append(s_sample.reshape(DEC_BATCH, GLA_HEADS, GLA_DK, GLA_DV))

    y_prompt = x_all[:PROMPT_ROWS].reshape(BATCH, PROMPT_LEN, D_MODEL)[:, BLOCK:]
    y_sample = x_all[PROMPT_ROWS:].reshape(DEC_BATCH, DEC_SEQ, D_MODEL)
    return (y_prompt, y_sample, jnp.stack(pk), jnp.stack(pv), jnp.stack(ps),
            jnp.stack(sk), jnp.stack(sv), jnp.stack(ss))
```

```python
import functools

import jax
import jax.numpy as jnp
import numpy as np
from jax import lax
from jax.experimental import pallas as pl
from jax.experimental.pallas import tpu as pltpu

F32 = jnp.float32
BF16 = jnp.bfloat16

D_MODEL = 1024
BATCH = 4
SEQ = 4096
DEPTH = 2
DEC_BATCH = 128
DEC_SEQ = 8
N_META = 16
WINDOW = 128
BLOCK = 128
PAD_LEN = BLOCK - N_META
N_HEADS = 8
N_KV_HEADS = 2
HEAD_DIM = 64
ATT_SCALE = HEAD_DIM ** -0.5
ATT_Q_W = N_HEADS * HEAD_DIM
ATT_KV_W = N_KV_HEADS * HEAD_DIM
GLA_HEADS = 4
GLA_DK = 64
GLA_DV = 128
GLA_K_W = GLA_HEADS * GLA_DK
GLA_V_W = GLA_HEADS * GLA_DV
GATE_RANK = 16
GATE_NORMALIZER = 16.0
D_MIX = ATT_Q_W + GLA_V_W
D_FFN = 2816
IN_W = 2320
RMS_EPS = 1e-6
MASK_VALUE = -1e30

LANES = 128
N_BLOCKS = 1 + SEQ // BLOCK
MAIN_ROWS = BATCH * SEQ
SAMPLE_ROWS = DEC_BATCH * DEC_SEQ
LEAD_ROWS = BATCH * BLOCK
TOTAL_ROWS = MAIN_ROWS + SAMPLE_ROWS + LEAD_ROWS
ROW_TILE = 512
MAIN_TILES = MAIN_ROWS // ROW_TILE
SAMPLE_TILES = SAMPLE_ROWS // ROW_TILE
N_TILES = TOTAL_ROWS // ROW_TILE
SAMPLE_BLK0 = MAIN_ROWS // BLOCK
LEAD_BLK0 = (MAIN_ROWS + SAMPLE_ROWS) // BLOCK
SEQ_GROUP = BLOCK // DEC_SEQ
N_LEVELS = 7
LOW_LEVELS = 3
LOG_DK = 6
N_PREP = 8
W_ROWS = D_MODEL // N_PREP
WD_ROWS = D_FFN // N_PREP

SRC_LOW, SRC_OG = 1792, 1808
OFF_Q, OFF_K, OFF_V = 0, 512, 640
OFF_GQ, OFF_GK, OFF_GV, OFF_OG, OFF_LOW = 768, 1024, 1280, 1792, 2304
PROJ_W = OFF_LOW + LANES
FFN_CHUNK = 256
VMEM_LIMIT = 56 * 1024 * 1024

V_NORM1, V_NORM2, V_QG, V_KG, V_BG, V_GN, V_ROWS = 0, 1, 2, 3, 4, 5, 8


def _dot(a, b):
    return jnp.dot(a, b, preferred_element_type=F32)


def _dot_nt(a, b):
    return lax.dot_general(a, b, (((1,), (1,)), ((), ())), preferred_element_type=F32)


def _dot_tn(a, b):
    return lax.dot_general(a, b, (((0,), (0,)), ((), ())), preferred_element_type=F32)


def _split(x):
    hi = x.astype(BF16)
    lo = (x - hi.astype(F32)).astype(BF16)
    return hi, lo


def _sigmoid(x):
    return 1.0 / (1.0 + jnp.exp(-x))


def _level_matrix(levels, n=BLOCK):
    out = np.zeros((len(levels) * n, n), np.float32)
    for i, l in enumerate(levels):
        size = 2 << l
        for t in range(n):
            mid = (t // size) * size + size // 2 - 1
            if (t >> l) & 1:
                out[i * n + t, mid + 1:t + 1] = 1.0
            else:
                out[i * n + t, t + 1:mid + 1] = 1.0
    return out


def _constants():
    r = np.arange(BLOCK)
    tri = (r[None, :] <= r[:, None]).astype(np.float32)
    same_seq = (r[None, :] // DEC_SEQ) == (r[:, None] // DEC_SEQ)
    c = np.arange(2 * LANES)
    bd256 = ((c[None, :] // HEAD_DIM) == (c[:, None] // HEAD_DIM)).astype(np.float32)
    return dict(
        tri=jnp.asarray(tri, BF16),
        seq_tri=jnp.asarray(tri * same_seq, BF16),
        seq_ones=jnp.asarray(same_seq.astype(np.float32), BF16),
        lev_all=jnp.asarray(_level_matrix(range(N_LEVELS)), BF16),
        lev_low=jnp.asarray(_level_matrix(range(LOW_LEVELS)), BF16),
        bd256=jnp.asarray(bd256, BF16),
        bd128=jnp.asarray(bd256[:LANES, :LANES], BF16),
    )


def _tile_of_step(i):
    return jnp.maximum(i - N_PREP, 0)


def _inproj_kernel(*refs, n_src, emit_x):
    x_refs, refs = refs[:n_src], refs[n_src:]
    vec_ref, w_ref, wg2_ref, bd256_ref, bd128_ref = refs[:5]
    refs = refs[5:]
    if emit_x:
        xo_ref, refs = refs[0], refs[1:]
    q_ref, k_ref, v_ref, gq_ref, gk_ref, ld_ref, gv_ref, og_ref, wbf_ref, wg2s_ref = refs
    i = pl.program_id(0)

    @pl.when(i < N_PREP)
    def _():
        rows = pl.ds(pl.multiple_of(i * W_ROWS, W_ROWS), W_ROWS)
        chunk = w_ref[0]
        wbf_ref[rows, 0:OFF_OG] = chunk[:, 0:SRC_LOW].astype(BF16)
        wbf_ref[rows, OFF_OG:OFF_LOW] = chunk[:, SRC_OG:IN_W].astype(BF16)
        lane = lax.broadcasted_iota(jnp.int32, (W_ROWS, LANES), 1)
        low = jnp.where(lane < GATE_RANK, chunk[:, SRC_LOW:SRC_LOW + LANES], 0.0)
        wbf_ref[rows, OFF_LOW:PROJ_W] = low.astype(BF16)

    @pl.when(i == 0)
    def _():
        wg2s_ref[...] = jnp.concatenate(
            [wg2_ref[0], jnp.zeros((LANES - GATE_RANK, GLA_K_W), F32)], axis=0).astype(BF16)

    @pl.when(i >= N_PREP)
    def _():
        if n_src == 1:
            x = x_refs[0][...]
        else:
            t = i - N_PREP
            x = jnp.where(t < MAIN_TILES, x_refs[0][...],
                          jnp.where(t < MAIN_TILES + SAMPLE_TILES, x_refs[1][...], x_refs[2][...]))
            xo_ref[...] = x
        ms = jnp.mean(x * x, axis=-1, keepdims=True)
        h = (x * lax.rsqrt(ms + RMS_EPS) * vec_ref[V_NORM1:V_NORM1 + 1, :]).astype(BF16)
        z = _dot(h, wbf_ref[...])

        q = z[:, OFF_Q:OFF_K]
        q2 = (q * q).astype(BF16)
        bd = bd256_ref[...]
        ssq = jnp.concatenate([_dot(q2[:, :256], bd), _dot(q2[:, 256:], bd)], axis=1)
        q_ref[...] = (q * lax.rsqrt(ssq * (1.0 / HEAD_DIM) + RMS_EPS)
                      * vec_ref[V_QG:V_QG + 1, :ATT_Q_W]).astype(BF16)
        k = z[:, OFF_K:OFF_V]
        ssk = _dot((k * k).astype(BF16), bd128_ref[...])
        k_ref[...] = k * lax.rsqrt(ssk * (1.0 / HEAD_DIM) + RMS_EPS) * vec_ref[V_KG:V_KG + 1, :ATT_KV_W]
        v_ref[...] = z[:, OFF_V:OFF_GQ]
        gq_ref[...] = z[:, OFF_GQ:OFF_GK] * (GLA_DK ** -0.5)
        gk_ref[...] = z[:, OFF_GK:OFF_GV]
        gv_ref[...] = z[:, OFF_GV:OFF_OG].astype(BF16)
        og_ref[...] = z[:, OFF_OG:OFF_LOW]
        logit = _dot(z[:, OFF_LOW:PROJ_W].astype(BF16), wg2s_ref[...]) + vec_ref[V_BG:V_BG + 1, :GLA_K_W]
        log_sig = jnp.minimum(logit, 0.0) - jnp.log1p(jnp.exp(-jnp.abs(logit)))
        ld_ref[...] = log_sig * (1.0 / GATE_NORMALIZER)


def _inproj(layer, x_srcs, vecs, w_in, w_g2, consts):
    n_src = len(x_srcs)
    tile = lambda w_: pl.BlockSpec((ROW_TILE, w_), lambda i: (_tile_of_step(i), 0))
    full = lambda a: pl.BlockSpec(a.shape, lambda i: (0,) * a.ndim)
    if n_src == 1:
        x_specs = [tile(D_MODEL)]
    else:
        x_specs = [
            pl.BlockSpec((ROW_TILE, D_MODEL), lambda i: (jnp.minimum(_tile_of_step(i), MAIN_TILES - 1), 0)),
            pl.BlockSpec((ROW_TILE, D_MODEL),
                         lambda i: (jnp.clip(_tile_of_step(i) - MAIN_TILES, 0, SAMPLE_TILES - 1), 0)),
            pl.BlockSpec((ROW_TILE, D_MODEL), lambda i: (0, 0)),
        ]
    in_specs = x_specs + [
        pl.BlockSpec((V_ROWS, D_MODEL), lambda i: (layer, 0)),
        pl.BlockSpec((1, W_ROWS, IN_W), lambda i: (layer, jnp.minimum(i, N_PREP - 1), 0)),
        pl.BlockSpec((1, GATE_RANK, GLA_K_W), lambda i: (layer, 0, 0)),
        full(consts["bd256"]), full(consts["bd128"]),
    ]
    outs = [(ATT_Q_W, BF16), (ATT_KV_W, F32), (ATT_KV_W, F32), (GLA_K_W, F32), (GLA_K_W, F32),
            (GLA_K_W, F32), (GLA_V_W, BF16), (GLA_V_W, F32)]
    if n_src > 1:
        outs = [(D_MODEL, F32)] + outs
    res = pl.pallas_call(
        functools.partial(_inproj_kernel, n_src=n_src, emit_x=n_src > 1),
        grid=(N_PREP + N_TILES,),
        in_specs=in_specs,
        out_specs=[tile(w_) for w_, _ in outs],
        out_shape=[jax.ShapeDtypeStruct((TOTAL_ROWS, w_), dt) for w_, dt in outs],
        scratch_shapes=[pltpu.VMEM((D_MODEL, PROJ_W), BF16), pltpu.VMEM((LANES, GLA_K_W), BF16)],
        compiler_params=pltpu.CompilerParams(dimension_semantics=("arbitrary",),
                                             vmem_limit_bytes=VMEM_LIMIT),
        name="inproj",
    )(*x_srcs, vecs, w_in, w_g2, consts["bd256"], consts["bd128"])
    if n_src > 1:
        return res[0], tuple(res[1:])
    return x_srcs[0], tuple(res)


def _head_masks(rows):
    lane = lax.broadcasted_iota(jnp.int32, (rows, GLA_K_W), 1)
    return [(lane >> LOG_DK) == h for h in range(GLA_HEADS)]


def _gla_intra(gq, gk, decay_levels, levels):
    rows = gq.shape[0]
    hm = _head_masks(rows)
    r = lax.broadcasted_iota(jnp.int32, (rows, GLA_HEADS * rows), 0)
    c = lax.broadcasted_iota(jnp.int32, (rows, GLA_HEADS * rows), 1) & (rows - 1)
    rbit = lax.broadcasted_iota(jnp.int32, (rows, 1), 0)

    def pair_products(qh, kh):
        kh = kh.astype(BF16)
        stacked = jnp.concatenate([jnp.where(m, kh, jnp.zeros_like(kh)) for m in hm], axis=0)
        return _dot_nt(qh.astype(BF16), stacked)

    a = jnp.where(r == c, pair_products(gq, gk), 0.0)
    for i, l in enumerate(levels):
        e = decay_levels[i * rows:(i + 1) * rows]
        upper = ((rbit >> l) & 1) == 1
        p = pair_products(jnp.where(upper, gq * e, 0.0), jnp.where(upper, 0.0, gk * e))
        a = jnp.where((((r ^ c) >> l) == 1) & (((r >> l) & 1) == 1), p, a)
    return a


def _gla_merge(o, og, gn):
    outs = []
    for h in range(GLA_HEADS):
        oh = o[:, h * GLA_DV:(h + 1) * GLA_DV]
        gh = og[:, h * GLA_DV:(h + 1) * GLA_DV]
        ms = jnp.mean(oh * oh, axis=-1, keepdims=True)
        outs.append(oh * lax.rsqrt(ms + RMS_EPS) * gn * (gh * _sigmoid(gh)))
    return outs


def _dup_halves(x):
    low = lax.broadcasted_iota(jnp.int32, x.shape, 1) < HEAD_DIM
    rolled = pltpu.roll(x, HEAD_DIM, axis=1)
    return jnp.where(low, x, rolled), jnp.where(low, rolled, x)


def _mix_prompt_kernel(*refs, layer, chained):
    sinks_ref, refs = refs[0], refs[1:]
    if chained:
        refs = refs[3:]
    (q_ref, k_ref, v_ref, gq_ref, gk_ref, ld_ref, gv_ref, og_ref, vec_ref, tri_ref, lev_ref, ones_ref,
     m_ref, s_out_ref, k_out_ref, v_out_ref, kprev_ref, vprev_ref, state_ref) = refs
    blk = pl.program_id(1)

    @pl.when(blk == 0)
    def _():
        kprev_ref[...] = jnp.zeros_like(kprev_ref)
        vprev_ref[...] = jnp.zeros_like(vprev_ref)
        state_ref[...] = jnp.zeros_like(state_ref)

    kdup = [x.astype(BF16) for x in _dup_halves(k_ref[...])]
    vdup = [x.astype(BF16) for x in _dup_halves(v_ref[...])]
    row = lax.broadcasted_iota(jnp.int32, (BLOCK, BLOCK), 0)
    col = lax.broadcasted_iota(jnp.int32, (BLOCK, BLOCK), 1)
    own = col <= row
    kpos = jnp.where(own, blk * BLOCK, (blk - 1) * BLOCK) + col - PAD_LEN
    live = kpos >= 0
    low_half = lax.broadcasted_iota(jnp.int32, (BLOCK, LANES), 1) < HEAD_DIM

    def attend(lhs, kk, vv, sink):
        s = _dot_nt(lhs, kk)
        s = jnp.where(live, jnp.where(own, s[:, BLOCK:], s[:, :BLOCK]), MASK_VALUE)
        m = jnp.maximum(jnp.max(s, axis=-1, keepdims=True), sink)
        p = jnp.exp(s - m)
        denom = jnp.sum(p, axis=-1, keepdims=True) + jnp.exp(sink - m)
        p2 = jnp.concatenate([jnp.where(own, 0.0, p), jnp.where(own, p, 0.0)], axis=1).astype(BF16)
        return _dot(p2, vv) / denom

    for j in range(N_HEADS // 2):
        g = (2 * j) // (N_HEADS // N_KV_HEADS)
        kk = jnp.concatenate([kprev_ref[g], kdup[g]], axis=0)
        vv = jnp.concatenate([vprev_ref[g], vdup[g]], axis=0)
        qt = q_ref[:, j * LANES:(j + 1) * LANES]
        zero = jnp.zeros_like(qt)
        o_lo = attend(jnp.where(low_half, qt, zero), kk, vv, sinks_ref[layer, 2 * j])
        o_hi = attend(jnp.where(low_half, zero, qt), kk, vv, sinks_ref[layer, 2 * j + 1])
        m_ref[:, j * LANES:(j + 1) * LANES] = jnp.where(low_half, o_lo, o_hi).astype(BF16)
    for g in range(N_KV_HEADS):
        kprev_ref[g] = kdup[g]
        vprev_ref[g] = vdup[g]

    rpos = blk * BLOCK + lax.broadcasted_iota(jnp.int32, (BLOCK, 1), 0) - PAD_LEN
    valid = (rpos >= 0).astype(F32)
    gq = gq_ref[...]
    gk = gk_ref[...] * valid
    gv = gv_ref[...]
    hi, lo = _split(ld_ref[...])
    tri = tri_ref[...]
    g_cum = _dot(tri, hi) + _dot(tri, lo)
    g_last = g_cum[BLOCK - 1:BLOCK, :]
    lev = lev_ref[...]
    decay_levels = jnp.exp(_dot(lev, hi) + _dot(lev, lo))
    a = _gla_intra(gq, gk, decay_levels, range(N_LEVELS)).astype(BF16)

    state = state_ref[...]
    sb = state.astype(BF16)
    hrow = lax.broadcasted_iota(jnp.int32, (GLA_K_W, GLA_DV), 0) >> LOG_DK
    s_bd = jnp.concatenate([jnp.where(hrow == h, sb, jnp.zeros_like(sb)) for h in range(GLA_HEADS)],
                           axis=1)
    o = _dot((gq * jnp.exp(g_cum)).astype(BF16), s_bd)
    o = o + jnp.concatenate(
        [_dot(a[:, h * BLOCK:(h + 1) * BLOCK], gv[:, h * GLA_DV:(h + 1) * GLA_DV])
         for h in range(GLA_HEADS)], axis=1)
    gn = vec_ref[V_GN:V_GN + 1, :GLA_DV]
    for h, gh in enumerate(_gla_merge(o, og_ref[...], gn)):
        m_ref[:, ATT_Q_W + h * GLA_DV:ATT_Q_W + (h + 1) * GLA_DV] = gh.astype(BF16)

    ones = ones_ref[...]
    decay_col = jnp.exp(_dot_tn(hi, ones) + _dot_tn(lo, ones))
    kv = _dot_tn((gk * jnp.exp(g_last - g_cum)).astype(BF16), gv)
    new_state = decay_col * state + jnp.concatenate(
        [kv[h * GLA_DK:(h + 1) * GLA_DK, h * GLA_DV:(h + 1) * GLA_DV] for h in range(GLA_HEADS)],
        axis=0)
    state_ref[...] = new_state

    @pl.when(blk == N_BLOCKS - 1)
    def _():
        s_out_ref[0, 0] = new_state
        k_out_ref[0, 0] = k_ref[...]
        v_out_ref[0, 0] = v_ref[...]


def _mix_prompt(layer, sinks, proj, vecs, consts, prev):
    def tok_map(b, i, *_):
        return (jnp.where(i == 0, LEAD_BLK0 + b, b * (N_BLOCKS - 1) + i - 1), 0)

    tok = lambda w_: pl.BlockSpec((BLOCK, w_), tok_map)
    full = lambda a: pl.BlockSpec(a.shape, lambda b, i, *_: (0,) * a.ndim)
    per_batch = lambda r, c: pl.BlockSpec((1, 1, r, c), lambda b, i, *_: (layer, b, 0, 0))
    ones = jnp.ones((BLOCK, LANES), BF16)
    const_args = (consts["tri"], consts["lev_all"], ones)
    chained = prev is not None
    prev_args = tuple(prev) if chained else ()
    in_specs = ([pl.BlockSpec(memory_space=pl.ANY)] * len(prev_args)
                + [tok(a.shape[1]) for a in proj]
                + [pl.BlockSpec((V_ROWS, D_MODEL), lambda b, i, *_: (layer, 0))]
                + [full(a) for a in const_args])
    grid_spec = pltpu.PrefetchScalarGridSpec(
        num_scalar_prefetch=1,
        grid=(BATCH, N_BLOCKS),
        in_specs=in_specs,
        out_specs=[tok(D_MIX), per_batch(GLA_K_W, GLA_DV), per_batch(BLOCK, ATT_KV_W),
                   per_batch(BLOCK, ATT_KV_W)],
        scratch_shapes=[pltpu.VMEM((N_KV_HEADS, BLOCK, ATT_KV_W), BF16),
                        pltpu.VMEM((N_KV_HEADS, BLOCK, ATT_KV_W), BF16),
                        pltpu.VMEM((GLA_K_W, GLA_DV), F32)],
    )
    return pl.pallas_call(
        functools.partial(_mix_prompt_kernel, layer=layer, chained=chained),
        grid_spec=grid_spec,
        out_shape=[jax.ShapeDtypeStruct((TOTAL_ROWS, D_MIX), BF16),
                   jax.ShapeDtypeStruct((DEPTH, BATCH, GLA_K_W, GLA_DV), F32),
                   jax.ShapeDtypeStruct((DEPTH, BATCH, BLOCK, ATT_KV_W), F32),
                   jax.ShapeDtypeStruct((DEPTH, BATCH, BLOCK, ATT_KV_W), F32)],
        input_output_aliases={1 + n: 1 + n for n in range(len(prev_args))},
        compiler_params=pltpu.CompilerParams(dimension_semantics=("arbitrary", "arbitrary"),
                                             vmem_limit_bytes=VMEM_LIMIT),
        name="mix_prompt",
    )(sinks, *prev_args, *proj, vecs, *const_args)


def _mix_sample_kernel(*refs, layer, n_alias):
    sinks_ref, refs = refs[0], refs[n_alias + 1:]
    (q_ref, k_ref, v_ref, gq_ref, gk_ref, ld_ref, gv_ref, og_ref, ck_ref, cv_ref, st_ref,
     vec_ref, tri_ref, sones_ref, lev_ref, ones_ref,
     m_ref, ck_out_ref, cv_out_ref, st_out_ref) = refs
    gq = gq_ref[...]
    gk = gk_ref[...]
    gv = gv_ref[...]
    gvf = gv.astype(F32)
    hi, lo = _split(ld_ref[...])
    tri = tri_ref[...]
    g_cum = _dot(tri, hi) + _dot(tri, lo)
    sones = sones_ref[...]
    g_tot = _dot(sones, hi) + _dot(sones, lo)
    lev = lev_ref[...]
    decay_levels = jnp.exp(_dot(lev, hi) + _dot(lev, lo))
    a = _gla_intra(gq, gk, decay_levels, range(LOW_LEVELS)).astype(BF16)
    o_intra = jnp.concatenate(
        [_dot(a[:, h * BLOCK:(h + 1) * BLOCK], gv[:, h * GLA_DV:(h + 1) * GLA_DV])
         for h in range(GLA_HEADS)], axis=1)
    q_dec = gq * jnp.exp(g_cum)
    k_dec = gk * jnp.exp(g_tot - g_cum)
    hm8 = _head_masks(DEC_SEQ)
    ones8 = ones_ref[...]
    hi_f, lo_f = hi.astype(F32), lo.astype(F32)

    qf = q_ref[...].astype(F32)
    kf = k_ref[...]
    vf = v_ref[...]
    low8 = lax.broadcasted_iota(jnp.int32, (DEC_SEQ, LANES), 1) < HEAD_DIM
    nkeys = WINDOW + DEC_SEQ
    srow = lax.broadcasted_iota(jnp.int32, (N_HEADS * DEC_SEQ, nkeys), 0)
    scol = lax.broadcasted_iota(jnp.int32, (N_HEADS * DEC_SEQ, nkeys), 1)
    t_of_row = srow & (DEC_SEQ - 1)
    amask = ((scol < WINDOW) & (scol > t_of_row)) | ((scol >= WINDOW) & (scol - WINDOW <= t_of_row))
    rid = lax.broadcasted_iota(jnp.int32, (N_HEADS * DEC_SEQ, 1), 0) >> LOW_LEVELS
    sink_col = jnp.zeros((N_HEADS * DEC_SEQ, 1), F32)
    for i in range(N_HEADS):
        sink_col = jnp.where(rid == i, sinks_ref[layer, i], sink_col)
    half = N_HEADS * DEC_SEQ // N_KV_HEADS

    att_rows, inter_rows = [], []
    for b in range(SEQ_GROUP):
        rows = slice(b * DEC_SEQ, (b + 1) * DEC_SEQ)
        pieces = []
        for j in range(N_HEADS // 2):
            qt = qf[rows, j * LANES:(j + 1) * LANES]
            pieces += [jnp.where(low8, qt, 0.0), jnp.where(low8, 0.0, qt)]
        qp = jnp.concatenate(pieces, axis=0).astype(BF16)
        kk = jnp.concatenate([ck_ref[0, b], kf[rows]], axis=0)
        vv = jnp.concatenate([cv_ref[0, b], vf[rows]], axis=0)
        kd = [x.astype(BF16) for x in _dup_halves(kk)]
        vd = [x.astype(BF16) for x in _dup_halves(vv)]
        s = jnp.concatenate([_dot_nt(qp[:half], kd[0]), _dot_nt(qp[half:], kd[1])], axis=0)
        s = jnp.where(amask, s, MASK_VALUE)
        m = jnp.maximum(jnp.max(s, axis=-1, keepdims=True), sink_col)
        p = jnp.exp(s - m)
        denom = jnp.sum(p, axis=-1, keepdims=True) + jnp.exp(sink_col - m)
        pb = p.astype(BF16)
        ob = jnp.concatenate([_dot(pb[:half], vd[0]), _dot(pb[half:], vd[1])], axis=0) / denom
        att_rows.append(jnp.concatenate(
            [jnp.where(low8, ob[(2 * j) * DEC_SEQ:(2 * j + 1) * DEC_SEQ],
                       ob[(2 * j + 1) * DEC_SEQ:(2 * j + 2) * DEC_SEQ])
             for j in range(N_HEADS // 2)], axis=1))
        ck_out_ref[0, b] = kk[DEC_SEQ:]
        cv_out_ref[0, b] = vv[DEC_SEQ:]

        state = st_ref[0, b]
        qb = q_dec[rows]
        qstack = jnp.concatenate([jnp.where(mk, qb, 0.0) for mk in hm8], axis=0).astype(BF16)
        oi = _dot(qstack, state.astype(BF16))
        inter_rows.append(jnp.concatenate(
            [oi[h * DEC_SEQ:(h + 1) * DEC_SEQ] for h in range(GLA_HEADS)], axis=1))
        kb = k_dec[rows]
        kstack = jnp.concatenate([jnp.where(mk, kb, 0.0) for mk in hm8], axis=0).astype(BF16)
        vb = gvf[rows]
        vstack = jnp.concatenate([vb[:, h * GLA_DV:(h + 1) * GLA_DV] for h in range(GLA_HEADS)],
                                 axis=0).astype(BF16)
        decay_col = jnp.exp(_dot_tn(hi_f[rows], ones8) + _dot_tn(lo_f[rows], ones8))
        st_out_ref[0, b] = decay_col * state + _dot_tn(kstack, vstack)

    m_ref[:, :ATT_Q_W] = jnp.concatenate(att_rows, axis=0).astype(BF16)
    o = o_intra + jnp.concatenate(inter_rows, axis=0)
    gn = vec_ref[V_GN:V_GN + 1, :GLA_DV]
    for h, gh in enumerate(_gla_merge(o, og_ref[...], gn)):
        m_ref[:, ATT_Q_W + h * GLA_DV:ATT_Q_W + (h + 1) * GLA_DV] = gh.astype(BF16)


def _mix_sample(layer, sinks, merged, proj, cache_k, cache_v, state, vecs, consts, prev):
    tok = lambda w_: pl.BlockSpec((BLOCK, w_), lambda i, *_: (SAMPLE_BLK0 + i, 0))
    full = lambda a: pl.BlockSpec(a.shape, lambda i, *_: (0,) * a.ndim)
    seq = lambda a: pl.BlockSpec((1, SEQ_GROUP) + a.shape[2:], lambda i, *_: (layer, i, 0, 0))
    ones8 = jnp.ones((DEC_SEQ, LANES), F32)
    const_args = (consts["seq_tri"], consts["seq_ones"], consts["lev_low"], ones8)
    seq_args = (cache_k, cache_v, state)
    alias_args = (merged,) + (tuple(prev) if prev is not None else ())
    grid_spec = pltpu.PrefetchScalarGridSpec(
        num_scalar_prefetch=1,
        grid=(DEC_BATCH // SEQ_GROUP,),
        in_specs=([pl.BlockSpec(memory_space=pl.ANY)] * len(alias_args)
                  + [tok(a.shape[1]) for a in proj] + [seq(a) for a in seq_args]
                  + [pl.BlockSpec((V_ROWS, D_MODEL), lambda i, *_: (layer, 0))]
                  + [full(a) for a in const_args]),
        out_specs=[tok(D_MIX)] + [seq(a) for a in seq_args],
    )
    return pl.pallas_call(
        functools.partial(_mix_sample_kernel, layer=layer, n_alias=len(alias_args)),
        grid_spec=grid_spec,
        out_shape=[jax.ShapeDtypeStruct(merged.shape, merged.dtype)]
        + [jax.ShapeDtypeStruct(a.shape, a.dtype) for a in seq_args],
        input_output_aliases={1 + n: n for n in range(len(alias_args))},
        compiler_params=pltpu.CompilerParams(dimension_semantics=("arbitrary",),
                                             vmem_limit_bytes=VMEM_LIMIT),
        name="mix_sample",
    )(sinks, *alias_args, *proj, *seq_args, vecs, *const_args)


def _out_ffn_kernel(*refs, last):
    x_ref, m_ref, vec_ref, wo_ref, wg_ref, wu_ref, wd_ref = refs[:7]
    out_refs = refs[7:9] if last else refs[7:8]
    wo_s, wg_s, wu_s, wd_s, act_ref = refs[-5:]
    i = pl.program_id(0)

    @pl.when(i < N_PREP)
    def _():
        rows = pl.ds(pl.multiple_of(i * W_ROWS, W_ROWS), W_ROWS)
        wo_s[rows, :] = wo_ref[0].astype(BF16)
        wg_s[rows, :] = wg_ref[0].astype(BF16)
        wu_s[rows, :] = wu_ref[0].astype(BF16)
        wd_s[pl.ds(pl.multiple_of(i * WD_ROWS, WD_ROWS), WD_ROWS), :] = wd_ref[0].astype(BF16)

    @pl.when(i >= N_PREP)
    def _():
        x1 = x_ref[...] + _dot(m_ref[...], wo_s[...])
        ms = jnp.mean(x1 * x1, axis=-1, keepdims=True)
        h = (x1 * lax.rsqrt(ms + RMS_EPS) * vec_ref[V_NORM2:V_NORM2 + 1, :]).astype(BF16)
        for c in range(D_FFN // FFN_CHUNK):
            cols = slice(c * FFN_CHUNK, (c + 1) * FFN_CHUNK)
            gate = _dot(h, wg_s[:, cols])
            up = _dot(h, wu_s[:, cols])
            act_ref[:, cols] = (gate * _sigmoid(gate) * up).astype(BF16)
        y = x1 + _dot(act_ref[...], wd_s[...])
        if last:
            t = i - N_PREP

            @pl.when(t < MAIN_TILES)
            def _():
                out_refs[0][...] = y

            @pl.when(t >= MAIN_TILES)
            def _():
                out_refs[1][...] = y
        else:
            out_refs[0][...] = y


def _out_ffn(layer, x_all, merged, vecs, w_o, w_gate, w_up, w_down, last):
    tile = lambda w_: pl.BlockSpec((ROW_TILE, w_), lambda i: (_tile_of_step(i), 0))
    chunk = lambda r, c: pl.BlockSpec((1, r, c), lambda i: (layer, jnp.minimum(i, N_PREP - 1), 0))
    if last:
        n_tiles = MAIN_TILES + SAMPLE_TILES
        out_specs = [pl.BlockSpec((ROW_TILE, D_MODEL),
                                  lambda i: (jnp.minimum(_tile_of_step(i), MAIN_TILES - 1), 0)),
                     pl.BlockSpec((ROW_TILE, D_MODEL),
                                  lambda i: (jnp.maximum(_tile_of_step(i) - MAIN_TILES, 0), 0))]
        out_shape = [jax.ShapeDtypeStruct((MAIN_ROWS, D_MODEL), F32),
                     jax.ShapeDtypeStruct((SAMPLE_ROWS, D_MODEL), F32)]
    else:
        n_tiles = N_TILES
        out_specs = [tile(D_MODEL)]
        out_shape = [jax.ShapeDtypeStruct((TOTAL_ROWS, D_MODEL), F32)]
    return pl.pallas_call(
        functools.partial(_out_ffn_kernel, last=last),
        grid=(N_PREP + n_tiles,),
        in_specs=[tile(D_MODEL), tile(D_MIX), pl.BlockSpec((V_ROWS, D_MODEL), lambda i: (layer, 0)),
                  chunk(W_ROWS, D_MODEL), chunk(W_ROWS, D_FFN), chunk(W_ROWS, D_FFN),
                  chunk(WD_ROWS, D_MODEL)],
        out_specs=out_specs,
        out_shape=out_shape,
        scratch_shapes=[pltpu.VMEM((D_MIX, D_MODEL), BF16), pltpu.VMEM((D_MODEL, D_FFN), BF16),
                        pltpu.VMEM((D_MODEL, D_FFN), BF16), pltpu.VMEM((D_FFN, D_MODEL), BF16),
                        pltpu.VMEM((ROW_TILE, D_FFN), BF16)],
        compiler_params=pltpu.CompilerParams(dimension_semantics=("arbitrary",),
                                             vmem_limit_bytes=VMEM_LIMIT),
        name="out_ffn",
    )(x_all, merged, vecs, w_o, w_gate, w_up, w_down)


def _vector_slab(norm1, norm2, q_norm, k_norm, b_g, gla_norm):
    pad = lambda a: jnp.pad(a.astype(F32), ((0, 0), (0, D_MODEL - a.shape[1])))
    rows = [norm1.astype(F32), norm2.astype(F32), pad(jnp.tile(q_norm, (1, N_HEADS)) * ATT_SCALE),
            pad(jnp.tile(k_norm, (1, N_KV_HEADS))), pad(b_g), pad(gla_norm)]
    rows += [jnp.zeros((DEPTH, D_MODEL), F32)] * (V_ROWS - len(rows))
    return jnp.stack(rows, axis=1).reshape(DEPTH * V_ROWS, D_MODEL)


def kernel(x_prompt, x_sample, cache_k, cache_v, state_gla, meta, norm1, w_in, q_norm, k_norm, sinks,
           w_g2, b_g, gla_norm, w_o, norm2, w_gate, w_up, w_down):
    consts = _constants()
    dt = x_prompt.dtype
    vecs = _vector_slab(norm1, norm2, q_norm, k_norm, b_g, gla_norm)
    sinks = sinks.astype(F32)
    lead = jnp.tile(jnp.concatenate([jnp.zeros((PAD_LEN, D_MODEL), dt), meta.astype(dt)], axis=0), (BATCH, 1))
    x_srcs = (x_prompt.reshape(MAIN_ROWS, D_MODEL), x_sample.reshape(SAMPLE_ROWS, D_MODEL), lead)
    ck_in = cache_k.reshape(DEPTH, DEC_BATCH, WINDOW, ATT_KV_W)
    cv_in = cache_v.reshape(DEPTH, DEC_BATCH, WINDOW, ATT_KV_W)
    st_in = state_gla.reshape(DEPTH, DEC_BATCH, GLA_K_W, GLA_DV)

    prompt_outs, sample_outs = None, None
    for l in range(DEPTH):
        x_all, proj = _inproj(l, x_srcs, vecs, w_in, w_g2, consts)
        merged, *prompt_outs = _mix_prompt(l, sinks, proj, vecs, consts, prompt_outs)
        merged, *sample_outs = _mix_sample(l, sinks, merged, proj, ck_in, cv_in, st_in, vecs, consts,
                                           sample_outs)
        x_srcs = tuple(_out_ffn(l, x_all, merged, vecs, w_o, w_gate, w_up, w_down, last=l == DEPTH - 1))

    y_main, y_sample = x_srcs
    ps, pk, pv = prompt_outs
    sk, sv, ss = sample_outs
    kv5 = lambda a, n: a.reshape(DEPTH, n, WINDOW, N_KV_HEADS, HEAD_DIM)
    st5 = lambda a, n: a.reshape(DEPTH, n, GLA_HEADS, GLA_DK, GLA_DV)
    return (y_main.reshape(BATCH, SEQ, D_MODEL), y_sample.reshape(DEC_BATCH, DEC_SEQ, D_MODEL),
            kv5(pk, BATCH), kv5(pv, BATCH), st5(ps, BATCH), kv5(sk, DEC_BATCH), kv5(sv, DEC_BATCH),
            st5(ss, DEC_BATCH))
```

```python
import functools

import jax
import jax.numpy as jnp
import numpy as np
from jax import lax
from jax.experimental import pallas as pl
from jax.experimental.pallas import tpu as pltpu

F32 = jnp.float32
BF16 = jnp.bfloat16

D_MODEL = 1024
BATCH = 4
SEQ = 4096
DEPTH = 2
DEC_BATCH = 128
DEC_SEQ = 8
N_META = 16
WINDOW = 128
BLOCK = 128
PAD_LEN = BLOCK - N_META
N_HEADS = 8
N_KV_HEADS = 2
HEAD_DIM = 64
ATT_SCALE = HEAD_DIM ** -0.5
ATT_Q_W = N_HEADS * HEAD_DIM
ATT_KV_W = N_KV_HEADS * HEAD_DIM
GLA_HEADS = 4
GLA_DK = 64
GLA_DV = 128
GLA_K_W = GLA_HEADS * GLA_DK
GLA_V_W = GLA_HEADS * GLA_DV
GATE_RANK = 16
GATE_NORMALIZER = 16.0
D_MIX = ATT_Q_W + GLA_V_W
D_FFN = 2816
IN_W = 2320
RMS_EPS = 1e-6
MASK_VALUE = -1e30

LANES = 128
N_BLOCKS = 1 + SEQ // BLOCK
MAIN_ROWS = BATCH * SEQ
SAMPLE_ROWS = DEC_BATCH * DEC_SEQ
LEAD_ROWS = BATCH * BLOCK
TOTAL_ROWS = MAIN_ROWS + SAMPLE_ROWS + LEAD_ROWS
ROW_TILE = 512
MAIN_TILES = MAIN_ROWS // ROW_TILE
SAMPLE_TILES = SAMPLE_ROWS // ROW_TILE
N_TILES = TOTAL_ROWS // ROW_TILE
SAMPLE_BLK0 = MAIN_ROWS // BLOCK
LEAD_BLK0 = (MAIN_ROWS + SAMPLE_ROWS) // BLOCK
SEQ_GROUP = BLOCK // DEC_SEQ
N_LEVELS = 7
LOW_LEVELS = 3
LOG_DK = 6
N_PREP = 8
W_ROWS = D_MODEL // N_PREP
WD_ROWS = D_FFN // N_PREP

SRC_LOW, SRC_OG = 1792, 1808
OFF_Q, OFF_K, OFF_V = 0, 512, 640
OFF_GQ, OFF_GK, OFF_GV, OFF_OG, OFF_LOW = 768, 1024, 1280, 1792, 2304
PROJ_W = OFF_LOW + LANES
FFN_CHUNK = 256
VMEM_LIMIT = 56 * 1024 * 1024

V_NORM1, V_NORM2, V_QG, V_KG, V_BG, V_GN, V_ROWS = 0, 1, 2, 3, 4, 5, 8


def _dot(a, b):
    return jnp.dot(a, b, preferred_element_type=F32)


def _dot_nt(a, b):
    return lax.dot_general(a, b, (((1,), (1,)), ((), ())), preferred_element_type=F32)


def _dot_tn(a, b):
    return lax.dot_general(a, b, (((0,), (0,)), ((), ())), preferred_element_type=F32)


def _split(x):
    hi = x.astype(BF16)
    lo = (x - hi.astype(F32)).astype(BF16)
    return hi, lo


def _sigmoid(x):
    return 1.0 / (1.0 + jnp.exp(-x))


def _level_matrix(levels, n=BLOCK):
    out = np.zeros((len(levels) * n, n), np.float32)
    for i, l in enumerate(levels):
        size = 2 << l
        for t in range(n):
            mid = (t // size) * size + size // 2 - 1
            if (t >> l) & 1:
                out[i * n + t, mid + 1:t + 1] = 1.0
            else:
                out[i * n + t, t + 1:mid + 1] = 1.0
    return out


def _constants():
    r = np.arange(BLOCK)
    tri = (r[None, :] <= r[:, None]).astype(np.float32)
    same_seq = (r[None, :] // DEC_SEQ) == (r[:, None] // DEC_SEQ)
    diff = np.maximum(r[:, None] ^ r[None, :], 1)
    pair_level = np.where(r[None, :] < r[:, None], np.floor(np.log2(diff)).astype(np.int32),
                          np.where(r[None, :] == r[:, None], N_LEVELS, N_LEVELS + 1)).astype(np.int32)
    c = np.arange(2 * LANES)
    bd256 =((c[None, :] // HEAD_DIM) == (c[:, None] // HEAD_DIM)).astype(np.float32)
    return dict(
        tri=jnp.asarray(tri, BF16),
        seq_tri=jnp.asarray(tri * same_seq, BF16),
        seq_ones=jnp.asarray(same_seq.astype(np.float32), BF16),
        lev_all=jnp.asarray(_level_matrix(range(N_LEVELS)), BF16),
        lev_low=jnp.asarray(_level_matrix(range(LOW_LEVELS)), BF16),
        bd256=jnp.asarray(bd256, BF16),
        bd128=jnp.asarray(bd256[:LANES, :LANES], BF16),
        pair_level=jnp.asarray(pair_level),
    )


def _tile_of_step(i):
    return jnp.maximum(i - N_PREP, 0)


def _inproj_kernel(*refs, n_src, emit_x):
    x_refs, refs = refs[:n_src], refs[n_src:]
    vec_ref, w_ref, wg2_ref, bd256_ref, bd128_ref = refs[:5]
    refs = refs[5:]
    if emit_x:
        xo_ref, refs = refs[0], refs[1:]
    q_ref, k_ref, v_ref, gq_ref, gk_ref, ld_ref, gv_ref, og_ref, wbf_ref, wg2s_ref = refs
    i = pl.program_id(0)

    @pl.when(i < N_PREP)
    def _():
        rows = pl.ds(pl.multiple_of(i * W_ROWS, W_ROWS), W_ROWS)
        chunk = w_ref[0]
        wbf_ref[rows, 0:OFF_OG] = chunk[:, 0:SRC_LOW].astype(BF16)
        wbf_ref[rows, OFF_OG:OFF_LOW] = chunk[:, SRC_OG:IN_W].astype(BF16)
        lane = lax.broadcasted_iota(jnp.int32, (W_ROWS, LANES), 1)
        low = jnp.where(lane < GATE_RANK, chunk[:, SRC_LOW:SRC_LOW + LANES], 0.0)
        wbf_ref[rows, OFF_LOW:PROJ_W] = low.astype(BF16)

    @pl.when(i == 0)
    def _():
        wg2s_ref[...] = jnp.concatenate(
            [wg2_ref[0], jnp.zeros((LANES - GATE_RANK, GLA_K_W), F32)], axis=0).astype(BF16)

    @pl.when(i >= N_PREP)
    def _():
        if n_src == 1:
            x = x_refs[0][...]
        else:
            t = i - N_PREP
            x = jnp.where(t < MAIN_TILES, x_refs[0][...].reshape(ROW_TILE, D_MODEL),
                          jnp.where(t < MAIN_TILES + SAMPLE_TILES, x_refs[1][...], x_refs[2][...]))
            xo_ref[...] = x
        ms = jnp.mean(x * x, axis=-1, keepdims=True)
        h = (x * lax.rsqrt(ms + RMS_EPS) * vec_ref[V_NORM1:V_NORM1 + 1, :]).astype(BF16)
        z = _dot(h, wbf_ref[...])

        q = z[:, OFF_Q:OFF_K]
        q2 = (q * q).astype(BF16)
        bd = bd256_ref[...]
        ssq = jnp.concatenate([_dot(q2[:, :256], bd), _dot(q2[:, 256:], bd)], axis=1)
        q_ref[...] = (q * lax.rsqrt(ssq * (1.0 / HEAD_DIM) + RMS_EPS)
                      * vec_ref[V_QG:V_QG + 1, :ATT_Q_W]).astype(BF16)
        k = z[:, OFF_K:OFF_V]
        ssk = _dot((k * k).astype(BF16), bd128_ref[...])
        k_ref[...] = k * lax.rsqrt(ssk * (1.0 / HEAD_DIM) + RMS_EPS) * vec_ref[V_KG:V_KG + 1, :ATT_KV_W]
        v_ref[...] = z[:, OFF_V:OFF_GQ]
        gq_ref[...] = z[:, OFF_GQ:OFF_GK] * (GLA_DK ** -0.5)
        gk_ref[...] = z[:, OFF_GK:OFF_GV]
        gv_ref[...] = z[:, OFF_GV:OFF_OG].astype(BF16)
        og_ref[...] = z[:, OFF_OG:OFF_LOW]
        logit = _dot(z[:, OFF_LOW:PROJ_W].astype(BF16), wg2s_ref[...]) + vec_ref[V_BG:V_BG + 1, :GLA_K_W]
        log_sig = jnp.minimum(logit, 0.0) - jnp.log1p(jnp.exp(-jnp.abs(logit)))
        ld_ref[...] = log_sig * (1.0 / GATE_NORMALIZER)


def _inproj(layer, x_srcs, vecs, w_in, w_g2, consts):
    n_src = len(x_srcs)
    tile = lambda w_: pl.BlockSpec((ROW_TILE, w_), lambda i: (_tile_of_step(i), 0))
    full = lambda a: pl.BlockSpec(a.shape, lambda i: (0,) * a.ndim)
    if n_src == 1:
        x_specs = [tile(D_MODEL)]
    else:
        x_specs = [
            pl.BlockSpec((BATCH, BLOCK, D_MODEL),
                         lambda i: (0, jnp.minimum(_tile_of_step(i), MAIN_TILES - 1), 0)),
            pl.BlockSpec((ROW_TILE, D_MODEL),
                         lambda i: (jnp.clip(_tile_of_step(i) - MAIN_TILES, 0, SAMPLE_TILES - 1), 0)),
            pl.BlockSpec((ROW_TILE, D_MODEL), lambda i: (0, 0)),
        ]
    in_specs = x_specs + [
        pl.BlockSpec((V_ROWS, D_MODEL), lambda i: (layer, 0)),
        pl.BlockSpec((1, W_ROWS, IN_W), lambda i: (layer, jnp.minimum(i, N_PREP - 1), 0)),
        pl.BlockSpec((1, GATE_RANK, GLA_K_W), lambda i: (layer, 0, 0)),
        full(consts["bd256"]), full(consts["bd128"]),
    ]
    outs = [(ATT_Q_W, BF16), (ATT_KV_W, F32), (ATT_KV_W, F32), (GLA_K_W, F32), (GLA_K_W, F32),
            (GLA_K_W, F32), (GLA_V_W, BF16), (GLA_V_W, F32)]
    if n_src > 1:
        outs = [(D_MODEL, F32)] + outs
    res = pl.pallas_call(
        functools.partial(_inproj_kernel, n_src=n_src, emit_x=n_src > 1),
        grid=(N_PREP + N_TILES,),
        in_specs=in_specs,
        out_specs=[tile(w_) for w_, _ in outs],
        out_shape=[jax.ShapeDtypeStruct((TOTAL_ROWS, w_), dt) for w_, dt in outs],
        scratch_shapes=[pltpu.VMEM((D_MODEL, PROJ_W), BF16), pltpu.VMEM((LANES, GLA_K_W), BF16)],
        compiler_params=pltpu.CompilerParams(dimension_semantics=("arbitrary",),
                                             vmem_limit_bytes=VMEM_LIMIT),
        name="inproj",
    )(*x_srcs, vecs, w_in, w_g2, consts["bd256"], consts["bd128"])
    if n_src > 1:
        return res[0], tuple(res[1:])
    return x_srcs[0], tuple(res)


def _head_masks(rows):
    lane = lax.broadcasted_iota(jnp.int32, (rows, GLA_K_W), 1)
    return [(lane >> LOG_DK) == h for h in range(GLA_HEADS)]


def _gla_intra(gq, gk, decay_levels, levels, pair_level):
    rows = gq.shape[0]
    hm = _head_masks(rows)

    def pair_products(qh, kh):
        kh = kh.astype(BF16)
        stacked = jnp.concatenate([jnp.where(m, kh, jnp.zeros_like(kh)) for m in hm], axis=0)
        p = _dot_nt(qh.astype(BF16), stacked)
        return [p[:, h * rows:(h + 1) * rows] for h in range(GLA_HEADS)]

    on_diag = pair_level == N_LEVELS
    a = [jnp.where(on_diag, p, 0.0) for p in pair_products(gq, gk)]
    for i, l in enumerate(levels):
        e = decay_levels[i * rows:(i + 1) * rows]
        at_level = pair_level == l
        a = [jnp.where(at_level, p, a_h) for p, a_h in zip(pair_products(gq * e, gk * e), a)]
    return a


def _gla_merge(o, og, gn):
    outs = []
    for h in range(GLA_HEADS):
        oh = o[:, h * GLA_DV:(h + 1) * GLA_DV]
        gh = og[:, h * GLA_DV:(h + 1) * GLA_DV]
        ms = jnp.mean(oh * oh, axis=-1, keepdims=True)
        outs.append(oh * lax.rsqrt(ms + RMS_EPS) * gn * (gh * _sigmoid(gh)))
    return outs


def _dup_halves(x):
    low = lax.broadcasted_iota(jnp.int32, x.shape, 1) < HEAD_DIM
    rolled = pltpu.roll(x, HEAD_DIM, axis=1)
    return jnp.where(low, x, rolled), jnp.where(low, rolled, x)


def _mix_prompt_kernel(*refs, layer, chained):
    sinks_ref, refs = refs[0], refs[1:]
    if chained:
        refs = refs[3:]
    (q_ref, k_ref, v_ref, gq_ref, gk_ref, ld_ref, gv_ref, og_ref, vec_ref, tri_ref, lev_ref, ones_ref, plev_ref,
     m_ref, s_out_ref, k_out_ref, v_out_ref, kprev_ref, vprev_ref, state_ref, sbd_ref) = refs
    blk = pl.program_id(0)

    @pl.when(blk == 0)
    def _():
        kprev_ref[...] = jnp.zeros_like(kprev_ref)
        vprev_ref[...] = jnp.zeros_like(vprev_ref)
        state_ref[...] = jnp.zeros_like(state_ref)
        sbd_ref[...] = jnp.zeros_like(sbd_ref)

    row = lax.broadcasted_iota(jnp.int32, (BLOCK, BLOCK), 0)
    col = lax.broadcasted_iota(jnp.int32, (BLOCK, BLOCK), 1)
    own = col <= row
    kpos = jnp.where(own, blk * BLOCK, (blk - 1) * BLOCK) + col - PAD_LEN
    live = kpos >= 0
    low_half = lax.broadcasted_iota(jnp.int32, (BLOCK, LANES), 1) < HEAD_DIM

    def attend(lhs, kk, vv, sink):
        s = _dot_nt(lhs, kk)
        s = jnp.where(live, jnp.where(own, s[:, BLOCK:], s[:, :BLOCK]), MASK_VALUE)
        m = jnp.maximum(jnp.max(s, axis=-1, keepdims=True), sink)
        p = jnp.exp(s - m)
        denom = jnp.sum(p, axis=-1, keepdims=True) + jnp.exp(sink - m)
        p2 = jnp.concatenate([jnp.where(own, 0.0, p), jnp.where(own, p, 0.0)], axis=1).astype(BF16)
        return _dot(p2, vv) / denom

    for b in range(BATCH):
        rows = slice(b * BLOCK, (b + 1) * BLOCK)
        kdup = [x.astype(BF16) for x in _dup_halves(k_ref[rows, :])]
        vdup = [x.astype(BF16) for x in _dup_halves(v_ref[rows, :])]
        for j in range(N_HEADS // 2):
            g = b * N_KV_HEADS + (2 * j) // (N_HEADS // N_KV_HEADS)
            kk = jnp.concatenate([kprev_ref[g], kdup[g % N_KV_HEADS]], axis=0)
            vv = jnp.concatenate([vprev_ref[g], vdup[g % N_KV_HEADS]], axis=0)
            qt = q_ref[rows, j * LANES:(j + 1) * LANES]
            zero = jnp.zeros_like(qt)
            o_lo = attend(jnp.where(low_half, qt, zero), kk, vv, sinks_ref[layer, 2 * j])
            o_hi = attend(jnp.where(low_half, zero, qt), kk, vv, sinks_ref[layer, 2 * j + 1])
            m_ref[rows, j * LANES:(j + 1) * LANES] = jnp.where(low_half, o_lo, o_hi).astype(BF16)
        for g in range(N_KV_HEADS):
            kprev_ref[b * N_KV_HEADS + g] = kdup[g]
            vprev_ref[b * N_KV_HEADS + g] = vdup[g]

    rpos = blk * BLOCK + lax.broadcasted_iota(jnp.int32, (BLOCK, 1), 0) - PAD_LEN
    valid = (rpos >= 0).astype(F32)
    ld_all = jnp.concatenate([ld_ref[b * BLOCK:(b + 1) * BLOCK, :] for b in range(BATCH)], axis=1)
    hi, lo = _split(ld_all)
    tri = tri_ref[...]
    g_cum_all = _dot(tri, hi) + _dot(tri, lo)
    lev = lev_ref[...]
    decay_all = jnp.exp(_dot(lev, hi) + _dot(lev, lo))
    ones = ones_ref[...]
    decay_col_all = jnp.exp(_dot_tn(hi, ones) + _dot_tn(lo, ones))
    pair_level = plev_ref[...]
    gn = vec_ref[V_GN:V_GN + 1, :GLA_DV]
    for b in range(BATCH):
        rows = slice(b * BLOCK, (b + 1) * BLOCK)
        cols = slice(b * GLA_K_W, (b + 1) * GLA_K_W)
        gq = gq_ref[rows, :]
        gk = gk_ref[rows, :] * valid
        gv = gv_ref[rows, :]
        g_cum = g_cum_all[:, cols]
        g_last = g_cum[BLOCK - 1:BLOCK, :]
        a = _gla_intra(gq, gk, decay_all[:, cols], range(N_LEVELS), pair_level)

        o = _dot((gq * jnp.exp(g_cum)).astype(BF16), sbd_ref[b])
        o = o + jnp.concatenate(
            [_dot(a[h].astype(BF16), gv[:, h * GLA_DV:(h + 1) * GLA_DV]) for h in range(GLA_HEADS)], axis=1)
        for h, gh in enumerate(_gla_merge(o, og_ref[rows, :], gn)):
            m_ref[rows, ATT_Q_W + h * GLA_DV:ATT_Q_W + (h + 1) * GLA_DV] = gh.astype(BF16)

        kv = _dot_tn((gk * jnp.exp(g_last - g_cum)).astype(BF16), gv)
        new_state = decay_col_all[cols, :] * state_ref[b] + jnp.concatenate(
            [kv[h * GLA_DK:(h + 1) * GLA_DK, h * GLA_DV:(h + 1) * GLA_DV] for h in range(GLA_HEADS)],
            axis=0)
        state_ref[b] = new_state
        for h in range(GLA_HEADS):
            hrows = slice(h * GLA_DK, (h + 1) * GLA_DK)
            sbd_ref[b, hrows, h * GLA_DV:(h + 1) * GLA_DV] = new_state[hrows].astype(BF16)

    @pl.when(blk == N_BLOCKS - 1)
    def _():
        s_out_ref[0] = state_ref[...]
        k_out_ref[0] = k_ref[...].reshape(BATCH, BLOCK, ATT_KV_W)
        v_out_ref[0] = v_ref[...].reshape(BATCH, BLOCK, ATT_KV_W)


def _mix_prompt(layer, sinks, proj, vecs, consts, prev):
    tok = lambda w_: pl.BlockSpec((ROW_TILE, w_), lambda i, *_: (jnp.where(i == 0, N_TILES - 1, i - 1), 0))
    full = lambda a: pl.BlockSpec(a.shape, lambda i, *_: (0,) * a.ndim)
    per_batch = lambda r, c: pl.BlockSpec((1, BATCH, r, c), lambda i, *_: (layer, 0, 0, 0))
    ones = jnp.ones((BLOCK, LANES), BF16)
    const_args = (consts["tri"], consts["lev_all"], ones, consts["pair_level"])
    chained = prev is not None
    prev_args = tuple(prev) if chained else ()
    in_specs = ([pl.BlockSpec(memory_space=pl.ANY)] * len(prev_args)
                + [tok(a.shape[1]) for a in proj]
                + [pl.BlockSpec((V_ROWS, D_MODEL), lambda i, *_: (layer, 0))]
                + [full(a) for a in const_args])
    grid_spec = pltpu.PrefetchScalarGridSpec(
        num_scalar_prefetch=1,
        grid=(N_BLOCKS,),
        in_specs=in_specs,
        out_specs=[tok(D_MIX), per_batch(GLA_K_W, GLA_DV), per_batch(BLOCK, ATT_KV_W),
                   per_batch(BLOCK, ATT_KV_W)],
        scratch_shapes=[pltpu.VMEM((BATCH * N_KV_HEADS, BLOCK, ATT_KV_W), BF16),
                        pltpu.VMEM((BATCH * N_KV_HEADS, BLOCK, ATT_KV_W), BF16),
                        pltpu.VMEM((BATCH, GLA_K_W, GLA_DV), F32),
                        pltpu.VMEM((BATCH, GLA_K_W, GLA_V_W), BF16)],
    )
    return pl.pallas_call(
        functools.partial(_mix_prompt_kernel, layer=layer, chained=chained),
        grid_spec=grid_spec,
        out_shape=[jax.ShapeDtypeStruct((TOTAL_ROWS, D_MIX), BF16),
                   jax.ShapeDtypeStruct((DEPTH, BATCH, GLA_K_W, GLA_DV), F32),
                   jax.ShapeDtypeStruct((DEPTH, BATCH, BLOCK, ATT_KV_W), F32),
                   jax.ShapeDtypeStruct((DEPTH, BATCH, BLOCK, ATT_KV_W), F32)],
        input_output_aliases={1 + n: 1 + n for n in range(len(prev_args))},
        compiler_params=pltpu.CompilerParams(dimension_semantics=("arbitrary",),
                                             vmem_limit_bytes=VMEM_LIMIT),
        name="mix_prompt",
    )(sinks, *prev_args, *proj, vecs, *const_args)


def _mix_sample_kernel(*refs, layer, n_alias):
    sinks_ref, refs = refs[0], refs[n_alias + 1:]
    (q_ref, k_ref, v_ref, gq_ref, gk_ref, ld_ref, gv_ref, og_ref, ck_ref, cv_ref, st_ref,
     vec_ref, tri_ref, sones_ref, lev_ref, ones_ref, plev_ref,
     m_ref, ck_out_ref, cv_out_ref, st_out_ref) = refs
    gq = gq_ref[...]
    gk = gk_ref[...]
    gv = gv_ref[...]
    gvf = gv.astype(F32)
    hi, lo = _split(ld_ref[...])
    tri = tri_ref[...]
    g_cum = _dot(tri, hi) + _dot(tri, lo)
    sones = sones_ref[...]
    g_tot = _dot(sones, hi) + _dot(sones, lo)
    lev = lev_ref[...]
    decay_levels = jnp.exp(_dot(lev, hi) + _dot(lev, lo))
    a = _gla_intra(gq, gk, decay_levels, range(LOW_LEVELS), plev_ref[...])
    o_intra = jnp.concatenate(
        [_dot(a[h].astype(BF16), gv[:, h * GLA_DV:(h + 1) * GLA_DV]) for h in range(GLA_HEADS)], axis=1)
    q_dec = gq * jnp.exp(g_cum)
    k_dec = gk * jnp.exp(g_tot - g_cum)
    hm8 = _head_masks(DEC_SEQ)
    ones8 = ones_ref[...]
    hi_f, lo_f = hi.astype(F32), lo.astype(F32)

    qf = q_ref[...].astype(F32)
    kf = k_ref[...]
    vf = v_ref[...]
    low8 = lax.broadcasted_iota(jnp.int32, (DEC_SEQ, LANES), 1) < HEAD_DIM
    nkeys = WINDOW + DEC_SEQ
    srow = lax.broadcasted_iota(jnp.int32, (N_HEADS * DEC_SEQ, nkeys), 0)
    scol = lax.broadcasted_iota(jnp.int32, (N_HEADS * DEC_SEQ, nkeys), 1)
    t_of_row = srow & (DEC_SEQ - 1)
    amask = ((scol < WINDOW) & (scol > t_of_row)) | ((scol >= WINDOW) & (scol - WINDOW <= t_of_row))
    rid = lax.broadcasted_iota(jnp.int32, (N_HEADS * DEC_SEQ, 1), 0) >> LOW_LEVELS
    sink_col = jnp.zeros((N_HEADS * DEC_SEQ, 1), F32)
    for i in range(N_HEADS):
        sink_col = jnp.where(rid == i, sinks_ref[layer, i], sink_col)
    half = N_HEADS * DEC_SEQ // N_KV_HEADS

    att_rows, inter_rows = [], []
    for b in range(SEQ_GROUP):
        rows = slice(b * DEC_SEQ, (b + 1) * DEC_SEQ)
        pieces = []
        for j in range(N_HEADS // 2):
            qt = qf[rows, j * LANES:(j + 1) * LANES]
            pieces += [jnp.where(low8, qt, 0.0), jnp.where(low8, 0.0, qt)]
        qp = jnp.concatenate(pieces, axis=0).astype(BF16)
        kk = jnp.concatenate([ck_ref[0, b], kf[rows]], axis=0)
        vv = jnp.concatenate([cv_ref[0, b], vf[rows]], axis=0)
        kd = [x.astype(BF16) for x in _dup_halves(kk)]
        vd = [x.astype(BF16) for x in _dup_halves(vv)]
        s = jnp.concatenate([_dot_nt(qp[:half], kd[0]), _dot_nt(qp[half:], kd[1])], axis=0)
        s = jnp.where(amask, s, MASK_VALUE)
        m = jnp.maximum(jnp.max(s, axis=-1, keepdims=True), sink_col)
        p = jnp.exp(s - m)
        denom = jnp.sum(p, axis=-1, keepdims=True) + jnp.exp(sink_col - m)
        pb = p.astype(BF16)
        ob = jnp.concatenate([_dot(pb[:half], vd[0]), _dot(pb[half:], vd[1])], axis=0) / denom
        att_rows.append(jnp.concatenate(
            [jnp.where(low8, ob[(2 * j) * DEC_SEQ:(2 * j + 1) * DEC_SEQ],
                       ob[(2 * j + 1) * DEC_SEQ:(2 * j + 2) * DEC_SEQ])
             for j in range(N_HEADS // 2)], axis=1))
        ck_out_ref[0, b] = kk[DEC_SEQ:]
        cv_out_ref[0, b] = vv[DEC_SEQ:]

        state = st_ref[0, b]
        qb = q_dec[rows]
        qstack = jnp.concatenate([jnp.where(mk, qb, 0.0) for mk in hm8], axis=0).astype(BF16)
        oi = _dot(qstack, state.astype(BF16))
        inter_rows.append(jnp.concatenate(
            [oi[h * DEC_SEQ:(h + 1) * DEC_SEQ] for h in range(GLA_HEADS)], axis=1))
        kb = k_dec[rows]
        kstack = jnp.concatenate([jnp.where(mk, kb, 0.0) for mk in hm8], axis=0).astype(BF16)
        vb = gvf[rows]
        vstack = jnp.concatenate([vb[:, h * GLA_DV:(h + 1) * GLA_DV] for h in range(GLA_HEADS)],
                                 axis=0).astype(BF16)
        decay_col = jnp.exp(_dot_tn(hi_f[rows], ones8) + _dot_tn(lo_f[rows], ones8))
        st_out_ref[0, b] = decay_col * state + _dot_tn(kstack, vstack)

    m_ref[:, :ATT_Q_W] = jnp.concatenate(att_rows, axis=0).astype(BF16)
    o = o_intra + jnp.concatenate(inter_rows, axis=0)
    gn = vec_ref[V_GN:V_GN + 1, :GLA_DV]
    for h, gh in enumerate(_gla_merge(o, og_ref[...], gn)):
        m_ref[:, ATT_Q_W + h * GLA_DV:ATT_Q_W + (h + 1) * GLA_DV] = gh.astype(BF16)


def _mix_sample(layer, sinks, merged, proj, cache_k, cache_v, state, vecs, consts, prev):
    tok = lambda w_: pl.BlockSpec((BLOCK, w_), lambda i, *_: (SAMPLE_BLK0 + i, 0))
    full = lambda a: pl.BlockSpec(a.shape, lambda i, *_: (0,) * a.ndim)
    seq = lambda a: pl.BlockSpec((1, SEQ_GROUP) + a.shape[2:], lambda i, *_: (layer, i, 0, 0))
    ones8 = jnp.ones((DEC_SEQ, LANES), F32)
    const_args = (consts["seq_tri"], consts["seq_ones"], consts["lev_low"], ones8, consts["pair_level"])
    seq_args = (cache_k, cache_v, state)
    alias_args = (merged,) + (tuple(prev) if prev is not None else ())
    grid_spec = pltpu.PrefetchScalarGridSpec(
        num_scalar_prefetch=1,
        grid=(DEC_BATCH // SEQ_GROUP,),
        in_specs=([pl.BlockSpec(memory_space=pl.ANY)] * len(alias_args)
                  + [tok(a.shape[1]) for a in proj] + [seq(a) for a in seq_args]
                  + [pl.BlockSpec((V_ROWS, D_MODEL), lambda i, *_: (layer, 0))]
                  + [full(a) for a in const_args]),
        out_specs=[tok(D_MIX)] + [seq(a) for a in seq_args],
    )
    return pl.pallas_call(
        functools.partial(_mix_sample_kernel, layer=layer, n_alias=len(alias_args)),
        grid_spec=grid_spec,
        out_shape=[jax.ShapeDtypeStruct(merged.shape, merged.dtype)]
        + [jax.ShapeDtypeStruct(a.shape, a.dtype) for a in seq_args],
        input_output_aliases={1 + n: n for n in range(len(alias_args))},
        compiler_params=pltpu.CompilerParams(dimension_semantics=("arbitrary",),
                                             vmem_limit_bytes=VMEM_LIMIT),
        name="mix_sample",
    )(sinks, *alias_args, *proj, *seq_args, vecs, *const_args)


def _out_ffn_kernel(*refs, last):
    x_ref, m_ref, vec_ref, wo_ref, wg_ref, wu_ref, wd_ref = refs[:7]
    out_refs = refs[7:9] if last else refs[7:8]
    wo_s, wg_s, wu_s, wd_s, act_ref = refs[-5:]
    i = pl.program_id(0)

    @pl.when(i < N_PREP)
    def _():
        rows = pl.ds(pl.multiple_of(i * W_ROWS, W_ROWS), W_ROWS)
        wo_s[rows, :] = wo_ref[0].astype(BF16)
        wg_s[rows, :] = wg_ref[0].astype(BF16)
        wu_s[rows, :] = wu_ref[0].astype(BF16)
        wd_s[pl.ds(pl.multiple_of(i * WD_ROWS, WD_ROWS), WD_ROWS), :] = wd_ref[0].astype(BF16)

    @pl.when(i >= N_PREP)
    def _():
        x1 = x_ref[...] + _dot(m_ref[...], wo_s[...])
        ms = jnp.mean(x1 * x1, axis=-1, keepdims=True)
        h = (x1 * lax.rsqrt(ms + RMS_EPS) * vec_ref[V_NORM2:V_NORM2 + 1, :]).astype(BF16)
        for c in range(D_FFN // FFN_CHUNK):
            cols = slice(c * FFN_CHUNK, (c + 1) * FFN_CHUNK)
            gate = _dot(h, wg_s[:, cols])
            up = _dot(h, wu_s[:, cols])
            act_ref[:, cols] = (gate * _sigmoid(gate) * up).astype(BF16)
        y = x1 + _dot(act_ref[...], wd_s[...])
        if last:
            t = i - N_PREP

            @pl.when(t < MAIN_TILES)
            def _():
                out_refs[0][...] = y.reshape(BATCH, BLOCK, D_MODEL)

            @pl.when(t >= MAIN_TILES)
            def _():
                out_refs[1][...] = y
        else:
            out_refs[0][...] = y


def _out_ffn(layer, x_all, merged, vecs, w_o, w_gate, w_up, w_down, last):
    tile = lambda w_: pl.BlockSpec((ROW_TILE, w_), lambda i: (_tile_of_step(i), 0))
    chunk = lambda r, c: pl.BlockSpec((1, r, c), lambda i: (layer, jnp.minimum(i, N_PREP - 1), 0))
    if last:
        n_tiles = MAIN_TILES + SAMPLE_TILES
        out_specs = [pl.BlockSpec((BATCH, BLOCK, D_MODEL),
                                  lambda i: (0, jnp.minimum(_tile_of_step(i), MAIN_TILES - 1), 0)),
                     pl.BlockSpec((ROW_TILE, D_MODEL),
                                  lambda i: (jnp.maximum(_tile_of_step(i) - MAIN_TILES, 0), 0))]
        out_shape = [jax.ShapeDtypeStruct((BATCH, SEQ, D_MODEL), F32),
                     jax.ShapeDtypeStruct((SAMPLE_ROWS, D_MODEL), F32)]
    else:
        n_tiles = N_TILES
        out_specs = [tile(D_MODEL)]
        out_shape = [jax.ShapeDtypeStruct((TOTAL_ROWS, D_MODEL), F32)]
    return pl.pallas_call(
        functools.partial(_out_ffn_kernel, last=last),
        grid=(N_PREP + n_tiles,),
        in_specs=[tile(D_MODEL), tile(D_MIX), pl.BlockSpec((V_ROWS, D_MODEL), lambda i: (layer, 0)),
                  chunk(W_ROWS, D_MODEL), chunk(W_ROWS, D_FFN), chunk(W_ROWS, D_FFN),
                  chunk(WD_ROWS, D_MODEL)],
        out_specs=out_specs,
        out_shape=out_shape,
        scratch_shapes=[pltpu.VMEM((D_MIX, D_MODEL), BF16), pltpu.VMEM((D_MODEL, D_FFN), BF16),
                        pltpu.VMEM((D_MODEL, D_FFN), BF16), pltpu.VMEM((D_FFN, D_MODEL), BF16),
                        pltpu.VMEM((ROW_TILE, D_FFN), BF16)],
        compiler_params=pltpu.CompilerParams(dimension_semantics=("arbitrary",),
                                             vmem_limit_bytes=VMEM_LIMIT),
        name="out_ffn",
    )(x_all, merged, vecs, w_o, w_gate, w_up, w_down)


def _vector_slab(norm1, norm2, q_norm, k_norm, b_g, gla_norm):
    pad = lambda a: jnp.pad(a.astype(F32), ((0, 0), (0, D_MODEL - a.shape[1])))
    rows = [norm1.astype(F32), norm2.astype(F32), pad(jnp.tile(q_norm, (1, N_HEADS)) * ATT_SCALE),
            pad(jnp.tile(k_norm, (1, N_KV_HEADS))), pad(b_g), pad(gla_norm)]
    rows += [jnp.zeros((DEPTH, D_MODEL), F32)] * (V_ROWS - len(rows))
    return jnp.stack(rows, axis=1).reshape(DEPTH * V_ROWS, D_MODEL)


def kernel(x_prompt, x_sample, cache_k, cache_v, state_gla, meta, norm1, w_in, q_norm, k_norm, sinks,
           w_g2, b_g, gla_norm, w_o, norm2, w_gate, w_up, w_down):
    consts = _constants()
    dt = x_prompt.dtype
    vecs = _vector_slab(norm1, norm2, q_norm, k_norm, b_g, gla_norm)
    sinks = sinks.astype(F32)
    lead = jnp.tile(jnp.concatenate([jnp.zeros((PAD_LEN, D_MODEL), dt), meta.astype(dt)], axis=0), (BATCH, 1))
    x_srcs = (x_prompt, x_sample.reshape(SAMPLE_ROWS, D_MODEL), lead)
    ck_in = cache_k.reshape(DEPTH, DEC_BATCH, WINDOW, ATT_KV_W)
    cv_in = cache_v.reshape(DEPTH, DEC_BATCH, WINDOW, ATT_KV_W)
    st_in = state_gla.reshape(DEPTH, DEC_BATCH, GLA_K_W, GLA_DV)

    prompt_outs, sample_outs = None, None
    for l in range(DEPTH):
        x_all, proj = _inproj(l, x_srcs, vecs, w_in, w_g2, consts)
        merged, *prompt_outs = _mix_prompt(l, sinks, proj, vecs, consts, prompt_outs)
        merged, *sample_outs = _mix_sample(l, sinks, merged, proj, ck_in, cv_in, st_in, vecs, consts,
                                           sample_outs)
        x_srcs = tuple(_out_ffn(l, x_all, merged, vecs, w_o, w_gate, w_up, w_down, last=l == DEPTH - 1))

    y_main, y_sample = x_srcs
    ps, pk, pv = prompt_outs
    sk, sv, ss = sample_outs
    kv5 = lambda a, n: a.reshape(DEPTH, n, WINDOW, N_KV_HEADS, HEAD_DIM)
    st5 = lambda a, n: a.reshape(DEPTH, n, GLA_HEADS, GLA_DK, GLA_DV)
    return (y_main, y_sample.reshape(DEC_BATCH, DEC_SEQ, D_MODEL),
            kv5(pk, BATCH), kv5(pv, BATCH), st5(ps, BATCH), kv5(sk, DEC_BATCH), kv5(sv, DEC_BATCH),
            st5(ss, DEC_BATCH))
```

```python
import functools

import jax
import jax.numpy as jnp
import numpy as np
from jax import lax
from jax.experimental import pallas as pl
from jax.experimental.pallas import tpu as pltpu

F32 = jnp.float32
BF16 = jnp.bfloat16

D_MODEL = 1024
BATCH = 4
SEQ = 4096
DEPTH = 2
DEC_BATCH = 128
DEC_SEQ = 8
N_META = 16
WINDOW = 128
BLOCK = 128
PAD_LEN = BLOCK - N_META
N_HEADS = 8
N_KV_HEADS = 2
HEAD_DIM = 64
ATT_SCALE = HEAD_DIM ** -0.5
ATT_Q_W = N_HEADS * HEAD_DIM
ATT_KV_W = N_KV_HEADS * HEAD_DIM
GLA_HEADS = 4
GLA_DK = 64
GLA_DV = 128
GLA_K_W = GLA_HEADS * GLA_DK
GLA_V_W = GLA_HEADS * GLA_DV
GATE_RANK = 16
GATE_NORMALIZER = 16.0
D_MIX = ATT_Q_W + GLA_V_W
D_FFN = 2816
IN_W = 2320
RMS_EPS = 1e-6
MASK_VALUE = -1e30

LANES = 128
N_BLOCKS = 1 + SEQ // BLOCK
MAIN_ROWS = BATCH * SEQ
SAMPLE_ROWS = DEC_BATCH * DEC_SEQ
LEAD_ROWS = BATCH * BLOCK
TOTAL_ROWS = MAIN_ROWS + SAMPLE_ROWS + LEAD_ROWS
ROW_TILE = 512
MAIN_TILES = MAIN_ROWS // ROW_TILE
SAMPLE_TILES = SAMPLE_ROWS // ROW_TILE
N_TILES = TOTAL_ROWS // ROW_TILE
SAMPLE_BLK0 = MAIN_ROWS // BLOCK
LEAD_BLK0 = (MAIN_ROWS + SAMPLE_ROWS) // BLOCK
SEQ_GROUP = BLOCK // DEC_SEQ
N_LEVELS = 7
LOW_LEVELS = 3
LOG_DK = 6
N_PREP = 8
W_ROWS = D_MODEL // N_PREP
WD_ROWS = D_FFN // N_PREP

SRC_LOW, SRC_OG = 1792, 1808
OFF_Q, OFF_K, OFF_V = 0, 512, 640
OFF_GQ, OFF_GK, OFF_GV, OFF_OG, OFF_LOW = 768, 1024, 1280, 1792, 2304
PROJ_W = OFF_LOW + LANES
PROJ_CHUNK = 512
FFN_CHUNK = 256
VMEM_LIMIT = 56 * 1024 * 1024

V_NORM1, V_NORM2, V_QG, V_KG, V_BG, V_GN, V_ROWS = 0, 1, 2, 3, 4, 5, 8


def _dot(a, b):
    return jnp.dot(a, b, preferred_element_type=F32)


def _dot_nt(a, b):
    return lax.dot_general(a, b, (((1,), (1,)), ((), ())), preferred_element_type=F32)


def _dot_tn(a, b):
    return lax.dot_general(a, b, (((0,), (0,)), ((), ())), preferred_element_type=F32)


def _split(x):
    hi = x.astype(BF16)
    lo = (x - hi.astype(F32)).astype(BF16)
    return hi, lo


def _sigmoid(x):
    return 1.0 / (1.0 + jnp.exp(-x))


def _level_matrix(levels, n=BLOCK):
    out = np.zeros((len(levels) * n, n), np.float32)
    for i, l in enumerate(levels):
        size = 2 << l
        for t in range(n):
            mid = (t // size) * size + size // 2 - 1
            if (t >> l) & 1:
                out[i * n + t, mid + 1:t + 1] = 1.0
            else:
                out[i * n + t, t + 1:mid + 1] = 1.0
    return out


def _constants():
    r = np.arange(BLOCK)
    tri = (r[None, :] <= r[:, None]).astype(np.float32)
    same_seq = (r[None, :] // DEC_SEQ) == (r[:, None] // DEC_SEQ)
    diff = np.maximum(r[:, None] ^ r[None, :], 1)
    pair_level = np.where(r[None, :] < r[:, None], np.floor(np.log2(diff)).astype(np.int32),
                          np.where(r[None, :] == r[:, None], N_LEVELS, N_LEVELS + 1)).astype(np.int32)
    c = np.arange(2 * LANES)
    bd256 =((c[None, :] // HEAD_DIM) == (c[:, None] // HEAD_DIM)).astype(np.float32)
    return dict(
        tri=jnp.asarray(tri, BF16),
        seq_tri=jnp.asarray(tri * same_seq, BF16),
        seq_ones=jnp.asarray(same_seq.astype(np.float32), BF16),
        lev_all=jnp.asarray(_level_matrix(range(N_LEVELS)), BF16),
        lev_low=jnp.asarray(_level_matrix(range(LOW_LEVELS)), BF16),
        bd256=jnp.asarray(bd256, BF16),
        bd128=jnp.asarray(bd256[:LANES, :LANES], BF16),
        pair_level=jnp.asarray(pair_level),
    )


def _tile_of_step(i):
    return jnp.maximum(i - N_PREP, 0)


def _inproj_kernel(*refs, n_src, emit_x):
    x_refs, refs = refs[:n_src], refs[n_src:]
    vec_ref, w_ref, wg2_ref, bd256_ref, bd128_ref = refs[:5]
    refs = refs[5:]
    if emit_x:
        xo_ref, refs = refs[0], refs[1:]
    q_ref, k_ref, v_ref, gq_ref, gk_ref, ld_ref, gv_ref, og_ref, wbf_ref, wg2s_ref, z0_ref, z1_ref = refs
    i = pl.program_id(0)

    @pl.when(i < N_PREP)
    def _():
        rows = pl.ds(pl.multiple_of(i * W_ROWS, W_ROWS), W_ROWS)
        chunk = w_ref[0]
        wbf_ref[rows, 0:OFF_OG] = chunk[:, 0:SRC_LOW].astype(BF16)
        wbf_ref[rows, OFF_OG:OFF_LOW] = chunk[:, SRC_OG:IN_W].astype(BF16)
        lane = lax.broadcasted_iota(jnp.int32, (W_ROWS, LANES), 1)
        low = jnp.where(lane < GATE_RANK, chunk[:, SRC_LOW:SRC_LOW + LANES], 0.0)
        wbf_ref[rows, OFF_LOW:PROJ_W] = low.astype(BF16)

    @pl.when(i == 0)
    def _():
        wg2s_ref[...] = jnp.concatenate(
            [wg2_ref[0], jnp.zeros((LANES - GATE_RANK, GLA_K_W), F32)], axis=0).astype(BF16)

    t = i - N_PREP

    def finish_steps(z_ref):
        def q_part():
            q = z_ref[:, OFF_Q:OFF_K]
            q2 = (q * q).astype(BF16)
            bd = bd256_ref[...]
            ssq = jnp.concatenate([_dot(q2[:, :256], bd), _dot(q2[:, 256:], bd)], axis=1)
            q_ref[...] = (q * lax.rsqrt(ssq * (1.0 / HEAD_DIM) + RMS_EPS)
                          * vec_ref[V_QG:V_QG + 1, :ATT_Q_W]).astype(BF16)

        def kv_part():
            k = z_ref[:, OFF_K:OFF_V]
            ssk = _dot((k * k).astype(BF16), bd128_ref[...])
            k_ref[...] = (k * lax.rsqrt(ssk * (1.0 / HEAD_DIM) + RMS_EPS)
                          * vec_ref[V_KG:V_KG + 1, :ATT_KV_W])
            v_ref[...] = z_ref[:, OFF_V:OFF_GQ]

        def gla_qk_part():
            gq_ref[...] = z_ref[:, OFF_GQ:OFF_GK] * (GLA_DK ** -0.5)
            gk_ref[...] = z_ref[:, OFF_GK:OFF_GV]

        def gla_v_part():
            gv_ref[...] = z_ref[:, OFF_GV:OFF_OG].astype(BF16)
            og_ref[...] = z_ref[:, OFF_OG:OFF_LOW]

        def gate_part():
            logit = (_dot(z_ref[:, OFF_LOW:PROJ_W].astype(BF16), wg2s_ref[...])
                     + vec_ref[V_BG:V_BG + 1, :GLA_K_W])
            log_sig = jnp.minimum(logit, 0.0) - jnp.log1p(jnp.exp(-jnp.abs(logit)))
            ld_ref[...] = log_sig * (1.0 / GATE_NORMALIZER)

        return [q_part, gate_part, kv_part, gla_qk_part, gla_v_part]

    def run(project_into, finish_from):
        pieces = finish_steps(finish_from) if finish_from is not None else []
        if project_into is None:
            for piece in pieces:
                piece()
            return
        if pieces:
            pieces.pop(0)()
        if n_src == 1:
            x = x_refs[0][...]
        else:
            x = jnp.where(t < MAIN_TILES, x_refs[0][...].reshape(ROW_TILE, D_MODEL),
                          jnp.where(t < MAIN_TILES + SAMPLE_TILES, x_refs[1][...], x_refs[2][...]))
            xo_ref[...] = x
        ms = jnp.mean(x * x, axis=-1, keepdims=True)
        h = (x * lax.rsqrt(ms + RMS_EPS) * vec_ref[V_NORM1:V_NORM1 + 1, :]).astype(BF16)
        n_chunks = -(-PROJ_W // PROJ_CHUNK)
        for c in range(n_chunks):
            cols = slice(c * PROJ_CHUNK, min((c + 1) * PROJ_CHUNK, PROJ_W))
            project_into[:, cols] = _dot(h, wbf_ref[:, cols])
            if c < len(pieces):
                pieces[c]()
        for piece in pieces[n_chunks:]:
            piece()

    @pl.when(t == 0)
    def _():
        run(z0_ref, None)

    inner = (t >= 1) & (t < N_TILES)

    @pl.when(inner & (t % 2 == 1))
    def _():
        run(z1_ref, z0_ref)

    @pl.when(inner & (t % 2 == 0))
    def _():
        run(z0_ref, z1_ref)

    @pl.when(t == N_TILES)
    def _():
        run(None, z0_ref if (N_TILES - 1) % 2 == 0 else z1_ref)


def _inproj(layer, x_srcs, vecs, w_in, w_g2, consts):
    n_src = len(x_srcs)
    t_in = lambda i: jnp.clip(i - N_PREP, 0, N_TILES - 1)
    t_out = lambda i: jnp.clip(i - N_PREP - 1, 0, N_TILES - 1)
    tile = lambda w_: pl.BlockSpec((ROW_TILE, w_), lambda i: (t_out(i), 0))
    full = lambda a: pl.BlockSpec(a.shape, lambda i: (0,) * a.ndim)
    if n_src == 1:
        x_specs = [pl.BlockSpec((ROW_TILE, D_MODEL), lambda i: (t_in(i), 0))]
    else:
        x_specs = [
            pl.BlockSpec((BATCH, BLOCK, D_MODEL), lambda i: (0, jnp.minimum(t_in(i), MAIN_TILES - 1), 0)),
            pl.BlockSpec((ROW_TILE, D_MODEL),
                         lambda i: (jnp.clip(t_in(i) - MAIN_TILES, 0, SAMPLE_TILES - 1), 0)),
            pl.BlockSpec((ROW_TILE, D_MODEL), lambda i: (0, 0)),
        ]
    in_specs = x_specs + [
        pl.BlockSpec((V_ROWS, D_MODEL), lambda i: (layer, 0)),
        pl.BlockSpec((1, W_ROWS, IN_W), lambda i: (layer, jnp.minimum(i, N_PREP - 1), 0)),
        pl.BlockSpec((1, GATE_RANK, GLA_K_W), lambda i: (layer, 0, 0)),
        full(consts["bd256"]), full(consts["bd128"]),
    ]
    outs = [(ATT_Q_W, BF16), (ATT_KV_W, F32), (ATT_KV_W, F32), (GLA_K_W, F32), (GLA_K_W, F32),
            (GLA_K_W, F32), (GLA_V_W, BF16), (GLA_V_W, F32)]
    out_specs = [tile(w_) for w_, _ in outs]
    if n_src > 1:
        outs = [(D_MODEL, F32)] + outs
        out_specs = [pl.BlockSpec((ROW_TILE, D_MODEL), lambda i: (t_in(i), 0))] + out_specs
    res = pl.pallas_call(
        functools.partial(_inproj_kernel, n_src=n_src, emit_x=n_src > 1),
        grid=(N_PREP + N_TILES + 1,),
        in_specs=in_specs,
        out_specs=out_specs,
        out_shape=[jax.ShapeDtypeStruct((TOTAL_ROWS, w_), dt) for w_, dt in outs],
        scratch_shapes=[pltpu.VMEM((D_MODEL, PROJ_W), BF16), pltpu.VMEM((LANES, GLA_K_W), BF16),
                        pltpu.VMEM((ROW_TILE, PROJ_W), F32), pltpu.VMEM((ROW_TILE, PROJ_W), F32)],
        compiler_params=pltpu.CompilerParams(dimension_semantics=("arbitrary",),
                                             vmem_limit_bytes=VMEM_LIMIT),
        name="inproj",
    )(*x_srcs, vecs, w_in, w_g2, consts["bd256"], consts["bd128"])
    if n_src > 1:
        return res[0], tuple(res[1:])
    return x_srcs[0], tuple(res)


def _head_masks(rows):
    lane = lax.broadcasted_iota(jnp.int32, (rows, GLA_K_W), 1)
    return [(lane >> LOG_DK) == h for h in range(GLA_HEADS)]


def _gla_intra(gq, gk, decay_levels, levels, pair_level):
    rows = gq.shape[0]
    hm = _head_masks(rows)

    def pair_products(qh, kh):
        kh = kh.astype(BF16)
        stacked = jnp.concatenate([jnp.where(m, kh, jnp.zeros_like(kh)) for m in hm], axis=0)
        p = _dot_nt(qh.astype(BF16), stacked)
        return [p[:, h * rows:(h + 1) * rows] for h in range(GLA_HEADS)]

    on_diag = pair_level == N_LEVELS
    a = [jnp.where(on_diag, p, 0.0) for p in pair_products(gq, gk)]
    for i, l in enumerate(levels):
        e = decay_levels[i * rows:(i + 1) * rows]
        at_level = pair_level == l
        a = [jnp.where(at_level, p, a_h) for p, a_h in zip(pair_products(gq * e, gk * e), a)]
    return a


def _gla_merge(o, og, gn):
    outs = []
    for h in range(GLA_HEADS):
        oh = o[:, h * GLA_DV:(h + 1) * GLA_DV]
        gh = og[:, h * GLA_DV:(h + 1) * GLA_DV]
        ms = jnp.mean(oh * oh, axis=-1, keepdims=True)
        outs.append(oh * lax.rsqrt(ms + RMS_EPS) * gn * (gh * _sigmoid(gh)))
    return outs


def _dup_halves(x):
    low = lax.broadcasted_iota(jnp.int32, x.shape, 1) < HEAD_DIM
    rolled = pltpu.roll(x, HEAD_DIM, axis=1)
    return jnp.where(low, x, rolled), jnp.where(low, rolled, x)


def _mix_prompt_kernel(*refs, layer, chained):
    sinks_ref, refs = refs[0], refs[1:]
    if chained:
        refs = refs[3:]
    (q_ref, k_ref, v_ref, gq_ref, gk_ref, ld_ref, gv_ref, og_ref, vec_ref, tri_ref, lev_ref, ones_ref, plev_ref,
     m_ref, s_out_ref, k_out_ref, v_out_ref, kprev_ref, vprev_ref, state_ref, sbd_ref) = refs
    blk = pl.program_id(0)

    @pl.when(blk == 0)
    def _():
        kprev_ref[...] = jnp.zeros_like(kprev_ref)
        vprev_ref[...] = jnp.zeros_like(vprev_ref)
        state_ref[...] = jnp.zeros_like(state_ref)
        sbd_ref[...] = jnp.zeros_like(sbd_ref)

    row = lax.broadcasted_iota(jnp.int32, (BLOCK, BLOCK), 0)
    col = lax.broadcasted_iota(jnp.int32, (BLOCK, BLOCK), 1)
    own = col <= row
    kpos = jnp.where(own, blk * BLOCK, (blk - 1) * BLOCK) + col - PAD_LEN
    live = kpos >= 0
    low_half = lax.broadcasted_iota(jnp.int32, (BLOCK, LANES), 1) < HEAD_DIM

    def attend(lhs, kk, vv, sink):
        s = _dot_nt(lhs, kk)
        s = jnp.where(live, jnp.where(own, s[:, BLOCK:], s[:, :BLOCK]), MASK_VALUE)
        m = jnp.maximum(jnp.max(s, axis=-1, keepdims=True), sink)
        p = jnp.exp(s - m)
        denom = jnp.sum(p, axis=-1, keepdims=True) + jnp.exp(sink - m)
        p2 = jnp.concatenate([jnp.where(own, 0.0, p), jnp.where(own, p, 0.0)], axis=1).astype(BF16)
        return _dot(p2, vv) / denom

    for b in range(BATCH):
        rows = slice(b * BLOCK, (b + 1) * BLOCK)
        kdup = [x.astype(BF16) for x in _dup_halves(k_ref[rows, :])]
        vdup = [x.astype(BF16) for x in _dup_halves(v_ref[rows, :])]
        for j in range(N_HEADS // 2):
            g = b * N_KV_HEADS + (2 * j) // (N_HEADS // N_KV_HEADS)
            kk = jnp.concatenate([kprev_ref[g], kdup[g % N_KV_HEADS]], axis=0)
            vv = jnp.concatenate([vprev_ref[g], vdup[g % N_KV_HEADS]], axis=0)
            qt = q_ref[rows, j * LANES:(j + 1) * LANES]
            zero = jnp.zeros_like(qt)
            o_lo = attend(jnp.where(low_half, qt, zero), kk, vv, sinks_ref[layer, 2 * j])
            o_hi = attend(jnp.where(low_half, zero, qt), kk, vv, sinks_ref[layer, 2 * j + 1])
            m_ref[rows, j * LANES:(j + 1) * LANES] = jnp.where(low_half, o_lo, o_hi).astype(BF16)
        for g in range(N_KV_HEADS):
            kprev_ref[b * N_KV_HEADS + g] = kdup[g]
            vprev_ref[b * N_KV_HEADS + g] = vdup[g]

    rpos = blk * BLOCK + lax.broadcasted_iota(jnp.int32, (BLOCK, 1), 0) - PAD_LEN
    valid = (rpos >= 0).astype(F32)
    ld_all = jnp.concatenate([ld_ref[b * BLOCK:(b + 1) * BLOCK, :] for b in range(BATCH)], axis=1)
    hi, lo = _split(ld_all)
    tri = tri_ref[...]
    g_cum_all = _dot(tri, hi) + _dot(tri, lo)
    lev = lev_ref[...]
    decay_all = jnp.exp(_dot(lev, hi) + _dot(lev, lo))
    ones = ones_ref[...]
    decay_col_all = jnp.exp(_dot_tn(hi, ones) + _dot_tn(lo, ones))
    pair_level = plev_ref[...]
    gn = vec_ref[V_GN:V_GN + 1, :GLA_DV]
    for b in range(BATCH):
        rows = slice(b * BLOCK, (b + 1) * BLOCK)
        cols = slice(b * GLA_K_W, (b + 1) * GLA_K_W)
        gq = gq_ref[rows, :]
        gk = gk_ref[rows, :] * valid
        gv = gv_ref[rows, :]
        g_cum = g_cum_all[:, cols]
        g_last = g_cum[BLOCK - 1:BLOCK, :]
        a = _gla_intra(gq, gk, decay_all[:, cols], range(N_LEVELS), pair_level)

        o = _dot((gq * jnp.exp(g_cum)).astype(BF16), sbd_ref[b])
        o = o + jnp.concatenate(
            [_dot(a[h].astype(BF16), gv[:, h * GLA_DV:(h + 1) * GLA_DV]) for h in range(GLA_HEADS)], axis=1)
        for h, gh in enumerate(_gla_merge(o, og_ref[rows, :], gn)):
            m_ref[rows, ATT_Q_W + h * GLA_DV:ATT_Q_W + (h + 1) * GLA_DV] = gh.astype(BF16)

        kv = _dot_tn((gk * jnp.exp(g_last - g_cum)).astype(BF16), gv)
        new_state = decay_col_all[cols, :] * state_ref[b] + jnp.concatenate(
            [kv[h * GLA_DK:(h + 1) * GLA_DK, h * GLA_DV:(h + 1) * GLA_DV] for h in range(GLA_HEADS)],
            axis=0)
        state_ref[b] = new_state
        for h in range(GLA_HEADS):
            hrows = slice(h * GLA_DK, (h + 1) * GLA_DK)
            sbd_ref[b, hrows, h * GLA_DV:(h + 1) * GLA_DV] = new_state[hrows].astype(BF16)

    @pl.when(blk == N_BLOCKS - 1)
    def _():
        s_out_ref[0] = state_ref[...]
        k_out_ref[0] = k_ref[...].reshape(BATCH, BLOCK, ATT_KV_W)
        v_out_ref[0] = v_ref[...].reshape(BATCH, BLOCK, ATT_KV_W)


def _mix_prompt(layer, sinks, proj, vecs, consts, prev):
    tok = lambda w_: pl.BlockSpec((ROW_TILE, w_), lambda i, *_: (jnp.where(i == 0, N_TILES - 1, i - 1), 0))
    full = lambda a: pl.BlockSpec(a.shape, lambda i, *_: (0,) * a.ndim)
    per_batch = lambda r, c: pl.BlockSpec((1, BATCH, r, c), lambda i, *_: (layer, 0, 0, 0))
    ones = jnp.ones((BLOCK, LANES), BF16)
    const_args = (consts["tri"], consts["lev_all"], ones, consts["pair_level"])
    chained = prev is not None
    prev_args = tuple(prev) if chained else ()
    in_specs = ([pl.BlockSpec(memory_space=pl.ANY)] * len(prev_args)
                + [tok(a.shape[1]) for a in proj]
                + [pl.BlockSpec((V_ROWS, D_MODEL), lambda i, *_: (layer, 0))]
                + [full(a) for a in const_args])
    grid_spec = pltpu.PrefetchScalarGridSpec(
        num_scalar_prefetch=1,
        grid=(N_BLOCKS,),
        in_specs=in_specs,
        out_specs=[tok(D_MIX), per_batch(GLA_K_W, GLA_DV), per_batch(BLOCK, ATT_KV_W),
                   per_batch(BLOCK, ATT_KV_W)],
        scratch_shapes=[pltpu.VMEM((BATCH * N_KV_HEADS, BLOCK, ATT_KV_W), BF16),
                        pltpu.VMEM((BATCH * N_KV_HEADS, BLOCK, ATT_KV_W), BF16),
                        pltpu.VMEM((BATCH, GLA_K_W, GLA_DV), F32),
                        pltpu.VMEM((BATCH, GLA_K_W, GLA_V_W), BF16)],
    )
    return pl.pallas_call(
        functools.partial(_mix_prompt_kernel, layer=layer, chained=chained),
        grid_spec=grid_spec,
        out_shape=[jax.ShapeDtypeStruct((TOTAL_ROWS, D_MIX), BF16),
                   jax.ShapeDtypeStruct((DEPTH, BATCH, GLA_K_W, GLA_DV), F32),
                   jax.ShapeDtypeStruct((DEPTH, BATCH, BLOCK, ATT_KV_W), F32),
                   jax.ShapeDtypeStruct((DEPTH, BATCH, BLOCK, ATT_KV_W), F32)],
        input_output_aliases={1 + n: 1 + n for n in range(len(prev_args))},
        compiler_params=pltpu.CompilerParams(dimension_semantics=("arbitrary",),
                                             vmem_limit_bytes=VMEM_LIMIT),
        name="mix_prompt",
    )(sinks, *prev_args, *proj, vecs, *const_args)


def _mix_sample_kernel(*refs, layer, n_alias):
    sinks_ref, refs = refs[0], refs[n_alias + 1:]
    (q_ref, k_ref, v_ref, gq_ref, gk_ref, ld_ref, gv_ref, og_ref, ck_ref, cv_ref, st_ref,
     vec_ref, tri_ref, sones_ref, lev_ref, ones_ref, plev_ref,
     m_ref, ck_out_ref, cv_out_ref, st_out_ref) = refs
    gq = gq_ref[...]
    gk = gk_ref[...]
    gv = gv_ref[...]
    gvf = gv.astype(F32)
    hi, lo = _split(ld_ref[...])
    tri = tri_ref[...]
    g_cum = _dot(tri, hi) + _dot(tri, lo)
    sones = sones_ref[...]
    g_tot = _dot(sones, hi) + _dot(sones, lo)
    lev = lev_ref[...]
    decay_levels = jnp.exp(_dot(lev, hi) + _dot(lev, lo))
    a = _gla_intra(gq, gk, decay_levels, range(LOW_LEVELS), plev_ref[...])
    o_intra = jnp.concatenate(
        [_dot(a[h].astype(BF16), gv[:, h * GLA_DV:(h + 1) * GLA_DV]) for h in range(GLA_HEADS)], axis=1)
    q_dec = gq * jnp.exp(g_cum)
    k_dec = gk * jnp.exp(g_tot - g_cum)
    hm8 = _head_masks(DEC_SEQ)
    ones8 = ones_ref[...]
    hi_f, lo_f = hi.astype(F32), lo.astype(F32)

    qf = q_ref[...].astype(F32)
    kf = k_ref[...]
    vf = v_ref[...]
    low8 = lax.broadcasted_iota(jnp.int32, (DEC_SEQ, LANES), 1) < HEAD_DIM
    nkeys = WINDOW + DEC_SEQ
    srow = lax.broadcasted_iota(jnp.int32, (N_HEADS * DEC_SEQ, nkeys), 0)
    scol = lax.broadcasted_iota(jnp.int32, (N_HEADS * DEC_SEQ, nkeys), 1)
    t_of_row = srow & (DEC_SEQ - 1)
    amask = ((scol < WINDOW) & (scol > t_of_row)) | ((scol >= WINDOW) & (scol - WINDOW <= t_of_row))
    rid = lax.broadcasted_iota(jnp.int32, (N_HEADS * DEC_SEQ, 1), 0) >> LOW_LEVELS
    sink_col = jnp.zeros((N_HEADS * DEC_SEQ, 1), F32)
    for i in range(N_HEADS):
        sink_col = jnp.where(rid == i, sinks_ref[layer, i], sink_col)
    half = N_HEADS * DEC_SEQ // N_KV_HEADS

    att_rows, inter_rows = [], []
    for b in range(SEQ_GROUP):
        rows = slice(b * DEC_SEQ, (b + 1) * DEC_SEQ)
        pieces = []
        for j in range(N_HEADS // 2):
            qt = qf[rows, j * LANES:(j + 1) * LANES]
            pieces += [jnp.where(low8, qt, 0.0), jnp.where(low8, 0.0, qt)]
        qp = jnp.concatenate(pieces, axis=0).astype(BF16)
        kk = jnp.concatenate([ck_ref[0, b], kf[rows]], axis=0)
        vv = jnp.concatenate([cv_ref[0, b], vf[rows]], axis=0)
        kd = [x.astype(BF16) for x in _dup_halves(kk)]
        vd = [x.astype(BF16) for x in _dup_halves(vv)]
        s = jnp.concatenate([_dot_nt(qp[:half], kd[0]), _dot_nt(qp[half:], kd[1])], axis=0)
        s = jnp.where(amask, s, MASK_VALUE)
        m = jnp.maximum(jnp.max(s, axis=-1, keepdims=True), sink_col)
        p = jnp.exp(s - m)
        denom = jnp.sum(p, axis=-1, keepdims=True) + jnp.exp(sink_col - m)
        pb = p.astype(BF16)
        ob = jnp.concatenate([_dot(pb[:half], vd[0]), _dot(pb[half:], vd[1])], axis=0) / denom
        att_rows.append(jnp.concatenate(
            [jnp.where(low8, ob[(2 * j) * DEC_SEQ:(2 * j + 1) * DEC_SEQ],
                       ob[(2 * j + 1) * DEC_SEQ:(2 * j + 2) * DEC_SEQ])
             for j in range(N_HEADS // 2)], axis=1))
        ck_out_ref[0, b] = kk[DEC_SEQ:]
        cv_out_ref[0, b] = vv[DEC_SEQ:]

        state = st_ref[0, b]
        qb = q_dec[rows]
        qstack = jnp.concatenate([jnp.where(mk, qb, 0.0) for mk in hm8], axis=0).astype(BF16)
        oi = _dot(qstack, state.astype(BF16))
        inter_rows.append(jnp.concatenate(
            [oi[h * DEC_SEQ:(h + 1) * DEC_SEQ] for h in range(GLA_HEADS)], axis=1))
        kb = k_dec[rows]
        kstack = jnp.concatenate([jnp.where(mk, kb, 0.0) for mk in hm8], axis=0).astype(BF16)
        vb = gvf[rows]
        vstack = jnp.concatenate([vb[:, h * GLA_DV:(h + 1) * GLA_DV] for h in range(GLA_HEADS)],
                                 axis=0).astype(BF16)
        decay_col = jnp.exp(_dot_tn(hi_f[rows], ones8) + _dot_tn(lo_f[rows], ones8))
        st_out_ref[0, b] = decay_col * state + _dot_tn(kstack, vstack)

    m_ref[:, :ATT_Q_W] = jnp.concatenate(att_rows, axis=0).astype(BF16)
    o = o_intra + jnp.concatenate(inter_rows, axis=0)
    gn = vec_ref[V_GN:V_GN + 1, :GLA_DV]
    for h, gh in enumerate(_gla_merge(o, og_ref[...], gn)):
        m_ref[:, ATT_Q_W + h * GLA_DV:ATT_Q_W + (h + 1) * GLA_DV] = gh.astype(BF16)


def _mix_sample(layer, sinks, merged, proj, cache_k, cache_v, state, vecs, consts, prev):
    tok = lambda w_: pl.BlockSpec((BLOCK, w_), lambda i, *_: (SAMPLE_BLK0 + i, 0))
    full = lambda a: pl.BlockSpec(a.shape, lambda i, *_: (0,) * a.ndim)
    seq = lambda a: pl.BlockSpec((1, SEQ_GROUP) + a.shape[2:], lambda i, *_: (layer, i, 0, 0))
    ones8 = jnp.ones((DEC_SEQ, LANES), F32)
    const_args = (consts["seq_tri"], consts["seq_ones"], consts["lev_low"], ones8, consts["pair_level"])
    seq_args = (cache_k, cache_v, state)
    alias_args = (merged,) + (tuple(prev) if prev is not None else ())
    grid_spec = pltpu.PrefetchScalarGridSpec(
        num_scalar_prefetch=1,
        grid=(DEC_BATCH // SEQ_GROUP,),
        in_specs=([pl.BlockSpec(memory_space=pl.ANY)] * len(alias_args)
                  + [tok(a.shape[1]) for a in proj] + [seq(a) for a in seq_args]
                  + [pl.BlockSpec((V_ROWS, D_MODEL), lambda i, *_: (layer, 0))]
                  + [full(a) for a in const_args]),
        out_specs=[tok(D_MIX)] + [seq(a) for a in seq_args],
    )
    return pl.pallas_call(
        functools.partial(_mix_sample_kernel, layer=layer, n_alias=len(alias_args)),
        grid_spec=grid_spec,
        out_shape=[jax.ShapeDtypeStruct(merged.shape, merged.dtype)]
        + [jax.ShapeDtypeStruct(a.shape, a.dtype) for a in seq_args],
        input_output_aliases={1 + n: n for n in range(len(alias_args))},
        compiler_params=pltpu.CompilerParams(dimension_semantics=("arbitrary",),
                                             vmem_limit_bytes=VMEM_LIMIT),
        name="mix_sample",
    )(sinks, *alias_args, *proj, *seq_args, vecs, *const_args)


def _out_ffn_kernel(*refs, last):
    x_ref, m_ref, vec_ref, wo_ref, wg_ref, wu_ref, wd_ref = refs[:7]
    out_refs = refs[7:9] if last else refs[7:8]
    wo_s, wg_s, wu_s, wd_s, act_ref = refs[-5:]
    i = pl.program_id(0)

    @pl.when(i < N_PREP)
    def _():
        rows = pl.ds(pl.multiple_of(i * W_ROWS, W_ROWS), W_ROWS)
        wo_s[rows, :] = wo_ref[0].astype(BF16)
        wg_s[rows, :] = wg_ref[0].astype(BF16)
        wu_s[rows, :] = wu_ref[0].astype(BF16)
        wd_s[pl.ds(pl.multiple_of(i * WD_ROWS, WD_ROWS), WD_ROWS), :] = wd_ref[0].astype(BF16)

    @pl.when(i >= N_PREP)
    def _():
        x1 = x_ref[...] + _dot(m_ref[...], wo_s[...])
        ms = jnp.mean(x1 * x1, axis=-1, keepdims=True)
        h = (x1 * lax.rsqrt(ms + RMS_EPS) * vec_ref[V_NORM2:V_NORM2 + 1, :]).astype(BF16)
        for c in range(D_FFN // FFN_CHUNK):
            cols = slice(c * FFN_CHUNK, (c + 1) * FFN_CHUNK)
            gate = _dot(h, wg_s[:, cols])
            up = _dot(h, wu_s[:, cols])
            act_ref[:, cols] = (gate * _sigmoid(gate) * up).astype(BF16)
        y = x1 + _dot(act_ref[...], wd_s[...])
        if last:
            t = i - N_PREP

            @pl.when(t < MAIN_TILES)
            def _():
                out_refs[0][...] = y.reshape(BATCH, BLOCK, D_MODEL)

            @pl.when(t >= MAIN_TILES)
            def _():
                out_refs[1][...] = y
        else:
            out_refs[0][...] = y


def _out_ffn(layer, x_all, merged, vecs, w_o, w_gate, w_up, w_down, last):
    tile = lambda w_: pl.BlockSpec((ROW_TILE, w_), lambda i: (_tile_of_step(i), 0))
    chunk = lambda r, c: pl.BlockSpec((1, r, c), lambda i: (layer, jnp.minimum(i, N_PREP - 1), 0))
    if last:
        n_tiles = MAIN_TILES + SAMPLE_TILES
        out_specs = [pl.BlockSpec((BATCH, BLOCK, D_MODEL),
                                  lambda i: (0, jnp.minimum(_tile_of_step(i), MAIN_TILES - 1), 0)),
                     pl.BlockSpec((ROW_TILE, D_MODEL),
                                  lambda i: (jnp.maximum(_tile_of_step(i) - MAIN_TILES, 0), 0))]
        out_shape = [jax.ShapeDtypeStruct((BATCH, SEQ, D_MODEL), F32),
                     jax.ShapeDtypeStruct((SAMPLE_ROWS, D_MODEL), F32)]
    else:
        n_tiles = N_TILES
        out_specs = [tile(D_MODEL)]
        out_shape = [jax.ShapeDtypeStruct((TOTAL_ROWS, D_MODEL), F32)]
    return pl.pallas_call(
        functools.partial(_out_ffn_kernel, last=last),
        grid=(N_PREP + n_tiles,),
        in_specs=[tile(D_MODEL), tile(D_MIX), pl.BlockSpec((V_ROWS, D_MODEL), lambda i: (layer, 0)),
                  chunk(W_ROWS, D_MODEL), chunk(W_ROWS, D_FFN), chunk(W_ROWS, D_FFN),
                  chunk(WD_ROWS, D_MODEL)],
        out_specs=out_specs,
        out_shape=out_shape,
        scratch_shapes=[pltpu.VMEM((D_MIX, D_MODEL), BF16), pltpu.VMEM((D_MODEL, D_FFN), BF16),
                        pltpu.VMEM((D_MODEL, D_FFN), BF16), pltpu.VMEM((D_FFN, D_MODEL), BF16),
                        pltpu.VMEM((ROW_TILE, D_FFN), BF16)],
        compiler_params=pltpu.CompilerParams(dimension_semantics=("arbitrary",),
                                             vmem_limit_bytes=VMEM_LIMIT),
        name="out_ffn",
    )(x_all, merged, vecs, w_o, w_gate, w_up, w_down)


def _vector_slab(norm1, norm2, q_norm, k_norm, b_g, gla_norm):
    pad = lambda a: jnp.pad(a.astype(F32), ((0, 0), (0, D_MODEL - a.shape[1])))
    rows = [norm1.astype(F32), norm2.astype(F32), pad(jnp.tile(q_norm, (1, N_HEADS)) * ATT_SCALE),
            pad(jnp.tile(k_norm, (1, N_KV_HEADS))), pad(b_g), pad(gla_norm)]
    rows += [jnp.zeros((DEPTH, D_MODEL), F32)] * (V_ROWS - len(rows))
    return jnp.stack(rows, axis=1).reshape(DEPTH * V_ROWS, D_MODEL)


def kernel(x_prompt, x_sample, cache_k, cache_v, state_gla, meta, norm1, w_in, q_norm, k_norm, sinks,
           w_g2, b_g, gla_norm, w_o, norm2, w_gate, w_up, w_down):
    consts = _constants()
    dt = x_prompt.dtype
    vecs = _vector_slab(norm1, norm2, q_norm, k_norm, b_g, gla_norm)
    sinks = sinks.astype(F32)
    lead = jnp.tile(jnp.concatenate([jnp.zeros((PAD_LEN, D_MODEL), dt), meta.astype(dt)], axis=0), (BATCH, 1))
    x_srcs = (x_prompt, x_sample.reshape(SAMPLE_ROWS, D_MODEL), lead)
    ck_in = cache_k.reshape(DEPTH, DEC_BATCH, WINDOW, ATT_KV_W)
    cv_in = cache_v.reshape(DEPTH, DEC_BATCH, WINDOW, ATT_KV_W)
    st_in = state_gla.reshape(DEPTH, DEC_BATCH, GLA_K_W, GLA_DV)

    prompt_outs, sample_outs = None, None
    for l in range(DEPTH):
        x_all, proj = _inproj(l, x_srcs, vecs, w_in, w_g2, consts)
        merged, *prompt_outs = _mix_prompt(l, sinks, proj, vecs, consts, prompt_outs)
        merged, *sample_outs = _mix_sample(l, sinks, merged, proj, ck_in, cv_in, st_in, vecs, consts,
                                           sample_outs)
        x_srcs = tuple(_out_ffn(l, x_all, merged, vecs, w_o, w_gate, w_up, w_down, last=l == DEPTH - 1))

    y_main, y_sample = x_srcs
    ps, pk, pv = prompt_outs
    sk, sv, ss = sample_outs
    kv5 = lambda a, n: a.reshape(DEPTH, n, WINDOW, N_KV_HEADS, HEAD_DIM)
    st5 = lambda a, n: a.reshape(DEPTH, n, GLA_HEADS, GLA_DK, GLA_DV)
    return (y_main, y_sample.reshape(DEC_BATCH, DEC_SEQ, D_MODEL),
            kv5(pk, BATCH), kv5(pv, BATCH), st5(ps, BATCH), kv5(sk, DEC_BATCH), kv5(sv, DEC_BATCH),
            st5(ss, DEC_BATCH))
```

```python
import functools

import jax
import jax.numpy as jnp
import numpy as np
from jax import lax
from jax.experimental import pallas as pl
from jax.experimental.pallas import tpu as pltpu

F32 = jnp.float32
BF16 = jnp.bfloat16

D_MODEL = 1024
BATCH = 4
SEQ = 4096
DEPTH = 2
DEC_BATCH = 128
DEC_SEQ = 8
N_META = 16
WINDOW = 128
BLOCK = 128
PAD_LEN = BLOCK - N_META
N_HEADS = 8
N_KV_HEADS = 2
HEAD_DIM = 64
ATT_SCALE = HEAD_DIM ** -0.5
ATT_Q_W = N_HEADS * HEAD_DIM
ATT_KV_W = N_KV_HEADS * HEAD_DIM
GLA_HEADS = 4
GLA_DK = 64
GLA_DV = 128
GLA_K_W = GLA_HEADS * GLA_DK
GLA_V_W = GLA_HEADS * GLA_DV
GATE_RANK = 16
GATE_NORMALIZER = 16.0
D_MIX = ATT_Q_W + GLA_V_W
D_FFN = 2816
IN_W = 2320
RMS_EPS = 1e-6
MASK_VALUE = -1e30

LANES = 128
N_BLOCKS = 1 + SEQ // BLOCK
MAIN_ROWS = BATCH * SEQ
SAMPLE_ROWS = DEC_BATCH * DEC_SEQ
LEAD_ROWS = BATCH * BLOCK
TOTAL_ROWS = MAIN_ROWS + SAMPLE_ROWS + LEAD_ROWS
ROW_TILE = 512
MAIN_TILES = MAIN_ROWS // ROW_TILE
SAMPLE_TILES = SAMPLE_ROWS // ROW_TILE
N_TILES = TOTAL_ROWS // ROW_TILE
SAMPLE_BLK0 = MAIN_ROWS // BLOCK
LEAD_BLK0 = (MAIN_ROWS + SAMPLE_ROWS) // BLOCK
SEQ_GROUP = BLOCK // DEC_SEQ
N_LEVELS = 7
LOW_LEVELS = 3
LOG_DK = 6
N_PREP = 8
W_ROWS = D_MODEL // N_PREP
WD_ROWS = D_FFN // N_PREP

SRC_LOW, SRC_OG = 1792, 1808
OFF_Q, OFF_K, OFF_V = 0, 512, 640
OFF_GQ, OFF_GK, OFF_GV, OFF_OG, OFF_LOW = 768, 1024, 1280, 1792, 2304
PROJ_W = OFF_LOW + LANES
PROJ_CHUNK = 512
FFN_CHUNK = 256
VMEM_LIMIT = 56 * 1024 * 1024

V_NORM1, V_NORM2, V_QG, V_KG, V_BG, V_GN, V_ROWS = 0, 1, 2, 3, 4, 5, 8

PF_K, PF_V, PF_GQ, PF_GK, PF_LD, PF_OG, PF_W = 0, 128, 256, 512, 768, 1024, 1536
PB_Q, PB_GV, PB_W = 0, 512, 1024


def _proj_views(pf_ref, pb_ref):
    f = lambda a, b: pf_ref.at[:, a:b]
    return (pb_ref.at[:, PB_Q:PB_GV], f(PF_K, PF_V), f(PF_V, PF_GQ), f(PF_GQ, PF_GK), f(PF_GK, PF_LD),
            f(PF_LD, PF_OG), pb_ref.at[:, PB_GV:PB_W], f(PF_OG, PF_W))


def _dot(a, b):
    return jnp.dot(a, b, preferred_element_type=F32)


def _dot_nt(a, b):
    return lax.dot_general(a, b, (((1,), (1,)), ((), ())), preferred_element_type=F32)


def _dot_tn(a, b):
    return lax.dot_general(a, b, (((0,), (0,)), ((), ())), preferred_element_type=F32)


def _split(x):
    hi = x.astype(BF16)
    lo = (x - hi.astype(F32)).astype(BF16)
    return hi, lo


def _sigmoid(x):
    return 1.0 / (1.0 + jnp.exp(-x))


def _level_matrix(levels, n=BLOCK):
    out = np.zeros((len(levels) * n, n), np.float32)
    for i, l in enumerate(levels):
        size = 2 << l
        for t in range(n):
            mid = (t // size) * size + size // 2 - 1
            if (t >> l) & 1:
                out[i * n + t, mid + 1:t + 1] = 1.0
            else:
                out[i * n + t, t + 1:mid + 1] = 1.0
    return out


def _constants():
    r = np.arange(BLOCK)
    tri = (r[None, :] <= r[:, None]).astype(np.float32)
    same_seq = (r[None, :] // DEC_SEQ) == (r[:, None] // DEC_SEQ)
    diff = np.maximum(r[:, None] ^ r[None, :], 1)
    pair_level = np.where(r[None, :] < r[:, None], np.floor(np.log2(diff)).astype(np.int32),
                          np.where(r[None, :] == r[:, None], N_LEVELS, N_LEVELS + 1)).astype(np.int32)
    c = np.arange(2 * LANES)
    bd256 =((c[None, :] // HEAD_DIM) == (c[:, None] // HEAD_DIM)).astype(np.float32)
    return dict(
        tri=jnp.asarray(tri, BF16),
        seq_tri=jnp.asarray(tri * same_seq, BF16),
        seq_ones=jnp.asarray(same_seq.astype(np.float32), BF16),
        lev_all=jnp.asarray(_level_matrix(range(N_LEVELS)), BF16),
        lev_low=jnp.asarray(_level_matrix(range(LOW_LEVELS)), BF16),
        bd256=jnp.asarray(bd256, BF16),
        bd128=jnp.asarray(bd256[:LANES, :LANES], BF16),
        pair_level=jnp.asarray(pair_level),
    )


def _tile_of_step(i):
    return jnp.maximum(i - N_PREP, 0)


def _inproj_kernel(*refs, n_src, emit_x):
    x_refs, refs = refs[:n_src], refs[n_src:]
    vec_ref, w_ref, wg2_ref, bd256_ref, bd128_ref = refs[:5]
    refs = refs[5:]
    if emit_x:
        xo_ref, refs = refs[0], refs[1:]
    pf_ref, pb_ref, wbf_ref, wg2s_ref, z0_ref, z1_ref = refs
    q_ref, k_ref, v_ref, gq_ref, gk_ref, ld_ref, gv_ref, og_ref = _proj_views(pf_ref, pb_ref)
    i = pl.program_id(0)

    @pl.when(i < N_PREP)
    def _():
        rows = pl.ds(pl.multiple_of(i * W_ROWS, W_ROWS), W_ROWS)
        chunk = w_ref[0]
        wbf_ref[rows, 0:OFF_OG] = chunk[:, 0:SRC_LOW].astype(BF16)
        wbf_ref[rows, OFF_OG:OFF_LOW] = chunk[:, SRC_OG:IN_W].astype(BF16)
        lane = lax.broadcasted_iota(jnp.int32, (W_ROWS, LANES), 1)
        low = jnp.where(lane < GATE_RANK, chunk[:, SRC_LOW:SRC_LOW + LANES], 0.0)
        wbf_ref[rows, OFF_LOW:PROJ_W] = low.astype(BF16)

    @pl.when(i == 0)
    def _():
        wg2s_ref[...] = jnp.concatenate(
            [wg2_ref[0], jnp.zeros((LANES - GATE_RANK, GLA_K_W), F32)], axis=0).astype(BF16)

    t = i - N_PREP

    def finish_steps(z_ref):
        def q_part():
            q = z_ref[:, OFF_Q:OFF_K]
            q2 = (q * q).astype(BF16)
            bd = bd256_ref[...]
            ssq = jnp.concatenate([_dot(q2[:, :256], bd), _dot(q2[:, 256:], bd)], axis=1)
            q_ref[...] = (q * lax.rsqrt(ssq * (1.0 / HEAD_DIM) + RMS_EPS)
                          * vec_ref[V_QG:V_QG + 1, :ATT_Q_W]).astype(BF16)

        def kv_part():
            k = z_ref[:, OFF_K:OFF_V]
            ssk = _dot((k * k).astype(BF16), bd128_ref[...])
            k_ref[...] = (k * lax.rsqrt(ssk * (1.0 / HEAD_DIM) + RMS_EPS)
                          * vec_ref[V_KG:V_KG + 1, :ATT_KV_W])
            v_ref[...] = z_ref[:, OFF_V:OFF_GQ]

        def gla_qk_part():
            gq_ref[...] = z_ref[:, OFF_GQ:OFF_GK] * (GLA_DK ** -0.5)
            gk_ref[...] = z_ref[:, OFF_GK:OFF_GV]

        def gla_v_part():
            gv_ref[...] = z_ref[:, OFF_GV:OFF_OG].astype(BF16)
            og_ref[...] = z_ref[:, OFF_OG:OFF_LOW]

        def gate_part():
            logit = (_dot(z_ref[:, OFF_LOW:PROJ_W].astype(BF16), wg2s_ref[...])
                     + vec_ref[V_BG:V_BG + 1, :GLA_K_W])
            log_sig = jnp.minimum(logit, 0.0) - jnp.log1p(jnp.exp(-jnp.abs(logit)))
            ld_ref[...] = log_sig * (1.0 / GATE_NORMALIZER)

        return [q_part, gate_part, kv_part, gla_qk_part, gla_v_part]

    def run(project_into, finish_from):
        pieces = finish_steps(finish_from) if finish_from is not None else []
        if project_into is None:
            for piece in pieces:
                piece()
            return
        if pieces:
            pieces.pop(0)()
        if n_src == 1:
            x = x_refs[0][...]
        else:
            x = jnp.where(t < MAIN_TILES, x_refs[0][...].reshape(ROW_TILE, D_MODEL),
                          jnp.where(t < MAIN_TILES + SAMPLE_TILES, x_refs[1][...], x_refs[2][...]))
            xo_ref[...] = x
        ms = jnp.mean(x * x, axis=-1, keepdims=True)
        h = (x * lax.rsqrt(ms + RMS_EPS) * vec_ref[V_NORM1:V_NORM1 + 1, :]).astype(BF16)
        n_chunks = -(-PROJ_W // PROJ_CHUNK)
        for c in range(n_chunks):
            cols = slice(c * PROJ_CHUNK, min((c + 1) * PROJ_CHUNK, PROJ_W))
            project_into[:, cols] = _dot(h, wbf_ref[:, cols])
            if c < len(pieces):
                pieces[c]()
        for piece in pieces[n_chunks:]:
            piece()

    @pl.when(t == 0)
    def _():
        run(z0_ref, None)

    inner = (t >= 1) & (t < N_TILES)

    @pl.when(inner & (t % 2 == 1))
    def _():
        run(z1_ref, z0_ref)

    @pl.when(inner & (t % 2 == 0))
    def _():
        run(z0_ref, z1_ref)

    @pl.when(t == N_TILES)
    def _():
        run(None, z0_ref if (N_TILES - 1) % 2 == 0 else z1_ref)


def _inproj(layer, x_srcs, vecs, w_in, w_g2, consts):
    n_src = len(x_srcs)
    t_in = lambda i: jnp.clip(i - N_PREP, 0, N_TILES - 1)
    t_out = lambda i: jnp.clip(i - N_PREP - 1, 0, N_TILES - 1)
    tile = lambda w_: pl.BlockSpec((ROW_TILE, w_), lambda i: (t_out(i), 0))
    full = lambda a: pl.BlockSpec(a.shape, lambda i: (0,) * a.ndim)
    if n_src == 1:
        x_specs = [pl.BlockSpec((ROW_TILE, D_MODEL), lambda i: (t_in(i), 0))]
    else:
        x_specs = [
            pl.BlockSpec((BATCH, BLOCK, D_MODEL), lambda i: (0, jnp.minimum(t_in(i), MAIN_TILES - 1), 0)),
            pl.BlockSpec((ROW_TILE, D_MODEL),
                         lambda i: (jnp.clip(t_in(i) - MAIN_TILES, 0, SAMPLE_TILES - 1), 0)),
            pl.BlockSpec((ROW_TILE, D_MODEL), lambda i: (0, 0)),
        ]
    in_specs = x_specs + [
        pl.BlockSpec((V_ROWS, D_MODEL), lambda i: (layer, 0)),
        pl.BlockSpec((1, W_ROWS, IN_W), lambda i: (layer, jnp.minimum(i, N_PREP - 1), 0)),
        pl.BlockSpec((1, GATE_RANK, GLA_K_W), lambda i: (layer, 0, 0)),
        full(consts["bd256"]), full(consts["bd128"]),
    ]
    outs = [(PF_W, F32), (PB_W, BF16)]
    out_specs = [tile(w_) for w_, _ in outs]
    if n_src > 1:
        outs = [(D_MODEL, F32)] + outs
        out_specs = [pl.BlockSpec((ROW_TILE, D_MODEL), lambda i: (t_in(i), 0))] + out_specs
    res = pl.pallas_call(
        functools.partial(_inproj_kernel, n_src=n_src, emit_x=n_src > 1),
        grid=(N_PREP + N_TILES + 1,),
        in_specs=in_specs,
        out_specs=out_specs,
        out_shape=[jax.ShapeDtypeStruct((TOTAL_ROWS, w_), dt) for w_, dt in outs],
        scratch_shapes=[pltpu.VMEM((D_MODEL, PROJ_W), BF16), pltpu.VMEM((LANES, GLA_K_W), BF16),
                        pltpu.VMEM((ROW_TILE, PROJ_W), F32), pltpu.VMEM((ROW_TILE, PROJ_W), F32)],
        compiler_params=pltpu.CompilerParams(dimension_semantics=("arbitrary",),
                                             vmem_limit_bytes=VMEM_LIMIT),
        name="inproj",
    )(*x_srcs, vecs, w_in, w_g2, consts["bd256"], consts["bd128"])
    if n_src > 1:
        return res[0], tuple(res[1:])
    return x_srcs[0], tuple(res)


def _head_masks(rows):
    lane = lax.broadcasted_iota(jnp.int32, (rows, GLA_K_W), 1)
    return [(lane >> LOG_DK) == h for h in range(GLA_HEADS)]


def _gla_intra(gq, gk, decay_levels, levels, pair_level):
    rows = gq.shape[0]
    hm = _head_masks(rows)

    def pair_products(qh, kh):
        kh = kh.astype(BF16)
        stacked = jnp.concatenate([jnp.where(m, kh, jnp.zeros_like(kh)) for m in hm], axis=0)
        p = _dot_nt(qh.astype(BF16), stacked)
        return [p[:, h * rows:(h + 1) * rows] for h in range(GLA_HEADS)]

    on_diag = pair_level == N_LEVELS
    a = [jnp.where(on_diag, p, 0.0) for p in pair_products(gq, gk)]
    for i, l in enumerate(levels):
        e = decay_levels[i * rows:(i + 1) * rows]
        at_level = pair_level == l
        a = [jnp.where(at_level, p, a_h) for p, a_h in zip(pair_products(gq * e, gk * e), a)]
    return a


def _gla_merge(o, og, gn):
    outs = []
    for h in range(GLA_HEADS):
        oh = o[:, h * GLA_DV:(h + 1) * GLA_DV]
        gh = og[:, h * GLA_DV:(h + 1) * GLA_DV]
        ms = jnp.mean(oh * oh, axis=-1, keepdims=True)
        outs.append(oh * lax.rsqrt(ms + RMS_EPS) * gn * (gh * _sigmoid(gh)))
    return outs


def _dup_halves(x):
    low = lax.broadcasted_iota(jnp.int32, x.shape, 1) < HEAD_DIM
    rolled = pltpu.roll(x, HEAD_DIM, axis=1)
    return jnp.where(low, x, rolled), jnp.where(low, rolled, x)


def _mix_prompt_kernel(*refs, layer, chained):
    sinks_ref, refs = refs[0], refs[1:]
    if chained:
        refs = refs[3:]
    (pf_ref, pb_ref, vec_ref, tri_ref, lev_ref, ones_ref, plev_ref,
     m_ref, s_out_ref, k_out_ref, v_out_ref, kprev_ref, vprev_ref, state_ref, sbd_ref) = refs
    q_ref, k_ref, v_ref, gq_ref, gk_ref, ld_ref, gv_ref, og_ref = _proj_views(pf_ref, pb_ref)
    blk = pl.program_id(0)

    @pl.when(blk == 0)
    def _():
        kprev_ref[...] = jnp.zeros_like(kprev_ref)
        vprev_ref[...] = jnp.zeros_like(vprev_ref)
        state_ref[...] = jnp.zeros_like(state_ref)
        sbd_ref[...] = jnp.zeros_like(sbd_ref)

    row = lax.broadcasted_iota(jnp.int32, (BLOCK, BLOCK), 0)
    col = lax.broadcasted_iota(jnp.int32, (BLOCK, BLOCK), 1)
    own = col <= row
    kpos = jnp.where(own, blk * BLOCK, (blk - 1) * BLOCK) + col - PAD_LEN
    live = kpos >= 0
    low_half = lax.broadcasted_iota(jnp.int32, (BLOCK, LANES), 1) < HEAD_DIM

    def attend(lhs, kk, vv, sink):
        s = _dot_nt(lhs, kk)
        s = jnp.where(live, jnp.where(own, s[:, BLOCK:], s[:, :BLOCK]), MASK_VALUE)
        m = jnp.maximum(jnp.max(s, axis=-1, keepdims=True), sink)
        p = jnp.exp(s - m)
        denom = jnp.sum(p, axis=-1, keepdims=True) + jnp.exp(sink - m)
        p2 = jnp.concatenate([jnp.where(own, 0.0, p), jnp.where(own, p, 0.0)], axis=1).astype(BF16)
        return _dot(p2, vv) / denom

    for b in range(BATCH):
        rows = slice(b * BLOCK, (b + 1) * BLOCK)
        kdup = [x.astype(BF16) for x in _dup_halves(k_ref[rows, :])]
        vdup = [x.astype(BF16) for x in _dup_halves(v_ref[rows, :])]
        for j in range(N_HEADS // 2):
            g = b * N_KV_HEADS + (2 * j) // (N_HEADS // N_KV_HEADS)
            kk = jnp.concatenate([kprev_ref[g], kdup[g % N_KV_HEADS]], axis=0)
            vv = jnp.concatenate([vprev_ref[g], vdup[g % N_KV_HEADS]], axis=0)
            qt = q_ref[rows, j * LANES:(j + 1) * LANES]
            zero = jnp.zeros_like(qt)
            o_lo = attend(jnp.where(low_half, qt, zero), kk, vv, sinks_ref[layer, 2 * j])
            o_hi = attend(jnp.where(low_half, zero, qt), kk, vv, sinks_ref[layer, 2 * j + 1])
            m_ref[rows, j * LANES:(j + 1) * LANES] = jnp.where(low_half, o_lo, o_hi).astype(BF16)
        for g in range(N_KV_HEADS):
            kprev_ref[b * N_KV_HEADS + g] = kdup[g]
            vprev_ref[b * N_KV_HEADS + g] = vdup[g]

    rpos = blk * BLOCK + lax.broadcasted_iota(jnp.int32, (BLOCK, 1), 0) - PAD_LEN
    valid = (rpos >= 0).astype(F32)
    ld_all = jnp.concatenate([ld_ref[b * BLOCK:(b + 1) * BLOCK, :] for b in range(BATCH)], axis=1)
    hi, lo = _split(ld_all)
    tri = tri_ref[...]
    g_cum_all = _dot(tri, hi) + _dot(tri, lo)
    lev = lev_ref[...]
    decay_all = jnp.exp(_dot(lev, hi) + _dot(lev, lo))
    ones = ones_ref[...]
    decay_col_all = jnp.exp(_dot_tn(hi, ones) + _dot_tn(lo, ones))
    pair_level = plev_ref[...]
    gn = vec_ref[V_GN:V_GN + 1, :GLA_DV]
    for b in range(BATCH):
        rows = slice(b * BLOCK, (b + 1) * BLOCK)
        cols = slice(b * GLA_K_W, (b + 1) * GLA_K_W)
        gq = gq_ref[rows, :]
        gk = gk_ref[rows, :] * valid
        gv = gv_ref[rows, :]
        g_cum = g_cum_all[:, cols]
        g_last = g_cum[BLOCK - 1:BLOCK, :]
        a = _gla_intra(gq, gk, decay_all[:, cols], range(N_LEVELS), pair_level)

        o = _dot((gq * jnp.exp(g_cum)).astype(BF16), sbd_ref[b])
        o = o + jnp.concatenate(
            [_dot(a[h].astype(BF16), gv[:, h * GLA_DV:(h + 1) * GLA_DV]) for h in range(GLA_HEADS)], axis=1)
        for h, gh in enumerate(_gla_merge(o, og_ref[rows, :], gn)):
            m_ref[rows, ATT_Q_W + h * GLA_DV:ATT_Q_W + (h + 1) * GLA_DV] = gh.astype(BF16)

        kv = _dot_tn((gk * jnp.exp(g_last - g_cum)).astype(BF16), gv)
        new_state = decay_col_all[cols, :] * state_ref[b] + jnp.concatenate(
            [kv[h * GLA_DK:(h + 1) * GLA_DK, h * GLA_DV:(h + 1) * GLA_DV] for h in range(GLA_HEADS)],
            axis=0)
        state_ref[b] = new_state
        for h in range(GLA_HEADS):
            hrows = slice(h * GLA_DK, (h + 1) * GLA_DK)
            sbd_ref[b, hrows, h * GLA_DV:(h + 1) * GLA_DV] = new_state[hrows].astype(BF16)

    @pl.when(blk == N_BLOCKS - 1)
    def _():
        s_out_ref[0] = state_ref[...]
        k_out_ref[0] = k_ref[...].reshape(BATCH, BLOCK, ATT_KV_W)
        v_out_ref[0] = v_ref[...].reshape(BATCH, BLOCK, ATT_KV_W)


def _mix_prompt(layer, sinks, proj, vecs, consts, prev):
    tok = lambda w_: pl.BlockSpec((ROW_TILE, w_), lambda i, *_: (jnp.where(i == 0, N_TILES - 1, i - 1), 0))
    full = lambda a: pl.BlockSpec(a.shape, lambda i, *_: (0,) * a.ndim)
    per_batch = lambda r, c: pl.BlockSpec((1, BATCH, r, c), lambda i, *_: (layer, 0, 0, 0))
    ones = jnp.ones((BLOCK, LANES), BF16)
    const_args = (consts["tri"], consts["lev_all"], ones, consts["pair_level"])
    chained = prev is not None
    prev_args = tuple(prev) if chained else ()
    in_specs = ([pl.BlockSpec(memory_space=pl.ANY)] * len(prev_args)
                + [tok(a.shape[1]) for a in proj]
                + [pl.BlockSpec((V_ROWS, D_MODEL), lambda i, *_: (layer, 0))]
                + [full(a) for a in const_args])
    grid_spec = pltpu.PrefetchScalarGridSpec(
        num_scalar_prefetch=1,
        grid=(N_BLOCKS,),
        in_specs=in_specs,
        out_specs=[tok(D_MIX), per_batch(GLA_K_W, GLA_DV), per_batch(BLOCK, ATT_KV_W),
                   per_batch(BLOCK, ATT_KV_W)],
        scratch_shapes=[pltpu.VMEM((BATCH * N_KV_HEADS, BLOCK, ATT_KV_W), BF16),
                        pltpu.VMEM((BATCH * N_KV_HEADS, BLOCK, ATT_KV_W), BF16),
                        pltpu.VMEM((BATCH, GLA_K_W, GLA_DV), F32),
                        pltpu.VMEM((BATCH, GLA_K_W, GLA_V_W), BF16)],
    )
    return pl.pallas_call(
        functools.partial(_mix_prompt_kernel, layer=layer, chained=chained),
        grid_spec=grid_spec,
        out_shape=[jax.ShapeDtypeStruct((TOTAL_ROWS, D_MIX), BF16),
                   jax.ShapeDtypeStruct((DEPTH, BATCH, GLA_K_W, GLA_DV), F32),
                   jax.ShapeDtypeStruct((DEPTH, BATCH, BLOCK, ATT_KV_W), F32),
                   jax.ShapeDtypeStruct((DEPTH, BATCH, BLOCK, ATT_KV_W), F32)],
        input_output_aliases={1 + n: 1 + n for n in range(len(prev_args))},
        compiler_params=pltpu.CompilerParams(dimension_semantics=("arbitrary",),
                                             vmem_limit_bytes=VMEM_LIMIT),
        name="mix_prompt",
    )(sinks, *prev_args, *proj, vecs, *const_args)


def _mix_sample_kernel(*refs, layer, n_alias):
    sinks_ref, refs = refs[0], refs[n_alias + 1:]
    (pf_ref, pb_ref, ck_ref, cv_ref, st_ref,
     vec_ref, tri_ref, sones_ref, lev_ref, ones_ref, plev_ref,
     m_ref, ck_out_ref, cv_out_ref, st_out_ref) = refs
    q_ref, k_ref, v_ref, gq_ref, gk_ref, ld_ref, gv_ref, og_ref = _proj_views(pf_ref, pb_ref)
    gq = gq_ref[...]
    gk = gk_ref[...]
    gv = gv_ref[...]
    gvf = gv.astype(F32)
    hi, lo = _split(ld_ref[...])
    tri = tri_ref[...]
    g_cum = _dot(tri, hi) + _dot(tri, lo)
    sones = sones_ref[...]
    g_tot = _dot(sones, hi) + _dot(sones, lo)
    lev = lev_ref[...]
    decay_levels = jnp.exp(_dot(lev, hi) + _dot(lev, lo))
    a = _gla_intra(gq, gk, decay_levels, range(LOW_LEVELS), plev_ref[...])
    o_intra = jnp.concatenate(
        [_dot(a[h].astype(BF16), gv[:, h * GLA_DV:(h + 1) * GLA_DV]) for h in range(GLA_HEADS)], axis=1)
    q_dec = gq * jnp.exp(g_cum)
    k_dec = gk * jnp.exp(g_tot - g_cum)
    hm8 = _head_masks(DEC_SEQ)
    ones8 = ones_ref[...]
    hi_f, lo_f = hi.astype(F32), lo.astype(F32)

    qf = q_ref[...].astype(F32)
    kf = k_ref[...]
    vf = v_ref[...]
    low8 = lax.broadcasted_iota(jnp.int32, (DEC_SEQ, LANES), 1) < HEAD_DIM
    nkeys = WINDOW + DEC_SEQ
    srow = lax.broadcasted_iota(jnp.int32, (N_HEADS * DEC_SEQ, nkeys), 0)
    scol = lax.broadcasted_iota(jnp.int32, (N_HEADS * DEC_SEQ, nkeys), 1)
    t_of_row = srow & (DEC_SEQ - 1)
    amask = ((scol < WINDOW) & (scol > t_of_row)) | ((scol >= WINDOW) & (scol - WINDOW <= t_of_row))
    rid = lax.broadcasted_iota(jnp.int32, (N_HEADS * DEC_SEQ, 1), 0) >> LOW_LEVELS
    sink_col = jnp.zeros((N_HEADS * DEC_SEQ, 1), F32)
    for i in range(N_HEADS):
        sink_col = jnp.where(rid == i, sinks_ref[layer, i], sink_col)
    half = N_HEADS * DEC_SEQ // N_KV_HEADS

    att_rows, inter_rows = [], []
    for b in range(SEQ_GROUP):
        rows = slice(b * DEC_SEQ, (b + 1) * DEC_SEQ)
        pieces = []
        for j in range(N_HEADS // 2):
            qt = qf[rows, j * LANES:(j + 1) * LANES]
            pieces += [jnp.where(low8, qt, 0.0), jnp.where(low8, 0.0, qt)]
        qp = jnp.concatenate(pieces, axis=0).astype(BF16)
        kk = jnp.concatenate([ck_ref[0, b], kf[rows]], axis=0)
        vv = jnp.concatenate([cv_ref[0, b], vf[rows]], axis=0)
        kd = [x.astype(BF16) for x in _dup_halves(kk)]
        vd = [x.astype(BF16) for x in _dup_halves(vv)]
        s = jnp.concatenate([_dot_nt(qp[:half], kd[0]), _dot_nt(qp[half:], kd[1])], axis=0)
        s = jnp.where(amask, s, MASK_VALUE)
        m = jnp.maximum(jnp.max(s, axis=-1, keepdims=True), sink_col)
        p = jnp.exp(s - m)
        denom = jnp.sum(p, axis=-1, keepdims=True) + jnp.exp(sink_col - m)
        pb = p.astype(BF16)
        ob = jnp.concatenate([_dot(pb[:half], vd[0]), _dot(pb[half:], vd[1])], axis=0) / denom
        att_rows.append(jnp.concatenate(
            [jnp.where(low8, ob[(2 * j) * DEC_SEQ:(2 * j + 1) * DEC_SEQ],
                       ob[(2 * j + 1) * DEC_SEQ:(2 * j + 2) * DEC_SEQ])
             for j in range(N_HEADS // 2)], axis=1))
        ck_out_ref[0, b] = kk[DEC_SEQ:]
        cv_out_ref[0, b] = vv[DEC_SEQ:]

        state = st_ref[0, b]
        qb = q_dec[rows]
        qstack = jnp.concatenate([jnp.where(mk, qb, 0.0) for mk in hm8], axis=0).astype(BF16)
        oi = _dot(qstack, state.astype(BF16))
        inter_rows.append(jnp.concatenate(
            [oi[h * DEC_SEQ:(h + 1) * DEC_SEQ] for h in range(GLA_HEADS)], axis=1))
        kb = k_dec[rows]
        kstack = jnp.concatenate([jnp.where(mk, kb, 0.0) for mk in hm8], axis=0).astype(BF16)
        vb = gvf[rows]
        vstack = jnp.concatenate([vb[:, h * GLA_DV:(h + 1) * GLA_DV] for h in range(GLA_HEADS)],
                                 axis=0).astype(BF16)
        decay_col = jnp.exp(_dot_tn(hi_f[rows], ones8) + _dot_tn(lo_f[rows], ones8))
        st_out_ref[0, b] = decay_col * state + _dot_tn(kstack, vstack)

    m_ref[:, :ATT_Q_W] = jnp.concatenate(att_rows, axis=0).astype(BF16)
    o = o_intra + jnp.concatenate(inter_rows, axis=0)
    gn = vec_ref[V_GN:V_GN + 1, :GLA_DV]
    for h, gh in enumerate(_gla_merge(o, og_ref[...], gn)):
        m_ref[:, ATT_Q_W + h * GLA_DV:ATT_Q_W + (h + 1) * GLA_DV] = gh.astype(BF16)


def _mix_sample(layer, sinks, merged, proj, cache_k, cache_v, state, vecs, consts, prev):
    tok = lambda w_: pl.BlockSpec((BLOCK, w_), lambda i, *_: (SAMPLE_BLK0 + i, 0))
    full = lambda a: pl.BlockSpec(a.shape, lambda i, *_: (0,) * a.ndim)
    seq = lambda a: pl.BlockSpec((1, SEQ_GROUP) + a.shape[2:], lambda i, *_: (layer, i, 0, 0))
    ones8 = jnp.ones((DEC_SEQ, LANES), F32)
    const_args = (consts["seq_tri"], consts["seq_ones"], consts["lev_low"], ones8, consts["pair_level"])
    seq_args = (cache_k, cache_v, state)
    alias_args = (merged,) + (tuple(prev) if prev is not None else ())
    grid_spec = pltpu.PrefetchScalarGridSpec(
        num_scalar_prefetch=1,
        grid=(DEC_BATCH // SEQ_GROUP,),
        in_specs=([pl.BlockSpec(memory_space=pl.ANY)] * len(alias_args)
                  + [tok(a.shape[1]) for a in proj] + [seq(a) for a in seq_args]
                  + [pl.BlockSpec((V_ROWS, D_MODEL), lambda i, *_: (layer, 0))]
                  + [full(a) for a in const_args]),
        out_specs=[tok(D_MIX)] + [seq(a) for a in seq_args],
    )
    return pl.pallas_call(
        functools.partial(_mix_sample_kernel, layer=layer, n_alias=len(alias_args)),
        grid_spec=grid_spec,
        out_shape=[jax.ShapeDtypeStruct(merged.shape, merged.dtype)]
        + [jax.ShapeDtypeStruct(a.shape, a.dtype) for a in seq_args],
        input_output_aliases={1 + n: n for n in range(len(alias_args))},
        compiler_params=pltpu.CompilerParams(dimension_semantics=("arbitrary",),
                                             vmem_limit_bytes=VMEM_LIMIT),
        name="mix_sample",
    )(sinks, *alias_args, *proj, *seq_args, vecs, *const_args)


def _out_ffn_kernel(*refs, last):
    x_ref, m_ref, vec_ref, wo_ref, wg_ref, wu_ref, wd_ref = refs[:7]
    out_refs = refs[7:9] if last else refs[7:8]
    wo_s, wg_s, wu_s, wd_s, act_ref = refs[-5:]
    i = pl.program_id(0)

    @pl.when(i < N_PREP)
    def _():
        rows = pl.ds(pl.multiple_of(i * W_ROWS, W_ROWS), W_ROWS)
        wo_s[rows, :] = wo_ref[0].astype(BF16)
        wg_s[rows, :] = wg_ref[0].astype(BF16)
        wu_s[rows, :] = wu_ref[0].astype(BF16)
        wd_s[pl.ds(pl.multiple_of(i * WD_ROWS, WD_ROWS), WD_ROWS), :] = wd_ref[0].astype(BF16)

    @pl.when(i >= N_PREP)
    def _():
        x1 = x_ref[...] + _dot(m_ref[...], wo_s[...])
        ms = jnp.mean(x1 * x1, axis=-1, keepdims=True)
        h = (x1 * lax.rsqrt(ms + RMS_EPS) * vec_ref[V_NORM2:V_NORM2 + 1, :]).astype(BF16)
        for c in range(D_FFN // FFN_CHUNK):
            cols = slice(c * FFN_CHUNK, (c + 1) * FFN_CHUNK)
            gate = _dot(h, wg_s[:, cols])
            up = _dot(h, wu_s[:, cols])
            act_ref[:, cols] = (gate * _sigmoid(gate) * up).astype(BF16)
        y = x1 + _dot(act_ref[...], wd_s[...])
        if last:
            t = i - N_PREP

            @pl.when(t < MAIN_TILES)
            def _():
                out_refs[0][...] = y.reshape(BATCH, BLOCK, D_MODEL)

            @pl.when(t >= MAIN_TILES)
            def _():
                out_refs[1][...] = y
        else:
            out_refs[0][...] = y


def _out_ffn(layer, x_all, merged, vecs, w_o, w_gate, w_up, w_down, last):
    tile = lambda w_: pl.BlockSpec((ROW_TILE, w_), lambda i: (_tile_of_step(i), 0))
    chunk = lambda r, c: pl.BlockSpec((1, r, c), lambda i: (layer, jnp.minimum(i, N_PREP - 1), 0))
    if last:
        n_tiles = MAIN_TILES + SAMPLE_TILES
        out_specs = [pl.BlockSpec((BATCH, BLOCK, D_MODEL),
                                  lambda i: (0, jnp.minimum(_tile_of_step(i), MAIN_TILES - 1), 0)),
                     pl.BlockSpec((ROW_TILE, D_MODEL),
                                  lambda i: (jnp.maximum(_tile_of_step(i) - MAIN_TILES, 0), 0))]
        out_shape = [jax.ShapeDtypeStruct((BATCH, SEQ, D_MODEL), F32),
                     jax.ShapeDtypeStruct((SAMPLE_ROWS, D_MODEL), F32)]
    else:
        n_tiles = N_TILES
        out_specs = [tile(D_MODEL)]
        out_shape = [jax.ShapeDtypeStruct((TOTAL_ROWS, D_MODEL), F32)]
    return pl.pallas_call(
        functools.partial(_out_ffn_kernel, last=last),
        grid=(N_PREP + n_tiles,),
        in_specs=[tile(D_MODEL), tile(D_MIX), pl.BlockSpec((V_ROWS, D_MODEL), lambda i: (layer, 0)),
                  chunk(W_ROWS, D_MODEL), chunk(W_ROWS, D_FFN), chunk(W_ROWS, D_FFN),
                  chunk(WD_ROWS, D_MODEL)],
        out_specs=out_specs,
        out_shape=out_shape,
        scratch_shapes=[pltpu.VMEM((D_MIX, D_MODEL), BF16), pltpu.VMEM((D_MODEL, D_FFN), BF16),
                        pltpu.VMEM((D_MODEL, D_FFN), BF16), pltpu.VMEM((D_FFN, D_MODEL), BF16),
                        pltpu.VMEM((ROW_TILE, D_FFN), BF16)],
        compiler_params=pltpu.CompilerParams(dimension_semantics=("arbitrary",),
                                             vmem_limit_bytes=VMEM_LIMIT),
        name="out_ffn",
    )(x_all, merged, vecs, w_o, w_gate, w_up, w_down)


def _vector_slab(norm1, norm2, q_norm, k_norm, b_g, gla_norm):
    pad = lambda a: jnp.pad(a.astype(F32), ((0, 0), (0, D_MODEL - a.shape[1])))
    rows = [norm1.astype(F32), norm2.astype(F32), pad(jnp.tile(q_norm, (1, N_HEADS)) * ATT_SCALE),
            pad(jnp.tile(k_norm, (1, N_KV_HEADS))), pad(b_g), pad(gla_norm)]
    rows += [jnp.zeros((DEPTH, D_MODEL), F32)] * (V_ROWS - len(rows))
    return jnp.stack(rows, axis=1).reshape(DEPTH * V_ROWS, D_MODEL)


def kernel(x_prompt, x_sample, cache_k, cache_v, state_gla, meta, norm1, w_in, q_norm, k_norm, sinks,
           w_g2, b_g, gla_norm, w_o, norm2, w_gate, w_up, w_down):
    consts = _constants()
    dt = x_prompt.dtype
    vecs = _vector_slab(norm1, norm2, q_norm, k_norm, b_g, gla_norm)
    sinks = sinks.astype(F32)
    lead = jnp.tile(jnp.concatenate([jnp.zeros((PAD_LEN, D_MODEL), dt), meta.astype(dt)], axis=0), (BATCH, 1))
    x_srcs = (x_prompt, x_sample.reshape(SAMPLE_ROWS, D_MODEL), lead)
    ck_in = cache_k.reshape(DEPTH, DEC_BATCH, WINDOW, ATT_KV_W)
    cv_in = cache_v.reshape(DEPTH, DEC_BATCH, WINDOW, ATT_KV_W)
    st_in = state_gla.reshape(DEPTH, DEC_BATCH, GLA_K_W, GLA_DV)

    prompt_outs, sample_outs = None, None
    for l in range(DEPTH):
        x_all, proj = _inproj(l, x_srcs, vecs, w_in, w_g2, consts)
        merged, *prompt_outs = _mix_prompt(l, sinks, proj, vecs, consts, prompt_outs)
        merged, *sample_outs = _mix_sample(l, sinks, merged, proj, ck_in, cv_in, st_in, vecs, consts,
                                           sample_outs)
        x_srcs = tuple(_out_ffn(l, x_all, merged, vecs, w_o, w_gate, w_up, w_down, last=l == DEPTH - 1))

    y_main, y_sample = x_srcs
    ps, pk, pv = prompt_outs
    sk, sv, ss = sample_outs
    kv5 = lambda a, n: a.reshape(DEPTH, n, WINDOW, N_KV_HEADS, HEAD_DIM)
    st5 = lambda a, n: a.reshape(DEPTH, n, GLA_HEADS, GLA_DK, GLA_DV)
    return (y_main, y_sample.reshape(DEC_BATCH, DEC_SEQ, D_MODEL),
            kv5(pk, BATCH), kv5(pv, BATCH), st5(ps, BATCH), kv5(sk, DEC_BATCH), kv5(sv, DEC_BATCH),
            st5(ss, DEC_BATCH))
```

```python
import functools

import jax
import jax.numpy as jnp
import numpy as np
from jax import lax
from jax.experimental import pallas as pl
from jax.experimental.pallas import tpu as pltpu

F32 = jnp.float32
BF16 = jnp.bfloat16

D_MODEL = 1024
BATCH = 4
SEQ = 4096
DEPTH = 2
DEC_BATCH = 128
DEC_SEQ = 8
N_META = 16
WINDOW = 128
BLOCK = 128
PAD_LEN = BLOCK - N_META
N_HEADS = 8
N_KV_HEADS = 2
HEAD_DIM = 64
ATT_SCALE = HEAD_DIM ** -0.5
ATT_Q_W = N_HEADS * HEAD_DIM
ATT_KV_W = N_KV_HEADS * HEAD_DIM
GLA_HEADS = 4
GLA_DK = 64
GLA_DV = 128
GLA_K_W = GLA_HEADS * GLA_DK
GLA_V_W = GLA_HEADS * GLA_DV
GATE_RANK = 16
GATE_NORMALIZER = 16.0
D_MIX = ATT_Q_W + GLA_V_W
D_FFN = 2816
IN_W = 2320
RMS_EPS = 1e-6
MASK_VALUE = -1e30

LANES = 128
N_BLOCKS = 1 + SEQ // BLOCK
MAIN_ROWS = BATCH * SEQ
SAMPLE_ROWS = DEC_BATCH * DEC_SEQ
LEAD_ROWS = BATCH * BLOCK
TOTAL_ROWS = MAIN_ROWS + SAMPLE_ROWS + LEAD_ROWS
ROW_TILE = 512
MAIN_TILES = MAIN_ROWS // ROW_TILE
SAMPLE_TILES = SAMPLE_ROWS // ROW_TILE
N_TILES = TOTAL_ROWS // ROW_TILE
LEAD_TILE = N_TILES - 1
SEQ_GROUP = BLOCK // DEC_SEQ
N_LEVELS = 7
LOW_LEVELS = 3
LOG_DK = 6
N_PREP = 8
W_ROWS = D_MODEL // N_PREP
WD_ROWS = D_FFN // N_PREP

SRC_LOW, SRC_OG = 1792, 1808
OFF_Q, OFF_K, OFF_V = 0, 512, 640
OFF_GQ, OFF_GK, OFF_GV, OFF_OG, OFF_LOW = 768, 1024, 1280, 1792, 2304
PROJ_W = OFF_LOW + LANES
PROJ_CHUNK = 512
FFN_CHUNK = 256
VMEM_LIMIT = 56 * 1024 * 1024

V_NORM1, V_NORM2, V_QG, V_KG, V_BG, V_GN, V_ROWS = 0, 1, 2, 3, 4, 5, 8

PF_K, PF_V, PF_GQ, PF_GK, PF_LD, PF_OG, PF_W = 0, 128, 256, 512, 768, 1024, 1536
PB_Q, PB_GV, PB_W = 0, 512, 1024


def _proj_views(pf_ref, pb_ref):
    f = lambda a, b: pf_ref.at[:, a:b]
    return (pb_ref.at[:, PB_Q:PB_GV], f(PF_K, PF_V), f(PF_V, PF_GQ), f(PF_GQ, PF_GK), f(PF_GK, PF_LD),
            f(PF_LD, PF_OG), pb_ref.at[:, PB_GV:PB_W], f(PF_OG, PF_W))


def _dot(a, b):
    return jnp.dot(a, b, preferred_element_type=F32)


def _dot_nt(a, b):
    return lax.dot_general(a, b, (((1,), (1,)), ((), ())), preferred_element_type=F32)


def _dot_tn(a, b):
    return lax.dot_general(a, b, (((0,), (0,)), ((), ())), preferred_element_type=F32)


def _split(x):
    hi = x.astype(BF16)
    lo = (x - hi.astype(F32)).astype(BF16)
    return hi, lo


def _sigmoid(x):
    return 1.0 / (1.0 + jnp.exp(-x))


def _level_matrix(levels, n=BLOCK):
    out = np.zeros((len(levels) * n, n), np.float32)
    for i, l in enumerate(levels):
        size = 2 << l
        for t in range(n):
            mid = (t // size) * size + size // 2 - 1
            if (t >> l) & 1:
                out[i * n + t, mid + 1:t + 1] = 1.0
            else:
                out[i * n + t, t + 1:mid + 1] = 1.0
    return out


def _constants():
    r = np.arange(BLOCK)
    tri = (r[None, :] <= r[:, None]).astype(np.float32)
    same_seq = (r[None, :] // DEC_SEQ) == (r[:, None] // DEC_SEQ)
    diff = np.maximum(r[:, None] ^ r[None, :], 1)
    pair_level = np.where(r[None, :] < r[:, None], np.floor(np.log2(diff)).astype(np.int32),
                          np.where(r[None, :] == r[:, None], N_LEVELS, N_LEVELS + 1)).astype(np.int32)
    c = np.arange(2 * LANES)
    bd256 = ((c[None, :] // HEAD_DIM) == (c[:, None] // HEAD_DIM)).astype(np.float32)
    return dict(
        tri=jnp.asarray(tri, BF16),
        seq_tri=jnp.asarray(tri * same_seq, BF16),
        seq_ones=jnp.asarray(same_seq.astype(np.float32), BF16),
        lev_all=jnp.asarray(_level_matrix(range(N_LEVELS)), BF16),
        lev_low=jnp.asarray(_level_matrix(range(LOW_LEVELS)), BF16),
        bd256=jnp.asarray(bd256, BF16),
        bd128=jnp.asarray(bd256[:LANES, :LANES], BF16),
        pair_level=jnp.asarray(pair_level),
    )


def _head_masks(rows):
    lane = lax.broadcasted_iota(jnp.int32, (rows, GLA_K_W), 1)
    return [(lane >> LOG_DK) == h for h in range(GLA_HEADS)]


def _gla_intra(gq, gk, decay_levels, levels, pair_level):
    rows = gq.shape[0]
    hm = _head_masks(rows)

    def pair_products(qh, kh):
        kh = kh.astype(BF16)
        stacked = jnp.concatenate([jnp.where(m, kh, jnp.zeros_like(kh)) for m in hm], axis=0)
        p = _dot_nt(qh.astype(BF16), stacked)
        return [p[:, h * rows:(h + 1) * rows] for h in range(GLA_HEADS)]

    on_diag = pair_level == N_LEVELS
    a = [jnp.where(on_diag, p, 0.0) for p in pair_products(gq, gk)]
    for i, l in enumerate(levels):
        e = decay_levels[i * rows:(i + 1) * rows]
        at_level = pair_level == l
        a = [jnp.where(at_level, p, a_h) for p, a_h in zip(pair_products(gq * e, gk * e), a)]
    return a


def _gla_merge(o, og, gn):
    outs = []
    for h in range(GLA_HEADS):
        oh = o[:, h * GLA_DV:(h + 1) * GLA_DV]
        gh = og[:, h * GLA_DV:(h + 1) * GLA_DV]
        ms = jnp.mean(oh * oh, axis=-1, keepdims=True)
        outs.append(oh * lax.rsqrt(ms + RMS_EPS) * gn * (gh * _sigmoid(gh)))
    return outs


def _dup_halves(x):
    low = lax.broadcasted_iota(jnp.int32, x.shape, 1) < HEAD_DIM
    rolled = pltpu.roll(x, HEAD_DIM, axis=1)
    return jnp.where(low, x, rolled), jnp.where(low, rolled, x)


def _convert_proj_weights(i, w_ref, wg2_ref, wbf_ref, wg2s_ref):
    rows = pl.ds(pl.multiple_of(i * W_ROWS, W_ROWS), W_ROWS)
    chunk = w_ref[0]
    wbf_ref[rows, 0:OFF_OG] = chunk[:, 0:SRC_LOW].astype(BF16)
    wbf_ref[rows, OFF_OG:OFF_LOW] = chunk[:, SRC_OG:IN_W].astype(BF16)
    lane = lax.broadcasted_iota(jnp.int32, (W_ROWS, LANES), 1)
    low = jnp.where(lane < GATE_RANK, chunk[:, SRC_LOW:SRC_LOW + LANES], 0.0)
    wbf_ref[rows, OFF_LOW:PROJ_W] = low.astype(BF16)

    @pl.when(i == 0)
    def _():
        wg2s_ref[...] = jnp.concatenate(
            [wg2_ref[0], jnp.zeros((LANES - GATE_RANK, GLA_K_W), F32)], axis=0).astype(BF16)


def _project_tile(x, vec_ref, wbf_ref, wg2s_ref, bd256_ref, bd128_ref, z_ref, pf_ref, pb_ref):
    q_ref, k_ref, v_ref, gq_ref, gk_ref, ld_ref, gv_ref, og_ref = _proj_views(pf_ref, pb_ref)
    ms = jnp.mean(x * x, axis=-1, keepdims=True)
    h = (x * lax.rsqrt(ms + RMS_EPS) * vec_ref[V_NORM1:V_NORM1 + 1, :]).astype(BF16)

    def q_part():
        q = z_ref[:, OFF_Q:OFF_K]
        q2 = (q * q).astype(BF16)
        bd = bd256_ref[...]
        ssq = jnp.concatenate([_dot(q2[:, :256], bd), _dot(q2[:, 256:], bd)], axis=1)
        q_ref[...] = (q * lax.rsqrt(ssq * (1.0 / HEAD_DIM) + RMS_EPS)
                      * vec_ref[V_QG:V_QG + 1, :ATT_Q_W]).astype(BF16)

    def kv_part():
        k = z_ref[:, OFF_K:OFF_V]
        ssk = _dot((k * k).astype(BF16), bd128_ref[...])
        k_ref[...] = k * lax.rsqrt(ssk * (1.0 / HEAD_DIM) + RMS_EPS) * vec_ref[V_KG:V_KG + 1, :ATT_KV_W]
        v_ref[...] = z_ref[:, OFF_V:OFF_GQ]
        gq_ref[...] = z_ref[:, OFF_GQ:OFF_GK] * (GLA_DK ** -0.5)

    def gk_part():
        gk_ref[...] = z_ref[:, OFF_GK:OFF_GV]

    def gv_part():
        gv_ref[...] = z_ref[:, OFF_GV:OFF_OG].astype(BF16)

    def tail_part():
        og_ref[...] = z_ref[:, OFF_OG:OFF_LOW]
        logit = (_dot(z_ref[:, OFF_LOW:PROJ_W].astype(BF16), wg2s_ref[...])
                 + vec_ref[V_BG:V_BG + 1, :GLA_K_W])
        log_sig = jnp.minimum(logit, 0.0) - jnp.log1p(jnp.exp(-jnp.abs(logit)))
        ld_ref[...] = log_sig * (1.0 / GATE_NORMALIZER)

    pieces = [q_part, kv_part, gk_part, gv_part, tail_part]
    for c, piece in enumerate(pieces):
        cols = slice(c * PROJ_CHUNK, min((c + 1) * PROJ_CHUNK, PROJ_W))
        z_ref[:, cols] = _dot(h, wbf_ref[:, cols])
        piece()


def _mix_prompt_tile(blk, layer, sinks_ref, pf_ref, pb_ref, vec_ref, tri_ref, lev_ref, ones_ref, plev_ref,
                     m_ref, kprev_ref, vprev_ref, state_ref, sbd_ref):
    q_ref, k_ref, v_ref, gq_ref, gk_ref, ld_ref, gv_ref, og_ref = _proj_views(pf_ref, pb_ref)

    row = lax.broadcasted_iota(jnp.int32, (BLOCK, BLOCK), 0)
    col = lax.broadcasted_iota(jnp.int32, (BLOCK, BLOCK), 1)
    own = col <= row
    kpos = jnp.where(own, blk * BLOCK, (blk - 1) * BLOCK) + col - PAD_LEN
    live = kpos >= 0
    low_half = lax.broadcasted_iota(jnp.int32, (BLOCK, LANES), 1) < HEAD_DIM

    def attend(lhs, kk, vv, sink):
        s = _dot_nt(lhs, kk)
        s = jnp.where(live, jnp.where(own, s[:, BLOCK:], s[:, :BLOCK]), MASK_VALUE)
        m = jnp.maximum(jnp.max(s, axis=-1, keepdims=True), sink)
        p = jnp.exp(s - m)
        denom = jnp.sum(p, axis=-1, keepdims=True) + jnp.exp(sink - m)
        p2 = jnp.concatenate([jnp.where(own, 0.0, p), jnp.where(own, p, 0.0)], axis=1).astype(BF16)
        return _dot(p2, vv) / denom

    for b in range(BATCH):
        rows = slice(b * BLOCK, (b + 1) * BLOCK)
        kdup = [x.astype(BF16) for x in _dup_halves(k_ref[rows, :])]
        vdup = [x.astype(BF16) for x in _dup_halves(v_ref[rows, :])]
        for j in range(N_HEADS // 2):
            g = b * N_KV_HEADS + (2 * j) // (N_HEADS // N_KV_HEADS)
            kk = jnp.concatenate([kprev_ref[g], kdup[g % N_KV_HEADS]], axis=0)
            vv = jnp.concatenate([vprev_ref[g], vdup[g % N_KV_HEADS]], axis=0)
            qt = q_ref[rows, j * LANES:(j + 1) * LANES]
            zero = jnp.zeros_like(qt)
            o_lo = attend(jnp.where(low_half, qt, zero), kk, vv, sinks_ref[layer, 2 * j])
            o_hi = attend(jnp.where(low_half, zero, qt), kk, vv, sinks_ref[layer, 2 * j + 1])
            m_ref[rows, j * LANES:(j + 1) * LANES] = jnp.where(low_half, o_lo, o_hi).astype(BF16)
        for g in range(N_KV_HEADS):
            kprev_ref[b * N_KV_HEADS + g] = kdup[g]
            vprev_ref[b * N_KV_HEADS + g] = vdup[g]

    rpos = blk * BLOCK + lax.broadcasted_iota(jnp.int32, (BLOCK, 1), 0) - PAD_LEN
    valid = (rpos >= 0).astype(F32)
    ld_all = jnp.concatenate([ld_ref[b * BLOCK:(b + 1) * BLOCK, :] for b in range(BATCH)], axis=1)
    hi, lo = _split(ld_all)
    tri = tri_ref[...]
    g_cum_all = _dot(tri, hi) + _dot(tri, lo)
    lev = lev_ref[...]
    decay_all = jnp.exp(_dot(lev, hi) + _dot(lev, lo))
    ones = ones_ref[...]
    decay_col_all = jnp.exp(_dot_tn(hi, ones) + _dot_tn(lo, ones))
    pair_level = plev_ref[...]
    gn = vec_ref[V_GN:V_GN + 1, :GLA_DV]
    for b in range(BATCH):
        rows = slice(b * BLOCK, (b + 1) * BLOCK)
        cols = slice(b * GLA_K_W, (b + 1) * GLA_K_W)
        gq = gq_ref[rows, :]
        gk = gk_ref[rows, :] * valid
        gv = gv_ref[rows, :]
        g_cum = g_cum_all[:, cols]
        g_last = g_cum[BLOCK - 1:BLOCK, :]
        a = _gla_intra(gq, gk, decay_all[:, cols], range(N_LEVELS), pair_level)

        o = _dot((gq * jnp.exp(g_cum)).astype(BF16), sbd_ref[b])
        o = o + jnp.concatenate(
            [_dot(a[h].astype(BF16), gv[:, h * GLA_DV:(h + 1) * GLA_DV]) for h in range(GLA_HEADS)], axis=1)
        for h, gh in enumerate(_gla_merge(o, og_ref[rows, :], gn)):
            m_ref[rows, ATT_Q_W + h * GLA_DV:ATT_Q_W + (h + 1) * GLA_DV] = gh.astype(BF16)

        kv = _dot_tn((gk * jnp.exp(g_last - g_cum)).astype(BF16), gv)
        new_state = decay_col_all[cols, :] * state_ref[b] + jnp.concatenate(
            [kv[h * GLA_DK:(h + 1) * GLA_DK, h * GLA_DV:(h + 1) * GLA_DV] for h in range(GLA_HEADS)],
            axis=0)
        state_ref[b] = new_state
        for h in range(GLA_HEADS):
            hrows = slice(h * GLA_DK, (h + 1) * GLA_DK)
            sbd_ref[b, hrows, h * GLA_DV:(h + 1) * GLA_DV] = new_state[hrows].astype(BF16)


def _front_kernel(*refs, layer, n_src, chained):
    sinks_ref, refs = refs[0], refs[1:]
    if chained:
        refs = refs[3:]
    x_refs, refs = refs[:n_src], refs[n_src:]
    vec_ref, w_ref, wg2_ref, bd256_ref, bd128_ref, tri_ref, lev_ref, ones_ref, plev_ref = refs[:9]
    refs = refs[9:]
    if n_src > 1:
        xo_ref, refs = refs[0], refs[1:]
    (m_ref, pfs_ref, pbs_ref, s_out_ref, k_out_ref, v_out_ref,
     wbf_ref, wg2s_ref, z_ref, pf_ref, pb_ref, kprev_ref, vprev_ref, state_ref, sbd_ref) = refs
    i = pl.program_id(0)
    t = i - N_PREP

    @pl.when(i < N_PREP)
    def _():
        _convert_proj_weights(i, w_ref, wg2_ref, wbf_ref, wg2s_ref)

    @pl.when(i == 0)
    def _():
        kprev_ref[...] = jnp.zeros_like(kprev_ref)
        vprev_ref[...] = jnp.zeros_like(vprev_ref)
        state_ref[...] = jnp.zeros_like(state_ref)
        sbd_ref[...] = jnp.zeros_like(sbd_ref)

    def load_x():
        if n_src == 1:
            return x_refs[0][...]
        x = jnp.where(t == 0, x_refs[2][...],
                      jnp.where(t < N_BLOCKS, x_refs[0][...].reshape(ROW_TILE, D_MODEL), x_refs[1][...]))
        xo_ref[...] = x
        return x

    @pl.when((t >= 0) & (t < N_BLOCKS))
    def _():
        _project_tile(load_x(), vec_ref, wbf_ref, wg2s_ref, bd256_ref, bd128_ref, z_ref, pf_ref, pb_ref)
        _mix_prompt_tile(t, layer, sinks_ref, pf_ref, pb_ref, vec_ref, tri_ref, lev_ref, ones_ref, plev_ref,
                         m_ref, kprev_ref, vprev_ref, state_ref, sbd_ref)

    @pl.when(t == N_BLOCKS - 1)
    def _():
        s_out_ref[0] = state_ref[...]
        k_out_ref[0] = pf_ref[:, PF_K:PF_V].reshape(BATCH, BLOCK, ATT_KV_W)
        v_out_ref[0] = pf_ref[:, PF_V:PF_GQ].reshape(BATCH, BLOCK, ATT_KV_W)

    @pl.when(t >= N_BLOCKS)
    def _():
        _project_tile(load_x(), vec_ref, wbf_ref, wg2s_ref, bd256_ref, bd128_ref, z_ref, pfs_ref, pbs_ref)


def _front(layer, sinks, x_srcs, vecs, w_in, w_g2, consts, prev):
    n_src = len(x_srcs)
    step = lambda i: i - N_PREP
    tile_of = lambda i: jnp.where(step(i) <= 0, LEAD_TILE, jnp.minimum(step(i) - 1, LEAD_TILE - 1))
    prompt_tile_of = lambda i: jnp.where(step(i) <= 0, LEAD_TILE, jnp.minimum(step(i) - 1, MAIN_TILES - 1))
    sample_tile_of = lambda i: jnp.clip(step(i) - N_BLOCKS, 0, SAMPLE_TILES - 1)
    full = lambda a: pl.BlockSpec(a.shape, lambda i, *_: (0,) * a.ndim)
    per_batch = lambda r, c: pl.BlockSpec((1, BATCH, r, c), lambda i, *_: (layer, 0, 0, 0))
    if n_src == 1:
        x_specs = [pl.BlockSpec((ROW_TILE, D_MODEL), lambda i, *_: (tile_of(i), 0))]
    else:
        x_specs = [
            pl.BlockSpec((BATCH, BLOCK, D_MODEL), lambda i, *_: (0, jnp.clip(step(i) - 1, 0, MAIN_TILES - 1), 0)),
            pl.BlockSpec((ROW_TILE, D_MODEL), lambda i, *_: (sample_tile_of(i), 0)),
            pl.BlockSpec((ROW_TILE, D_MODEL), lambda i, *_: (0, 0)),
        ]
    ones = jnp.ones((BLOCK, LANES), BF16)
    const_args = (consts["bd256"], consts["bd128"], consts["tri"], consts["lev_all"], ones, consts["pair_level"])
    chained = prev is not None
    prev_args = tuple(prev) if chained else ()
    in_specs = ([pl.BlockSpec(memory_space=pl.ANY)] * len(prev_args) + x_specs + [
        pl.BlockSpec((V_ROWS, D_MODEL), lambda i, *_: (layer, 0)),
        pl.BlockSpec((1, W_ROWS, IN_W), lambda i, *_: (layer, jnp.minimum(i, N_PREP - 1), 0)),
        pl.BlockSpec((1, GATE_RANK, GLA_K_W), lambda i, *_: (layer, 0, 0)),
    ] + [full(a) for a in const_args])
    out_specs = [pl.BlockSpec((ROW_TILE, D_MIX), lambda i, *_: (prompt_tile_of(i), 0)),
                 pl.BlockSpec((ROW_TILE, PF_W), lambda i, *_: (sample_tile_of(i), 0)),
                 pl.BlockSpec((ROW_TILE, PB_W), lambda i, *_: (sample_tile_of(i), 0)),
                 per_batch(GLA_K_W, GLA_DV), per_batch(BLOCK, ATT_KV_W), per_batch(BLOCK, ATT_KV_W)]
    out_shape = [jax.ShapeDtypeStruct((TOTAL_ROWS, D_MIX), BF16),
                 jax.ShapeDtypeStruct((SAMPLE_ROWS, PF_W), F32),
                 jax.ShapeDtypeStruct((SAMPLE_ROWS, PB_W), BF16),
                 jax.ShapeDtypeStruct((DEPTH, BATCH, GLA_K_W, GLA_DV), F32),
                 jax.ShapeDtypeStruct((DEPTH, BATCH, BLOCK, ATT_KV_W), F32),
                 jax.ShapeDtypeStruct((DEPTH, BATCH, BLOCK, ATT_KV_W), F32)]
    n_lead_out = 0
    if n_src > 1:
        out_specs = [pl.BlockSpec((ROW_TILE, D_MODEL), lambda i, *_: (tile_of(i), 0))] + out_specs
        out_shape = [jax.ShapeDtypeStruct((TOTAL_ROWS, D_MODEL), F32)] + out_shape
        n_lead_out = 1
    grid_spec = pltpu.PrefetchScalarGridSpec(
        num_scalar_prefetch=1,
        grid=(N_PREP + N_BLOCKS + SAMPLE_TILES,),
        in_specs=in_specs,
        out_specs=out_specs,
        scratch_shapes=[pltpu.VMEM((D_MODEL, PROJ_W), BF16), pltpu.VMEM((LANES, GLA_K_W), BF16),
                        pltpu.VMEM((ROW_TILE, PROJ_W), F32),
                        pltpu.VMEM((ROW_TILE, PF_W), F32), pltpu.VMEM((ROW_TILE, PB_W), BF16),
                        pltpu.VMEM((BATCH * N_KV_HEADS, BLOCK, ATT_KV_W), BF16),
                        pltpu.VMEM((BATCH * N_KV_HEADS, BLOCK, ATT_KV_W), BF16),
                        pltpu.VMEM((BATCH, GLA_K_W, GLA_DV), F32),
                        pltpu.VMEM((BATCH, GLA_K_W, GLA_V_W), BF16)],
    )
    res = pl.pallas_call(
        functools.partial(_front_kernel, layer=layer, n_src=n_src, chained=chained),
        grid_spec=grid_spec,
        out_shape=out_shape,
        input_output_aliases={1 + n: n_lead_out + 3 + n for n in range(len(prev_args))},
        compiler_params=pltpu.CompilerParams(dimension_semantics=("arbitrary",),
                                             vmem_limit_bytes=VMEM_LIMIT),
        name="front",
    )(sinks, *prev_args, *x_srcs, vecs, w_in, w_g2, *const_args)
    x_all = res[0] if n_src > 1 else x_srcs[0]
    merged, pfs, pbs, ps, pk, pv = res[n_lead_out:]
    return x_all, merged, (pfs, pbs), (ps, pk, pv)


def _mix_sample_kernel(*refs, layer, n_alias):
    sinks_ref, refs = refs[0], refs[n_alias + 1:]
    (pf_ref, pb_ref, ck_ref, cv_ref, st_ref,
     vec_ref, tri_ref, sones_ref, lev_ref, ones_ref, plev_ref,
     m_ref, ck_out_ref, cv_out_ref, st_out_ref) = refs
    q_ref, k_ref, v_ref, gq_ref, gk_ref, ld_ref, gv_ref, og_ref = _proj_views(pf_ref, pb_ref)
    gq = gq_ref[...]
    gk = gk_ref[...]
    gv = gv_ref[...]
    gvf = gv.astype(F32)
    hi, lo = _split(ld_ref[...])
    tri = tri_ref[...]
    g_cum = _dot(tri, hi) + _dot(tri, lo)
    sones = sones_ref[...]
    g_tot = _dot(sones, hi) + _dot(sones, lo)
    lev = lev_ref[...]
    decay_levels = jnp.exp(_dot(lev, hi) + _dot(lev, lo))
    a = _gla_intra(gq, gk, decay_levels, range(LOW_LEVELS), plev_ref[...])
    o_intra = jnp.concatenate(
        [_dot(a[h].astype(BF16), gv[:, h * GLA_DV:(h + 1) * GLA_DV]) for h in range(GLA_HEADS)], axis=1)
    q_dec = gq * jnp.exp(g_cum)
    k_dec = gk * jnp.exp(g_tot - g_cum)
    hm8 = _head_masks(DEC_SEQ)
    ones8 = ones_ref[...]
    hi_f, lo_f = hi.astype(F32), lo.astype(F32)

    qf = q_ref[...].astype(F32)
    kf = k_ref[...]
    vf = v_ref[...]
    low8 = lax.broadcasted_iota(jnp.int32, (DEC_SEQ, LANES), 1) < HEAD_DIM
    nkeys = WINDOW + DEC_SEQ
    srow = lax.broadcasted_iota(jnp.int32, (N_HEADS * DEC_SEQ, nkeys), 0)
    scol = lax.broadcasted_iota(jnp.int32, (N_HEADS * DEC_SEQ, nkeys), 1)
    t_of_row = srow & (DEC_SEQ - 1)
    amask = ((scol < WINDOW) & (scol > t_of_row)) | ((scol >= WINDOW) & (scol - WINDOW <= t_of_row))
    rid = lax.broadcasted_iota(jnp.int32, (N_HEADS * DEC_SEQ, 1), 0) >> LOW_LEVELS
    sink_col = jnp.zeros((N_HEADS * DEC_SEQ, 1), F32)
    for i in range(N_HEADS):
        sink_col = jnp.where(rid == i, sinks_ref[layer, i], sink_col)
    half = N_HEADS * DEC_SEQ // N_KV_HEADS

    att_rows, inter_rows = [], []
    for b in range(SEQ_GROUP):
        rows = slice(b * DEC_SEQ, (b + 1) * DEC_SEQ)
        pieces = []
        for j in range(N_HEADS // 2):
            qt = qf[rows, j * LANES:(j + 1) * LANES]
            pieces += [jnp.where(low8, qt, 0.0), jnp.where(low8, 0.0, qt)]
        qp = jnp.concatenate(pieces, axis=0).astype(BF16)
        kk = jnp.concatenate([ck_ref[0, b], kf[rows]], axis=0)
        vv = jnp.concatenate([cv_ref[0, b], vf[rows]], axis=0)
        kd = [x.astype(BF16) for x in _dup_halves(kk)]
        vd = [x.astype(BF16) for x in _dup_halves(vv)]
        s = jnp.concatenate([_dot_nt(qp[:half], kd[0]), _dot_nt(qp[half:], kd[1])], axis=0)
        s = jnp.where(amask, s, MASK_VALUE)
        m = jnp.maximum(jnp.max(s, axis=-1, keepdims=True), sink_col)
        p = jnp.exp(s - m)
        denom = jnp.sum(p, axis=-1, keepdims=True) + jnp.exp(sink_col - m)
        pb = p.astype(BF16)
        ob = jnp.concatenate([_dot(pb[:half], vd[0]), _dot(pb[half:], vd[1])], axis=0) / denom
        att_rows.append(jnp.concatenate(
            [jnp.where(low8, ob[(2 * j) * DEC_SEQ:(2 * j + 1) * DEC_SEQ],
                       ob[(2 * j + 1) * DEC_SEQ:(2 * j + 2) * DEC_SEQ])
             for j in range(N_HEADS // 2)], axis=1))
        ck_out_ref[0, b] = kk[DEC_SEQ:]
        cv_out_ref[0, b] = vv[DEC_SEQ:]

        state = st_ref[0, b]
        qb = q_dec[rows]
        qstack = jnp.concatenate([jnp.where(mk, qb, 0.0) for mk in hm8], axis=0).astype(BF16)
        oi = _dot(qstack, state.astype(BF16))
        inter_rows.append(jnp.concatenate(
            [oi[h * DEC_SEQ:(h + 1) * DEC_SEQ] for h in range(GLA_HEADS)], axis=1))
        kb = k_dec[rows]
        kstack = jnp.concatenate([jnp.where(mk, kb, 0.0) for mk in hm8], axis=0).astype(BF16)
        vb = gvf[rows]
        vstack = jnp.concatenate([vb[:, h * GLA_DV:(h + 1) * GLA_DV] for h in range(GLA_HEADS)],
                                 axis=0).astype(BF16)
        decay_col = jnp.exp(_dot_tn(hi_f[rows], ones8) + _dot_tn(lo_f[rows], ones8))
        st_out_ref[0, b] = decay_col * state + _dot_tn(kstack, vstack)

    m_ref[:, :ATT_Q_W] = jnp.concatenate(att_rows, axis=0).astype(BF16)
    o = o_intra + jnp.concatenate(inter_rows, axis=0)
    gn = vec_ref[V_GN:V_GN + 1, :GLA_DV]
    for h, gh in enumerate(_gla_merge(o, og_ref[...], gn)):
        m_ref[:, ATT_Q_W + h * GLA_DV:ATT_Q_W + (h + 1) * GLA_DV] = gh.astype(BF16)


def _mix_sample(layer, sinks, merged, proj, cache_k, cache_v, state, vecs, consts, prev):
    tok = lambda w_: pl.BlockSpec((BLOCK, w_), lambda i, *_: (i, 0))
    merged_blk = pl.BlockSpec((BLOCK, D_MIX), lambda i, *_: (MAIN_ROWS // BLOCK + i, 0))
    full = lambda a: pl.BlockSpec(a.shape, lambda i, *_: (0,) * a.ndim)
    seq = lambda a: pl.BlockSpec((1, SEQ_GROUP) + a.shape[2:], lambda i, *_: (layer, i, 0, 0))
    ones8 = jnp.ones((DEC_SEQ, LANES), F32)
    const_args = (consts["seq_tri"], consts["seq_ones"], consts["lev_low"], ones8, consts["pair_level"])
    seq_args = (cache_k, cache_v, state)
    alias_args = (merged,) + (tuple(prev) if prev is not None else ())
    grid_spec = pltpu.PrefetchScalarGridSpec(
        num_scalar_prefetch=1,
        grid=(DEC_BATCH // SEQ_GROUP,),
        in_specs=([pl.BlockSpec(memory_space=pl.ANY)] * len(alias_args)
                  + [tok(a.shape[1]) for a in proj] + [seq(a) for a in seq_args]
                  + [pl.BlockSpec((V_ROWS, D_MODEL), lambda i, *_: (layer, 0))]
                  + [full(a) for a in const_args]),
        out_specs=[merged_blk] + [seq(a) for a in seq_args],
    )
    return pl.pallas_call(
        functools.partial(_mix_sample_kernel, layer=layer, n_alias=len(alias_args)),
        grid_spec=grid_spec,
        out_shape=[jax.ShapeDtypeStruct(merged.shape, merged.dtype)]
        + [jax.ShapeDtypeStruct(a.shape, a.dtype) for a in seq_args],
        input_output_aliases={1 + n: n for n in range(len(alias_args))},
        compiler_params=pltpu.CompilerParams(dimension_semantics=("arbitrary",),
                                             vmem_limit_bytes=VMEM_LIMIT),
        name="mix_sample",
    )(sinks, *alias_args, *proj, *seq_args, vecs, *const_args)


def _tile_of_step(i):
    return jnp.maximum(i - N_PREP, 0)


def _out_ffn_kernel(*refs, last):
    x_ref, m_ref, vec_ref, wo_ref, wg_ref, wu_ref, wd_ref = refs[:7]
    out_refs = refs[7:9] if last else refs[7:8]
    wo_s, wg_s, wu_s, wd_s, act_ref = refs[-5:]
    i = pl.program_id(0)

    @pl.when(i < N_PREP)
    def _():
        rows = pl.ds(pl.multiple_of(i * W_ROWS, W_ROWS), W_ROWS)
        wo_s[rows, :] = wo_ref[0].astype(BF16)
        wg_s[rows, :] = wg_ref[0].astype(BF16)
        wu_s[rows, :] = wu_ref[0].astype(BF16)
        wd_s[pl.ds(pl.multiple_of(i * WD_ROWS, WD_ROWS), WD_ROWS), :] = wd_ref[0].astype(BF16)

    @pl.when(i >= N_PREP)
    def _():
        x1 = x_ref[...] + _dot(m_ref[...], wo_s[...])
        ms = jnp.mean(x1 * x1, axis=-1, keepdims=True)
        h = (x1 * lax.rsqrt(ms + RMS_EPS) * vec_ref[V_NORM2:V_NORM2 + 1, :]).astype(BF16)
        for c in range(D_FFN // FFN_CHUNK):
            cols = slice(c * FFN_CHUNK, (c + 1) * FFN_CHUNK)
            gate = _dot(h, wg_s[:, cols])
            up = _dot(h, wu_s[:, cols])
            act_ref[:, cols] = (gate * _sigmoid(gate) * up).astype(BF16)
        y = x1 + _dot(act_ref[...], wd_s[...])
        if last:
            t = i - N_PREP

            @pl.when(t < MAIN_TILES)
            def _():
                out_refs[0][...] = y.reshape(BATCH, BLOCK, D_MODEL)

            @pl.when(t >= MAIN_TILES)
            def _():
                out_refs[1][...] = y
        else:
            out_refs[0][...] = y


def _out_ffn(layer, x_all, merged, vecs, w_o, w_gate, w_up, w_down, last):
    tile = lambda w_: pl.BlockSpec((ROW_TILE, w_), lambda i: (_tile_of_step(i), 0))
    chunk = lambda r, c: pl.BlockSpec((1, r, c), lambda i: (layer, jnp.minimum(i, N_PREP - 1), 0))
    if last:
        n_tiles = MAIN_TILES + SAMPLE_TILES
        out_specs = [pl.BlockSpec((BATCH, BLOCK, D_MODEL),
                                  lambda i: (0, jnp.minimum(_tile_of_step(i), MAIN_TILES - 1), 0)),
                     pl.BlockSpec((ROW_TILE, D_MODEL),
                                  lambda i: (jnp.maximum(_tile_of_step(i) - MAIN_TILES, 0), 0))]
        out_shape = [jax.ShapeDtypeStruct((BATCH, SEQ, D_MODEL), F32),
                     jax.ShapeDtypeStruct((SAMPLE_ROWS, D_MODEL), F32)]
    else:
        n_tiles = N_TILES
        out_specs = [tile(D_MODEL)]
        out_shape = [jax.ShapeDtypeStruct((TOTAL_ROWS, D_MODEL), F32)]
    return pl.pallas_call(
        functools.partial(_out_ffn_kernel, last=last),
        grid=(N_PREP + n_tiles,),
        in_specs=[tile(D_MODEL), tile(D_MIX), pl.BlockSpec((V_ROWS, D_MODEL), lambda i: (layer, 0)),
                  chunk(W_ROWS, D_MODEL), chunk(W_ROWS, D_FFN), chunk(W_ROWS, D_FFN),
                  chunk(WD_ROWS, D_MODEL)],
        out_specs=out_specs,
        out_shape=out_shape,
        scratch_shapes=[pltpu.VMEM((D_MIX, D_MODEL), BF16), pltpu.VMEM((D_MODEL, D_FFN), BF16),
                        pltpu.VMEM((D_MODEL, D_FFN), BF16), pltpu.VMEM((D_FFN, D_MODEL), BF16),
                        pltpu.VMEM((ROW_TILE, D_FFN), BF16)],
        compiler_params=pltpu.CompilerParams(dimension_semantics=("arbitrary",),
                                             vmem_limit_bytes=VMEM_LIMIT),
        name="out_ffn",
    )(x_all, merged, vecs, w_o, w_gate, w_up, w_down)


def _vector_slab(norm1, norm2, q_norm, k_norm, b_g, gla_norm):
    pad = lambda a: jnp.pad(a.astype(F32), ((0, 0), (0, D_MODEL - a.shape[1])))
    rows = [norm1.astype(F32), norm2.astype(F32), pad(jnp.tile(q_norm, (1, N_HEADS)) * ATT_SCALE),
            pad(jnp.tile(k_norm, (1, N_KV_HEADS))), pad(b_g), pad(gla_norm)]
    rows += [jnp.zeros((DEPTH, D_MODEL), F32)] * (V_ROWS - len(rows))
    return jnp.stack(rows, axis=1).reshape(DEPTH * V_ROWS, D_MODEL)


def kernel(x_prompt, x_sample, cache_k, cache_v, state_gla, meta, norm1, w_in, q_norm, k_norm, sinks,
           w_g2, b_g, gla_norm, w_o, norm2, w_gate, w_up, w_down):
    consts = _constants()
    dt = x_prompt.dtype
    vecs = _vector_slab(norm1, norm2, q_norm, k_norm, b_g, gla_norm)
    sinks = sinks.astype(F32)
    lead = jnp.tile(jnp.concatenate([jnp.zeros((PAD_LEN, D_MODEL), dt), meta.astype(dt)], axis=0), (BATCH, 1))
    x_srcs = (x_prompt, x_sample.reshape(SAMPLE_ROWS, D_MODEL), lead)
    ck_in = cache_k.reshape(DEPTH, DEC_BATCH, WINDOW, ATT_KV_W)
    cv_in = cache_v.reshape(DEPTH, DEC_BATCH, WINDOW, ATT_KV_W)
    st_in = state_gla.reshape(DEPTH, DEC_BATCH, GLA_K_W, GLA_DV)

    prompt_outs, sample_outs = None, None
    for l in range(DEPTH):
        x_all, merged, proj_sample, prompt_outs = _front(l, sinks, x_srcs, vecs, w_in, w_g2, consts, prompt_outs)
        merged, *sample_outs = _mix_sample(l, sinks, merged, proj_sample, ck_in, cv_in, st_in, vecs, consts,
                                           sample_outs)
        x_srcs = tuple(_out_ffn(l, x_all, merged, vecs, w_o, w_gate, w_up, w_down, last=l == DEPTH - 1))

    y_main, y_sample = x_srcs
    ps, pk, pv = prompt_outs
    sk, sv, ss = sample_outs
    kv5 = lambda a, n: a.reshape(DEPTH, n, WINDOW, N_KV_HEADS, HEAD_DIM)
    st5 = lambda a, n: a.reshape(DEPTH, n, GLA_HEADS, GLA_DK, GLA_DV)
    return (y_main, y_sample.reshape(DEC_BATCH, DEC_SEQ, D_MODEL),
            kv5(pk, BATCH), kv5(pv, BATCH), st5(ps, BATCH), kv5(sk, DEC_BATCH), kv5(sv, DEC_BATCH),
            st5(ss, DEC_BATCH))
```

```python
import functools

import jax
import jax.numpy as jnp
import numpy as np
from jax import lax
from jax.experimental import pallas as pl
from jax.experimental.pallas import tpu as pltpu

F32 = jnp.float32
BF16 = jnp.bfloat16

D_MODEL = 1024
BATCH = 4
SEQ = 4096
DEPTH = 2
DEC_BATCH = 128
DEC_SEQ = 8
N_META = 16
WINDOW = 128
BLOCK = 128
PAD_LEN = BLOCK - N_META
N_HEADS = 8
N_KV_HEADS = 2
HEAD_DIM = 64
ATT_SCALE = HEAD_DIM ** -0.5
ATT_Q_W = N_HEADS * HEAD_DIM
ATT_KV_W = N_KV_HEADS * HEAD_DIM
GLA_HEADS = 4
GLA_DK = 64
GLA_DV = 128
GLA_K_W = GLA_HEADS * GLA_DK
GLA_V_W = GLA_HEADS * GLA_DV
GATE_RANK = 16
GATE_NORMALIZER = 16.0
D_MIX = ATT_Q_W + GLA_V_W
D_FFN = 2816
IN_W = 2320
RMS_EPS = 1e-6
MASK_VALUE = -1e30

LANES = 128
N_BLOCKS = 1 + SEQ // BLOCK
MAIN_ROWS = BATCH * SEQ
SAMPLE_ROWS = DEC_BATCH * DEC_SEQ
LEAD_ROWS = BATCH * BLOCK
TOTAL_ROWS = MAIN_ROWS + SAMPLE_ROWS + LEAD_ROWS
ROW_TILE = 512
MAIN_TILES = MAIN_ROWS // ROW_TILE
SAMPLE_TILES = SAMPLE_ROWS // ROW_TILE
N_TILES = TOTAL_ROWS // ROW_TILE
LEAD_TILE = N_TILES - 1
SEQ_GROUP = BLOCK // DEC_SEQ
N_LEVELS = 7
LOW_LEVELS = 3
LOG_DK = 6
N_PREP = 8
W_ROWS = D_MODEL // N_PREP
WD_ROWS = D_FFN // N_PREP

SRC_LOW, SRC_OG = 1792, 1808
OFF_Q, OFF_K, OFF_V = 0, 512, 640
OFF_GQ, OFF_GK, OFF_GV, OFF_OG, OFF_LOW = 768, 1024, 1280, 1792, 2304
PROJ_W = OFF_LOW + LANES
PROJ_CHUNK = 512
FFN_CHUNK = 256
VMEM_LIMIT = 56 * 1024 * 1024

V_NORM1, V_NORM2, V_QG, V_KG, V_BG, V_GN, V_ROWS = 0, 1, 2, 3, 4, 5, 8

PF_K, PF_V, PF_GQ, PF_GK, PF_LD, PF_OG, PF_W = 0, 128, 256, 512, 768, 1024, 1536
PB_Q, PB_GV, PB_W = 0, 512, 1024


def _proj_views(pf_ref, pb_ref):
    f = lambda a, b: pf_ref.at[:, a:b]
    return (pb_ref.at[:, PB_Q:PB_GV], f(PF_K, PF_V), f(PF_V, PF_GQ), f(PF_GQ, PF_GK), f(PF_GK, PF_LD),
            f(PF_LD, PF_OG), pb_ref.at[:, PB_GV:PB_W], f(PF_OG, PF_W))


def _dot(a, b):
    return jnp.dot(a, b, preferred_element_type=F32)


def _dot_nt(a, b):
    return lax.dot_general(a, b, (((1,), (1,)), ((), ())), preferred_element_type=F32)


def _dot_tn(a, b):
    return lax.dot_general(a, b, (((0,), (0,)), ((), ())), preferred_element_type=F32)


def _split(x):
    hi = x.astype(BF16)
    lo = (x - hi.astype(F32)).astype(BF16)
    return hi, lo


def _sigmoid(x):
    return 1.0 / (1.0 + jnp.exp(-x))


def _level_matrix(levels, n=BLOCK):
    out = np.zeros((len(levels) * n, n), np.float32)
    for i, l in enumerate(levels):
        size = 2 << l
        for t in range(n):
            mid = (t // size) * size + size // 2 - 1
            if (t >> l) & 1:
                out[i * n + t, mid + 1:t + 1] = 1.0
            else:
                out[i * n + t, t + 1:mid + 1] = 1.0
    return out


def _constants():
    r = np.arange(BLOCK)
    tri = (r[None, :] <= r[:, None]).astype(np.float32)
    same_seq = (r[None, :] // DEC_SEQ) == (r[:, None] // DEC_SEQ)
    diff = np.maximum(r[:, None] ^ r[None, :], 1)
    pair_level = np.where(r[None, :] < r[:, None], np.floor(np.log2(diff)).astype(np.int32),
                          np.where(r[None, :] == r[:, None], N_LEVELS, N_LEVELS + 1)).astype(np.int32)
    c = np.arange(2 * LANES)
    bd256 = ((c[None, :] // HEAD_DIM) == (c[:, None] // HEAD_DIM)).astype(np.float32)
    return dict(
        tri=jnp.asarray(tri, BF16),
        seq_tri=jnp.asarray(tri * same_seq, BF16),
        seq_ones=jnp.asarray(same_seq.astype(np.float32), BF16),
        lev_all=jnp.asarray(_level_matrix(range(N_LEVELS)), BF16),
        lev_low=jnp.asarray(_level_matrix(range(LOW_LEVELS)), BF16),
        bd256=jnp.asarray(bd256, BF16),
        bd128=jnp.asarray(bd256[:LANES, :LANES], BF16),
        pair_level=jnp.asarray(pair_level),
    )


def _head_masks(rows):
    lane = lax.broadcasted_iota(jnp.int32, (rows, GLA_K_W), 1)
    return [(lane >> LOG_DK) == h for h in range(GLA_HEADS)]


def _gla_intra(gq, gk, decay_levels, levels, pair_level):
    rows = gq[0].shape[0]
    streams = range(len(gq))
    low_half = lax.broadcasted_iota(jnp.int32, (rows, LANES), 1) < GLA_DK

    def pair_products(qh, kh):
        qh, kh = qh.astype(BF16), kh.astype(BF16)
        out = []
        for pair in range(GLA_HEADS // 2):
            kl = kh[:, pair * LANES:(pair + 1) * LANES]
            zero = jnp.zeros_like(kl)
            stacked = jnp.concatenate([jnp.where(low_half, kl, zero), jnp.where(low_half, zero, kl)], axis=0)
            p = _dot_nt(qh[:, pair * LANES:(pair + 1) * LANES], stacked)
            out += [p[:, :rows], p[:, rows:]]
        return out

    on_diag = pair_level == N_LEVELS
    a = [[jnp.where(on_diag, p, 0.0) for p in pair_products(gq[b], gk[b])] for b in streams]
    for i, l in enumerate(levels):
        at_level = pair_level == l
        for b in streams:
            e = decay_levels[b][i * rows:(i + 1) * rows]
            a[b] = [jnp.where(at_level, p, a_h)
                    for p, a_h in zip(pair_products(gq[b] * e, gk[b] * e), a[b])]
    return a


def _gla_merge(o, og, gn):
    outs = []
    for h in range(GLA_HEADS):
        oh = o[:, h * GLA_DV:(h + 1) * GLA_DV]
        gh = og[:, h * GLA_DV:(h + 1) * GLA_DV]
        ms = jnp.mean(oh * oh, axis=-1, keepdims=True)
        outs.append(oh * lax.rsqrt(ms + RMS_EPS) * gn * (gh * _sigmoid(gh)))
    return outs


def _dup_halves(x):
    low = lax.broadcasted_iota(jnp.int32, x.shape, 1) < HEAD_DIM
    rolled = pltpu.roll(x, HEAD_DIM, axis=1)
    return jnp.where(low, x, rolled), jnp.where(low, rolled, x)


def _convert_proj_weights(i, w_ref, wg2_ref, wbf_ref, wg2s_ref):
    rows = pl.ds(pl.multiple_of(i * W_ROWS, W_ROWS), W_ROWS)
    chunk = w_ref[0]
    wbf_ref[rows, 0:OFF_OG] = chunk[:, 0:SRC_LOW].astype(BF16)
    wbf_ref[rows, OFF_OG:OFF_LOW] = chunk[:, SRC_OG:IN_W].astype(BF16)
    lane = lax.broadcasted_iota(jnp.int32, (W_ROWS, LANES), 1)
    low = jnp.where(lane < GATE_RANK, chunk[:, SRC_LOW:SRC_LOW + LANES], 0.0)
    wbf_ref[rows, OFF_LOW:PROJ_W] = low.astype(BF16)

    @pl.when(i == 0)
    def _():
        wg2s_ref[...] = jnp.concatenate(
            [wg2_ref[0], jnp.zeros((LANES - GATE_RANK, GLA_K_W), F32)], axis=0).astype(BF16)


def _project_tile(x, vec_ref, wbf_ref, wg2s_ref, bd256_ref, bd128_ref, z_ref, pf_ref, pb_ref):
    q_ref, k_ref, v_ref, gq_ref, gk_ref, ld_ref, gv_ref, og_ref = _proj_views(pf_ref, pb_ref)
    ms = jnp.mean(x * x, axis=-1, keepdims=True)
    h = (x * lax.rsqrt(ms + RMS_EPS) * vec_ref[V_NORM1:V_NORM1 + 1, :]).astype(BF16)

    def q_part():
        q = z_ref[:, OFF_Q:OFF_K]
        q2 = (q * q).astype(BF16)
        bd = bd256_ref[...]
        ssq = jnp.concatenate([_dot(q2[:, :256], bd), _dot(q2[:, 256:], bd)], axis=1)
        q_ref[...] = (q * lax.rsqrt(ssq * (1.0 / HEAD_DIM) + RMS_EPS)
                      * vec_ref[V_QG:V_QG + 1, :ATT_Q_W]).astype(BF16)

    def kv_part():
        k = z_ref[:, OFF_K:OFF_V]
        ssk = _dot((k * k).astype(BF16), bd128_ref[...])
        k_ref[...] = k * lax.rsqrt(ssk * (1.0 / HEAD_DIM) + RMS_EPS) * vec_ref[V_KG:V_KG + 1, :ATT_KV_W]
        v_ref[...] = z_ref[:, OFF_V:OFF_GQ]
        gq_ref[...] = z_ref[:, OFF_GQ:OFF_GK] * (GLA_DK ** -0.5)

    def gk_part():
        gk_ref[...] = z_ref[:, OFF_GK:OFF_GV]

    def gv_part():
        gv_ref[...] = z_ref[:, OFF_GV:OFF_OG].astype(BF16)

    def tail_part():
        og_ref[...] = z_ref[:, OFF_OG:OFF_LOW]
        logit = (_dot(z_ref[:, OFF_LOW:PROJ_W].astype(BF16), wg2s_ref[...])
                 + vec_ref[V_BG:V_BG + 1, :GLA_K_W])
        log_sig = jnp.minimum(logit, 0.0) - jnp.log1p(jnp.exp(-jnp.abs(logit)))
        ld_ref[...] = log_sig * (1.0 / GATE_NORMALIZER)

    pieces = [q_part, kv_part, gk_part, gv_part, tail_part]
    for c, piece in enumerate(pieces):
        cols = slice(c * PROJ_CHUNK, min((c + 1) * PROJ_CHUNK, PROJ_W))
        z_ref[:, cols] = _dot(h, wbf_ref[:, cols])
        piece()


def _mix_prompt_tile(blk, layer, sinks_ref, pf_ref, pb_ref, vec_ref, tri_ref, lev_ref, ones_ref, plev_ref,
                     m_ref, kprev_ref, vprev_ref, state_ref, sbd_ref):
    q_ref, k_ref, v_ref, gq_ref, gk_ref, ld_ref, gv_ref, og_ref = _proj_views(pf_ref, pb_ref)

    per_group = N_HEADS // N_KV_HEADS
    grows = per_group * BLOCK
    row = lax.broadcasted_iota(jnp.int32, (grows, BLOCK), 0) & (BLOCK - 1)
    col = lax.broadcasted_iota(jnp.int32, (grows, BLOCK), 1)
    own = col <= row
    kpos = jnp.where(own, blk * BLOCK, (blk - 1) * BLOCK) + col - PAD_LEN
    live = kpos >= 0
    low_half = lax.broadcasted_iota(jnp.int32, (BLOCK, LANES), 1) < HEAD_DIM
    head_of_row = lax.broadcasted_iota(jnp.int32, (grows, 1), 0) >> N_LEVELS
    sink_cols = []
    for g in range(N_KV_HEADS):
        sink = jnp.zeros((grows, 1), F32)
        for r in range(per_group):
            sink = jnp.where(head_of_row == r, sinks_ref[layer, g * per_group + r], sink)
        sink_cols.append(sink)

    for b in range(BATCH):
        rows = slice(b * BLOCK, (b + 1) * BLOCK)
        kdup = [x.astype(BF16) for x in _dup_halves(k_ref[rows, :])]
        vdup = [x.astype(BF16) for x in _dup_halves(v_ref[rows, :])]
        for g in range(N_KV_HEADS):
            kk = jnp.concatenate([kprev_ref[b * N_KV_HEADS + g], kdup[g]], axis=0)
            vv = jnp.concatenate([vprev_ref[b * N_KV_HEADS + g], vdup[g]], axis=0)
            tiles = range(g * per_group // 2, (g + 1) * per_group // 2)
            pieces = []
            for j in tiles:
                qt = q_ref[rows, j * LANES:(j + 1) * LANES]
                zero = jnp.zeros_like(qt)
                pieces += [jnp.where(low_half, qt, zero), jnp.where(low_half, zero, qt)]
            s = _dot_nt(jnp.concatenate(pieces, axis=0), kk)
            s = jnp.where(live, jnp.where(own, s[:, BLOCK:], s[:, :BLOCK]), MASK_VALUE)
            m = jnp.maximum(jnp.max(s, axis=-1, keepdims=True), sink_cols[g])
            p = jnp.exp(s - m)
            denom = jnp.sum(p, axis=-1, keepdims=True) + jnp.exp(sink_cols[g] - m)
            p2 = jnp.concatenate([jnp.where(own, 0.0, p), jnp.where(own, p, 0.0)], axis=1).astype(BF16)
            o = _dot(p2, vv) / denom
            for n, j in enumerate(tiles):
                o_lo = o[2 * n * BLOCK:(2 * n + 1) * BLOCK]
                o_hi = o[(2 * n + 1) * BLOCK:(2 * n + 2) * BLOCK]
                m_ref[rows, j * LANES:(j + 1) * LANES] = jnp.where(low_half, o_lo, o_hi).astype(BF16)
        for g in range(N_KV_HEADS):
            kprev_ref[b * N_KV_HEADS + g] = kdup[g]
            vprev_ref[b * N_KV_HEADS + g] = vdup[g]

    rpos = blk * BLOCK + lax.broadcasted_iota(jnp.int32, (BLOCK, 1), 0) - PAD_LEN
    valid = (rpos >= 0).astype(F32)
    ld_all = jnp.concatenate([ld_ref[b * BLOCK:(b + 1) * BLOCK, :] for b in range(BATCH)], axis=1)
    hi, lo = _split(ld_all)
    tri = tri_ref[...]
    g_cum_all = _dot(tri, hi) + _dot(tri, lo)
    lev = lev_ref[...]
    decay_all = jnp.exp(_dot(lev, hi) + _dot(lev, lo))
    ones = ones_ref[...]
    decay_col_all = jnp.exp(_dot_tn(hi, ones) + _dot_tn(lo, ones))
    pair_level = plev_ref[...]
    gn = vec_ref[V_GN:V_GN + 1, :GLA_DV]
    streams = range(BATCH)
    rows = [slice(b * BLOCK, (b + 1) * BLOCK) for b in streams]
    cols = [slice(b * GLA_K_W, (b + 1) * GLA_K_W) for b in streams]
    gq = [gq_ref[rows[b], :] for b in streams]
    gk = [gk_ref[rows[b], :] * valid for b in streams]
    gv = [gv_ref[rows[b], :] for b in streams]
    g_cum = [g_cum_all[:, cols[b]] for b in streams]
    a = _gla_intra(gq, gk, [decay_all[:, cols[b]] for b in streams], range(N_LEVELS), pair_level)

    pairs = range(GLA_HEADS // 2)
    q_dec = [(gq[b] * jnp.exp(g_cum[b])).astype(BF16) for b in streams]
    o = [jnp.concatenate([_dot(q_dec[b][:, i * LANES:(i + 1) * LANES], sbd_ref[2 * b + i]) for i in pairs], axis=1)
         for b in streams]
    o = [o[b] + jnp.concatenate(
        [_dot(a[b][h].astype(BF16), gv[b][:, h * GLA_DV:(h + 1) * GLA_DV]) for h in range(GLA_HEADS)], axis=1)
        for b in streams]
    for b in streams:
        for h, gh in enumerate(_gla_merge(o[b], og_ref[rows[b], :], gn)):
            m_ref[rows[b], ATT_Q_W + h * GLA_DV:ATT_Q_W + (h + 1) * GLA_DV] = gh.astype(BF16)

    k_dec = [(gk[b] * jnp.exp(g_cum[b][BLOCK - 1:BLOCK, :] - g_cum[b])).astype(BF16) for b in streams]
    kv = [[_dot_tn(k_dec[b][:, i * LANES:(i + 1) * LANES], gv[b][:, 2 * i * GLA_DV:(2 * i + 2) * GLA_DV])
           for i in pairs] for b in streams]
    for b in streams:
        new_state = decay_col_all[cols[b], :] * state_ref[b] + jnp.concatenate(
            [kv[b][h // 2][(h % 2) * GLA_DK:(h % 2 + 1) * GLA_DK, (h % 2) * GLA_DV:(h % 2 + 1) * GLA_DV]
             for h in range(GLA_HEADS)], axis=0)
        state_ref[b] = new_state
        for h in range(GLA_HEADS):
            sbd_ref[2 * b + h // 2, (h % 2) * GLA_DK:(h % 2 + 1) * GLA_DK,
                    (h % 2) * GLA_DV:(h % 2 + 1) * GLA_DV] = new_state[h * GLA_DK:(h + 1) * GLA_DK].astype(BF16)


def _front_kernel(*refs, layer, n_src, chained):
    sinks_ref, refs = refs[0], refs[1:]
    if chained:
        refs = refs[3:]
    x_refs, refs = refs[:n_src], refs[n_src:]
    vec_ref, w_ref, wg2_ref, bd256_ref, bd128_ref, tri_ref, lev_ref, ones_ref, plev_ref = refs[:9]
    refs = refs[9:]
    if n_src > 1:
        xo_ref, refs = refs[0], refs[1:]
    (m_ref, pfs_ref, pbs_ref, s_out_ref, k_out_ref, v_out_ref,
     wbf_ref, wg2s_ref, z_ref, pf_ref, pb_ref, kprev_ref, vprev_ref, state_ref, sbd_ref) = refs
    i = pl.program_id(0)
    t = i - N_PREP

    @pl.when(i < N_PREP)
    def _():
        _convert_proj_weights(i, w_ref, wg2_ref, wbf_ref, wg2s_ref)

    @pl.when(i == 0)
    def _():
        kprev_ref[...] = jnp.zeros_like(kprev_ref)
        vprev_ref[...] = jnp.zeros_like(vprev_ref)
        state_ref[...] = jnp.zeros_like(state_ref)
        sbd_ref[...] = jnp.zeros_like(sbd_ref)

    def load_x():
        if n_src == 1:
            return x_refs[0][...]
        x = jnp.where(t == 0, x_refs[2][...],
                      jnp.where(t < N_BLOCKS, x_refs[0][...].reshape(ROW_TILE, D_MODEL), x_refs[1][...]))
        xo_ref[...] = x
        return x

    @pl.when((t >= 0) & (t < N_BLOCKS))
    def _():
        _project_tile(load_x(), vec_ref, wbf_ref, wg2s_ref, bd256_ref, bd128_ref, z_ref, pf_ref, pb_ref)
        _mix_prompt_tile(t, layer, sinks_ref, pf_ref, pb_ref, vec_ref, tri_ref, lev_ref, ones_ref, plev_ref,
                         m_ref, kprev_ref, vprev_ref, state_ref, sbd_ref)

    @pl.when(t == N_BLOCKS - 1)
    def _():
        s_out_ref[0] = state_ref[...]
        k_out_ref[0] = pf_ref[:, PF_K:PF_V].reshape(BATCH, BLOCK, ATT_KV_W)
        v_out_ref[0] = pf_ref[:, PF_V:PF_GQ].reshape(BATCH, BLOCK, ATT_KV_W)

    @pl.when(t >= N_BLOCKS)
    def _():
        _project_tile(load_x(), vec_ref, wbf_ref, wg2s_ref, bd256_ref, bd128_ref, z_ref, pfs_ref, pbs_ref)


def _front(layer, sinks, x_srcs, vecs, w_in, w_g2, consts, prev):
    n_src = len(x_srcs)
    step = lambda i: i - N_PREP
    tile_of = lambda i: jnp.where(step(i) <= 0, LEAD_TILE, jnp.minimum(step(i) - 1, LEAD_TILE - 1))
    prompt_tile_of = lambda i: jnp.where(step(i) <= 0, LEAD_TILE, jnp.minimum(step(i) - 1, MAIN_TILES - 1))
    sample_tile_of = lambda i: jnp.clip(step(i) - N_BLOCKS, 0, SAMPLE_TILES - 1)
    full = lambda a: pl.BlockSpec(a.shape, lambda i, *_: (0,) * a.ndim)
    per_batch = lambda r, c: pl.BlockSpec((1, BATCH, r, c), lambda i, *_: (layer, 0, 0, 0))
    if n_src == 1:
        x_specs = [pl.BlockSpec((ROW_TILE, D_MODEL), lambda i, *_: (tile_of(i), 0))]
    else:
        x_specs = [
            pl.BlockSpec((BATCH, BLOCK, D_MODEL), lambda i, *_: (0, jnp.clip(step(i) - 1, 0, MAIN_TILES - 1), 0)),
            pl.BlockSpec((ROW_TILE, D_MODEL), lambda i, *_: (sample_tile_of(i), 0)),
            pl.BlockSpec((ROW_TILE, D_MODEL), lambda i, *_: (0, 0)),
        ]
    ones = jnp.ones((BLOCK, LANES), BF16)
    const_args = (consts["bd256"], consts["bd128"], consts["tri"], consts["lev_all"], ones, consts["pair_level"])
    chained = prev is not None
    prev_args = tuple(prev) if chained else ()
    in_specs = ([pl.BlockSpec(memory_space=pl.ANY)] * len(prev_args) + x_specs + [
        pl.BlockSpec((V_ROWS, D_MODEL), lambda i, *_: (layer, 0)),
        pl.BlockSpec((1, W_ROWS, IN_W), lambda i, *_: (layer, jnp.minimum(i, N_PREP - 1), 0)),
        pl.BlockSpec((1, GATE_RANK, GLA_K_W), lambda i, *_: (layer, 0, 0)),
    ] + [full(a) for a in const_args])
    out_specs = [pl.BlockSpec((ROW_TILE, D_MIX), lambda i, *_: (prompt_tile_of(i), 0)),
                 pl.BlockSpec((ROW_TILE, PF_W), lambda i, *_: (sample_tile_of(i), 0)),
                 pl.BlockSpec((ROW_TILE, PB_W), lambda i, *_: (sample_tile_of(i), 0)),
                 per_batch(GLA_K_W, GLA_DV), per_batch(BLOCK, ATT_KV_W), per_batch(BLOCK, ATT_KV_W)]
    out_shape = [jax.ShapeDtypeStruct((TOTAL_ROWS, D_MIX), BF16),
                 jax.ShapeDtypeStruct((SAMPLE_ROWS, PF_W), F32),
                 jax.ShapeDtypeStruct((SAMPLE_ROWS, PB_W), BF16),
                 jax.ShapeDtypeStruct((DEPTH, BATCH, GLA_K_W, GLA_DV), F32),
                 jax.ShapeDtypeStruct((DEPTH, BATCH, BLOCK, ATT_KV_W), F32),
                 jax.ShapeDtypeStruct((DEPTH, BATCH, BLOCK, ATT_KV_W), F32)]
    n_lead_out = 0
    if n_src > 1:
        out_specs = [pl.BlockSpec((ROW_TILE, D_MODEL), lambda i, *_: (tile_of(i), 0))] + out_specs
        out_shape = [jax.ShapeDtypeStruct((TOTAL_ROWS, D_MODEL), F32)] + out_shape
        n_lead_out = 1
    grid_spec = pltpu.PrefetchScalarGridSpec(
        num_scalar_prefetch=1,
        grid=(N_PREP + N_BLOCKS + SAMPLE_TILES,),
        in_specs=in_specs,
        out_specs=out_specs,
        scratch_shapes=[pltpu.VMEM((D_MODEL, PROJ_W), BF16), pltpu.VMEM((LANES, GLA_K_W), BF16),
                        pltpu.VMEM((ROW_TILE, PROJ_W), F32),
                        pltpu.VMEM((ROW_TILE, PF_W), F32), pltpu.VMEM((ROW_TILE, PB_W), BF16),
                        pltpu.VMEM((BATCH * N_KV_HEADS, BLOCK, ATT_KV_W), BF16),
                        pltpu.VMEM((BATCH * N_KV_HEADS, BLOCK, ATT_KV_W), BF16),
                        pltpu.VMEM((BATCH, GLA_K_W, GLA_DV), F32),
                        pltpu.VMEM((BATCH * GLA_HEADS // 2, 2 * GLA_DK, 2 * GLA_DV), BF16)],
    )
    res = pl.pallas_call(
        functools.partial(_front_kernel, layer=layer, n_src=n_src, chained=chained),
        grid_spec=grid_spec,
        out_shape=out_shape,
        input_output_aliases={1 + n: n_lead_out + 3 + n for n in range(len(prev_args))},
        compiler_params=pltpu.CompilerParams(dimension_semantics=("arbitrary",),
                                             vmem_limit_bytes=VMEM_LIMIT),
        name="front",
    )(sinks, *prev_args, *x_srcs, vecs, w_in, w_g2, *const_args)
    x_all = res[0] if n_src > 1 else x_srcs[0]
    merged, pfs, pbs, ps, pk, pv = res[n_lead_out:]
    return x_all, merged, (pfs, pbs), (ps, pk, pv)


def _mix_sample_kernel(*refs, layer, n_alias):
    sinks_ref, refs = refs[0], refs[n_alias + 1:]
    (pf_ref, pb_ref, ck_ref, cv_ref, st_ref,
     vec_ref, tri_ref, sones_ref, lev_ref, ones_ref, plev_ref,
     m_ref, ck_out_ref, cv_out_ref, st_out_ref) = refs
    q_ref, k_ref, v_ref, gq_ref, gk_ref, ld_ref, gv_ref, og_ref = _proj_views(pf_ref, pb_ref)
    gq = gq_ref[...]
    gk = gk_ref[...]
    gv = gv_ref[...]
    gvf = gv.astype(F32)
    hi, lo = _split(ld_ref[...])
    tri = tri_ref[...]
    g_cum = _dot(tri, hi) + _dot(tri, lo)
    sones = sones_ref[...]
    g_tot = _dot(sones, hi) + _dot(sones, lo)
    lev = lev_ref[...]
    decay_levels = jnp.exp(_dot(lev, hi) + _dot(lev, lo))
    a = _gla_intra([gq], [gk], [decay_levels], range(LOW_LEVELS), plev_ref[...])[0]
    o_intra = jnp.concatenate(
        [_dot(a[h].astype(BF16), gv[:, h * GLA_DV:(h + 1) * GLA_DV]) for h in range(GLA_HEADS)], axis=1)
    q_dec = gq * jnp.exp(g_cum)
    k_dec = gk * jnp.exp(g_tot - g_cum)
    hm8 = _head_masks(DEC_SEQ)
    ones8 = ones_ref[...]
    hi_f, lo_f = hi.astype(F32), lo.astype(F32)

    qf = q_ref[...].astype(F32)
    kf = k_ref[...]
    vf = v_ref[...]
    low8 = lax.broadcasted_iota(jnp.int32, (DEC_SEQ, LANES), 1) < HEAD_DIM
    nkeys = WINDOW + DEC_SEQ
    srow = lax.broadcasted_iota(jnp.int32, (N_HEADS * DEC_SEQ, nkeys), 0)
    scol = lax.broadcasted_iota(jnp.int32, (N_HEADS * DEC_SEQ, nkeys), 1)
    t_of_row = srow & (DEC_SEQ - 1)
    amask = ((scol < WINDOW) & (scol > t_of_row)) | ((scol >= WINDOW) & (scol - WINDOW <= t_of_row))
    rid = lax.broadcasted_iota(jnp.int32, (N_HEADS * DEC_SEQ, 1), 0) >> LOW_LEVELS
    sink_col = jnp.zeros((N_HEADS * DEC_SEQ, 1), F32)
    for i in range(N_HEADS):
        sink_col = jnp.where(rid == i, sinks_ref[layer, i], sink_col)
    half = N_HEADS * DEC_SEQ // N_KV_HEADS

    att_rows, inter_rows = [], []
    for b in range(SEQ_GROUP):
        rows = slice(b * DEC_SEQ, (b + 1) * DEC_SEQ)
        pieces = []
        for j in range(N_HEADS // 2):
            qt = qf[rows, j * LANES:(j + 1) * LANES]
            pieces += [jnp.where(low8, qt, 0.0), jnp.where(low8, 0.0, qt)]
        qp = jnp.concatenate(pieces, axis=0).astype(BF16)
        kk = jnp.concatenate([ck_ref[0, b], kf[rows]], axis=0)
        vv = jnp.concatenate([cv_ref[0, b], vf[rows]], axis=0)
        kd = [x.astype(BF16) for x in _dup_halves(kk)]
        vd = [x.astype(BF16) for x in _dup_halves(vv)]
        s = jnp.concatenate([_dot_nt(qp[:half], kd[0]), _dot_nt(qp[half:], kd[1])], axis=0)
        s = jnp.where(amask, s, MASK_VALUE)
        m = jnp.maximum(jnp.max(s, axis=-1, keepdims=True), sink_col)
        p = jnp.exp(s - m)
        denom = jnp.sum(p, axis=-1, keepdims=True) + jnp.exp(sink_col - m)
        pb = p.astype(BF16)
        ob = jnp.concatenate([_dot(pb[:half], vd[0]), _dot(pb[half:], vd[1])], axis=0) / denom
        att_rows.append(jnp.concatenate(
            [jnp.where(low8, ob[(2 * j) * DEC_SEQ:(2 * j + 1) * DEC_SEQ],
                       ob[(2 * j + 1) * DEC_SEQ:(2 * j + 2) * DEC_SEQ])
             for j in range(N_HEADS // 2)], axis=1))
        ck_out_ref[0, b] = kk[DEC_SEQ:]
        cv_out_ref[0, b] = vv[DEC_SEQ:]

        state = st_ref[0, b]
        qb = q_dec[rows]
        qstack = jnp.concatenate([jnp.where(mk, qb, 0.0) for mk in hm8], axis=0).astype(BF16)
        oi = _dot(qstack, state.astype(BF16))
        inter_rows.append(jnp.concatenate(
            [oi[h * DEC_SEQ:(h + 1) * DEC_SEQ] for h in range(GLA_HEADS)], axis=1))
        kb = k_dec[rows]
        kstack = jnp.concatenate([jnp.where(mk, kb, 0.0) for mk in hm8], axis=0).astype(BF16)
        vb = gvf[rows]
        vstack = jnp.concatenate([vb[:, h * GLA_DV:(h + 1) * GLA_DV] for h in range(GLA_HEADS)],
                                 axis=0).astype(BF16)
        decay_col = jnp.exp(_dot_tn(hi_f[rows], ones8) + _dot_tn(lo_f[rows], ones8))
        st_out_ref[0, b] = decay_col * state + _dot_tn(kstack, vstack)

    m_ref[:, :ATT_Q_W] = jnp.concatenate(att_rows, axis=0).astype(BF16)
    o = o_intra + jnp.concatenate(inter_rows, axis=0)
    gn = vec_ref[V_GN:V_GN + 1, :GLA_DV]
    for h, gh in enumerate(_gla_merge(o, og_ref[...], gn)):
        m_ref[:, ATT_Q_W + h * GLA_DV:ATT_Q_W + (h + 1) * GLA_DV] = gh.astype(BF16)


def _mix_sample(layer, sinks, merged, proj, cache_k, cache_v, state, vecs, consts, prev):
    tok = lambda w_: pl.BlockSpec((BLOCK, w_), lambda i, *_: (i, 0))
    merged_blk = pl.BlockSpec((BLOCK, D_MIX), lambda i, *_: (MAIN_ROWS // BLOCK + i, 0))
    full = lambda a: pl.BlockSpec(a.shape, lambda i, *_: (0,) * a.ndim)
    seq = lambda a: pl.BlockSpec((1, SEQ_GROUP) + a.shape[2:], lambda i, *_: (layer, i, 0, 0))
    ones8 = jnp.ones((DEC_SEQ, LANES), F32)
    const_args = (consts["seq_tri"], consts["seq_ones"], consts["lev_low"], ones8, consts["pair_level"])
    seq_args = (cache_k, cache_v, state)
    alias_args = (merged,) + (tuple(prev) if prev is not None else ())
    grid_spec = pltpu.PrefetchScalarGridSpec(
        num_scalar_prefetch=1,
        grid=(DEC_BATCH // SEQ_GROUP,),
        in_specs=([pl.BlockSpec(memory_space=pl.ANY)] * len(alias_args)
                  + [tok(a.shape[1]) for a in proj] + [seq(a) for a in seq_args]
                  + [pl.BlockSpec((V_ROWS, D_MODEL), lambda i, *_: (layer, 0))]
                  + [full(a) for a in const_args]),
        out_specs=[merged_blk] + [seq(a) for a in seq_args],
    )
    return pl.pallas_call(
        functools.partial(_mix_sample_kernel, layer=layer, n_alias=len(alias_args)),
        grid_spec=grid_spec,
        out_shape=[jax.ShapeDtypeStruct(merged.shape, merged.dtype)]
        + [jax.ShapeDtypeStruct(a.shape, a.dtype) for a in seq_args],
        input_output_aliases={1 + n: n for n in range(len(alias_args))},
        compiler_params=pltpu.CompilerParams(dimension_semantics=("arbitrary",),
                                             vmem_limit_bytes=VMEM_LIMIT),
        name="mix_sample",
    )(sinks, *alias_args, *proj, *seq_args, vecs, *const_args)


def _tile_of_step(i):
    return jnp.maximum(i - N_PREP, 0)


def _out_ffn_kernel(*refs, last):
    x_ref, m_ref, vec_ref, wo_ref, wg_ref, wu_ref, wd_ref = refs[:7]
    out_refs = refs[7:9] if last else refs[7:8]
    wo_s, wg_s, wu_s, wd_s, act_ref = refs[-5:]
    i = pl.program_id(0)

    @pl.when(i < N_PREP)
    def _():
        rows = pl.ds(pl.multiple_of(i * W_ROWS, W_ROWS), W_ROWS)
        wo_s[rows, :] = wo_ref[0].astype(BF16)
        wg_s[rows, :] = wg_ref[0].astype(BF16)
        wu_s[rows, :] = wu_ref[0].astype(BF16)
        wd_s[pl.ds(pl.multiple_of(i * WD_ROWS, WD_ROWS), WD_ROWS), :] = wd_ref[0].astype(BF16)

    @pl.when(i >= N_PREP)
    def _():
        x1 = x_ref[...] + _dot(m_ref[...], wo_s[...])
        ms = jnp.mean(x1 * x1, axis=-1, keepdims=True)
        h = (x1 * lax.rsqrt(ms + RMS_EPS) * vec_ref[V_NORM2:V_NORM2 + 1, :]).astype(BF16)
        for c in range(D_FFN // FFN_CHUNK):
            cols = slice(c * FFN_CHUNK, (c + 1) * FFN_CHUNK)
            gate = _dot(h, wg_s[:, cols])
            up = _dot(h, wu_s[:, cols])
            act_ref[:, cols] = (gate * _sigmoid(gate) * up).astype(BF16)
        y = x1 + _dot(act_ref[...], wd_s[...])
        if last:
            t = i - N_PREP

            @pl.when(t < MAIN_TILES)
            def _():
                out_refs[0][...] = y.reshape(BATCH, BLOCK, D_MODEL)

            @pl.when(t >= MAIN_TILES)
            def _():
                out_refs[1][...] = y
        else:
            out_refs[0][...] = y


def _out_ffn(layer, x_all, merged, vecs, w_o, w_gate, w_up, w_down, last):
    tile = lambda w_: pl.BlockSpec((ROW_TILE, w_), lambda i: (_tile_of_step(i), 0))
    chunk = lambda r, c: pl.BlockSpec((1, r, c), lambda i: (layer, jnp.minimum(i, N_PREP - 1), 0))
    if last:
        n_tiles = MAIN_TILES + SAMPLE_TILES
        out_specs = [pl.BlockSpec((BATCH, BLOCK, D_MODEL),
                                  lambda i: (0, jnp.minimum(_tile_of_step(i), MAIN_TILES - 1), 0)),
                     pl.BlockSpec((ROW_TILE, D_MODEL),
                                  lambda i: (jnp.maximum(_tile_of_step(i) - MAIN_TILES, 0), 0))]
        out_shape = [jax.ShapeDtypeStruct((BATCH, SEQ, D_MODEL), F32),
                     jax.ShapeDtypeStruct((SAMPLE_ROWS, D_MODEL), F32)]
    else:
        n_tiles = N_TILES
        out_specs = [tile(D_MODEL)]
        out_shape = [jax.ShapeDtypeStruct((TOTAL_ROWS, D_MODEL), F32)]
    return pl.pallas_call(
        functools.partial(_out_ffn_kernel, last=last),
        grid=(N_PREP + n_tiles,),
        in_specs=[tile(D_MODEL), tile(D_MIX), pl.BlockSpec((V_ROWS, D_MODEL), lambda i: (layer, 0)),
                  chunk(W_ROWS, D_MODEL), chunk(W_ROWS, D_FFN), chunk(W_ROWS, D_FFN),
                  chunk(WD_ROWS, D_MODEL)],
        out_specs=out_specs,
        out_shape=out_shape,
        scratch_shapes=[pltpu.VMEM((D_MIX, D_MODEL), BF16), pltpu.VMEM((D_MODEL, D_FFN), BF16),
                        pltpu.VMEM((D_MODEL, D_FFN), BF16), pltpu.VMEM((D_FFN, D_MODEL), BF16),
                        pltpu.VMEM((ROW_TILE, D_FFN), BF16)],
        compiler_params=pltpu.CompilerParams(dimension_semantics=("arbitrary",),
                                             vmem_limit_bytes=VMEM_LIMIT),
        name="out_ffn",
    )(x_all, merged, vecs, w_o, w_gate, w_up, w_down)


def _vector_slab(norm1, norm2, q_norm, k_norm, b_g, gla_norm):
    pad = lambda a: jnp.pad(a.astype(F32), ((0, 0), (0, D_MODEL - a.shape[1])))
    rows = [norm1.astype(F32), norm2.astype(F32), pad(jnp.tile(q_norm, (1, N_HEADS)) * ATT_SCALE),
            pad(jnp.tile(k_norm, (1, N_KV_HEADS))), pad(b_g), pad(gla_norm)]
    rows += [jnp.zeros((DEPTH, D_MODEL), F32)] * (V_ROWS - len(rows))
    return jnp.stack(rows, axis=1).reshape(DEPTH * V_ROWS, D_MODEL)


def kernel(x_prompt, x_sample, cache_k, cache_v, state_gla, meta, norm1, w_in, q_norm, k_norm, sinks,
           w_g2, b_g, gla_norm, w_o, norm2, w_gate, w_up, w_down):
    consts = _constants()
    dt = x_prompt.dtype
    vecs = _vector_slab(norm1, norm2, q_norm, k_norm, b_g, gla_norm)
    sinks = sinks.astype(F32)
    lead = jnp.tile(jnp.concatenate([jnp.zeros((PAD_LEN, D_MODEL), dt), meta.astype(dt)], axis=0), (BATCH, 1))
    x_srcs = (x_prompt, x_sample.reshape(SAMPLE_ROWS, D_MODEL), lead)
    ck_in = cache_k.reshape(DEPTH, DEC_BATCH, WINDOW, ATT_KV_W)
    cv_in = cache_v.reshape(DEPTH, DEC_BATCH, WINDOW, ATT_KV_W)
    st_in = state_gla.reshape(DEPTH, DEC_BATCH, GLA_K_W, GLA_DV)

    prompt_outs, sample_outs = None, None
    for l in range(DEPTH):
        x_all, merged, proj_sample, prompt_outs = _front(l, sinks, x_srcs, vecs, w_in, w_g2, consts, prompt_outs)
        merged, *sample_outs = _mix_sample(l, sinks, merged, proj_sample, ck_in, cv_in, st_in, vecs, consts,
                                           sample_outs)
        x_srcs = tuple(_out_ffn(l, x_all, merged, vecs, w_o, w_gate, w_up, w_down, last=l == DEPTH - 1))

    y_main, y_sample = x_srcs
    ps, pk, pv = prompt_outs
    sk, sv, ss = sample_outs
    kv5 = lambda a, n: a.reshape(DEPTH, n, WINDOW, N_KV_HEADS, HEAD_DIM)
    st5 = lambda a, n: a.reshape(DEPTH, n, GLA_HEADS, GLA_DK, GLA_DV)
    return (y_main, y_sample.reshape(DEC_BATCH, DEC_SEQ, D_MODEL),
            kv5(pk, BATCH), kv5(pv, BATCH), st5(ps, BATCH), kv5(sk, DEC_BATCH), kv5(sv, DEC_BATCH),
            st5(ss, DEC_BATCH))
```

```python
import functools

import jax
import jax.numpy as jnp
import numpy as np
from jax import lax
from jax.experimental import pallas as pl
from jax.experimental.pallas import tpu as pltpu

F32 = jnp.float32
BF16 = jnp.bfloat16

D_MODEL = 1024
BATCH = 4
SEQ = 4096
DEPTH = 2
DEC_BATCH = 128
DEC_SEQ = 8
N_META = 16
WINDOW = 128
BLOCK = 128
PAD_LEN = BLOCK - N_META
N_HEADS = 8
N_KV_HEADS = 2
HEAD_DIM = 64
ATT_SCALE = HEAD_DIM ** -0.5
ATT_Q_W = N_HEADS * HEAD_DIM
ATT_KV_W = N_KV_HEADS * HEAD_DIM
GLA_HEADS = 4
GLA_DK = 64
GLA_DV = 128
GLA_K_W = GLA_HEADS * GLA_DK
GLA_V_W = GLA_HEADS * GLA_DV
GATE_RANK = 16
GATE_NORMALIZER = 16.0
D_MIX = ATT_Q_W + GLA_V_W
D_FFN = 2816
IN_W = 2320
RMS_EPS = 1e-6
MASK_VALUE = -1e30

LANES = 128
N_BLOCKS = 1 + SEQ // BLOCK
MAIN_ROWS = BATCH * SEQ
SAMPLE_ROWS = DEC_BATCH * DEC_SEQ
LEAD_ROWS = BATCH * BLOCK
TOTAL_ROWS = MAIN_ROWS + SAMPLE_ROWS + LEAD_ROWS
ROW_TILE = 512
MAIN_TILES = MAIN_ROWS // ROW_TILE
SAMPLE_TILES = SAMPLE_ROWS // ROW_TILE
N_TILES = TOTAL_ROWS // ROW_TILE
LEAD_TILE = N_TILES - 1
SEQ_GROUP = BLOCK // DEC_SEQ
N_LEVELS = 7
LOW_LEVELS = 3
LOG_DK = 6
N_PREP = 8
W_ROWS = D_MODEL // N_PREP
WD_ROWS = D_FFN // N_PREP

SRC_LOW, SRC_OG = 1792, 1808
OFF_Q, OFF_K, OFF_V = 0, 512, 640
OFF_GQ, OFF_GK, OFF_GV, OFF_OG, OFF_LOW = 768, 1024, 1280, 1792, 2304
PROJ_W = OFF_LOW + LANES
PROJ_CHUNK = 512
FFN_CHUNK = 256
VMEM_LIMIT = 56 * 1024 * 1024

V_NORM1, V_NORM2, V_QG, V_KG, V_BG, V_GN, V_ROWS = 0, 1, 2, 3, 4, 5, 8

PF_K, PF_V, PF_GQ, PF_GK, PF_LD, PF_OG, PF_W = 0, 128, 256, 512, 768, 1024, 1536
PB_Q, PB_GV, PB_W = 0, 512, 1024


def _proj_views(pf_ref, pb_ref):
    f = lambda a, b: pf_ref.at[:, a:b]
    return (pb_ref.at[:, PB_Q:PB_GV], f(PF_K, PF_V), f(PF_V, PF_GQ), f(PF_GQ, PF_GK), f(PF_GK, PF_LD),
            f(PF_LD, PF_OG), pb_ref.at[:, PB_GV:PB_W], f(PF_OG, PF_W))


def _dot(a, b):
    return jnp.dot(a, b, preferred_element_type=F32)


def _dot_nt(a, b):
    return lax.dot_general(a, b, (((1,), (1,)), ((), ())), preferred_element_type=F32)


def _dot_tn(a, b):
    return lax.dot_general(a, b, (((0,), (0,)), ((), ())), preferred_element_type=F32)


def _split(x):
    hi = x.astype(BF16)
    lo = (x - hi.astype(F32)).astype(BF16)
    return hi, lo


def _sigmoid(x):
    return 1.0 / (1.0 + jnp.exp(-x))


def _level_matrix(levels, n=BLOCK):
    out = np.zeros((len(levels) * n, n), np.float32)
    for i, l in enumerate(levels):
        size = 2 << l
        for t in range(n):
            mid = (t // size) * size + size // 2 - 1
            if (t >> l) & 1:
                out[i * n + t, mid + 1:t + 1] = 1.0
            else:
                out[i * n + t, t + 1:mid + 1] = 1.0
    return out


def _constants():
    r = np.arange(BLOCK)
    tri = (r[None, :] <= r[:, None]).astype(np.float32)
    same_seq = (r[None, :] // DEC_SEQ) == (r[:, None] // DEC_SEQ)
    diff = np.maximum(r[:, None] ^ r[None, :], 1)
    pair_level = np.where(r[None, :] < r[:, None], np.floor(np.log2(diff)).astype(np.int32),
                          np.where(r[None, :] == r[:, None], N_LEVELS, N_LEVELS + 1)).astype(np.int32)
    c = np.arange(2 * LANES)
    bd256 = ((c[None, :] // HEAD_DIM) == (c[:, None] // HEAD_DIM)).astype(np.float32)
    return dict(
        tri=jnp.asarray(tri, BF16),
        seq_tri=jnp.asarray(tri * same_seq, BF16),
        seq_ones=jnp.asarray(same_seq.astype(np.float32), BF16),
        lev_all=jnp.asarray(_level_matrix(range(N_LEVELS)), BF16),
        lev_low=jnp.asarray(_level_matrix(range(LOW_LEVELS)), BF16),
        bd256=jnp.asarray(bd256, BF16),
        bd128=jnp.asarray(bd256[:LANES, :LANES], BF16),
        pair_level=jnp.asarray(pair_level),
    )


def _head_masks(rows):
    lane = lax.broadcasted_iota(jnp.int32, (rows, GLA_K_W), 1)
    return [(lane >> LOG_DK) == h for h in range(GLA_HEADS)]


def _gla_intra(gq, gk, decay_levels, levels, pair_level):
    rows = gq[0].shape[0]
    streams = range(len(gq))
    low_half = lax.broadcasted_iota(jnp.int32, (rows, LANES), 1) < GLA_DK

    def pair_products(qh, kh):
        qh, kh = qh.astype(BF16), kh.astype(BF16)
        out = []
        for pair in range(GLA_HEADS // 2):
            kl = kh[:, pair * LANES:(pair + 1) * LANES]
            zero = jnp.zeros_like(kl)
            stacked = jnp.concatenate([jnp.where(low_half, kl, zero), jnp.where(low_half, zero, kl)], axis=0)
            p = _dot_nt(qh[:, pair * LANES:(pair + 1) * LANES], stacked)
            out += [p[:, :rows], p[:, rows:]]
        return out

    on_diag = pair_level == N_LEVELS
    a = [[jnp.where(on_diag, p, 0.0) for p in pair_products(gq[b], gk[b])] for b in streams]
    for i, l in enumerate(levels):
        at_level = pair_level == l
        for b in streams:
            e = decay_levels[b][i * rows:(i + 1) * rows]
            a[b] = [jnp.where(at_level, p, a_h)
                    for p, a_h in zip(pair_products(gq[b] * e, gk[b] * e), a[b])]
    return a


def _gla_merge(o, og, gn):
    outs = []
    for h in range(GLA_HEADS):
        oh = o[:, h * GLA_DV:(h + 1) * GLA_DV]
        gh = og[:, h * GLA_DV:(h + 1) * GLA_DV]
        ms = jnp.mean(oh * oh, axis=-1, keepdims=True)
        outs.append(oh * lax.rsqrt(ms + RMS_EPS) * gn * (gh * _sigmoid(gh)))
    return outs


def _dup_halves(x):
    low = lax.broadcasted_iota(jnp.int32, x.shape, 1) < HEAD_DIM
    rolled = pltpu.roll(x, HEAD_DIM, axis=1)
    return jnp.where(low, x, rolled), jnp.where(low, rolled, x)


def _convert_proj_weights(i, w_ref, wg2_ref, wbf_ref, wg2s_ref):
    rows = pl.ds(pl.multiple_of(i * W_ROWS, W_ROWS), W_ROWS)
    chunk = w_ref[0]
    wbf_ref[rows, 0:OFF_OG] = chunk[:, 0:SRC_LOW].astype(BF16)
    wbf_ref[rows, OFF_OG:OFF_LOW] = chunk[:, SRC_OG:IN_W].astype(BF16)
    lane = lax.broadcasted_iota(jnp.int32, (W_ROWS, LANES), 1)
    low = jnp.where(lane < GATE_RANK, chunk[:, SRC_LOW:SRC_LOW + LANES], 0.0)
    wbf_ref[rows, OFF_LOW:PROJ_W] = low.astype(BF16)

    @pl.when(i == 0)
    def _():
        wg2s_ref[...] = jnp.concatenate(
            [wg2_ref[0], jnp.zeros((LANES - GATE_RANK, GLA_K_W), F32)], axis=0).astype(BF16)


def _project_tile(x, vec_ref, wbf_ref, wg2s_ref, bd256_ref, bd128_ref, z_ref, pf_ref, pb_ref):
    q_ref, k_ref, v_ref, gq_ref, gk_ref, ld_ref, gv_ref, og_ref = _proj_views(pf_ref, pb_ref)
    ms = jnp.mean(x * x, axis=-1, keepdims=True)
    h = (x * lax.rsqrt(ms + RMS_EPS) * vec_ref[V_NORM1:V_NORM1 + 1, :]).astype(BF16)

    def q_part():
        q = z_ref[:, OFF_Q:OFF_K]
        q2 = (q * q).astype(BF16)
        bd = bd256_ref[...]
        ssq = jnp.concatenate([_dot(q2[:, :256], bd), _dot(q2[:, 256:], bd)], axis=1)
        q_ref[...] = (q * lax.rsqrt(ssq * (1.0 / HEAD_DIM) + RMS_EPS)
                      * vec_ref[V_QG:V_QG + 1, :ATT_Q_W]).astype(BF16)

    def kv_part():
        k = z_ref[:, OFF_K:OFF_V]
        ssk = _dot((k * k).astype(BF16), bd128_ref[...])
        k_ref[...] = k * lax.rsqrt(ssk * (1.0 / HEAD_DIM) + RMS_EPS) * vec_ref[V_KG:V_KG + 1, :ATT_KV_W]
        v_ref[...] = z_ref[:, OFF_V:OFF_GQ]
        gq_ref[...] = z_ref[:, OFF_GQ:OFF_GK] * (GLA_DK ** -0.5)

    def gk_part():
        gk_ref[...] = z_ref[:, OFF_GK:OFF_GV]

    def gv_part():
        gv_ref[...] = z_ref[:, OFF_GV:OFF_OG].astype(BF16)

    def tail_part():
        og_ref[...] = z_ref[:, OFF_OG:OFF_LOW]
        logit = (_dot(z_ref[:, OFF_LOW:PROJ_W].astype(BF16), wg2s_ref[...])
                 + vec_ref[V_BG:V_BG + 1, :GLA_K_W])
        log_sig = jnp.minimum(logit, 0.0) - jnp.log1p(jnp.exp(-jnp.abs(logit)))
        ld_ref[...] = log_sig * (1.0 / GATE_NORMALIZER)

    pieces = [q_part, kv_part, gk_part, gv_part, tail_part]
    for c, piece in enumerate(pieces):
        cols = slice(c * PROJ_CHUNK, min((c + 1) * PROJ_CHUNK, PROJ_W))
        z_ref[:, cols] = _dot(h, wbf_ref[:, cols])
        piece()


def _attention_tile(blk, layer, sinks_ref, pf_ref, pb_ref, m_ref, kprev_ref, vprev_ref):
    q_ref, k_ref, v_ref = _proj_views(pf_ref, pb_ref)[:3]
    per_group = N_HEADS // N_KV_HEADS
    grows = per_group * BLOCK
    row = lax.broadcasted_iota(jnp.int32, (grows, BLOCK), 0) & (BLOCK - 1)
    col = lax.broadcasted_iota(jnp.int32, (grows, BLOCK), 1)
    own = col <= row
    kpos = jnp.where(own, blk * BLOCK, (blk - 1) * BLOCK) + col - PAD_LEN
    live = kpos >= 0
    low_half = lax.broadcasted_iota(jnp.int32, (BLOCK, LANES), 1) < HEAD_DIM
    head_of_row = lax.broadcasted_iota(jnp.int32, (grows, 1), 0) >> N_LEVELS
    sink_cols = []
    for g in range(N_KV_HEADS):
        sink = jnp.zeros((grows, 1), F32)
        for r in range(per_group):
            sink = jnp.where(head_of_row == r, sinks_ref[layer, g * per_group + r], sink)
        sink_cols.append(sink)

    units =[(b, g) for b in range(BATCH) for g in range(N_KV_HEADS)]
    rows = [slice(b * BLOCK, (b + 1) * BLOCK) for b in range(BATCH)]
    tiles = [range(g * per_group // 2, (g + 1) * per_group // 2) for g in range(N_KV_HEADS)]
    kdup = [[x.astype(BF16) for x in _dup_halves(k_ref[rows[b], :])] for b in range(BATCH)]
    vdup = [[x.astype(BF16) for x in _dup_halves(v_ref[rows[b], :])] for b in range(BATCH)]
    kk = [jnp.concatenate([kprev_ref[b * N_KV_HEADS + g], kdup[b][g]], axis=0) for b, g in units]
    vv = [jnp.concatenate([vprev_ref[b * N_KV_HEADS + g], vdup[b][g]], axis=0) for b, g in units]
    for b, g in units:
        kprev_ref[b * N_KV_HEADS + g] = kdup[b][g]
        vprev_ref[b * N_KV_HEADS + g] = vdup[b][g]

    def stacked_queries(b, g):
        pieces = []
        for j in tiles[g]:
            qt = q_ref[rows[b], j * LANES:(j + 1) * LANES]
            zero = jnp.zeros_like(qt)
            pieces += [jnp.where(low_half, qt, zero), jnp.where(low_half, zero, qt)]
        return jnp.concatenate(pieces, axis=0)

    for u, (b, g) in enumerate(units):
        s = _dot_nt(stacked_queries(b, g), kk[u])
        s = jnp.where(live, jnp.where(own, s[:, BLOCK:], s[:, :BLOCK]), MASK_VALUE)
        m = jnp.maximum(jnp.max(s, axis=-1, keepdims=True), sink_cols[g])
        p = jnp.exp(s - m)
        denom = jnp.sum(p, axis=-1, keepdims=True) + jnp.exp(sink_cols[g] - m)
        p2 = jnp.concatenate([jnp.where(own, 0.0, p), jnp.where(own, p, 0.0)], axis=1).astype(BF16)
        o = _dot(p2, vv[u]) / denom
        for n, j in enumerate(tiles[g]):
            o_lo = o[2 * n * BLOCK:(2 * n + 1) * BLOCK]
            o_hi = o[(2 * n + 1) * BLOCK:(2 * n + 2) * BLOCK]
            m_ref[rows[b], j * LANES:(j + 1) * LANES] = jnp.where(low_half, o_lo, o_hi).astype(BF16)


def _gla_tile(blk, pf_ref, pb_ref, vec_ref, tri_ref, lev_ref, ones_ref, plev_ref, m_ref, state_ref, sbd_ref):
    gq_ref, gk_ref, ld_ref, gv_ref, og_ref = _proj_views(pf_ref, pb_ref)[3:]
    rpos = blk * BLOCK + lax.broadcasted_iota(jnp.int32, (BLOCK, 1), 0) - PAD_LEN
    valid = (rpos >= 0).astype(F32)
    ld_all = jnp.concatenate([ld_ref[b * BLOCK:(b + 1) * BLOCK, :] for b in range(BATCH)], axis=1)
    hi, lo = _split(ld_all)
    tri = tri_ref[...]
    g_cum_all = _dot(tri, hi) + _dot(tri, lo)
    lev = lev_ref[...]
    decay_all = jnp.exp(_dot(lev, hi) + _dot(lev, lo))
    ones = ones_ref[...]
    decay_col_all = jnp.exp(_dot_tn(hi, ones) + _dot_tn(lo, ones))
    pair_level = plev_ref[...]
    gn = vec_ref[V_GN:V_GN + 1, :GLA_DV]
    streams = range(BATCH)
    rows = [slice(b * BLOCK, (b + 1) * BLOCK) for b in streams]
    cols = [slice(b * GLA_K_W, (b + 1) * GLA_K_W) for b in streams]
    gq = [gq_ref[rows[b], :] for b in streams]
    gk = [gk_ref[rows[b], :] * valid for b in streams]
    gv = [gv_ref[rows[b], :] for b in streams]
    g_cum = [g_cum_all[:, cols[b]] for b in streams]
    a = _gla_intra(gq, gk, [decay_all[:, cols[b]] for b in streams], range(N_LEVELS), pair_level)

    pairs = range(GLA_HEADS // 2)
    q_dec = [(gq[b] * jnp.exp(g_cum[b])).astype(BF16) for b in streams]
    o = [jnp.concatenate([_dot(q_dec[b][:, i * LANES:(i + 1) * LANES], sbd_ref[2 * b + i]) for i in pairs], axis=1)
         for b in streams]
    o = [o[b] + jnp.concatenate(
        [_dot(a[b][h].astype(BF16), gv[b][:, h * GLA_DV:(h + 1) * GLA_DV]) for h in range(GLA_HEADS)], axis=1)
        for b in streams]
    for b in streams:
        for h, gh in enumerate(_gla_merge(o[b], og_ref[rows[b], :], gn)):
            m_ref[rows[b], ATT_Q_W + h * GLA_DV:ATT_Q_W + (h + 1) * GLA_DV] = gh.astype(BF16)

    k_dec = [(gk[b] * jnp.exp(g_cum[b][BLOCK - 1:BLOCK, :] - g_cum[b])).astype(BF16) for b in streams]
    kv = [[_dot_tn(k_dec[b][:, i * LANES:(i + 1) * LANES], gv[b][:, 2 * i * GLA_DV:(2 * i + 2) * GLA_DV])
           for i in pairs] for b in streams]
    for b in streams:
        new_state = decay_col_all[cols[b], :] * state_ref[b] + jnp.concatenate(
            [kv[b][h // 2][(h % 2) * GLA_DK:(h % 2 + 1) * GLA_DK, (h % 2) * GLA_DV:(h % 2 + 1) * GLA_DV]
             for h in range(GLA_HEADS)], axis=0)
        state_ref[b] = new_state
        for h in range(GLA_HEADS):
            sbd_ref[2 * b + h // 2, (h % 2) * GLA_DK:(h % 2 + 1) * GLA_DK,
                    (h % 2) * GLA_DV:(h % 2 + 1) * GLA_DV] = new_state[h * GLA_DK:(h + 1) * GLA_DK].astype(BF16)


def _front_kernel(*refs, layer, n_src, chained):
    sinks_ref, refs = refs[0], refs[1:]
    if chained:
        refs = refs[3:]
    x_refs, refs = refs[:n_src], refs[n_src:]
    vec_ref, w_ref, wg2_ref, bd256_ref, bd128_ref, tri_ref, lev_ref, ones_ref, plev_ref = refs[:9]
    refs = refs[9:]
    if n_src > 1:
        xo_ref, refs = refs[0], refs[1:]
    (m_ref, pfs_ref, pbs_ref, s_out_ref, k_out_ref, v_out_ref,
     wbf_ref, wg2s_ref, z_ref, pf_ref, pb_ref, kprev_ref, vprev_ref, state_ref, sbd_ref) = refs
    i = pl.program_id(0)
    t = i - N_PREP

    @pl.when(i < N_PREP)
    def _():
        _convert_proj_weights(i, w_ref, wg2_ref, wbf_ref, wg2s_ref)

    @pl.when(i == 0)
    def _():
        kprev_ref[...] = jnp.zeros_like(kprev_ref)
        vprev_ref[...] = jnp.zeros_like(vprev_ref)
        state_ref[...] = jnp.zeros_like(state_ref)
        sbd_ref[...] = jnp.zeros_like(sbd_ref)

    def load_x():
        if n_src == 1:
            return x_refs[0][...]
        x = jnp.where(t == 0, x_refs[2][...],
                      jnp.where(t < N_BLOCKS, x_refs[0][...].reshape(ROW_TILE, D_MODEL), x_refs[1][...]))
        xo_ref[...] = x
        return x

    @pl.when((t >= 0) & (t < N_BLOCKS))
    def _():
        _project_tile(load_x(), vec_ref, wbf_ref, wg2s_ref, bd256_ref, bd128_ref, z_ref, pf_ref, pb_ref)
        _attention_tile(t, layer, sinks_ref, pf_ref, pb_ref, m_ref, kprev_ref, vprev_ref)
        _gla_tile(t, pf_ref, pb_ref, vec_ref, tri_ref, lev_ref, ones_ref, plev_ref, m_ref, state_ref, sbd_ref)

    @pl.when(t == N_BLOCKS - 1)
    def _():
        s_out_ref[0] = state_ref[...]
        k_out_ref[0] = pf_ref[:, PF_K:PF_V].reshape(BATCH, BLOCK, ATT_KV_W)
        v_out_ref[0] = pf_ref[:, PF_V:PF_GQ].reshape(BATCH, BLOCK, ATT_KV_W)

    @pl.when(t >= N_BLOCKS)
    def _():
        _project_tile(load_x(), vec_ref, wbf_ref, wg2s_ref, bd256_ref, bd128_ref, z_ref, pfs_ref, pbs_ref)


def _front(layer, sinks, x_srcs, vecs, w_in, w_g2, consts, prev):
    n_src = len(x_srcs)
    step = lambda i: i - N_PREP
    tile_of = lambda i: jnp.where(step(i) <= 0, LEAD_TILE, jnp.minimum(step(i) - 1, LEAD_TILE - 1))
    prompt_tile_of = lambda i: jnp.where(step(i) <= 0, LEAD_TILE, jnp.minimum(step(i) - 1, MAIN_TILES - 1))
    sample_tile_of = lambda i: jnp.clip(step(i) - N_BLOCKS, 0, SAMPLE_TILES - 1)
    full = lambda a: pl.BlockSpec(a.shape, lambda i, *_: (0,) * a.ndim)
    per_batch = lambda r, c: pl.BlockSpec((1, BATCH, r, c), lambda i, *_: (layer, 0, 0, 0))
    if n_src == 1:
        x_specs = [pl.BlockSpec((ROW_TILE, D_MODEL), lambda i, *_: (tile_of(i), 0))]
    else:
        x_specs = [
            pl.BlockSpec((BATCH, BLOCK, D_MODEL), lambda i, *_: (0, jnp.clip(step(i) - 1, 0, MAIN_TILES - 1), 0)),
            pl.BlockSpec((ROW_TILE, D_MODEL), lambda i, *_: (sample_tile_of(i), 0)),
            pl.BlockSpec((ROW_TILE, D_MODEL), lambda i, *_: (0, 0)),
        ]
    ones = jnp.ones((BLOCK, LANES), BF16)
    const_args = (consts["bd256"], consts["bd128"], consts["tri"], consts["lev_all"], ones, consts["pair_level"])
    chained = prev is not None
    prev_args = tuple(prev) if chained else ()
    in_specs = ([pl.BlockSpec(memory_space=pl.ANY)] * len(prev_args) + x_specs + [
        pl.BlockSpec((V_ROWS, D_MODEL), lambda i, *_: (layer, 0)),
        pl.BlockSpec((1, W_ROWS, IN_W), lambda i, *_: (layer, jnp.minimum(i, N_PREP - 1), 0)),
        pl.BlockSpec((1, GATE_RANK, GLA_K_W), lambda i, *_: (layer, 0, 0)),
    ] + [full(a) for a in const_args])
    out_specs = [pl.BlockSpec((ROW_TILE, D_MIX), lambda i, *_: (prompt_tile_of(i), 0)),
                 pl.BlockSpec((ROW_TILE, PF_W), lambda i, *_: (sample_tile_of(i), 0)),
                 pl.BlockSpec((ROW_TILE, PB_W), lambda i, *_: (sample_tile_of(i), 0)),
                 per_batch(GLA_K_W, GLA_DV), per_batch(BLOCK, ATT_KV_W), per_batch(BLOCK, ATT_KV_W)]
    out_shape = [jax.ShapeDtypeStruct((TOTAL_ROWS, D_MIX), BF16),
                 jax.ShapeDtypeStruct((SAMPLE_ROWS, PF_W), F32),
                 jax.ShapeDtypeStruct((SAMPLE_ROWS, PB_W), BF16),
                 jax.ShapeDtypeStruct((DEPTH, BATCH, GLA_K_W, GLA_DV), F32),
                 jax.ShapeDtypeStruct((DEPTH, BATCH, BLOCK, ATT_KV_W), F32),
                 jax.ShapeDtypeStruct((DEPTH, BATCH, BLOCK, ATT_KV_W), F32)]
    n_lead_out = 0
    if n_src > 1:
        out_specs = [pl.BlockSpec((ROW_TILE, D_MODEL), lambda i, *_: (tile_of(i), 0))] + out_specs
        out_shape = [jax.ShapeDtypeStruct((TOTAL_ROWS, D_MODEL), F32)] + out_shape
        n_lead_out = 1
    grid_spec = pltpu.PrefetchScalarGridSpec(
        num_scalar_prefetch=1,
        grid=(N_PREP + N_BLOCKS + SAMPLE_TILES,),
        in_specs=in_specs,
        out_specs=out_specs,
        scratch_shapes=[pltpu.VMEM((D_MODEL, PROJ_W), BF16), pltpu.VMEM((LANES, GLA_K_W), BF16),
                        pltpu.VMEM((ROW_TILE, PROJ_W), F32),
                        pltpu.VMEM((ROW_TILE, PF_W), F32), pltpu.VMEM((ROW_TILE, PB_W), BF16),
                        pltpu.VMEM((BATCH * N_KV_HEADS, BLOCK, ATT_KV_W), BF16),
                        pltpu.VMEM((BATCH * N_KV_HEADS, BLOCK, ATT_KV_W), BF16),
                        pltpu.VMEM((BATCH, GLA_K_W, GLA_DV), F32),
                        pltpu.VMEM((BATCH * GLA_HEADS // 2, 2 * GLA_DK, 2 * GLA_DV), BF16)],
    )
    res = pl.pallas_call(
        functools.partial(_front_kernel, layer=layer, n_src=n_src, chained=chained),
        grid_spec=grid_spec,
        out_shape=out_shape,
        input_output_aliases={1 + n: n_lead_out + 3 + n for n in range(len(prev_args))},
        compiler_params=pltpu.CompilerParams(dimension_semantics=("arbitrary",),
                                             vmem_limit_bytes=VMEM_LIMIT),
        name="front",
    )(sinks, *prev_args, *x_srcs, vecs, w_in, w_g2, *const_args)
    x_all = res[0] if n_src > 1 else x_srcs[0]
    merged, pfs, pbs, ps, pk, pv = res[n_lead_out:]
    return x_all, merged, (pfs, pbs), (ps, pk, pv)


def _mix_sample_kernel(*refs, layer, n_alias):
    sinks_ref, refs = refs[0], refs[n_alias + 1:]
    (pf_ref, pb_ref, ck_ref, cv_ref, st_ref,
     vec_ref, tri_ref, sones_ref, lev_ref, ones_ref, plev_ref,
     m_ref, ck_out_ref, cv_out_ref, st_out_ref) = refs
    q_ref, k_ref, v_ref, gq_ref, gk_ref, ld_ref, gv_ref, og_ref = _proj_views(pf_ref, pb_ref)
    gq = gq_ref[...]
    gk = gk_ref[...]
    gv = gv_ref[...]
    gvf = gv.astype(F32)
    hi, lo = _split(ld_ref[...])
    tri = tri_ref[...]
    g_cum = _dot(tri, hi) + _dot(tri, lo)
    sones = sones_ref[...]
    g_tot = _dot(sones, hi) + _dot(sones, lo)
    lev = lev_ref[...]
    decay_levels = jnp.exp(_dot(lev, hi) + _dot(lev, lo))
    a = _gla_intra([gq], [gk], [decay_levels], range(LOW_LEVELS), plev_ref[...])[0]
    o_intra = jnp.concatenate(
        [_dot(a[h].astype(BF16), gv[:, h * GLA_DV:(h + 1) * GLA_DV]) for h in range(GLA_HEADS)], axis=1)
    q_dec = gq * jnp.exp(g_cum)
    k_dec = gk * jnp.exp(g_tot - g_cum)
    hm8 = _head_masks(DEC_SEQ)
    ones8 = ones_ref[...]
    hi_f, lo_f = hi.astype(F32), lo.astype(F32)

    qf = q_ref[...].astype(F32)
    kf = k_ref[...]
    vf = v_ref[...]
    low8 = lax.broadcasted_iota(jnp.int32, (DEC_SEQ, LANES), 1) < HEAD_DIM
    nkeys = WINDOW + DEC_SEQ
    srow = lax.broadcasted_iota(jnp.int32, (N_HEADS * DEC_SEQ, nkeys), 0)
    scol = lax.broadcasted_iota(jnp.int32, (N_HEADS * DEC_SEQ, nkeys), 1)
    t_of_row = srow & (DEC_SEQ - 1)
    amask = ((scol < WINDOW) & (scol > t_of_row)) | ((scol >= WINDOW) & (scol - WINDOW <= t_of_row))
    rid = lax.broadcasted_iota(jnp.int32, (N_HEADS * DEC_SEQ, 1), 0) >> LOW_LEVELS
    sink_col = jnp.zeros((N_HEADS * DEC_SEQ, 1), F32)
    for i in range(N_HEADS):
        sink_col = jnp.where(rid == i, sinks_ref[layer, i], sink_col)
    half = N_HEADS * DEC_SEQ // N_KV_HEADS

    seqs = range(SEQ_GROUP)
    rows = [slice(b * DEC_SEQ, (b + 1) * DEC_SEQ) for b in seqs]

    def stacked_queries(b):
        pieces = []
        for j in range(N_HEADS // 2):
            qt = qf[rows[b], j * LANES:(j + 1) * LANES]
            pieces += [jnp.where(low8, qt, 0.0), jnp.where(low8, 0.0, qt)]
        return jnp.concatenate(pieces, axis=0).astype(BF16)

    qp = [stacked_queries(b) for b in seqs]
    kk = [jnp.concatenate([ck_ref[0, b], kf[rows[b]]], axis=0) for b in seqs]
    vv = [jnp.concatenate([cv_ref[0, b], vf[rows[b]]], axis=0) for b in seqs]
    for b in seqs:
        ck_out_ref[0, b] = kk[b][DEC_SEQ:]
        cv_out_ref[0, b] = vv[b][DEC_SEQ:]
    kd = [[x.astype(BF16) for x in _dup_halves(kk[b])] for b in seqs]
    vd = [[x.astype(BF16) for x in _dup_halves(vv[b])] for b in seqs]
    s = [jnp.concatenate([_dot_nt(qp[b][:half], kd[b][0]), _dot_nt(qp[b][half:], kd[b][1])], axis=0)
         for b in seqs]
    s = [jnp.where(amask, s[b], MASK_VALUE) for b in seqs]
    m = [jnp.maximum(jnp.max(s[b], axis=-1, keepdims=True), sink_col) for b in seqs]
    p = [jnp.exp(s[b] - m[b]) for b in seqs]
    denom = [jnp.sum(p[b], axis=-1, keepdims=True) + jnp.exp(sink_col - m[b]) for b in seqs]
    pb = [p[b].astype(BF16) for b in seqs]
    ob = [jnp.concatenate([_dot(pb[b][:half], vd[b][0]), _dot(pb[b][half:], vd[b][1])], axis=0) / denom[b]
          for b in seqs]
    att_rows = [jnp.concatenate(
        [jnp.where(low8, ob[b][(2 * j) * DEC_SEQ:(2 * j + 1) * DEC_SEQ],
                   ob[b][(2 * j + 1) * DEC_SEQ:(2 * j + 2) * DEC_SEQ])
         for j in range(N_HEADS // 2)], axis=1) for b in seqs]

    head_stack = lambda x: jnp.concatenate([jnp.where(mk, x, 0.0) for mk in hm8], axis=0).astype(BF16)
    state = [st_ref[0, b] for b in seqs]
    oi = [_dot(head_stack(q_dec[rows[b]]), state[b].astype(BF16)) for b in seqs]
    inter_rows = [jnp.concatenate([oi[b][h * DEC_SEQ:(h + 1) * DEC_SEQ] for h in range(GLA_HEADS)], axis=1)
                  for b in seqs]
    vstack = [jnp.concatenate([gvf[rows[b]][:, h * GLA_DV:(h + 1) * GLA_DV] for h in range(GLA_HEADS)],
                              axis=0).astype(BF16) for b in seqs]
    kv = [_dot_tn(head_stack(k_dec[rows[b]]), vstack[b]) for b in seqs]
    decay_col = [jnp.exp(_dot_tn(hi_f[rows[b]], ones8) + _dot_tn(lo_f[rows[b]], ones8)) for b in seqs]
    for b in seqs:
        st_out_ref[0, b] = decay_col[b] * state[b] + kv[b]

    m_ref[:, :ATT_Q_W] = jnp.concatenate(att_rows, axis=0).astype(BF16)
    o = o_intra + jnp.concatenate(inter_rows, axis=0)
    gn = vec_ref[V_GN:V_GN + 1, :GLA_DV]
    for h, gh in enumerate(_gla_merge(o, og_ref[...], gn)):
        m_ref[:, ATT_Q_W + h * GLA_DV:ATT_Q_W + (h + 1) * GLA_DV] = gh.astype(BF16)


def _mix_sample(layer, sinks, merged, proj, cache_k, cache_v, state, vecs, consts, prev):
    tok = lambda w_: pl.BlockSpec((BLOCK, w_), lambda i, *_: (i, 0))
    merged_blk = pl.BlockSpec((BLOCK, D_MIX), lambda i, *_: (MAIN_ROWS // BLOCK + i, 0))
    full = lambda a: pl.BlockSpec(a.shape, lambda i, *_: (0,) * a.ndim)
    seq = lambda a: pl.BlockSpec((1, SEQ_GROUP) + a.shape[2:], lambda i, *_: (layer, i, 0, 0))
    ones8 = jnp.ones((DEC_SEQ, LANES), F32)
    const_args = (consts["seq_tri"], consts["seq_ones"], consts["lev_low"], ones8, consts["pair_level"])
    seq_args = (cache_k, cache_v, state)
    alias_args = (merged,) + (tuple(prev) if prev is not None else ())
    grid_spec = pltpu.PrefetchScalarGridSpec(
        num_scalar_prefetch=1,
        grid=(DEC_BATCH // SEQ_GROUP,),
        in_specs=([pl.BlockSpec(memory_space=pl.ANY)] * len(alias_args)
                  + [tok(a.shape[1]) for a in proj] + [seq(a) for a in seq_args]
                  + [pl.BlockSpec((V_ROWS, D_MODEL), lambda i, *_: (layer, 0))]
                  + [full(a) for a in const_args]),
        out_specs=[merged_blk] + [seq(a) for a in seq_args],
    )
    return pl.pallas_call(
        functools.partial(_mix_sample_kernel, layer=layer, n_alias=len(alias_args)),
        grid_spec=grid_spec,
        out_shape=[jax.ShapeDtypeStruct(merged.shape, merged.dtype)]
        + [jax.ShapeDtypeStruct(a.shape, a.dtype) for a in seq_args],
        input_output_aliases={1 + n: n for n in range(len(alias_args))},
        compiler_params=pltpu.CompilerParams(dimension_semantics=("arbitrary",),
                                             vmem_limit_bytes=VMEM_LIMIT),
        name="mix_sample",
    )(sinks, *alias_args, *proj, *seq_args, vecs, *const_args)


def _tile_of_step(i):
    return jnp.maximum(i - N_PREP, 0)


def _out_ffn_kernel(*refs, last):
    x_ref, m_ref, vec_ref, wo_ref, wg_ref, wu_ref, wd_ref = refs[:7]
    out_refs = refs[7:9] if last else refs[7:8]
    wo_s, wg_s, wu_s, wd_s, act_ref = refs[-5:]
    i = pl.program_id(0)

    @pl.when(i < N_PREP)
    def _():
        rows = pl.ds(pl.multiple_of(i * W_ROWS, W_ROWS), W_ROWS)
        wo_s[rows, :] = wo_ref[0].astype(BF16)
        wg_s[rows, :] = wg_ref[0].astype(BF16)
        wu_s[rows, :] = wu_ref[0].astype(BF16)
        wd_s[pl.ds(pl.multiple_of(i * WD_ROWS, WD_ROWS), WD_ROWS), :] = wd_ref[0].astype(BF16)

    @pl.when(i >= N_PREP)
    def _():
        x1 = x_ref[...] + _dot(m_ref[...], wo_s[...])
        ms = jnp.mean(x1 * x1, axis=-1, keepdims=True)
        h = (x1 * lax.rsqrt(ms + RMS_EPS) * vec_ref[V_NORM2:V_NORM2 + 1, :]).astype(BF16)
        for c in range(D_FFN // FFN_CHUNK):
            cols = slice(c * FFN_CHUNK, (c + 1) * FFN_CHUNK)
            gate = _dot(h, wg_s[:, cols])
            up = _dot(h, wu_s[:, cols])
            act_ref[:, cols] = (gate * _sigmoid(gate) * up).astype(BF16)
        y = x1 + _dot(act_ref[...], wd_s[...])
        if last:
            t = i - N_PREP

            @pl.when(t < MAIN_TILES)
            def _():
                out_refs[0][...] = y.reshape(BATCH, BLOCK, D_MODEL)

            @pl.when(t >= MAIN_TILES)
            def _():
                out_refs[1][...] = y
        else:
            out_refs[0][...] = y


def _out_ffn(layer, x_all, merged, vecs, w_o, w_gate, w_up, w_down, last):
    tile = lambda w_: pl.BlockSpec((ROW_TILE, w_), lambda i: (_tile_of_step(i), 0))
    chunk = lambda r, c: pl.BlockSpec((1, r, c), lambda i: (layer, jnp.minimum(i, N_PREP - 1), 0))
    if last:
        n_tiles = MAIN_TILES + SAMPLE_TILES
        out_specs = [pl.BlockSpec((BATCH, BLOCK, D_MODEL),
                                  lambda i: (0, jnp.minimum(_tile_of_step(i), MAIN_TILES - 1), 0)),
                     pl.BlockSpec((ROW_TILE, D_MODEL),
                                  lambda i: (jnp.maximum(_tile_of_step(i) - MAIN_TILES, 0), 0))]
        out_shape = [jax.ShapeDtypeStruct((BATCH, SEQ, D_MODEL), F32),
                     jax.ShapeDtypeStruct((SAMPLE_ROWS, D_MODEL), F32)]
    else:
        n_tiles = N_TILES
        out_specs = [tile(D_MODEL)]
        out_shape = [jax.ShapeDtypeStruct((TOTAL_ROWS, D_MODEL), F32)]
    return pl.pallas_call(
        functools.partial(_out_ffn_kernel, last=last),
        grid=(N_PREP + n_tiles,),
        in_specs=[tile(D_MODEL), tile(D_MIX), pl.BlockSpec((V_ROWS, D_MODEL), lambda i: (layer, 0)),
                  chunk(W_ROWS, D_MODEL), chunk(W_ROWS, D_FFN), chunk(W_ROWS, D_FFN),
                  chunk(WD_ROWS, D_MODEL)],
        out_specs=out_specs,
        out_shape=out_shape,
        scratch_shapes=[pltpu.VMEM((D_MIX, D_MODEL), BF16), pltpu.VMEM((D_MODEL, D_FFN), BF16),
                        pltpu.VMEM((D_MODEL, D_FFN), BF16), pltpu.VMEM((D_FFN, D_MODEL), BF16),
                        pltpu.VMEM((ROW_TILE, D_FFN), BF16)],
        compiler_params=pltpu.CompilerParams(dimension_semantics=("arbitrary",),
                                             vmem_limit_bytes=VMEM_LIMIT),
        name="out_ffn",
    )(x_all, merged, vecs, w_o, w_gate, w_up, w_down)


def _vector_slab(norm1, norm2, q_norm, k_norm, b_g, gla_norm):
    pad = lambda a: jnp.pad(a.astype(F32), ((0, 0), (0, D_MODEL - a.shape[1])))
    rows = [norm1.astype(F32), norm2.astype(F32), pad(jnp.tile(q_norm, (1, N_HEADS)) * ATT_SCALE),
            pad(jnp.tile(k_norm, (1, N_KV_HEADS))), pad(b_g), pad(gla_norm)]
    rows += [jnp.zeros((DEPTH, D_MODEL), F32)] * (V_ROWS - len(rows))
    return jnp.stack(rows, axis=1).reshape(DEPTH * V_ROWS, D_MODEL)


def kernel(x_prompt, x_sample, cache_k, cache_v, state_gla, meta, norm1, w_in, q_norm, k_norm, sinks,
           w_g2, b_g, gla_norm, w_o, norm2, w_gate, w_up, w_down):
    consts = _constants()
    dt = x_prompt.dtype
    vecs = _vector_slab(norm1, norm2, q_norm, k_norm, b_g, gla_norm)
    sinks = sinks.astype(F32)
    lead = jnp.tile(jnp.concatenate([jnp.zeros((PAD_LEN, D_MODEL), dt), meta.astype(dt)], axis=0), (BATCH, 1))
    x_srcs = (x_prompt, x_sample.reshape(SAMPLE_ROWS, D_MODEL), lead)
    ck_in = cache_k.reshape(DEPTH, DEC_BATCH, WINDOW, ATT_KV_W)
    cv_in = cache_v.reshape(DEPTH, DEC_BATCH, WINDOW, ATT_KV_W)
    st_in = state_gla.reshape(DEPTH, DEC_BATCH, GLA_K_W, GLA_DV)

    prompt_outs, sample_outs = None, None
    for l in range(DEPTH):
        x_all, merged, proj_sample, prompt_outs = _front(l, sinks, x_srcs, vecs, w_in, w_g2, consts, prompt_outs)
        merged, *sample_outs = _mix_sample(l, sinks, merged, proj_sample, ck_in, cv_in, st_in, vecs, consts,
                                           sample_outs)
        x_srcs = tuple(_out_ffn(l, x_all, merged, vecs, w_o, w_gate, w_up, w_down, last=l == DEPTH - 1))

    y_main, y_sample = x_srcs
    ps, pk, pv = prompt_outs
    sk, sv, ss = sample_outs
    kv5 = lambda a, n: a.reshape(DEPTH, n, WINDOW, N_KV_HEADS, HEAD_DIM)
    st5 = lambda a, n: a.reshape(DEPTH, n, GLA_HEADS, GLA_DK, GLA_DV)
    return (y_main, y_sample.reshape(DEC_BATCH, DEC_SEQ, D_MODEL),
            kv5(pk, BATCH), kv5(pv, BATCH), st5(ps, BATCH), kv5(sk, DEC_BATCH), kv5(sv, DEC_BATCH),
            st5(ss, DEC_BATCH))
```

```python
import functools

import jax
import jax.numpy as jnp
import numpy as np
from jax import lax
from jax.experimental import pallas as pl
from jax.experimental.pallas import tpu as pltpu

F32 = jnp.float32
BF16 = jnp.bfloat16

D_MODEL = 1024
BATCH = 4
SEQ = 4096
DEPTH = 2
DEC_BATCH = 128
DEC_SEQ = 8
N_META = 16
WINDOW = 128
BLOCK = 128
PAD_LEN = BLOCK - N_META
N_HEADS = 8
N_KV_HEADS = 2
HEAD_DIM = 64
ATT_SCALE = HEAD_DIM ** -0.5
ATT_Q_W = N_HEADS * HEAD_DIM
ATT_KV_W = N_KV_HEADS * HEAD_DIM
GLA_HEADS = 4
GLA_DK = 64
GLA_DV = 128
GLA_K_W = GLA_HEADS * GLA_DK
GLA_V_W = GLA_HEADS * GLA_DV
GATE_RANK = 16
GATE_NORMALIZER = 16.0
D_MIX = ATT_Q_W + GLA_V_W
D_FFN = 2816
IN_W = 2320
RMS_EPS = 1e-6
MASK_VALUE = -1e30

LANES = 128
N_BLOCKS = 1 + SEQ // BLOCK
MAIN_ROWS = BATCH * SEQ
SAMPLE_ROWS = DEC_BATCH * DEC_SEQ
LEAD_ROWS = BATCH * BLOCK
TOTAL_ROWS = MAIN_ROWS + SAMPLE_ROWS + LEAD_ROWS
ROW_TILE = 512
MAIN_TILES = MAIN_ROWS // ROW_TILE
SAMPLE_TILES = SAMPLE_ROWS // ROW_TILE
N_TILES = TOTAL_ROWS // ROW_TILE
LEAD_TILE = N_TILES - 1
SEQ_GROUP = BLOCK // DEC_SEQ
N_LEVELS = 7
LOW_LEVELS = 3
LOG_DK = 6
N_PREP = 8
W_ROWS = D_MODEL // N_PREP
WD_ROWS = D_FFN // N_PREP

SRC_LOW, SRC_OG = 1792, 1808
OFF_Q, OFF_K, OFF_V = 0, 512, 640
OFF_GQ, OFF_GK, OFF_GV, OFF_OG, OFF_LOW = 768, 1024, 1280, 1792, 2304
PROJ_W = OFF_LOW + LANES
PROJ_CHUNK = 512
FFN_CHUNK = 256
VMEM_LIMIT = 56 * 1024 * 1024

V_NORM1, V_NORM2, V_QG, V_KG, V_BG, V_GN, V_ROWS = 0, 1, 2, 3, 4, 5, 8

PF_K, PF_V, PF_GQ, PF_GK, PF_LD, PF_OG, PF_W = 0, 128, 256, 512, 768, 1024, 1536
PB_Q, PB_GV, PB_W = 0, 512, 1024


def _proj_views(pf_ref, pb_ref):
    f = lambda a, b: pf_ref.at[:, a:b]
    return (pb_ref.at[:, PB_Q:PB_GV], f(PF_K, PF_V), f(PF_V, PF_GQ), f(PF_GQ, PF_GK), f(PF_GK, PF_LD),
            f(PF_LD, PF_OG), pb_ref.at[:, PB_GV:PB_W], f(PF_OG, PF_W))


def _dot(a, b):
    return jnp.dot(a, b, preferred_element_type=F32)


def _dot_nt(a, b):
    return lax.dot_general(a, b, (((1,), (1,)), ((), ())), preferred_element_type=F32)


def _dot_tn(a, b):
    return lax.dot_general(a, b, (((0,), (0,)), ((), ())), preferred_element_type=F32)


def _split(x):
    hi = x.astype(BF16)
    lo = (x - hi.astype(F32)).astype(BF16)
    return hi, lo


def _sigmoid(x):
    return 1.0 / (1.0 + jnp.exp(-x))


def _level_matrix(levels, n=BLOCK):
    out = np.zeros((len(levels) * n, n), np.float32)
    for i, l in enumerate(levels):
        size = 2 << l
        for t in range(n):
            mid = (t // size) * size + size // 2 - 1
            if (t >> l) & 1:
                out[i * n + t, mid + 1:t + 1] = 1.0
            else:
                out[i * n + t, t + 1:mid + 1] = 1.0
    return out


def _constants():
    r = np.arange(BLOCK)
    tri = (r[None, :] <= r[:, None]).astype(np.float32)
    same_seq = (r[None, :] // DEC_SEQ) == (r[:, None] // DEC_SEQ)
    diff = np.maximum(r[:, None] ^ r[None, :], 1)
    pair_level = np.where(r[None, :] < r[:, None], np.floor(np.log2(diff)).astype(np.int32),
                          np.where(r[None, :] == r[:, None], N_LEVELS, N_LEVELS + 1)).astype(np.int32)
    c = np.arange(2 * LANES)
    bd256 = ((c[None, :] // HEAD_DIM) == (c[:, None] // HEAD_DIM)).astype(np.float32)
    return dict(
        tri=jnp.asarray(tri, BF16),
        seq_tri=jnp.asarray(tri * same_seq, BF16),
        seq_ones=jnp.asarray(same_seq.astype(np.float32), BF16),
        lev_low=jnp.asarray(_level_matrix(range(LOW_LEVELS)), BF16),
        bd256=jnp.asarray(bd256, BF16),
        bd128=jnp.asarray(bd256[:LANES, :LANES], BF16),
        pair_level=jnp.asarray(pair_level),
    )


def _head_masks(rows):
    lane = lax.broadcasted_iota(jnp.int32, (rows, GLA_K_W), 1)
    return [(lane >> LOG_DK) == h for h in range(GLA_HEADS)]


def _gla_intra(gq, gk, decay_levels, levels, pair_level):
    rows = gq[0].shape[0]
    streams = range(len(gq))
    low_half = lax.broadcasted_iota(jnp.int32, (rows, LANES), 1) < GLA_DK

    def pair_products(qh, kh):
        qh, kh = qh.astype(BF16), kh.astype(BF16)
        out = []
        for pair in range(GLA_HEADS // 2):
            kl = kh[:, pair * LANES:(pair + 1) * LANES]
            zero = jnp.zeros_like(kl)
            stacked = jnp.concatenate([jnp.where(low_half, kl, zero), jnp.where(low_half, zero, kl)], axis=0)
            p = _dot_nt(qh[:, pair * LANES:(pair + 1) * LANES], stacked)
            out += [p[:, :rows], p[:, rows:]]
        return out

    row = lax.broadcasted_iota(jnp.int32, (rows, 1), 0)

    def upper_q_lower_k(b, l):
        size = 1 << l
        if size % 8:
            return jnp.where(((row >> l) & 1) == 1, gq[b], gk[b])
        return jnp.concatenate([(gq[b] if i & 1 else gk[b])[i * size:(i + 1) * size]
                                for i in range(rows // size)], axis=0)

    on_diag = pair_level == N_LEVELS
    a = [[jnp.where(on_diag, p, 0.0) for p in pair_products(gq[b], gk[b])] for b in streams]
    for i, l in enumerate(levels):
        at_level = pair_level == l
        for b in streams:
            x = upper_q_lower_k(b, l) * decay_levels[b][i]
            a[b] = [jnp.where(at_level, p, a_h) for p, a_h in zip(pair_products(x, x), a[b])]
    return a


def _gla_merge(o, og, gn):
    outs = []
    for h in range(GLA_HEADS):
        oh = o[:, h * GLA_DV:(h + 1) * GLA_DV]
        gh = og[:, h * GLA_DV:(h + 1) * GLA_DV]
        ms = jnp.mean(oh * oh, axis=-1, keepdims=True)
        outs.append(oh * lax.rsqrt(ms + RMS_EPS) * gn * (gh * _sigmoid(gh)))
    return outs


def _dup_halves(x):
    low = lax.broadcasted_iota(jnp.int32, x.shape, 1) < HEAD_DIM
    rolled = pltpu.roll(x, HEAD_DIM, axis=1)
    return jnp.where(low, x, rolled), jnp.where(low, rolled, x)


def _convert_proj_weights(i, w_ref, wg2_ref, wbf_ref, wg2s_ref):
    rows = pl.ds(pl.multiple_of(i * W_ROWS, W_ROWS), W_ROWS)
    chunk = w_ref[0]
    wbf_ref[rows, 0:OFF_OG] = chunk[:, 0:SRC_LOW].astype(BF16)
    wbf_ref[rows, OFF_OG:OFF_LOW] = chunk[:, SRC_OG:IN_W].astype(BF16)
    lane = lax.broadcasted_iota(jnp.int32, (W_ROWS, LANES), 1)
    low = jnp.where(lane < GATE_RANK, chunk[:, SRC_LOW:SRC_LOW + LANES], 0.0)
    wbf_ref[rows, OFF_LOW:PROJ_W] = low.astype(BF16)

    @pl.when(i == 0)
    def _():
        wg2s_ref[...] = jnp.concatenate(
            [wg2_ref[0], jnp.zeros((LANES - GATE_RANK, GLA_K_W), F32)], axis=0).astype(BF16)


def _project_tile(x, vec_ref, wbf_ref, wg2s_ref, bd256_ref, bd128_ref, z_ref, pf_ref, pb_ref):
    q_ref, k_ref, v_ref, gq_ref, gk_ref, ld_ref, gv_ref, og_ref = _proj_views(pf_ref, pb_ref)
    ms = jnp.mean(x * x, axis=-1, keepdims=True)
    h = (x * lax.rsqrt(ms + RMS_EPS) * vec_ref[V_NORM1:V_NORM1 + 1, :]).astype(BF16)

    def q_part():
        q = z_ref[:, OFF_Q:OFF_K]
        q2 = (q * q).astype(BF16)
        bd = bd256_ref[...]
        ssq = jnp.concatenate([_dot(q2[:, :256], bd), _dot(q2[:, 256:], bd)], axis=1)
        q_ref[...] = (q * lax.rsqrt(ssq * (1.0 / HEAD_DIM) + RMS_EPS)
                      * vec_ref[V_QG:V_QG + 1, :ATT_Q_W]).astype(BF16)

    def kv_part():
        k = z_ref[:, OFF_K:OFF_V]
        ssk = _dot((k * k).astype(BF16), bd128_ref[...])
        k_ref[...] = k * lax.rsqrt(ssk * (1.0 / HEAD_DIM) + RMS_EPS) * vec_ref[V_KG:V_KG + 1, :ATT_KV_W]
        v_ref[...] = z_ref[:, OFF_V:OFF_GQ]
        gq_ref[...] = z_ref[:, OFF_GQ:OFF_GK] * (GLA_DK ** -0.5)

    def gk_part():
        gk_ref[...] = z_ref[:, OFF_GK:OFF_GV]

    def gv_part():
        gv_ref[...] = z_ref[:, OFF_GV:OFF_OG].astype(BF16)

    def tail_part():
        og_ref[...] = z_ref[:, OFF_OG:OFF_LOW]
        logit = (_dot(z_ref[:, OFF_LOW:PROJ_W].astype(BF16), wg2s_ref[...])
                 + vec_ref[V_BG:V_BG + 1, :GLA_K_W])
        log_sig = jnp.minimum(logit, 0.0) - jnp.log1p(jnp.exp(-jnp.abs(logit)))
        ld_ref[...] = log_sig * (1.0 / GATE_NORMALIZER)

    pieces = [q_part, kv_part, gk_part, gv_part, tail_part]
    for c, piece in enumerate(pieces):
        cols = slice(c * PROJ_CHUNK, min((c + 1) * PROJ_CHUNK, PROJ_W))
        z_ref[:, cols] = _dot(h, wbf_ref[:, cols])
        piece()


def _attention_units(blk, layer, sinks_ref, pf_ref, pb_ref, m_ref, kprev_ref, vprev_ref):
    q_ref, k_ref, v_ref = _proj_views(pf_ref, pb_ref)[:3]
    per_group = N_HEADS // N_KV_HEADS
    grows = per_group * BLOCK
    row = lax.broadcasted_iota(jnp.int32, (grows, BLOCK), 0) & (BLOCK - 1)
    col = lax.broadcasted_iota(jnp.int32, (grows, BLOCK), 1)
    own = col <= row
    kpos = jnp.where(own, blk * BLOCK, (blk - 1) * BLOCK) + col - PAD_LEN
    live = kpos >= 0
    low_half = lax.broadcasted_iota(jnp.int32, (BLOCK, LANES), 1) < HEAD_DIM
    head_of_row = lax.broadcasted_iota(jnp.int32, (grows, 1), 0) >> N_LEVELS
    sink_cols = []
    for g in range(N_KV_HEADS):
        sink = jnp.zeros((grows, 1), F32)
        for r in range(per_group):
            sink = jnp.where(head_of_row == r, sinks_ref[layer, g * per_group + r], sink)
        sink_cols.append(sink)

    units =[(b, g) for b in range(BATCH) for g in range(N_KV_HEADS)]
    rows = [slice(b * BLOCK, (b + 1) * BLOCK) for b in range(BATCH)]
    tiles = [range(g * per_group // 2, (g + 1) * per_group // 2) for g in range(N_KV_HEADS)]
    kdup = [[x.astype(BF16) for x in _dup_halves(k_ref[rows[b], :])] for b in range(BATCH)]
    vdup = [[x.astype(BF16) for x in _dup_halves(v_ref[rows[b], :])] for b in range(BATCH)]
    kk = [jnp.concatenate([kprev_ref[b * N_KV_HEADS + g], kdup[b][g]], axis=0) for b, g in units]
    vv = [jnp.concatenate([vprev_ref[b * N_KV_HEADS + g], vdup[b][g]], axis=0) for b, g in units]
    for b, g in units:
        kprev_ref[b * N_KV_HEADS + g] = kdup[b][g]
        vprev_ref[b * N_KV_HEADS + g] = vdup[b][g]

    def stacked_queries(b, g):
        pieces = []
        for j in tiles[g]:
            qt = q_ref[rows[b], j * LANES:(j + 1) * LANES]
            zero = jnp.zeros_like(qt)
            pieces += [jnp.where(low_half, qt, zero), jnp.where(low_half, zero, qt)]
        return jnp.concatenate(pieces, axis=0)

    def attend(u, b, g):
        s = _dot_nt(stacked_queries(b, g), kk[u])
        s = jnp.where(live, jnp.where(own, s[:, BLOCK:], s[:, :BLOCK]), MASK_VALUE)
        m = jnp.maximum(jnp.max(s, axis=-1, keepdims=True), sink_cols[g])
        p = jnp.exp(s - m)
        denom = jnp.sum(p, axis=-1, keepdims=True) + jnp.exp(sink_cols[g] - m)
        p2 = jnp.concatenate([jnp.where(own, 0.0, p), jnp.where(own, p, 0.0)], axis=1).astype(BF16)
        o = _dot(p2, vv[u]) / denom
        for n, j in enumerate(tiles[g]):
            o_lo = o[2 * n * BLOCK:(2 * n + 1) * BLOCK]
            o_hi = o[(2 * n + 1) * BLOCK:(2 * n + 2) * BLOCK]
            m_ref[rows[b], j * LANES:(j + 1) * LANES] = jnp.where(low_half, o_lo, o_hi).astype(BF16)

    return [functools.partial(attend, u, b, g) for u, (b, g) in enumerate(units)]


def _gla_tile(blk, pf_ref, pb_ref, vec_ref, tri_ref, lev_ref, ones_ref, plev_ref, m_ref, state_ref, sbd_ref):
    gq_ref, gk_ref, ld_ref, gv_ref, og_ref = _proj_views(pf_ref, pb_ref)[3:]
    rpos = blk * BLOCK + lax.broadcasted_iota(jnp.int32, (BLOCK, 1), 0) - PAD_LEN
    valid = (rpos >= 0).astype(F32)
    ld_all = jnp.concatenate([ld_ref[b * BLOCK:(b + 1) * BLOCK, :] for b in range(BATCH)], axis=1)
    hi, lo = _split(ld_all)
    tri = tri_ref[...]
    g_cum_all = _dot(tri, hi) + _dot(tri, lo)
    lev = lev_ref[...]
    low_sums = _dot(lev, hi) + _dot(lev, lo)
    level_sums = [low_sums[l * BLOCK:(l + 1) * BLOCK] for l in range(LOW_LEVELS)]
    for l in range(LOW_LEVELS, N_LEVELS):
        half = 1 << l
        pieces = []
        for p in range(BLOCK // (2 * half)):
            mid = g_cum_all[p * 2 * half + half - 1:p * 2 * half + half, :]
            pieces += [mid - g_cum_all[p * 2 * half:p * 2 * half + half],
                       g_cum_all[p * 2 * half + half:(p + 1) * 2 * half] - mid]
        level_sums.append(jnp.concatenate(pieces, axis=0))
    decay_all = [jnp.exp(s) for s in level_sums]
    ones = ones_ref[...]
    decay_col_all = jnp.exp(_dot_tn(hi, ones) + _dot_tn(lo, ones))
    pair_level = plev_ref[...]
    gn = vec_ref[V_GN:V_GN + 1, :GLA_DV]
    streams = range(BATCH)
    rows = [slice(b * BLOCK, (b + 1) * BLOCK) for b in streams]
    cols = [slice(b * GLA_K_W, (b + 1) * GLA_K_W) for b in streams]
    gq = [gq_ref[rows[b], :] for b in streams]
    gk = [gk_ref[rows[b], :] * valid for b in streams]
    gv = [gv_ref[rows[b], :] for b in streams]
    g_cum = [g_cum_all[:, cols[b]] for b in streams]
    a = _gla_intra(gq, gk, [[d[:, cols[b]] for d in decay_all] for b in streams], range(N_LEVELS), pair_level)

    pairs = range(GLA_HEADS // 2)
    q_dec = [(gq[b] * jnp.exp(g_cum[b])).astype(BF16) for b in streams]
    o = [jnp.concatenate([_dot(q_dec[b][:, i * LANES:(i + 1) * LANES], sbd_ref[2 * b + i]) for i in pairs], axis=1)
         for b in streams]
    o = [o[b] + jnp.concatenate(
        [_dot(a[b][h].astype(BF16), gv[b][:, h * GLA_DV:(h + 1) * GLA_DV]) for h in range(GLA_HEADS)], axis=1)
        for b in streams]
    for b in streams:
        for h, gh in enumerate(_gla_merge(o[b], og_ref[rows[b], :], gn)):
            m_ref[rows[b], ATT_Q_W + h * GLA_DV:ATT_Q_W + (h + 1) * GLA_DV] = gh.astype(BF16)

    k_dec = [(gk[b] * jnp.exp(g_cum[b][BLOCK - 1:BLOCK, :] - g_cum[b])).astype(BF16) for b in streams]
    kv = [[_dot_tn(k_dec[b][:, i * LANES:(i + 1) * LANES], gv[b][:, 2 * i * GLA_DV:(2 * i + 2) * GLA_DV])
           for i in pairs] for b in streams]
    for b in streams:
        new_state = decay_col_all[cols[b], :] * state_ref[b] + jnp.concatenate(
            [kv[b][h // 2][(h % 2) * GLA_DK:(h % 2 + 1) * GLA_DK, (h % 2) * GLA_DV:(h % 2 + 1) * GLA_DV]
             for h in range(GLA_HEADS)], axis=0)
        state_ref[b] = new_state
        for h in range(GLA_HEADS):
            sbd_ref[2 * b + h // 2, (h % 2) * GLA_DK:(h % 2 + 1) * GLA_DK,
                    (h % 2) * GLA_DV:(h % 2 + 1) * GLA_DV] = new_state[h * GLA_DK:(h + 1) * GLA_DK].astype(BF16)


def _front_kernel(*refs, layer, n_src, chained):
    sinks_ref, refs = refs[0], refs[1:]
    if chained:
        refs = refs[3:]
    x_refs, refs = refs[:n_src], refs[n_src:]
    vec_ref, w_ref, wg2_ref, bd256_ref, bd128_ref, tri_ref, lev_ref, ones_ref, plev_ref = refs[:9]
    refs = refs[9:]
    if n_src > 1:
        xo_ref, refs = refs[0], refs[1:]
    (m_ref, pfs_ref, pbs_ref, s_out_ref, k_out_ref, v_out_ref,
     wbf_ref, wg2s_ref, z_ref, pf_ref, pb_ref, kprev_ref, vprev_ref, state_ref, sbd_ref) = refs
    i = pl.program_id(0)
    t = i - N_PREP

    @pl.when(i < N_PREP)
    def _():
        _convert_proj_weights(i, w_ref, wg2_ref, wbf_ref, wg2s_ref)

    @pl.when(i == 0)
    def _():
        kprev_ref[...] = jnp.zeros_like(kprev_ref)
        vprev_ref[...] = jnp.zeros_like(vprev_ref)
        state_ref[...] = jnp.zeros_like(state_ref)
        sbd_ref[...] = jnp.zeros_like(sbd_ref)

    def load_x():
        if n_src == 1:
            return x_refs[0][...]
        x = jnp.where(t == 0, x_refs[2][...],
                      jnp.where(t < N_BLOCKS, x_refs[0][...].reshape(ROW_TILE, D_MODEL), x_refs[1][...]))
        xo_ref[...] = x
        return x

    @pl.when((t >= 0) & (t < N_BLOCKS))
    def _():
        _project_tile(load_x(), vec_ref, wbf_ref, wg2s_ref, bd256_ref, bd128_ref, z_ref, pf_ref, pb_ref)
        for attend in _attention_units(t, layer, sinks_ref, pf_ref, pb_ref, m_ref, kprev_ref, vprev_ref):
            attend()
        _gla_tile(t, pf_ref, pb_ref, vec_ref, tri_ref, lev_ref, ones_ref, plev_ref, m_ref, state_ref, sbd_ref)

    @pl.when(t == N_BLOCKS - 1)
    def _():
        s_out_ref[0] = state_ref[...]
        k_out_ref[0] = pf_ref[:, PF_K:PF_V].reshape(BATCH, BLOCK, ATT_KV_W)
        v_out_ref[0] = pf_ref[:, PF_V:PF_GQ].reshape(BATCH, BLOCK, ATT_KV_W)

    @pl.when(t >= N_BLOCKS)
    def _():
        _project_tile(load_x(), vec_ref, wbf_ref, wg2s_ref, bd256_ref, bd128_ref, z_ref, pfs_ref, pbs_ref)


def _front(layer, sinks, x_srcs, vecs, w_in, w_g2, consts, prev):
    n_src = len(x_srcs)
    step = lambda i: i - N_PREP
    tile_of = lambda i: jnp.where(step(i) <= 0, LEAD_TILE, jnp.minimum(step(i) - 1, LEAD_TILE - 1))
    prompt_tile_of = lambda i: jnp.where(step(i) <= 0, LEAD_TILE, jnp.minimum(step(i) - 1, MAIN_TILES - 1))
    sample_tile_of = lambda i: jnp.clip(step(i) - N_BLOCKS, 0, SAMPLE_TILES - 1)
    full = lambda a: pl.BlockSpec(a.shape, lambda i, *_: (0,) * a.ndim)
    per_batch = lambda r, c: pl.BlockSpec((1, BATCH, r, c), lambda i, *_: (layer, 0, 0, 0))
    if n_src == 1:
        x_specs = [pl.BlockSpec((ROW_TILE, D_MODEL), lambda i, *_: (tile_of(i), 0))]
    else:
        x_specs = [
            pl.BlockSpec((BATCH, BLOCK, D_MODEL), lambda i, *_: (0, jnp.clip(step(i) - 1, 0, MAIN_TILES - 1), 0)),
            pl.BlockSpec((ROW_TILE, D_MODEL), lambda i, *_: (sample_tile_of(i), 0)),
            pl.BlockSpec((ROW_TILE, D_MODEL), lambda i, *_: (0, 0)),
        ]
    ones = jnp.ones((BLOCK, LANES), BF16)
    const_args = (consts["bd256"], consts["bd128"], consts["tri"], consts["lev_low"], ones, consts["pair_level"])
    chained = prev is not None
    prev_args = tuple(prev) if chained else ()
    in_specs = ([pl.BlockSpec(memory_space=pl.ANY)] * len(prev_args) + x_specs + [
        pl.BlockSpec((V_ROWS, D_MODEL), lambda i, *_: (layer, 0)),
        pl.BlockSpec((1, W_ROWS, IN_W), lambda i, *_: (layer, jnp.minimum(i, N_PREP - 1), 0)),
        pl.BlockSpec((1, GATE_RANK, GLA_K_W), lambda i, *_: (layer, 0, 0)),
    ] + [full(a) for a in const_args])
    out_specs = [pl.BlockSpec((ROW_TILE, D_MIX), lambda i, *_: (prompt_tile_of(i), 0)),
                 pl.BlockSpec((ROW_TILE, PF_W), lambda i, *_: (sample_tile_of(i), 0)),
                 pl.BlockSpec((ROW_TILE, PB_W), lambda i, *_: (sample_tile_of(i), 0)),
                 per_batch(GLA_K_W, GLA_DV), per_batch(BLOCK, ATT_KV_W), per_batch(BLOCK, ATT_KV_W)]
    out_shape = [jax.ShapeDtypeStruct((TOTAL_ROWS, D_MIX), BF16),
                 jax.ShapeDtypeStruct((SAMPLE_ROWS, PF_W), F32),
                 jax.ShapeDtypeStruct((SAMPLE_ROWS, PB_W), BF16),
                 jax.ShapeDtypeStruct((DEPTH, BATCH, GLA_K_W, GLA_DV), F32),
                 jax.ShapeDtypeStruct((DEPTH, BATCH, BLOCK, ATT_KV_W), F32),
                 jax.ShapeDtypeStruct((DEPTH, BATCH, BLOCK, ATT_KV_W), F32)]
    n_lead_out = 0
    if n_src > 1:
        out_specs = [pl.BlockSpec((ROW_TILE, D_MODEL), lambda i, *_: (tile_of(i), 0))] + out_specs
        out_shape = [jax.ShapeDtypeStruct((TOTAL_ROWS, D_MODEL), F32)] + out_shape
        n_lead_out = 1
    grid_spec = pltpu.PrefetchScalarGridSpec(
        num_scalar_prefetch=1,
        grid=(N_PREP + N_BLOCKS + SAMPLE_TILES,),
        in_specs=in_specs,
        out_specs=out_specs,
        scratch_shapes=[pltpu.VMEM((D_MODEL, PROJ_W), BF16), pltpu.VMEM((LANES, GLA_K_W), BF16),
                        pltpu.VMEM((ROW_TILE, PROJ_W), F32),
                        pltpu.VMEM((ROW_TILE, PF_W), F32), pltpu.VMEM((ROW_TILE, PB_W), BF16),
                        pltpu.VMEM((BATCH * N_KV_HEADS, BLOCK, ATT_KV_W), BF16),
                        pltpu.VMEM((BATCH * N_KV_HEADS, BLOCK, ATT_KV_W), BF16),
                        pltpu.VMEM((BATCH, GLA_K_W, GLA_DV), F32),
                        pltpu.VMEM((BATCH * GLA_HEADS // 2, 2 * GLA_DK, 2 * GLA_DV), BF16)],
    )
    res = pl.pallas_call(
        functools.partial(_front_kernel, layer=layer, n_src=n_src, chained=chained),
        grid_spec=grid_spec,
        out_shape=out_shape,
        input_output_aliases={1 + n: n_lead_out + 3 + n for n in range(len(prev_args))},
        compiler_params=pltpu.CompilerParams(dimension_semantics=("arbitrary",),
                                             vmem_limit_bytes=VMEM_LIMIT),
        name="front",
    )(sinks, *prev_args, *x_srcs, vecs, w_in, w_g2, *const_args)
    x_all = res[0] if n_src > 1 else x_srcs[0]
    merged, pfs, pbs, ps, pk, pv = res[n_lead_out:]
    return x_all, merged, (pfs, pbs), (ps, pk, pv)


def _mix_sample_kernel(*refs, layer, n_alias):
    sinks_ref, refs = refs[0], refs[n_alias + 1:]
    (pf_ref, pb_ref, ck_ref, cv_ref, st_ref,
     vec_ref, tri_ref, sones_ref, lev_ref, ones_ref, plev_ref,
     m_ref, ck_out_ref, cv_out_ref, st_out_ref) = refs
    q_ref, k_ref, v_ref, gq_ref, gk_ref, ld_ref, gv_ref, og_ref = _proj_views(pf_ref, pb_ref)
    gq = gq_ref[...]
    gk = gk_ref[...]
    gv = gv_ref[...]
    gvf = gv.astype(F32)
    hi, lo = _split(ld_ref[...])
    tri = tri_ref[...]
    g_cum = _dot(tri, hi) + _dot(tri, lo)
    sones = sones_ref[...]
    g_tot = _dot(sones, hi) + _dot(sones, lo)
    lev = lev_ref[...]
    decay_levels = jnp.exp(_dot(lev, hi) + _dot(lev, lo))
    a = _gla_intra([gq], [gk], [[decay_levels[l * BLOCK:(l + 1) * BLOCK] for l in range(LOW_LEVELS)]],
                   range(LOW_LEVELS), plev_ref[...])[0]
    o_intra = jnp.concatenate(
        [_dot(a[h].astype(BF16), gv[:, h * GLA_DV:(h + 1) * GLA_DV]) for h in range(GLA_HEADS)], axis=1)
    q_dec = gq * jnp.exp(g_cum)
    k_dec = gk * jnp.exp(g_tot - g_cum)
    hm8 = _head_masks(DEC_SEQ)
    ones8 = ones_ref[...]
    hi_f, lo_f = hi.astype(F32), lo.astype(F32)

    qf = q_ref[...].astype(F32)
    kf = k_ref[...]
    vf = v_ref[...]
    low8 = lax.broadcasted_iota(jnp.int32, (DEC_SEQ, LANES), 1) < HEAD_DIM
    nkeys = WINDOW + DEC_SEQ
    srow = lax.broadcasted_iota(jnp.int32, (N_HEADS * DEC_SEQ, nkeys), 0)
    scol = lax.broadcasted_iota(jnp.int32, (N_HEADS * DEC_SEQ, nkeys), 1)
    t_of_row = srow & (DEC_SEQ - 1)
    amask = ((scol < WINDOW) & (scol > t_of_row)) | ((scol >= WINDOW) & (scol - WINDOW <= t_of_row))
    rid = lax.broadcasted_iota(jnp.int32, (N_HEADS * DEC_SEQ, 1), 0) >> LOW_LEVELS
    sink_col = jnp.zeros((N_HEADS * DEC_SEQ, 1), F32)
    for i in range(N_HEADS):
        sink_col = jnp.where(rid == i, sinks_ref[layer, i], sink_col)
    half = N_HEADS * DEC_SEQ // N_KV_HEADS

    seqs = range(SEQ_GROUP)
    rows = [slice(b * DEC_SEQ, (b + 1) * DEC_SEQ) for b in seqs]

    def stacked_queries(b):
        pieces = []
        for j in range(N_HEADS // 2):
            qt = qf[rows[b], j * LANES:(j + 1) * LANES]
            pieces += [jnp.where(low8, qt, 0.0), jnp.where(low8, 0.0, qt)]
        return jnp.concatenate(pieces, axis=0).astype(BF16)

    qp = [stacked_queries(b) for b in seqs]
    kk = [jnp.concatenate([ck_ref[0, b], kf[rows[b]]], axis=0) for b in seqs]
    vv = [jnp.concatenate([cv_ref[0, b], vf[rows[b]]], axis=0) for b in seqs]
    for b in seqs:
        ck_out_ref[0, b] = kk[b][DEC_SEQ:]
        cv_out_ref[0, b] = vv[b][DEC_SEQ:]
    kd = [[x.astype(BF16) for x in _dup_halves(kk[b])] for b in seqs]
    vd = [[x.astype(BF16) for x in _dup_halves(vv[b])] for b in seqs]
    s = [jnp.concatenate([_dot_nt(qp[b][:half], kd[b][0]), _dot_nt(qp[b][half:], kd[b][1])], axis=0)
         for b in seqs]
    s = [jnp.where(amask, s[b], MASK_VALUE) for b in seqs]
    m = [jnp.maximum(jnp.max(s[b], axis=-1, keepdims=True), sink_col) for b in seqs]
    p = [jnp.exp(s[b] - m[b]) for b in seqs]
    denom = [jnp.sum(p[b], axis=-1, keepdims=True) + jnp.exp(sink_col - m[b]) for b in seqs]
    pb = [p[b].astype(BF16) for b in seqs]
    ob = [jnp.concatenate([_dot(pb[b][:half], vd[b][0]), _dot(pb[b][half:], vd[b][1])], axis=0) / denom[b]
          for b in seqs]
    att_rows = [jnp.concatenate(
        [jnp.where(low8, ob[b][(2 * j) * DEC_SEQ:(2 * j + 1) * DEC_SEQ],
                   ob[b][(2 * j + 1) * DEC_SEQ:(2 * j + 2) * DEC_SEQ])
         for j in range(N_HEADS // 2)], axis=1) for b in seqs]

    head_stack = lambda x: jnp.concatenate([jnp.where(mk, x, 0.0) for mk in hm8], axis=0).astype(BF16)
    state = [st_ref[0, b] for b in seqs]
    oi = [_dot(head_stack(q_dec[rows[b]]), state[b].astype(BF16)) for b in seqs]
    inter_rows = [jnp.concatenate([oi[b][h * DEC_SEQ:(h + 1) * DEC_SEQ] for h in range(GLA_HEADS)], axis=1)
                  for b in seqs]
    vstack = [jnp.concatenate([gvf[rows[b]][:, h * GLA_DV:(h + 1) * GLA_DV] for h in range(GLA_HEADS)],
                              axis=0).astype(BF16) for b in seqs]
    kv = [_dot_tn(head_stack(k_dec[rows[b]]), vstack[b]) for b in seqs]
    decay_col = [jnp.exp(_dot_tn(hi_f[rows[b]], ones8) + _dot_tn(lo_f[rows[b]], ones8)) for b in seqs]
    for b in seqs:
        st_out_ref[0, b] = decay_col[b] * state[b] + kv[b]

    m_ref[:, :ATT_Q_W] = jnp.concatenate(att_rows, axis=0).astype(BF16)
    o = o_intra + jnp.concatenate(inter_rows, axis=0)
    gn = vec_ref[V_GN:V_GN + 1, :GLA_DV]
    for h, gh in enumerate(_gla_merge(o, og_ref[...], gn)):
        m_ref[:, ATT_Q_W + h * GLA_DV:ATT_Q_W + (h + 1) * GLA_DV] = gh.astype(BF16)


def _mix_sample(layer, sinks, merged, proj, cache_k, cache_v, state, vecs, consts, prev):
    tok = lambda w_: pl.BlockSpec((BLOCK, w_), lambda i, *_: (i, 0))
    merged_blk = pl.BlockSpec((BLOCK, D_MIX), lambda i, *_: (MAIN_ROWS // BLOCK + i, 0))
    full = lambda a: pl.BlockSpec(a.shape, lambda i, *_: (0,) * a.ndim)
    seq = lambda a: pl.BlockSpec((1, SEQ_GROUP) + a.shape[2:], lambda i, *_: (layer, i, 0, 0))
    ones8 = jnp.ones((DEC_SEQ, LANES), F32)
    const_args = (consts["seq_tri"], consts["seq_ones"], consts["lev_low"], ones8, consts["pair_level"])
    seq_args = (cache_k, cache_v, state)
    alias_args = (merged,) + (tuple(prev) if prev is not None else ())
    grid_spec = pltpu.PrefetchScalarGridSpec(
        num_scalar_prefetch=1,
        grid=(DEC_BATCH // SEQ_GROUP,),
        in_specs=([pl.BlockSpec(memory_space=pl.ANY)] * len(alias_args)
                  + [tok(a.shape[1]) for a in proj] + [seq(a) for a in seq_args]
                  + [pl.BlockSpec((V_ROWS, D_MODEL), lambda i, *_: (layer, 0))]
                  + [full(a) for a in const_args]),
        out_specs=[merged_blk] + [seq(a) for a in seq_args],
    )
    return pl.pallas_call(
        functools.partial(_mix_sample_kernel, layer=layer, n_alias=len(alias_args)),
        grid_spec=grid_spec,
        out_shape=[jax.ShapeDtypeStruct(merged.shape, merged.dtype)]
        + [jax.ShapeDtypeStruct(a.shape, a.dtype) for a in seq_args],
        input_output_aliases={1 + n: n for n in range(len(alias_args))},
        compiler_params=pltpu.CompilerParams(dimension_semantics=("arbitrary",),
                                             vmem_limit_bytes=VMEM_LIMIT),
        name="mix_sample",
    )(sinks, *alias_args, *proj, *seq_args, vecs, *const_args)


def _tile_of_step(i):
    return jnp.maximum(i - N_PREP, 0)


def _out_ffn_kernel(*refs, last):
    x_ref, m_ref, vec_ref, wo_ref, wg_ref, wu_ref, wd_ref = refs[:7]
    out_refs = refs[7:9] if last else refs[7:8]
    wo_s, wg_s, wu_s, wd_s, act_ref = refs[-5:]
    i = pl.program_id(0)

    @pl.when(i < N_PREP)
    def _():
        rows = pl.ds(pl.multiple_of(i * W_ROWS, W_ROWS), W_ROWS)
        wo_s[rows, :] = wo_ref[0].astype(BF16)
        wg_s[rows, :] = wg_ref[0].astype(BF16)
        wu_s[rows, :] = wu_ref[0].astype(BF16)
        wd_s[pl.ds(pl.multiple_of(i * WD_ROWS, WD_ROWS), WD_ROWS), :] = wd_ref[0].astype(BF16)

    @pl.when(i >= N_PREP)
    def _():
        x1 = x_ref[...] + _dot(m_ref[...], wo_s[...])
        ms = jnp.mean(x1 * x1, axis=-1, keepdims=True)
        h = (x1 * lax.rsqrt(ms + RMS_EPS) * vec_ref[V_NORM2:V_NORM2 + 1, :]).astype(BF16)
        for c in range(D_FFN // FFN_CHUNK):
            cols = slice(c * FFN_CHUNK, (c + 1) * FFN_CHUNK)
            gate = _dot(h, wg_s[:, cols])
            up = _dot(h, wu_s[:, cols])
            act_ref[:, cols] = (gate * _sigmoid(gate) * up).astype(BF16)
        y = x1 + _dot(act_ref[...], wd_s[...])
        if last:
            t = i - N_PREP

            @pl.when(t < MAIN_TILES)
            def _():
                out_refs[0][...] = y.reshape(BATCH, BLOCK, D_MODEL)

            @pl.when(t >= MAIN_TILES)
            def _():
                out_refs[1][...] = y
        else:
            out_refs[0][...] = y


def _out_ffn(layer, x_all, merged, vecs, w_o, w_gate, w_up, w_down, last):
    tile = lambda w_: pl.BlockSpec((ROW_TILE, w_), lambda i: (_tile_of_step(i), 0))
    chunk = lambda r, c: pl.BlockSpec((1, r, c), lambda i: (layer, jnp.minimum(i, N_PREP - 1), 0))
    if last:
        n_tiles = MAIN_TILES + SAMPLE_TILES
        out_specs = [pl.BlockSpec((BATCH, BLOCK, D_MODEL),
                                  lambda i: (0, jnp.minimum(_tile_of_step(i), MAIN_TILES - 1), 0)),
                     pl.BlockSpec((ROW_TILE, D_MODEL),
                                  lambda i: (jnp.maximum(_tile_of_step(i) - MAIN_TILES, 0), 0))]
        out_shape = [jax.ShapeDtypeStruct((BATCH, SEQ, D_MODEL), F32),
                     jax.ShapeDtypeStruct((SAMPLE_ROWS, D_MODEL), F32)]
    else:
        n_tiles = N_TILES
        out_specs = [tile(D_MODEL)]
        out_shape = [jax.ShapeDtypeStruct((TOTAL_ROWS, D_MODEL), F32)]
    return pl.pallas_call(
        functools.partial(_out_ffn_kernel, last=last),
        grid=(N_PREP + n_tiles,),
        in_specs=[tile(D_MODEL), tile(D_MIX), pl.BlockSpec((V_ROWS, D_MODEL), lambda i: (layer, 0)),
                  chunk(W_ROWS, D_MODEL), chunk(W_ROWS, D_FFN), chunk(W_ROWS, D_FFN),
                  chunk(WD_ROWS, D_MODEL)],
        out_specs=out_specs,
        out_shape=out_shape,
        scratch_shapes=[pltpu.VMEM((D_MIX, D_MODEL), BF16), pltpu.VMEM((D_MODEL, D_FFN), BF16),
                        pltpu.VMEM((D_MODEL, D_FFN), BF16), pltpu.VMEM((D_FFN, D_MODEL), BF16),
                        pltpu.VMEM((ROW_TILE, D_FFN), BF16)],
        compiler_params=pltpu.CompilerParams(dimension_semantics=("arbitrary",),
                                             vmem_limit_bytes=VMEM_LIMIT),
        name="out_ffn",
    )(x_all, merged, vecs, w_o, w_gate, w_up, w_down)


def _vector_slab(norm1, norm2, q_norm, k_norm, b_g, gla_norm):
    pad = lambda a: jnp.pad(a.astype(F32), ((0, 0), (0, D_MODEL - a.shape[1])))
    rows = [norm1.astype(F32), norm2.astype(F32), pad(jnp.tile(q_norm, (1, N_HEADS)) * ATT_SCALE),
            pad(jnp.tile(k_norm, (1, N_KV_HEADS))), pad(b_g), pad(gla_norm)]
    rows += [jnp.zeros((DEPTH, D_MODEL), F32)] * (V_ROWS - len(rows))
    return jnp.stack(rows, axis=1).reshape(DEPTH * V_ROWS, D_MODEL)


def kernel(x_prompt, x_sample, cache_k, cache_v, state_gla, meta, norm1, w_in, q_norm, k_norm, sinks,
           w_g2, b_g, gla_norm, w_o, norm2, w_gate, w_up, w_down):
    consts = _constants()
    dt = x_prompt.dtype
    vecs = _vector_slab(norm1, norm2, q_norm, k_norm, b_g, gla_norm)
    sinks = sinks.astype(F32)
    lead = jnp.tile(jnp.concatenate([jnp.zeros((PAD_LEN, D_MODEL), dt), meta.astype(dt)], axis=0), (BATCH, 1))
    x_srcs = (x_prompt, x_sample.reshape(SAMPLE_ROWS, D_MODEL), lead)
    ck_in = cache_k.reshape(DEPTH, DEC_BATCH, WINDOW, ATT_KV_W)
    cv_in = cache_v.reshape(DEPTH, DEC_BATCH, WINDOW, ATT_KV_W)
    st_in = state_gla.reshape(DEPTH, DEC_BATCH, GLA_K_W, GLA_DV)

    prompt_outs, sample_outs = None, None
    for l in range(DEPTH):
        x_all, merged, proj_sample, prompt_outs = _front(l, sinks, x_srcs, vecs, w_in, w_g2, consts, prompt_outs)
        merged, *sample_outs = _mix_sample(l, sinks, merged, proj_sample, ck_in, cv_in, st_in, vecs, consts,
                                           sample_outs)
        x_srcs = tuple(_out_ffn(l, x_all, merged, vecs, w_o, w_gate, w_up, w_down, last=l == DEPTH - 1))

    y_main, y_sample = x_srcs
    ps, pk, pv = prompt_outs
    sk, sv, ss = sample_outs
    kv5 = lambda a, n: a.reshape(DEPTH, n, WINDOW, N_KV_HEADS, HEAD_DIM)
    st5 = lambda a, n: a.reshape(DEPTH, n, GLA_HEADS, GLA_DK, GLA_DV)
    return (y_main, y_sample.reshape(DEC_BATCH, DEC_SEQ, D_MODEL),
            kv5(pk, BATCH), kv5(pv, BATCH), st5(ps, BATCH), kv5(sk, DEC_BATCH), kv5(sv, DEC_BATCH),
            st5(ss, DEC_BATCH))
```

```python
import functools

import jax
import jax.numpy as jnp
import numpy as np
from jax import lax
from jax.experimental import pallas as pl
from jax.experimental.pallas import tpu as pltpu

F32 = jnp.float32
BF16 = jnp.bfloat16

D_MODEL = 1024
BATCH = 4
SEQ = 4096
DEPTH = 2
DEC_BATCH = 128
DEC_SEQ = 8
N_META = 16
WINDOW = 128
BLOCK = 128
PAD_LEN = BLOCK - N_META
N_HEADS = 8
N_KV_HEADS = 2
HEAD_DIM = 64
ATT_SCALE = HEAD_DIM ** -0.5
ATT_Q_W = N_HEADS * HEAD_DIM
ATT_KV_W = N_KV_HEADS * HEAD_DIM
GLA_HEADS = 4
GLA_DK = 64
GLA_DV = 128
GLA_K_W = GLA_HEADS * GLA_DK
GLA_V_W = GLA_HEADS * GLA_DV
GATE_RANK = 16
GATE_NORMALIZER = 16.0
D_MIX = ATT_Q_W + GLA_V_W
D_FFN = 2816
IN_W = 2320
RMS_EPS = 1e-6
MASK_VALUE = -1e30

LANES = 128
N_BLOCKS = 1 + SEQ // BLOCK
MAIN_ROWS = BATCH * SEQ
SAMPLE_ROWS = DEC_BATCH * DEC_SEQ
LEAD_ROWS = BATCH * BLOCK
TOTAL_ROWS = MAIN_ROWS + SAMPLE_ROWS + LEAD_ROWS
ROW_TILE = 512
MAIN_TILES = MAIN_ROWS // ROW_TILE
SAMPLE_TILES = SAMPLE_ROWS // ROW_TILE
N_TILES = TOTAL_ROWS // ROW_TILE
LEAD_TILE = N_TILES - 1
SEQ_GROUP = BLOCK // DEC_SEQ
N_LEVELS = 7
LOW_LEVELS = 3
LOG_DK = 6
N_PREP = 8
W_ROWS = D_MODEL // N_PREP
N_CONV = 16

SRC_LOW, SRC_OG = 1792, 1808
OFF_Q, OFF_K, OFF_V = 0, 512, 640
OFF_GQ, OFF_GK, OFF_GV, OFF_OG, OFF_LOW = 768, 1024, 1280, 1792, 2304
PROJ_W = OFF_LOW + LANES
PROJ_CHUNK = 512
FFN_CHUNK = 256
VMEM_LIMIT = 58 * 1024 * 1024

V_NORM1, V_NORM2, V_QG, V_KG, V_BG, V_GN, V_ROWS = 0, 1, 2, 3, 4, 5, 8

PF_K, PF_V, PF_GQ, PF_GK, PF_LD, PF_OG, PF_W = 0, 128, 256, 512, 768, 1024, 1536
PB_Q, PB_GV, PB_W = 0, 512, 1024


def _proj_views(pf_ref, pb_ref):
    f = lambda a, b: pf_ref.at[:, a:b]
    return (pb_ref.at[:, PB_Q:PB_GV], f(PF_K, PF_V), f(PF_V, PF_GQ), f(PF_GQ, PF_GK), f(PF_GK, PF_LD),
            f(PF_LD, PF_OG), pb_ref.at[:, PB_GV:PB_W], f(PF_OG, PF_W))


def _dot(a, b):
    return jnp.dot(a, b, preferred_element_type=F32)


def _dot_nt(a, b):
    return lax.dot_general(a, b, (((1,), (1,)), ((), ())), preferred_element_type=F32)


def _dot_tn(a, b):
    return lax.dot_general(a, b, (((0,), (0,)), ((), ())), preferred_element_type=F32)


def _split(x):
    hi = x.astype(BF16)
    lo = (x - hi.astype(F32)).astype(BF16)
    return hi, lo


def _sigmoid(x):
    return 1.0 / (1.0 + jnp.exp(-x))


def _level_matrix(levels, n=BLOCK):
    out = np.zeros((len(levels) * n, n), np.float32)
    for i, l in enumerate(levels):
        size = 2 << l
        for t in range(n):
            mid = (t // size) * size + size // 2 - 1
            if (t >> l) & 1:
                out[i * n + t, mid + 1:t + 1] = 1.0
            else:
                out[i * n + t, t + 1:mid + 1] = 1.0
    return out


def _constants():
    r = np.arange(BLOCK)
    tri = (r[None, :] <= r[:, None]).astype(np.float32)
    same_seq = (r[None, :] // DEC_SEQ) == (r[:, None] // DEC_SEQ)
    diff = np.maximum(r[:, None] ^ r[None, :], 1)
    pair_level = np.where(r[None, :] < r[:, None], np.floor(np.log2(diff)).astype(np.int32),
                          np.where(r[None, :] == r[:, None], N_LEVELS, N_LEVELS + 1)).astype(np.int32)
    c = np.arange(2 * LANES)
    bd256 = ((c[None, :] // HEAD_DIM) == (c[:, None] // HEAD_DIM)).astype(np.float32)
    return dict(
        tri=jnp.asarray(tri, BF16),
        seq_tri=jnp.asarray(tri * same_seq, BF16),
        seq_ones=jnp.asarray(same_seq.astype(np.float32), BF16),
        lev_low=jnp.asarray(_level_matrix(range(LOW_LEVELS)), BF16),
        bd256=jnp.asarray(bd256, BF16),
        bd128=jnp.asarray(bd256[:LANES, :LANES], BF16),
        pair_level=jnp.asarray(pair_level),
    )


def _head_masks(rows):
    lane = lax.broadcasted_iota(jnp.int32, (rows, GLA_K_W), 1)
    return [(lane >> LOG_DK) == h for h in range(GLA_HEADS)]


def _gla_intra(gq, gk, decay_levels, levels, pair_level):
    rows = gq[0].shape[0]
    streams = range(len(gq))
    low_half = lax.broadcasted_iota(jnp.int32, (rows, LANES), 1) < GLA_DK

    def pair_products(qh, kh):
        qh, kh = qh.astype(BF16), kh.astype(BF16)
        out = []
        for pair in range(GLA_HEADS // 2):
            kl = kh[:, pair * LANES:(pair + 1) * LANES]
            zero = jnp.zeros_like(kl)
            stacked = jnp.concatenate([jnp.where(low_half, kl, zero), jnp.where(low_half, zero, kl)], axis=0)
            p = _dot_nt(qh[:, pair * LANES:(pair + 1) * LANES], stacked)
            out += [p[:, :rows], p[:, rows:]]
        return out

    row = lax.broadcasted_iota(jnp.int32, (rows, 1), 0)

    def upper_q_lower_k(b, l):
        size = 1 << l
        if size % 8:
            return jnp.where(((row >> l) & 1) == 1, gq[b], gk[b])
        return jnp.concatenate([(gq[b] if i & 1 else gk[b])[i * size:(i + 1) * size]
                                for i in range(rows // size)], axis=0)

    on_diag = pair_level == N_LEVELS
    a = [[jnp.where(on_diag, p, 0.0) for p in pair_products(gq[b], gk[b])] for b in streams]
    for i, l in enumerate(levels):
        at_level = pair_level == l
        for b in streams:
            x = upper_q_lower_k(b, l) * decay_levels[b][i]
            a[b] = [jnp.where(at_level, p, a_h) for p, a_h in zip(pair_products(x, x), a[b])]
    return a


def _gla_merge(o, og, gn):
    outs = []
    for h in range(GLA_HEADS):
        oh = o[:, h * GLA_DV:(h + 1) * GLA_DV]
        gh = og[:, h * GLA_DV:(h + 1) * GLA_DV]
        ms = jnp.mean(oh * oh, axis=-1, keepdims=True)
        outs.append(oh * lax.rsqrt(ms + RMS_EPS) * gn * (gh * _sigmoid(gh)))
    return outs


def _dup_halves(x):
    low = lax.broadcasted_iota(jnp.int32, x.shape, 1) < HEAD_DIM
    rolled = pltpu.roll(x, HEAD_DIM, axis=1)
    return jnp.where(low, x, rolled), jnp.where(low, rolled, x)


def _convert_proj_weights(i, w_ref, wg2_ref, wbf_ref, wg2s_ref):
    rows = pl.ds(pl.multiple_of(i * W_ROWS, W_ROWS), W_ROWS)
    chunk = w_ref[0]
    wbf_ref[rows, 0:OFF_OG] = chunk[:, 0:SRC_LOW].astype(BF16)
    wbf_ref[rows, OFF_OG:OFF_LOW] = chunk[:, SRC_OG:IN_W].astype(BF16)
    lane = lax.broadcasted_iota(jnp.int32, (W_ROWS, LANES), 1)
    low = jnp.where(lane < GATE_RANK, chunk[:, SRC_LOW:SRC_LOW + LANES], 0.0)
    wbf_ref[rows, OFF_LOW:PROJ_W] = low.astype(BF16)

    @pl.when(i == 0)
    def _():
        wg2s_ref[...] = jnp.concatenate(
            [wg2_ref[0], jnp.zeros((LANES - GATE_RANK, GLA_K_W), F32)], axis=0).astype(BF16)


def _project_tile(x, vec_ref, wbf_ref, wg2s_ref, bd256_ref, bd128_ref, z_ref, pf_ref, pb_ref):
    q_ref, k_ref, v_ref, gq_ref, gk_ref, ld_ref, gv_ref, og_ref = _proj_views(pf_ref, pb_ref)
    ms = jnp.mean(x * x, axis=-1, keepdims=True)
    h = (x * lax.rsqrt(ms + RMS_EPS) * vec_ref[V_NORM1:V_NORM1 + 1, :]).astype(BF16)

    def q_part():
        q = z_ref[:, OFF_Q:OFF_K]
        q2 = (q * q).astype(BF16)
        bd = bd256_ref[...]
        ssq = jnp.concatenate([_dot(q2[:, :256], bd), _dot(q2[:, 256:], bd)], axis=1)
        q_ref[...] = (q * lax.rsqrt(ssq * (1.0 / HEAD_DIM) + RMS_EPS)
                      * vec_ref[V_QG:V_QG + 1, :ATT_Q_W]).astype(BF16)

    def kv_part():
        k = z_ref[:, OFF_K:OFF_V]
        ssk = _dot((k * k).astype(BF16), bd128_ref[...])
        k_ref[...] = k * lax.rsqrt(ssk * (1.0 / HEAD_DIM) + RMS_EPS) * vec_ref[V_KG:V_KG + 1, :ATT_KV_W]
        v_ref[...] = z_ref[:, OFF_V:OFF_GQ]
        gq_ref[...] = z_ref[:, OFF_GQ:OFF_GK] * (GLA_DK ** -0.5)

    def gk_part():
        gk_ref[...] = z_ref[:, OFF_GK:OFF_GV]

    def gv_part():
        gv_ref[...] = z_ref[:, OFF_GV:OFF_OG].astype(BF16)

    def tail_part():
        og_ref[...] = z_ref[:, OFF_OG:OFF_LOW]
        logit = (_dot(z_ref[:, OFF_LOW:PROJ_W].astype(BF16), wg2s_ref[...])
                 + vec_ref[V_BG:V_BG + 1, :GLA_K_W])
        log_sig = jnp.minimum(logit, 0.0) - jnp.log1p(jnp.exp(-jnp.abs(logit)))
        ld_ref[...] = log_sig * (1.0 / GATE_NORMALIZER)

    pieces = [q_part, kv_part, gk_part, gv_part, tail_part]
    for c, piece in enumerate(pieces):
        cols = slice(c * PROJ_CHUNK, min((c + 1) * PROJ_CHUNK, PROJ_W))
        z_ref[:, cols] = _dot(h, wbf_ref[:, cols])
        piece()


def _attention_units(blk, layer, sinks_ref, pf_ref, pb_ref, m_ref, kprev_ref, vprev_ref):
    q_ref, k_ref, v_ref = _proj_views(pf_ref, pb_ref)[:3]
    per_group = N_HEADS // N_KV_HEADS
    grows = per_group * BLOCK
    row = lax.broadcasted_iota(jnp.int32, (grows, BLOCK), 0) & (BLOCK - 1)
    col = lax.broadcasted_iota(jnp.int32, (grows, BLOCK), 1)
    own = col <= row
    kpos = jnp.where(own, blk * BLOCK, (blk - 1) * BLOCK) + col - PAD_LEN
    live = kpos >= 0
    low_half = lax.broadcasted_iota(jnp.int32, (BLOCK, LANES), 1) < HEAD_DIM
    head_of_row = lax.broadcasted_iota(jnp.int32, (grows, 1), 0) >> N_LEVELS
    sink_cols = []
    for g in range(N_KV_HEADS):
        sink = jnp.zeros((grows, 1), F32)
        for r in range(per_group):
            sink = jnp.where(head_of_row == r, sinks_ref[layer, g * per_group + r], sink)
        sink_cols.append(sink)

    units =[(b, g) for b in range(BATCH) for g in range(N_KV_HEADS)]
    rows = [slice(b * BLOCK, (b + 1) * BLOCK) for b in range(BATCH)]
    tiles = [range(g * per_group // 2, (g + 1) * per_group // 2) for g in range(N_KV_HEADS)]
    kdup = [[x.astype(BF16) for x in _dup_halves(k_ref[rows[b], :])] for b in range(BATCH)]
    vdup = [[x.astype(BF16) for x in _dup_halves(v_ref[rows[b], :])] for b in range(BATCH)]
    kk = [jnp.concatenate([kprev_ref[b * N_KV_HEADS + g], kdup[b][g]], axis=0) for b, g in units]
    vv = [jnp.concatenate([vprev_ref[b * N_KV_HEADS + g], vdup[b][g]], axis=0) for b, g in units]
    for b, g in units:
        kprev_ref[b * N_KV_HEADS + g] = kdup[b][g]
        vprev_ref[b * N_KV_HEADS + g] = vdup[b][g]

    def stacked_queries(b, g):
        pieces = []
        for j in tiles[g]:
            qt = q_ref[rows[b], j * LANES:(j + 1) * LANES]
            zero = jnp.zeros_like(qt)
            pieces += [jnp.where(low_half, qt, zero), jnp.where(low_half, zero, qt)]
        return jnp.concatenate(pieces, axis=0)

    def attend(u, b, g):
        s = _dot_nt(stacked_queries(b, g), kk[u])
        s = jnp.where(live, jnp.where(own, s[:, BLOCK:], s[:, :BLOCK]), MASK_VALUE)
        m = jnp.maximum(jnp.max(s, axis=-1, keepdims=True), sink_cols[g])
        p = jnp.exp(s - m)
        denom = jnp.sum(p, axis=-1, keepdims=True) + jnp.exp(sink_cols[g] - m)
        p2 = jnp.concatenate([jnp.where(own, 0.0, p), jnp.where(own, p, 0.0)], axis=1).astype(BF16)
        o = _dot(p2, vv[u]) / denom
        for n, j in enumerate(tiles[g]):
            o_lo = o[2 * n * BLOCK:(2 * n + 1) * BLOCK]
            o_hi = o[(2 * n + 1) * BLOCK:(2 * n + 2) * BLOCK]
            m_ref[rows[b], j * LANES:(j + 1) * LANES] = jnp.where(low_half, o_lo, o_hi).astype(BF16)

    return [functools.partial(attend, u, b, g) for u, (b, g) in enumerate(units)]


def _gla_tile(blk, pf_ref, pb_ref, vec_ref, tri_ref, lev_ref, ones_ref, plev_ref, m_ref, state_ref, sbd_ref):
    gq_ref, gk_ref, ld_ref, gv_ref, og_ref = _proj_views(pf_ref, pb_ref)[3:]
    rpos = blk * BLOCK + lax.broadcasted_iota(jnp.int32, (BLOCK, 1), 0) - PAD_LEN
    valid = (rpos >= 0).astype(F32)
    ld_all = jnp.concatenate([ld_ref[b * BLOCK:(b + 1) * BLOCK, :] for b in range(BATCH)], axis=1)
    hi, lo = _split(ld_all)
    tri = tri_ref[...]
    g_cum_all = _dot(tri, hi) + _dot(tri, lo)
    lev = lev_ref[...]
    low_sums = _dot(lev, hi) + _dot(lev, lo)
    level_sums = [low_sums[l * BLOCK:(l + 1) * BLOCK] for l in range(LOW_LEVELS)]
    for l in range(LOW_LEVELS, N_LEVELS):
        half = 1 << l
        pieces = []
        for p in range(BLOCK // (2 * half)):
            mid = g_cum_all[p * 2 * half + half - 1:p * 2 * half + half, :]
            pieces += [mid - g_cum_all[p * 2 * half:p * 2 * half + half],
                       g_cum_all[p * 2 * half + half:(p + 1) * 2 * half] - mid]
        level_sums.append(jnp.concatenate(pieces, axis=0))
    decay_all = [jnp.exp(s) for s in level_sums]
    ones = ones_ref[...]
    decay_col_all = jnp.exp(_dot_tn(hi, ones) + _dot_tn(lo, ones))
    pair_level = plev_ref[...]
    gn = vec_ref[V_GN:V_GN + 1, :GLA_DV]
    streams = range(BATCH)
    rows = [slice(b * BLOCK, (b + 1) * BLOCK) for b in streams]
    cols = [slice(b * GLA_K_W, (b + 1) * GLA_K_W) for b in streams]
    gq = [gq_ref[rows[b], :] for b in streams]
    gk = [gk_ref[rows[b], :] * valid for b in streams]
    gv = [gv_ref[rows[b], :] for b in streams]
    g_cum = [g_cum_all[:, cols[b]] for b in streams]
    a = _gla_intra(gq, gk, [[d[:, cols[b]] for d in decay_all] for b in streams], range(N_LEVELS), pair_level)

    pairs = range(GLA_HEADS // 2)
    q_dec = [(gq[b] * jnp.exp(g_cum[b])).astype(BF16) for b in streams]
    o = [jnp.concatenate([_dot(q_dec[b][:, i * LANES:(i + 1) * LANES], sbd_ref[2 * b + i]) for i in pairs], axis=1)
         for b in streams]
    o = [o[b] + jnp.concatenate(
        [_dot(a[b][h].astype(BF16), gv[b][:, h * GLA_DV:(h + 1) * GLA_DV]) for h in range(GLA_HEADS)], axis=1)
        for b in streams]
    for b in streams:
        for h, gh in enumerate(_gla_merge(o[b], og_ref[rows[b], :], gn)):
            m_ref[rows[b], ATT_Q_W + h * GLA_DV:ATT_Q_W + (h + 1) * GLA_DV] = gh.astype(BF16)

    k_dec = [(gk[b] * jnp.exp(g_cum[b][BLOCK - 1:BLOCK, :] - g_cum[b])).astype(BF16) for b in streams]
    kv = [[_dot_tn(k_dec[b][:, i * LANES:(i + 1) * LANES], gv[b][:, 2 * i * GLA_DV:(2 * i + 2) * GLA_DV])
           for i in pairs] for b in streams]
    for b in streams:
        new_state = decay_col_all[cols[b], :] * state_ref[b] + jnp.concatenate(
            [kv[b][h // 2][(h % 2) * GLA_DK:(h % 2 + 1) * GLA_DK, (h % 2) * GLA_DV:(h % 2 + 1) * GLA_DV]
             for h in range(GLA_HEADS)], axis=0)
        state_ref[b] = new_state
        for h in range(GLA_HEADS):
            sbd_ref[2 * b + h // 2, (h % 2) * GLA_DK:(h % 2 + 1) * GLA_DK,
                    (h % 2) * GLA_DV:(h % 2 + 1) * GLA_DV] = new_state[h * GLA_DK:(h + 1) * GLA_DK].astype(BF16)


def _front_kernel(*refs, layer, n_src, chained):
    sinks_ref, refs = refs[0], refs[1:]
    if chained:
        refs = refs[3:]
    x_refs, refs = refs[:n_src], refs[n_src:]
    vec_ref, w_ref, wg2_ref, bd256_ref, bd128_ref, tri_ref, lev_ref, ones_ref, plev_ref = refs[:9]
    ffn_f32_refs, refs = refs[9:13], refs[13:]
    if n_src > 1:
        xo_ref, refs = refs[0], refs[1:]
    (m_ref, pfs_ref, pbs_ref, s_out_ref, k_out_ref, v_out_ref) = refs[:6]
    ffn_bf16_refs, refs = refs[6:10], refs[10:]
    wbf_ref, wg2s_ref, z_ref, pf_ref, pb_ref, kprev_ref, vprev_ref, state_ref, sbd_ref = refs
    i = pl.program_id(0)
    t = i - N_PREP

    @pl.when(i < N_PREP)
    def _():
        _convert_proj_weights(i, w_ref, wg2_ref, wbf_ref, wg2s_ref)

    @pl.when((t >= 1) & (t <= N_CONV))
    def _():
        for src, dst in zip(ffn_f32_refs, ffn_bf16_refs):
            dst[...] = src[0].astype(BF16)

    @pl.when(i == 0)
    def _():
        kprev_ref[...] = jnp.zeros_like(kprev_ref)
        vprev_ref[...] = jnp.zeros_like(vprev_ref)
        state_ref[...] = jnp.zeros_like(state_ref)
        sbd_ref[...] = jnp.zeros_like(sbd_ref)

    def load_x():
        if n_src == 1:
            return x_refs[0][...]
        x = jnp.where(t == 0, x_refs[2][...],
                      jnp.where(t < N_BLOCKS, x_refs[0][...].reshape(ROW_TILE, D_MODEL), x_refs[1][...]))
        xo_ref[...] = x
        return x

    @pl.when((t >= 0) & (t < N_BLOCKS))
    def _():
        _project_tile(load_x(), vec_ref, wbf_ref, wg2s_ref, bd256_ref, bd128_ref, z_ref, pf_ref, pb_ref)
        for attend in _attention_units(t, layer, sinks_ref, pf_ref, pb_ref, m_ref, kprev_ref, vprev_ref):
            attend()
        _gla_tile(t, pf_ref, pb_ref, vec_ref, tri_ref, lev_ref, ones_ref, plev_ref, m_ref, state_ref, sbd_ref)

    @pl.when(t == N_BLOCKS - 1)
    def _():
        s_out_ref[0] = state_ref[...]
        k_out_ref[0] = pf_ref[:, PF_K:PF_V].reshape(BATCH, BLOCK, ATT_KV_W)
        v_out_ref[0] = pf_ref[:, PF_V:PF_GQ].reshape(BATCH, BLOCK, ATT_KV_W)

    @pl.when(t >= N_BLOCKS)
    def _():
        _project_tile(load_x(), vec_ref, wbf_ref, wg2s_ref, bd256_ref, bd128_ref, z_ref, pfs_ref, pbs_ref)


def _front(layer, sinks, x_srcs, vecs, w_in, w_g2, ffn_f32, consts, prev):
    n_src = len(x_srcs)
    step = lambda i: i - N_PREP
    conv_of = lambda i: jnp.clip(step(i) - 1, 0, N_CONV - 1)
    tile_of = lambda i: jnp.where(step(i) <= 0, LEAD_TILE, jnp.minimum(step(i) - 1, LEAD_TILE - 1))
    prompt_tile_of = lambda i: jnp.where(step(i) <= 0, LEAD_TILE, jnp.minimum(step(i) - 1, MAIN_TILES - 1))
    sample_tile_of = lambda i: jnp.clip(step(i) - N_BLOCKS, 0, SAMPLE_TILES - 1)
    full = lambda a: pl.BlockSpec(a.shape, lambda i, *_: (0,) * a.ndim)
    per_batch = lambda r, c: pl.BlockSpec((1, BATCH, r, c), lambda i, *_: (layer, 0, 0, 0))
    if n_src == 1:
        x_specs = [pl.BlockSpec((ROW_TILE, D_MODEL), lambda i, *_: (tile_of(i), 0))]
    else:
        x_specs = [
            pl.BlockSpec((BATCH, BLOCK, D_MODEL), lambda i, *_: (0, jnp.clip(step(i) - 1, 0, MAIN_TILES - 1), 0)),
            pl.BlockSpec((ROW_TILE, D_MODEL), lambda i, *_: (sample_tile_of(i), 0)),
            pl.BlockSpec((ROW_TILE, D_MODEL), lambda i, *_: (0, 0)),
        ]
    ones = jnp.ones((BLOCK, LANES), BF16)
    const_args = (consts["bd256"], consts["bd128"], consts["tri"], consts["lev_low"], ones, consts["pair_level"])
    chained = prev is not None
    prev_args = tuple(prev) if chained else ()
    in_specs = ([pl.BlockSpec(memory_space=pl.ANY)] * len(prev_args) + x_specs + [
        pl.BlockSpec((V_ROWS, D_MODEL), lambda i, *_: (layer, 0)),
        pl.BlockSpec((1, W_ROWS, IN_W), lambda i, *_: (layer, jnp.minimum(i, N_PREP - 1), 0)),
        pl.BlockSpec((1, GATE_RANK, GLA_K_W), lambda i, *_: (layer, 0, 0)),
    ] + [full(a) for a in const_args] + [
        pl.BlockSpec((1, a.shape[1] // N_CONV, a.shape[2]), lambda i, *_: (layer, conv_of(i), 0)) for a in ffn_f32])
    out_specs = [pl.BlockSpec((ROW_TILE, D_MIX), lambda i, *_: (prompt_tile_of(i), 0)),
                 pl.BlockSpec((ROW_TILE, PF_W), lambda i, *_: (sample_tile_of(i), 0)),
                 pl.BlockSpec((ROW_TILE, PB_W), lambda i, *_: (sample_tile_of(i), 0)),
                 per_batch(GLA_K_W, GLA_DV), per_batch(BLOCK, ATT_KV_W), per_batch(BLOCK, ATT_KV_W)]
    out_specs += [pl.BlockSpec((a.shape[1] // N_CONV, a.shape[2]), lambda i, *_: (conv_of(i), 0)) for a in ffn_f32]
    out_shape = [jax.ShapeDtypeStruct((TOTAL_ROWS, D_MIX), BF16),
                 jax.ShapeDtypeStruct((SAMPLE_ROWS, PF_W), F32),
                 jax.ShapeDtypeStruct((SAMPLE_ROWS, PB_W), BF16),
                 jax.ShapeDtypeStruct((DEPTH, BATCH, GLA_K_W, GLA_DV), F32),
                 jax.ShapeDtypeStruct((DEPTH, BATCH, BLOCK, ATT_KV_W), F32),
                 jax.ShapeDtypeStruct((DEPTH, BATCH, BLOCK, ATT_KV_W), F32)]
    out_shape += [jax.ShapeDtypeStruct(a.shape[1:], BF16) for a in ffn_f32]
    n_lead_out = 0
    if n_src > 1:
        out_specs = [pl.BlockSpec((ROW_TILE, D_MODEL), lambda i, *_: (tile_of(i), 0))] + out_specs
        out_shape = [jax.ShapeDtypeStruct((TOTAL_ROWS, D_MODEL), F32)] + out_shape
        n_lead_out = 1
    grid_spec = pltpu.PrefetchScalarGridSpec(
        num_scalar_prefetch=1,
        grid=(N_PREP + N_BLOCKS + SAMPLE_TILES,),
        in_specs=in_specs,
        out_specs=out_specs,
        scratch_shapes=[pltpu.VMEM((D_MODEL, PROJ_W), BF16), pltpu.VMEM((LANES, GLA_K_W), BF16),
                        pltpu.VMEM((ROW_TILE, PROJ_W), F32),
                        pltpu.VMEM((ROW_TILE, PF_W), F32), pltpu.VMEM((ROW_TILE, PB_W), BF16),
                        pltpu.VMEM((BATCH * N_KV_HEADS, BLOCK, ATT_KV_W), BF16),
                        pltpu.VMEM((BATCH * N_KV_HEADS, BLOCK, ATT_KV_W), BF16),
                        pltpu.VMEM((BATCH, GLA_K_W, GLA_DV), F32),
                        pltpu.VMEM((BATCH * GLA_HEADS // 2, 2 * GLA_DK, 2 * GLA_DV), BF16)],
    )
    res = pl.pallas_call(
        functools.partial(_front_kernel, layer=layer, n_src=n_src, chained=chained),
        grid_spec=grid_spec,
        out_shape=out_shape,
        input_output_aliases={1 + n: n_lead_out + 3 + n for n in range(len(prev_args))},
        compiler_params=pltpu.CompilerParams(dimension_semantics=("arbitrary",),
                                             vmem_limit_bytes=VMEM_LIMIT),
        name="front",
    )(sinks, *prev_args, *x_srcs, vecs, w_in, w_g2, *const_args, *ffn_f32)
    x_all = res[0] if n_src > 1 else x_srcs[0]
    merged, pfs, pbs, ps, pk, pv = res[n_lead_out:n_lead_out + 6]
    return x_all, merged, (pfs, pbs), (ps, pk, pv), tuple(res[n_lead_out + 6:])


def _mix_sample_kernel(*refs, layer, n_alias):
    sinks_ref, refs = refs[0], refs[n_alias + 1:]
    (pf_ref, pb_ref, ck_ref, cv_ref, st_ref,
     vec_ref, tri_ref, sones_ref, lev_ref, ones_ref, plev_ref,
     m_ref, ck_out_ref, cv_out_ref, st_out_ref) = refs
    q_ref, k_ref, v_ref, gq_ref, gk_ref, ld_ref, gv_ref, og_ref = _proj_views(pf_ref, pb_ref)
    gq = gq_ref[...]
    gk = gk_ref[...]
    gv = gv_ref[...]
    gvf = gv.astype(F32)
    hi, lo = _split(ld_ref[...])
    tri = tri_ref[...]
    g_cum = _dot(tri, hi) + _dot(tri, lo)
    sones = sones_ref[...]
    g_tot = _dot(sones, hi) + _dot(sones, lo)
    lev = lev_ref[...]
    decay_levels = jnp.exp(_dot(lev, hi) + _dot(lev, lo))
    a = _gla_intra([gq], [gk], [[decay_levels[l * BLOCK:(l + 1) * BLOCK] for l in range(LOW_LEVELS)]],
                   range(LOW_LEVELS), plev_ref[...])[0]
    o_intra = jnp.concatenate(
        [_dot(a[h].astype(BF16), gv[:, h * GLA_DV:(h + 1) * GLA_DV]) for h in range(GLA_HEADS)], axis=1)
    q_dec = gq * jnp.exp(g_cum)
    k_dec = gk * jnp.exp(g_tot - g_cum)
    hm8 = _head_masks(DEC_SEQ)
    ones8 = ones_ref[...]
    hi_f, lo_f = hi.astype(F32), lo.astype(F32)

    qf = q_ref[...].astype(F32)
    kf = k_ref[...]
    vf = v_ref[...]
    low8 = lax.broadcasted_iota(jnp.int32, (DEC_SEQ, LANES), 1) < HEAD_DIM
    nkeys = WINDOW + DEC_SEQ
    srow = lax.broadcasted_iota(jnp.int32, (N_HEADS * DEC_SEQ, nkeys), 0)
    scol = lax.broadcasted_iota(jnp.int32, (N_HEADS * DEC_SEQ, nkeys), 1)
    t_of_row = srow & (DEC_SEQ - 1)
    amask = ((scol < WINDOW) & (scol > t_of_row)) | ((scol >= WINDOW) & (scol - WINDOW <= t_of_row))
    rid = lax.broadcasted_iota(jnp.int32, (N_HEADS * DEC_SEQ, 1), 0) >> LOW_LEVELS
    sink_col = jnp.zeros((N_HEADS * DEC_SEQ, 1), F32)
    for i in range(N_HEADS):
        sink_col = jnp.where(rid == i, sinks_ref[layer, i], sink_col)
    half = N_HEADS * DEC_SEQ // N_KV_HEADS

    seqs = range(SEQ_GROUP)
    rows = [slice(b * DEC_SEQ, (b + 1) * DEC_SEQ) for b in seqs]

    def stacked_queries(b):
        pieces = []
        for j in range(N_HEADS // 2):
            qt = qf[rows[b], j * LANES:(j + 1) * LANES]
            pieces += [jnp.where(low8, qt, 0.0), jnp.where(low8, 0.0, qt)]
        return jnp.concatenate(pieces, axis=0).astype(BF16)

    qp = [stacked_queries(b) for b in seqs]
    kk = [jnp.concatenate([ck_ref[0, b], kf[rows[b]]], axis=0) for b in seqs]
    vv = [jnp.concatenate([cv_ref[0, b], vf[rows[b]]], axis=0) for b in seqs]
    for b in seqs:
        ck_out_ref[0, b] = kk[b][DEC_SEQ:]
        cv_out_ref[0, b] = vv[b][DEC_SEQ:]
    kd = [[x.astype(BF16) for x in _dup_halves(kk[b])] for b in seqs]
    vd = [[x.astype(BF16) for x in _dup_halves(vv[b])] for b in seqs]
    s = [jnp.concatenate([_dot_nt(qp[b][:half], kd[b][0]), _dot_nt(qp[b][half:], kd[b][1])], axis=0)
         for b in seqs]
    s = [jnp.where(amask, s[b], MASK_VALUE) for b in seqs]
    m = [jnp.maximum(jnp.max(s[b], axis=-1, keepdims=True), sink_col) for b in seqs]
    p = [jnp.exp(s[b] - m[b]) for b in seqs]
    denom = [jnp.sum(p[b], axis=-1, keepdims=True) + jnp.exp(sink_col - m[b]) for b in seqs]
    pb = [p[b].astype(BF16) for b in seqs]
    ob = [jnp.concatenate([_dot(pb[b][:half], vd[b][0]), _dot(pb[b][half:], vd[b][1])], axis=0) / denom[b]
          for b in seqs]
    att_rows = [jnp.concatenate(
        [jnp.where(low8, ob[b][(2 * j) * DEC_SEQ:(2 * j + 1) * DEC_SEQ],
                   ob[b][(2 * j + 1) * DEC_SEQ:(2 * j + 2) * DEC_SEQ])
         for j in range(N_HEADS // 2)], axis=1) for b in seqs]

    head_stack = lambda x: jnp.concatenate([jnp.where(mk, x, 0.0) for mk in hm8], axis=0).astype(BF16)
    state = [st_ref[0, b] for b in seqs]
    oi = [_dot(head_stack(q_dec[rows[b]]), state[b].astype(BF16)) for b in seqs]
    inter_rows = [jnp.concatenate([oi[b][h * DEC_SEQ:(h + 1) * DEC_SEQ] for h in range(GLA_HEADS)], axis=1)
                  for b in seqs]
    vstack = [jnp.concatenate([gvf[rows[b]][:, h * GLA_DV:(h + 1) * GLA_DV] for h in range(GLA_HEADS)],
                              axis=0).astype(BF16) for b in seqs]
    kv = [_dot_tn(head_stack(k_dec[rows[b]]), vstack[b]) for b in seqs]
    decay_col = [jnp.exp(_dot_tn(hi_f[rows[b]], ones8) + _dot_tn(lo_f[rows[b]], ones8)) for b in seqs]
    for b in seqs:
        st_out_ref[0, b] = decay_col[b] * state[b] + kv[b]

    m_ref[:, :ATT_Q_W] = jnp.concatenate(att_rows, axis=0).astype(BF16)
    o = o_intra + jnp.concatenate(inter_rows, axis=0)
    gn = vec_ref[V_GN:V_GN + 1, :GLA_DV]
    for h, gh in enumerate(_gla_merge(o, og_ref[...], gn)):
        m_ref[:, ATT_Q_W + h * GLA_DV:ATT_Q_W + (h + 1) * GLA_DV] = gh.astype(BF16)


def _mix_sample(layer, sinks, merged, proj, cache_k, cache_v, state, vecs, consts, prev):
    tok = lambda w_: pl.BlockSpec((BLOCK, w_), lambda i, *_: (i, 0))
    merged_blk = pl.BlockSpec((BLOCK, D_MIX), lambda i, *_: (MAIN_ROWS // BLOCK + i, 0))
    full = lambda a: pl.BlockSpec(a.shape, lambda i, *_: (0,) * a.ndim)
    seq = lambda a: pl.BlockSpec((1, SEQ_GROUP) + a.shape[2:], lambda i, *_: (layer, i, 0, 0))
    ones8 = jnp.ones((DEC_SEQ, LANES), F32)
    const_args = (consts["seq_tri"], consts["seq_ones"], consts["lev_low"], ones8, consts["pair_level"])
    seq_args = (cache_k, cache_v, state)
    alias_args = (merged,) + (tuple(prev) if prev is not None else ())
    grid_spec = pltpu.PrefetchScalarGridSpec(
        num_scalar_prefetch=1,
        grid=(DEC_BATCH // SEQ_GROUP,),
        in_specs=([pl.BlockSpec(memory_space=pl.ANY)] * len(alias_args)
                  + [tok(a.shape[1]) for a in proj] + [seq(a) for a in seq_args]
                  + [pl.BlockSpec((V_ROWS, D_MODEL), lambda i, *_: (layer, 0))]
                  + [full(a) for a in const_args]),
        out_specs=[merged_blk] + [seq(a) for a in seq_args],
    )
    return pl.pallas_call(
        functools.partial(_mix_sample_kernel, layer=layer, n_alias=len(alias_args)),
        grid_spec=grid_spec,
        out_shape=[jax.ShapeDtypeStruct(merged.shape, merged.dtype)]
        + [jax.ShapeDtypeStruct(a.shape, a.dtype) for a in seq_args],
        input_output_aliases={1 + n: n for n in range(len(alias_args))},
        compiler_params=pltpu.CompilerParams(dimension_semantics=("arbitrary",),
                                             vmem_limit_bytes=VMEM_LIMIT),
        name="mix_sample",
    )(sinks, *alias_args, *proj, *seq_args, vecs, *const_args)


def _out_ffn_kernel(*refs, last):
    x_ref, m_ref, vec_ref, wo_ref, wg_ref, wu_ref, wd_ref = refs[:7]
    out_refs = refs[7:9] if last else refs[7:8]
    act_ref = refs[-1]
    x1 = x_ref[...] + _dot(m_ref[...], wo_ref[...])
    ms = jnp.mean(x1 * x1, axis=-1, keepdims=True)
    h = (x1 * lax.rsqrt(ms + RMS_EPS) * vec_ref[V_NORM2:V_NORM2 + 1, :]).astype(BF16)
    for c in range(D_FFN // FFN_CHUNK):
        cols = slice(c * FFN_CHUNK, (c + 1) * FFN_CHUNK)
        gate = _dot(h, wg_ref[:, cols])
        up = _dot(h, wu_ref[:, cols])
        act_ref[:, cols] = (gate * _sigmoid(gate) * up).astype(BF16)
    y = x1 + _dot(act_ref[...], wd_ref[...])
    if last:
        out_refs[0][...] = y.reshape(BATCH, BLOCK, D_MODEL)

        @pl.when(pl.program_id(0) < SAMPLE_TILES)
        def _():
            out_refs[1][...] = out_refs[0][...].reshape(ROW_TILE, D_MODEL)
    else:
        out_refs[0][...] = y


def _out_ffn(layer, x_all, merged, vecs, ffn_weights, last):
    resident = lambda a: pl.BlockSpec(a.shape, lambda i: (0, 0), pipeline_mode=pl.Buffered(1))
    if last:
        n_tiles = MAIN_TILES + SAMPLE_TILES
        tile_of = lambda i: jnp.where(i < SAMPLE_TILES, MAIN_TILES + i, i - SAMPLE_TILES)
        out_specs = [pl.BlockSpec((BATCH, BLOCK, D_MODEL), lambda i: (0, jnp.maximum(i - SAMPLE_TILES, 0), 0)),
                     pl.BlockSpec((ROW_TILE, D_MODEL), lambda i: (jnp.minimum(i, SAMPLE_TILES - 1), 0))]
        out_shape = [jax.ShapeDtypeStruct((BATCH, SEQ, D_MODEL), F32),
                     jax.ShapeDtypeStruct((SAMPLE_ROWS, D_MODEL), F32)]
    else:
        n_tiles = N_TILES
        tile_of = lambda i: i
        out_specs = [pl.BlockSpec((ROW_TILE, D_MODEL), lambda i: (i, 0))]
        out_shape = [jax.ShapeDtypeStruct((TOTAL_ROWS, D_MODEL), F32)]
    tile = lambda w_: pl.BlockSpec((ROW_TILE, w_), lambda i: (tile_of(i), 0))
    return pl.pallas_call(
        functools.partial(_out_ffn_kernel, last=last),
        grid=(n_tiles,),
        in_specs=[tile(D_MODEL), tile(D_MIX), pl.BlockSpec((V_ROWS, D_MODEL), lambda i: (layer, 0))]
        + [resident(w) for w in ffn_weights],
        out_specs=out_specs,
        out_shape=out_shape,
        scratch_shapes=[pltpu.VMEM((ROW_TILE, D_FFN), BF16)],
        compiler_params=pltpu.CompilerParams(dimension_semantics=("arbitrary",),
                                             vmem_limit_bytes=VMEM_LIMIT),
        name="out_ffn",
    )(x_all, merged, vecs, *ffn_weights)


def _vector_slab(norm1, norm2, q_norm, k_norm, b_g, gla_norm):
    pad = lambda a: jnp.pad(a.astype(F32), ((0, 0), (0, D_MODEL - a.shape[1])))
    rows = [norm1.astype(F32), norm2.astype(F32), pad(jnp.tile(q_norm, (1, N_HEADS)) * ATT_SCALE),
            pad(jnp.tile(k_norm, (1, N_KV_HEADS))), pad(b_g), pad(gla_norm)]
    rows += [jnp.zeros((DEPTH, D_MODEL), F32)] * (V_ROWS - len(rows))
    return jnp.stack(rows, axis=1).reshape(DEPTH * V_ROWS, D_MODEL)


def kernel(x_prompt, x_sample, cache_k, cache_v, state_gla, meta, norm1, w_in, q_norm, k_norm, sinks,
           w_g2, b_g, gla_norm, w_o, norm2, w_gate, w_up, w_down):
    consts = _constants()
    dt = x_prompt.dtype
    vecs = _vector_slab(norm1, norm2, q_norm, k_norm, b_g, gla_norm)
    sinks = sinks.astype(F32)
    lead = jnp.tile(jnp.concatenate([jnp.zeros((PAD_LEN, D_MODEL), dt), meta.astype(dt)], axis=0), (BATCH, 1))
    x_srcs = (x_prompt, x_sample.reshape(SAMPLE_ROWS, D_MODEL), lead)
    ck_in = cache_k.reshape(DEPTH, DEC_BATCH, WINDOW, ATT_KV_W)
    cv_in = cache_v.reshape(DEPTH, DEC_BATCH, WINDOW, ATT_KV_W)
    st_in = state_gla.reshape(DEPTH, DEC_BATCH, GLA_K_W, GLA_DV)

    prompt_outs, sample_outs = None, None
    for l in range(DEPTH):
        x_all, merged, proj_sample, prompt_outs, ffn_bf16 = _front(
            l, sinks, x_srcs, vecs, w_in, w_g2, (w_o, w_gate, w_up, w_down), consts, prompt_outs)
        merged, *sample_outs = _mix_sample(l, sinks, merged, proj_sample, ck_in, cv_in, st_in, vecs, consts,
                                           sample_outs)
        x_srcs = tuple(_out_ffn(l, x_all, merged, vecs, ffn_bf16, last=l == DEPTH - 1))

    y_main, y_sample = x_srcs
    ps, pk, pv = prompt_outs
    sk, sv, ss = sample_outs
    kv5 = lambda a, n: a.reshape(DEPTH, n, WINDOW, N_KV_HEADS, HEAD_DIM)
    st5 = lambda a, n: a.reshape(DEPTH, n, GLA_HEADS, GLA_DK, GLA_DV)
    return (y_main, y_sample.reshape(DEC_BATCH, DEC_SEQ, D_MODEL),
            kv5(pk, BATCH), kv5(pv, BATCH), st5(ps, BATCH), kv5(sk, DEC_BATCH), kv5(sv, DEC_BATCH),
            st5(ss, DEC_BATCH))
```

```python
import functools

import jax
import jax.numpy as jnp
import numpy as np
from jax import lax
from jax.experimental import pallas as pl
from jax.experimental.pallas import tpu as pltpu

F32 = jnp.float32
BF16 = jnp.bfloat16

D_MODEL = 1024
BATCH = 4
SEQ = 4096
DEPTH = 2
DEC_BATCH = 128
DEC_SEQ = 8
N_META = 16
WINDOW = 128
BLOCK = 128
PAD_LEN = BLOCK - N_META
N_HEADS = 8
N_KV_HEADS = 2
HEAD_DIM = 64
ATT_SCALE = HEAD_DIM ** -0.5
ATT_Q_W = N_HEADS * HEAD_DIM
ATT_KV_W = N_KV_HEADS * HEAD_DIM
GLA_HEADS = 4
GLA_DK = 64
GLA_DV = 128
GLA_K_W = GLA_HEADS * GLA_DK
GLA_V_W = GLA_HEADS * GLA_DV
GATE_RANK = 16
GATE_NORMALIZER = 16.0
D_MIX = ATT_Q_W + GLA_V_W
D_FFN = 2816
IN_W = 2320
RMS_EPS = 1e-6
MASK_VALUE = -1e30

LANES = 128
N_BLOCKS = 1 + SEQ // BLOCK
MAIN_ROWS = BATCH * SEQ
SAMPLE_ROWS = DEC_BATCH * DEC_SEQ
LEAD_ROWS = BATCH * BLOCK
TOTAL_ROWS = MAIN_ROWS + SAMPLE_ROWS + LEAD_ROWS
ROW_TILE = 512
MAIN_TILES = MAIN_ROWS // ROW_TILE
SAMPLE_TILES = SAMPLE_ROWS // ROW_TILE
N_TILES = TOTAL_ROWS // ROW_TILE
LEAD_TILE = N_TILES - 1
SEQ_GROUP = BLOCK // DEC_SEQ
N_LEVELS = 7
LOW_LEVELS = 3
LOG_DK = 6
N_PREP = 8
W_ROWS = D_MODEL // N_PREP
N_CONV = 16

SRC_LOW, SRC_OG = 1792, 1808
OFF_Q, OFF_K, OFF_V = 0, 512, 640
OFF_GQ, OFF_GK, OFF_GV, OFF_OG, OFF_LOW = 768, 1024, 1280, 1792, 2304
PROJ_W = OFF_LOW + LANES
PROJ_CHUNK = 512
FFN_CHUNK = 256
VMEM_LIMIT = 58 * 1024 * 1024

V_NORM1, V_NORM2, V_QG, V_KG, V_BG, V_GN, V_ROWS = 0, 1, 2, 3, 4, 5, 8

PF_K, PF_V, PF_GQ, PF_GK, PF_LD, PF_OG, PF_W = 0, 128, 256, 512, 768, 1024, 1536
PB_Q, PB_GV, PB_W = 0, 512, 1024


def _proj_views(pf_ref, pb_ref):
    f = lambda a, b: pf_ref.at[:, a:b]
    return (pb_ref.at[:, PB_Q:PB_GV], f(PF_K, PF_V), f(PF_V, PF_GQ), f(PF_GQ, PF_GK), f(PF_GK, PF_LD),
            f(PF_LD, PF_OG), pb_ref.at[:, PB_GV:PB_W], f(PF_OG, PF_W))


def _dot(a, b):
    return jnp.dot(a, b, preferred_element_type=F32)


def _dot_nt(a, b):
    return lax.dot_general(a, b, (((1,), (1,)), ((), ())), preferred_element_type=F32)


def _dot_tn(a, b):
    return lax.dot_general(a, b, (((0,), (0,)), ((), ())), preferred_element_type=F32)


def _split(x):
    hi = x.astype(BF16)
    lo = (x - hi.astype(F32)).astype(BF16)
    return hi, lo


def _sigmoid(x):
    return 1.0 / (1.0 + jnp.exp(-x))


def _level_matrix(levels, n=BLOCK):
    out = np.zeros((len(levels) * n, n), np.float32)
    for i, l in enumerate(levels):
        size = 2 << l
        for t in range(n):
            mid = (t // size) * size + size // 2 - 1
            if (t >> l) & 1:
                out[i * n + t, mid + 1:t + 1] = 1.0
            else:
                out[i * n + t, t + 1:mid + 1] = 1.0
    return out


def _constants():
    r = np.arange(BLOCK)
    tri = (r[None, :] <= r[:, None]).astype(np.float32)
    same_seq = (r[None, :] // DEC_SEQ) == (r[:, None] // DEC_SEQ)
    diff = np.maximum(r[:, None] ^ r[None, :], 1)
    pair_level = np.where(r[None, :] < r[:, None], np.floor(np.log2(diff)).astype(np.int32),
                          np.where(r[None, :] == r[:, None], N_LEVELS, N_LEVELS + 1)).astype(np.int32)
    c = np.arange(2 * LANES)
    bd256 = ((c[None, :] // HEAD_DIM) == (c[:, None] // HEAD_DIM)).astype(np.float32)
    return dict(
        tri=jnp.asarray(tri, BF16),
        seq_tri=jnp.asarray(tri * same_seq, BF16),
        seq_ones=jnp.asarray(same_seq.astype(np.float32), BF16),
        lev_low=jnp.asarray(_level_matrix(range(LOW_LEVELS)), BF16),
        bd256=jnp.asarray(bd256, BF16),
        bd128=jnp.asarray(bd256[:LANES, :LANES], BF16),
        pair_level=jnp.asarray(pair_level),
    )


def _head_masks(rows):
    lane = lax.broadcasted_iota(jnp.int32, (rows, GLA_K_W), 1)
    return [(lane >> LOG_DK) == h for h in range(GLA_HEADS)]


def _gla_intra(gq, gk, decay_levels, levels, pair_level):
    rows = gq[0].shape[0]
    streams = range(len(gq))
    low_half = lax.broadcasted_iota(jnp.int32, (rows, LANES), 1) < GLA_DK

    def pair_products(qh, kh):
        qh, kh = qh.astype(BF16), kh.astype(BF16)
        out = []
        for pair in range(GLA_HEADS // 2):
            kl = kh[:, pair * LANES:(pair + 1) * LANES]
            zero = jnp.zeros_like(kl)
            stacked = jnp.concatenate([jnp.where(low_half, kl, zero), jnp.where(low_half, zero, kl)], axis=0)
            p = _dot_nt(qh[:, pair * LANES:(pair + 1) * LANES], stacked)
            out += [p[:, :rows], p[:, rows:]]
        return out

    row = lax.broadcasted_iota(jnp.int32, (rows, 1), 0)

    def upper_q_lower_k(b, l):
        size = 1 << l
        if size % 8:
            return jnp.where(((row >> l) & 1) == 1, gq[b], gk[b])
        return jnp.concatenate([(gq[b] if i & 1 else gk[b])[i * size:(i + 1) * size]
                                for i in range(rows // size)], axis=0)

    on_diag = pair_level == N_LEVELS
    a = [[jnp.where(on_diag, p, 0.0) for p in pair_products(gq[b], gk[b])] for b in streams]
    for i, l in enumerate(levels):
        at_level = pair_level == l
        for b in streams:
            x = upper_q_lower_k(b, l) * decay_levels[b][i]
            a[b] = [jnp.where(at_level, p, a_h) for p, a_h in zip(pair_products(x, x), a[b])]
    return a


def _gla_merge(o, og, gn):
    outs = []
    for h in range(GLA_HEADS):
        oh = o[:, h * GLA_DV:(h + 1) * GLA_DV]
        gh = og[:, h * GLA_DV:(h + 1) * GLA_DV]
        ms = jnp.mean(oh * oh, axis=-1, keepdims=True)
        outs.append(oh * lax.rsqrt(ms + RMS_EPS) * gn * (gh * _sigmoid(gh)))
    return outs


def _dup_halves(x):
    low = lax.broadcasted_iota(jnp.int32, x.shape, 1) < HEAD_DIM
    rolled = pltpu.roll(x, HEAD_DIM, axis=1)
    return jnp.where(low, x, rolled), jnp.where(low, rolled, x)


def _convert_proj_weights(i, w_ref, wg2_ref, wbf_ref, wg2s_ref):
    rows = pl.ds(pl.multiple_of(i * W_ROWS, W_ROWS), W_ROWS)

    def put(dst, src, n=LANES):
        tile = w_ref[0, src:src + n, :]
        if n < LANES:
            tile = jnp.concatenate([tile, jnp.zeros((LANES - n, W_ROWS), F32)], axis=0)
        wbf_ref[rows, dst:dst + LANES] = tile.T.astype(BF16)

    for c in range(SRC_LOW // LANES):
        put(c * LANES, c * LANES)
    for c in range((IN_W - SRC_OG) // LANES):
        put(OFF_OG + c * LANES, SRC_OG + c * LANES)
    put(OFF_LOW, SRC_LOW, GATE_RANK)

    @pl.when(i == 0)
    def _():
        wg2s_ref[...] = jnp.concatenate(
            [wg2_ref[0], jnp.zeros((LANES - GATE_RANK, GLA_K_W), F32)], axis=0).astype(BF16)


def _project_tile(x, vec_ref, wbf_ref, wg2s_ref, bd256_ref, bd128_ref, z_ref, pf_ref, pb_ref):
    q_ref, k_ref, v_ref, gq_ref, gk_ref, ld_ref, gv_ref, og_ref = _proj_views(pf_ref, pb_ref)
    ms = jnp.mean(x * x, axis=-1, keepdims=True)
    h = (x * lax.rsqrt(ms + RMS_EPS) * vec_ref[V_NORM1:V_NORM1 + 1, :]).astype(BF16)

    def q_part():
        q = z_ref[:, OFF_Q:OFF_K]
        q2 = (q * q).astype(BF16)
        bd = bd256_ref[...]
        ssq = jnp.concatenate([_dot(q2[:, :256], bd), _dot(q2[:, 256:], bd)], axis=1)
        q_ref[...] = (q * lax.rsqrt(ssq * (1.0 / HEAD_DIM) + RMS_EPS)
                      * vec_ref[V_QG:V_QG + 1, :ATT_Q_W]).astype(BF16)

    def kv_part():
        k = z_ref[:, OFF_K:OFF_V]
        ssk = _dot((k * k).astype(BF16), bd128_ref[...])
        k_ref[...] = k * lax.rsqrt(ssk * (1.0 / HEAD_DIM) + RMS_EPS) * vec_ref[V_KG:V_KG + 1, :ATT_KV_W]
        v_ref[...] = z_ref[:, OFF_V:OFF_GQ]
        gq_ref[...] = z_ref[:, OFF_GQ:OFF_GK] * (GLA_DK ** -0.5)

    def gk_part():
        gk_ref[...] = z_ref[:, OFF_GK:OFF_GV]

    def gv_part():
        gv_ref[...] = z_ref[:, OFF_GV:OFF_OG].astype(BF16)

    def tail_part():
        og_ref[...] = z_ref[:, OFF_OG:OFF_LOW]
        logit = (_dot(z_ref[:, OFF_LOW:PROJ_W].astype(BF16), wg2s_ref[...])
                 + vec_ref[V_BG:V_BG + 1, :GLA_K_W])
        log_sig = jnp.minimum(logit, 0.0) - jnp.log1p(jnp.exp(-jnp.abs(logit)))
        ld_ref[...] = log_sig * (1.0 / GATE_NORMALIZER)

    pieces = [q_part, kv_part, gk_part, gv_part, tail_part]
    for c, piece in enumerate(pieces):
        cols = slice(c * PROJ_CHUNK, min((c + 1) * PROJ_CHUNK, PROJ_W))
        z_ref[:, cols] = _dot(h, wbf_ref[:, cols])
        piece()


def _attention_units(blk, layer, sinks_ref, pf_ref, pb_ref, m_ref, kprev_ref, vprev_ref):
    q_ref, k_ref, v_ref = _proj_views(pf_ref, pb_ref)[:3]
    per_group = N_HEADS // N_KV_HEADS
    grows = per_group * BLOCK
    row = lax.broadcasted_iota(jnp.int32, (grows, BLOCK), 0) & (BLOCK - 1)
    col = lax.broadcasted_iota(jnp.int32, (grows, BLOCK), 1)
    own = col <= row
    kpos = jnp.where(own, blk * BLOCK, (blk - 1) * BLOCK) + col - PAD_LEN
    live = kpos >= 0
    low_half = lax.broadcasted_iota(jnp.int32, (BLOCK, LANES), 1) < HEAD_DIM
    head_of_row = lax.broadcasted_iota(jnp.int32, (grows, 1), 0) >> N_LEVELS
    sink_cols = []
    for g in range(N_KV_HEADS):
        sink = jnp.zeros((grows, 1), F32)
        for r in range(per_group):
            sink = jnp.where(head_of_row == r, sinks_ref[layer, g * per_group + r], sink)
        sink_cols.append(sink)

    units =[(b, g) for b in range(BATCH) for g in range(N_KV_HEADS)]
    rows = [slice(b * BLOCK, (b + 1) * BLOCK) for b in range(BATCH)]
    tiles = [range(g * per_group // 2, (g + 1) * per_group // 2) for g in range(N_KV_HEADS)]
    kdup = [[x.astype(BF16) for x in _dup_halves(k_ref[rows[b], :])] for b in range(BATCH)]
    vdup = [[x.astype(BF16) for x in _dup_halves(v_ref[rows[b], :])] for b in range(BATCH)]
    kk = [jnp.concatenate([kprev_ref[b * N_KV_HEADS + g], kdup[b][g]], axis=0) for b, g in units]
    vv = [jnp.concatenate([vprev_ref[b * N_KV_HEADS + g], vdup[b][g]], axis=0) for b, g in units]
    for b, g in units:
        kprev_ref[b * N_KV_HEADS + g] = kdup[b][g]
        vprev_ref[b * N_KV_HEADS + g] = vdup[b][g]

    def stacked_queries(b, g):
        pieces = []
        for j in tiles[g]:
            qt = q_ref[rows[b], j * LANES:(j + 1) * LANES]
            zero = jnp.zeros_like(qt)
            pieces += [jnp.where(low_half, qt, zero), jnp.where(low_half, zero, qt)]
        return jnp.concatenate(pieces, axis=0)

    def attend(u, b, g):
        s = _dot_nt(stacked_queries(b, g), kk[u])
        s = jnp.where(live, jnp.where(own, s[:, BLOCK:], s[:, :BLOCK]), MASK_VALUE)
        m = jnp.maximum(jnp.max(s, axis=-1, keepdims=True), sink_cols[g])
        p = jnp.exp(s - m)
        denom = jnp.sum(p, axis=-1, keepdims=True) + jnp.exp(sink_cols[g] - m)
        p2 = jnp.concatenate([jnp.where(own, 0.0, p), jnp.where(own, p, 0.0)], axis=1).astype(BF16)
        o = _dot(p2, vv[u]) / denom
        for n, j in enumerate(tiles[g]):
            o_lo = o[2 * n * BLOCK:(2 * n + 1) * BLOCK]
            o_hi = o[(2 * n + 1) * BLOCK:(2 * n + 2) * BLOCK]
            m_ref[rows[b], j * LANES:(j + 1) * LANES] = jnp.where(low_half, o_lo, o_hi).astype(BF16)

    return [functools.partial(attend, u, b, g) for u, (b, g) in enumerate(units)]


def _gla_tile(blk, pf_ref, pb_ref, vec_ref, tri_ref, lev_ref, ones_ref, plev_ref, m_ref, state_ref, sbd_ref):
    gq_ref, gk_ref, ld_ref, gv_ref, og_ref = _proj_views(pf_ref, pb_ref)[3:]
    rpos = blk * BLOCK + lax.broadcasted_iota(jnp.int32, (BLOCK, 1), 0) - PAD_LEN
    valid = (rpos >= 0).astype(F32)
    ld_all = jnp.concatenate([ld_ref[b * BLOCK:(b + 1) * BLOCK, :] for b in range(BATCH)], axis=1)
    hi, lo = _split(ld_all)
    tri = tri_ref[...]
    g_cum_all = _dot(tri, hi) + _dot(tri, lo)
    lev = lev_ref[...]
    low_sums = _dot(lev, hi) + _dot(lev, lo)
    level_sums = [low_sums[l * BLOCK:(l + 1) * BLOCK] for l in range(LOW_LEVELS)]
    for l in range(LOW_LEVELS, N_LEVELS):
        half = 1 << l
        pieces = []
        for p in range(BLOCK // (2 * half)):
            mid = g_cum_all[p * 2 * half + half - 1:p * 2 * half + half, :]
            pieces += [mid - g_cum_all[p * 2 * half:p * 2 * half + half],
                       g_cum_all[p * 2 * half + half:(p + 1) * 2 * half] - mid]
        level_sums.append(jnp.concatenate(pieces, axis=0))
    decay_all = [jnp.exp(s) for s in level_sums]
    ones = ones_ref[...]
    decay_col_all = jnp.exp(_dot_tn(hi, ones) + _dot_tn(lo, ones))
    pair_level = plev_ref[...]
    gn = vec_ref[V_GN:V_GN + 1, :GLA_DV]
    streams = range(BATCH)
    rows = [slice(b * BLOCK, (b + 1) * BLOCK) for b in streams]
    cols = [slice(b * GLA_K_W, (b + 1) * GLA_K_W) for b in streams]
    gq = [gq_ref[rows[b], :] for b in streams]
    gk = [gk_ref[rows[b], :] * valid for b in streams]
    gv = [gv_ref[rows[b], :] for b in streams]
    g_cum = [g_cum_all[:, cols[b]] for b in streams]
    a = _gla_intra(gq, gk, [[d[:, cols[b]] for d in decay_all] for b in streams], range(N_LEVELS), pair_level)

    pairs = range(GLA_HEADS // 2)
    q_dec = [(gq[b] * jnp.exp(g_cum[b])).astype(BF16) for b in streams]
    o = [jnp.concatenate([_dot(q_dec[b][:, i * LANES:(i + 1) * LANES], sbd_ref[2 * b + i]) for i in pairs], axis=1)
         for b in streams]
    o = [o[b] + jnp.concatenate(
        [_dot(a[b][h].astype(BF16), gv[b][:, h * GLA_DV:(h + 1) * GLA_DV]) for h in range(GLA_HEADS)], axis=1)
        for b in streams]
    for b in streams:
        for h, gh in enumerate(_gla_merge(o[b], og_ref[rows[b], :], gn)):
            m_ref[rows[b], ATT_Q_W + h * GLA_DV:ATT_Q_W + (h + 1) * GLA_DV] = gh.astype(BF16)

    k_dec = [(gk[b] * jnp.exp(g_cum[b][BLOCK - 1:BLOCK, :] - g_cum[b])).astype(BF16) for b in streams]
    kv = [[_dot_tn(k_dec[b][:, i * LANES:(i + 1) * LANES], gv[b][:, 2 * i * GLA_DV:(2 * i + 2) * GLA_DV])
           for i in pairs] for b in streams]
    for b in streams:
        new_state = decay_col_all[cols[b], :] * state_ref[b] + jnp.concatenate(
            [kv[b][h // 2][(h % 2) * GLA_DK:(h % 2 + 1) * GLA_DK, (h % 2) * GLA_DV:(h % 2 + 1) * GLA_DV]
             for h in range(GLA_HEADS)], axis=0)
        state_ref[b] = new_state
        for h in range(GLA_HEADS):
            sbd_ref[2 * b + h // 2, (h % 2) * GLA_DK:(h % 2 + 1) * GLA_DK,
                    (h % 2) * GLA_DV:(h % 2 + 1) * GLA_DV] = new_state[h * GLA_DK:(h + 1) * GLA_DK].astype(BF16)


def _front_kernel(*refs, layer, n_src, chained):
    sinks_ref, refs = refs[0], refs[1:]
    if chained:
        refs = refs[3:]
    x_refs, refs = refs[:n_src], refs[n_src:]
    vec_ref, w_ref, wg2_ref, bd256_ref, bd128_ref, tri_ref, lev_ref, ones_ref, plev_ref = refs[:9]
    ffn_f32_refs, refs = refs[9:13], refs[13:]
    if n_src > 1:
        xo_ref, refs = refs[0], refs[1:]
    (m_ref, pfs_ref, pbs_ref, s_out_ref, k_out_ref, v_out_ref) = refs[:6]
    ffn_bf16_refs, refs = refs[6:10], refs[10:]
    wbf_ref, wg2s_ref, z_ref, pf_ref, pb_ref, kprev_ref, vprev_ref, state_ref, sbd_ref = refs
    i = pl.program_id(0)
    t = i - N_PREP

    @pl.when(i < N_PREP)
    def _():
        _convert_proj_weights(i, w_ref, wg2_ref, wbf_ref, wg2s_ref)

    @pl.when((t >= 1) & (t <= N_CONV))
    def _():
        for src, dst in zip(ffn_f32_refs, ffn_bf16_refs):
            dst[...] = src[0].astype(BF16)

    @pl.when(i == 0)
    def _():
        kprev_ref[...] = jnp.zeros_like(kprev_ref)
        vprev_ref[...] = jnp.zeros_like(vprev_ref)
        state_ref[...] = jnp.zeros_like(state_ref)
        sbd_ref[...] = jnp.zeros_like(sbd_ref)

    def load_x():
        if n_src == 1:
            return x_refs[0][...]
        x = jnp.where(t == 0, x_refs[2][...],
                      jnp.where(t < N_BLOCKS, x_refs[0][...].reshape(ROW_TILE, D_MODEL), x_refs[1][...]))
        xo_ref[...] = x
        return x

    @pl.when((t >= 0) & (t < N_BLOCKS))
    def _():
        _project_tile(load_x(), vec_ref, wbf_ref, wg2s_ref, bd256_ref, bd128_ref, z_ref, pf_ref, pb_ref)
        for attend in _attention_units(t, layer, sinks_ref, pf_ref, pb_ref, m_ref, kprev_ref, vprev_ref):
            attend()
        _gla_tile(t, pf_ref, pb_ref, vec_ref, tri_ref, lev_ref, ones_ref, plev_ref, m_ref, state_ref, sbd_ref)

    @pl.when(t == N_BLOCKS - 1)
    def _():
        s_out_ref[0] = state_ref[...]
        for b in range(BATCH):
            k_out_ref[0, b] = pf_ref[b * BLOCK:(b + 1) * BLOCK, PF_K:PF_V].T
            v_out_ref[0, b] = pf_ref[b * BLOCK:(b + 1) * BLOCK, PF_V:PF_GQ].T

    @pl.when(t >= N_BLOCKS)
    def _():
        _project_tile(load_x(), vec_ref, wbf_ref, wg2s_ref, bd256_ref, bd128_ref, z_ref, pfs_ref, pbs_ref)


def _front(layer, sinks, x_srcs, vecs, w_in, w_g2, ffn_f32, consts, prev):
    n_src = len(x_srcs)
    step = lambda i: i - N_PREP
    conv_of = lambda i: jnp.clip(step(i) - 1, 0, N_CONV - 1)
    tile_of = lambda i: jnp.where(step(i) <= 0, LEAD_TILE, jnp.minimum(step(i) - 1, LEAD_TILE - 1))
    prompt_tile_of = lambda i: jnp.where(step(i) <= 0, LEAD_TILE, jnp.minimum(step(i) - 1, MAIN_TILES - 1))
    sample_tile_of = lambda i: jnp.clip(step(i) - N_BLOCKS, 0, SAMPLE_TILES - 1)
    full = lambda a: pl.BlockSpec(a.shape, lambda i, *_: (0,) * a.ndim)
    per_batch = lambda r, c: pl.BlockSpec((1, BATCH, r, c), lambda i, *_: (layer, 0, 0, 0))
    if n_src == 1:
        x_specs = [pl.BlockSpec((ROW_TILE, D_MODEL), lambda i, *_: (tile_of(i), 0))]
    else:
        x_specs = [
            pl.BlockSpec((BATCH, BLOCK, D_MODEL), lambda i, *_: (0, jnp.clip(step(i) - 1, 0, MAIN_TILES - 1), 0)),
            pl.BlockSpec((ROW_TILE, D_MODEL), lambda i, *_: (sample_tile_of(i), 0)),
            pl.BlockSpec((ROW_TILE, D_MODEL), lambda i, *_: (0, 0)),
        ]
    ones = jnp.ones((BLOCK, LANES), BF16)
    const_args = (consts["bd256"], consts["bd128"], consts["tri"], consts["lev_low"], ones, consts["pair_level"])
    chained = prev is not None
    prev_args = tuple(prev) if chained else ()
    in_specs = ([pl.BlockSpec(memory_space=pl.ANY)] * len(prev_args) + x_specs + [
        pl.BlockSpec((V_ROWS, D_MODEL), lambda i, *_: (layer, 0)),
        pl.BlockSpec((1, IN_W, W_ROWS), lambda i, *_: (layer, 0, jnp.minimum(i, N_PREP - 1))),
        pl.BlockSpec((1, GATE_RANK, GLA_K_W), lambda i, *_: (layer, 0, 0)),
    ] + [full(a) for a in const_args] + [
        pl.BlockSpec((1, a.shape[1] // N_CONV, a.shape[2]), lambda i, *_: (layer, conv_of(i), 0)) for a in ffn_f32])
    out_specs = [pl.BlockSpec((ROW_TILE, D_MIX), lambda i, *_: (prompt_tile_of(i), 0)),
                 pl.BlockSpec((ROW_TILE, PF_W), lambda i, *_: (sample_tile_of(i), 0)),
                 pl.BlockSpec((ROW_TILE, PB_W), lambda i, *_: (sample_tile_of(i), 0)),
                 per_batch(GLA_K_W, GLA_DV), per_batch(BLOCK, ATT_KV_W), per_batch(BLOCK, ATT_KV_W)]
    out_specs += [pl.BlockSpec((a.shape[1] // N_CONV, a.shape[2]), lambda i, *_: (conv_of(i), 0)) for a in ffn_f32]
    out_shape = [jax.ShapeDtypeStruct((TOTAL_ROWS, D_MIX), BF16),
                 jax.ShapeDtypeStruct((SAMPLE_ROWS, PF_W), F32),
                 jax.ShapeDtypeStruct((SAMPLE_ROWS, PB_W), BF16),
                 jax.ShapeDtypeStruct((DEPTH, BATCH, GLA_K_W, GLA_DV), F32),
                 jax.ShapeDtypeStruct((DEPTH, BATCH, BLOCK, ATT_KV_W), F32),
                 jax.ShapeDtypeStruct((DEPTH, BATCH, BLOCK, ATT_KV_W), F32)]
    out_shape += [jax.ShapeDtypeStruct(a.shape[1:], BF16) for a in ffn_f32]
    n_lead_out = 0
    if n_src > 1:
        out_specs = [pl.BlockSpec((ROW_TILE, D_MODEL), lambda i, *_: (tile_of(i), 0))] + out_specs
        out_shape = [jax.ShapeDtypeStruct((TOTAL_ROWS, D_MODEL), F32)] + out_shape
        n_lead_out = 1
    grid_spec = pltpu.PrefetchScalarGridSpec(
        num_scalar_prefetch=1,
        grid=(N_PREP + N_BLOCKS + SAMPLE_TILES,),
        in_specs=in_specs,
        out_specs=out_specs,
        scratch_shapes=[pltpu.VMEM((D_MODEL, PROJ_W), BF16), pltpu.VMEM((LANES, GLA_K_W), BF16),
                        pltpu.VMEM((ROW_TILE, PROJ_W), F32),
                        pltpu.VMEM((ROW_TILE, PF_W), F32), pltpu.VMEM((ROW_TILE, PB_W), BF16),
                        pltpu.VMEM((BATCH * N_KV_HEADS, BLOCK, ATT_KV_W), BF16),
                        pltpu.VMEM((BATCH * N_KV_HEADS, BLOCK, ATT_KV_W), BF16),
                        pltpu.VMEM((BATCH, GLA_K_W, GLA_DV), F32),
                        pltpu.VMEM((BATCH * GLA_HEADS // 2, 2 * GLA_DK, 2 * GLA_DV), BF16)],
    )
    res = pl.pallas_call(
        functools.partial(_front_kernel, layer=layer, n_src=n_src, chained=chained),
        grid_spec=grid_spec,
        out_shape=out_shape,
        input_output_aliases={1 + n: n_lead_out + 3 + n for n in range(len(prev_args))},
        compiler_params=pltpu.CompilerParams(dimension_semantics=("arbitrary",),
                                             vmem_limit_bytes=VMEM_LIMIT),
        name="front",
    )(sinks, *prev_args, *x_srcs, vecs, w_in, w_g2, *const_args, *ffn_f32)
    x_all = res[0] if n_src > 1 else x_srcs[0]
    merged, pfs, pbs, ps, pk, pv = res[n_lead_out:n_lead_out + 6]
    return x_all, merged, (pfs, pbs), (ps, pk, pv), tuple(res[n_lead_out + 6:])


def _mix_sample_kernel(*refs, layer, n_alias):
    sinks_ref, refs = refs[0], refs[n_alias + 1:]
    (pf_ref, pb_ref, ck_ref, cv_ref, st_ref,
     vec_ref, tri_ref, sones_ref, lev_ref, ones_ref, plev_ref,
     m_ref, ck_out_ref, cv_out_ref, st_out_ref) = refs
    q_ref, k_ref, v_ref, gq_ref, gk_ref, ld_ref, gv_ref, og_ref = _proj_views(pf_ref, pb_ref)
    gq = gq_ref[...]
    gk = gk_ref[...]
    gv = gv_ref[...]
    gvf = gv.astype(F32)
    hi, lo = _split(ld_ref[...])
    tri = tri_ref[...]
    g_cum = _dot(tri, hi) + _dot(tri, lo)
    sones = sones_ref[...]
    g_tot = _dot(sones, hi) + _dot(sones, lo)
    lev = lev_ref[...]
    decay_levels = jnp.exp(_dot(lev, hi) + _dot(lev, lo))
    a = _gla_intra([gq], [gk], [[decay_levels[l * BLOCK:(l + 1) * BLOCK] for l in range(LOW_LEVELS)]],
                   range(LOW_LEVELS), plev_ref[...])[0]
    o_intra = jnp.concatenate(
        [_dot(a[h].astype(BF16), gv[:, h * GLA_DV:(h + 1) * GLA_DV]) for h in range(GLA_HEADS)], axis=1)
    q_dec = gq * jnp.exp(g_cum)
    k_dec = gk * jnp.exp(g_tot - g_cum)
    hm8 = _head_masks(DEC_SEQ)
    ones8 = ones_ref[...]
    hi_f, lo_f = hi.astype(F32), lo.astype(F32)

    qf = q_ref[...].astype(F32)
    kf = k_ref[...]
    vf = v_ref[...]
    low8 = lax.broadcasted_iota(jnp.int32, (DEC_SEQ, LANES), 1) < HEAD_DIM
    nkeys = WINDOW + DEC_SEQ
    srow = lax.broadcasted_iota(jnp.int32, (N_HEADS * DEC_SEQ, nkeys), 0)
    scol = lax.broadcasted_iota(jnp.int32, (N_HEADS * DEC_SEQ, nkeys), 1)
    t_of_row = srow & (DEC_SEQ - 1)
    amask = ((scol < WINDOW) & (scol > t_of_row)) | ((scol >= WINDOW) & (scol - WINDOW <= t_of_row))
    rid = lax.broadcasted_iota(jnp.int32, (N_HEADS * DEC_SEQ, 1), 0) >> LOW_LEVELS
    sink_col = jnp.zeros((N_HEADS * DEC_SEQ, 1), F32)
    for i in range(N_HEADS):
        sink_col = jnp.where(rid == i, sinks_ref[layer, i], sink_col)

    seqs = range(SEQ_GROUP)
    rows = [slice(b * DEC_SEQ, (b + 1) * DEC_SEQ) for b in seqs]

    def stacked_queries(b):
        pieces = []
        for j in range(N_HEADS // 2):
            g = (2 * j) // (N_HEADS // N_KV_HEADS)
            qt = qf[rows[b], j * LANES:(j + 1) * LANES]
            swapped = pltpu.roll(qt, HEAD_DIM, axis=1)
            own_lanes = low8 if g == 0 else ~low8
            even, odd = (qt, swapped) if g == 0 else (swapped, qt)
            pieces += [jnp.where(own_lanes, even, 0.0), jnp.where(own_lanes, odd, 0.0)]
        return jnp.concatenate(pieces, axis=0).astype(BF16)

    lane = lax.broadcasted_iota(jnp.int32, (WINDOW, WINDOW), 1)
    old_lanes = lane < WINDOW - DEC_SEQ

    def shifted_buffer(buf, new_rows):
        wide = jnp.concatenate([jnp.zeros((WINDOW - DEC_SEQ, LANES), F32), new_rows], axis=0)
        return jnp.where(old_lanes, pltpu.roll(buf, WINDOW - DEC_SEQ, axis=1), wide.T)

    qp = [stacked_queries(b) for b in seqs]
    ck = [ck_ref[0, b] for b in seqs]
    cv = [cv_ref[0, b] for b in seqs]
    for b in seqs:
        ck_out_ref[0, b] = shifted_buffer(ck[b], kf[rows[b]])
        cv_out_ref[0, b] = shifted_buffer(cv[b], vf[rows[b]])
    s = [jnp.concatenate([_dot(qp[b], ck[b].astype(BF16)), _dot_nt(qp[b], kf[rows[b]].astype(BF16))], axis=1)
         for b in seqs]
    s = [jnp.where(amask, s[b], MASK_VALUE) for b in seqs]
    m = [jnp.maximum(jnp.max(s[b], axis=-1, keepdims=True), sink_col) for b in seqs]
    p = [jnp.exp(s[b] - m[b]) for b in seqs]
    denom = [jnp.sum(p[b], axis=-1, keepdims=True) + jnp.exp(sink_col - m[b]) for b in seqs]
    pb = [p[b].astype(BF16) for b in seqs]
    ob = [(_dot_nt(pb[b][:, :WINDOW], cv[b].astype(BF16))
           + _dot(pb[b][:, WINDOW:], vf[rows[b]].astype(BF16))) / denom[b] for b in seqs]

    def head_tiles(o):
        tiles = []
        for j in range(N_HEADS // 2):
            g = (2 * j) // (N_HEADS // N_KV_HEADS)
            even = o[(2 * j) * DEC_SEQ:(2 * j + 1) * DEC_SEQ]
            odd = o[(2 * j + 1) * DEC_SEQ:(2 * j + 2) * DEC_SEQ]
            if g == 0:
                tiles.append(jnp.where(low8, even, pltpu.roll(odd, HEAD_DIM, axis=1)))
            else:
                tiles.append(jnp.where(low8, pltpu.roll(even, HEAD_DIM, axis=1), odd))
        return jnp.concatenate(tiles, axis=1)

    att_rows = [head_tiles(ob[b]) for b in seqs]

    head_stack = lambda x: jnp.concatenate([jnp.where(mk, x, 0.0) for mk in hm8], axis=0).astype(BF16)
    state = [st_ref[0, b] for b in seqs]
    oi = [_dot(head_stack(q_dec[rows[b]]), state[b].astype(BF16)) for b in seqs]
    inter_rows = [jnp.concatenate([oi[b][h * DEC_SEQ:(h + 1) * DEC_SEQ] for h in range(GLA_HEADS)], axis=1)
                  for b in seqs]
    vstack = [jnp.concatenate([gvf[rows[b]][:, h * GLA_DV:(h + 1) * GLA_DV] for h in range(GLA_HEADS)],
                              axis=0).astype(BF16) for b in seqs]
    kv = [_dot_tn(head_stack(k_dec[rows[b]]), vstack[b]) for b in seqs]
    decay_col = [jnp.exp(_dot_tn(hi_f[rows[b]], ones8) + _dot_tn(lo_f[rows[b]], ones8)) for b in seqs]
    for b in seqs:
        st_out_ref[0, b] = decay_col[b] * state[b] + kv[b]

    m_ref[:, :ATT_Q_W] = jnp.concatenate(att_rows, axis=0).astype(BF16)
    o = o_intra + jnp.concatenate(inter_rows, axis=0)
    gn = vec_ref[V_GN:V_GN + 1, :GLA_DV]
    for h, gh in enumerate(_gla_merge(o, og_ref[...], gn)):
        m_ref[:, ATT_Q_W + h * GLA_DV:ATT_Q_W + (h + 1) * GLA_DV] = gh.astype(BF16)


def _mix_sample(layer, sinks, merged, proj, cache_k, cache_v, state, vecs, consts, prev):
    tok = lambda w_: pl.BlockSpec((BLOCK, w_), lambda i, *_: (i, 0))
    merged_blk = pl.BlockSpec((BLOCK, D_MIX), lambda i, *_: (MAIN_ROWS // BLOCK + i, 0))
    full = lambda a: pl.BlockSpec(a.shape, lambda i, *_: (0,) * a.ndim)
    seq = lambda a: pl.BlockSpec((1, SEQ_GROUP) + a.shape[2:], lambda i, *_: (layer, i, 0, 0))
    ones8 = jnp.ones((DEC_SEQ, LANES), F32)
    const_args = (consts["seq_tri"], consts["seq_ones"], consts["lev_low"], ones8, consts["pair_level"])
    seq_args = (cache_k, cache_v, state)
    alias_args = (merged,) + (tuple(prev) if prev is not None else ())
    grid_spec = pltpu.PrefetchScalarGridSpec(
        num_scalar_prefetch=1,
        grid=(DEC_BATCH // SEQ_GROUP,),
        in_specs=([pl.BlockSpec(memory_space=pl.ANY)] * len(alias_args)
                  + [tok(a.shape[1]) for a in proj] + [seq(a) for a in seq_args]
                  + [pl.BlockSpec((V_ROWS, D_MODEL), lambda i, *_: (layer, 0))]
                  + [full(a) for a in const_args]),
        out_specs=[merged_blk] + [seq(a) for a in seq_args],
    )
    return pl.pallas_call(
        functools.partial(_mix_sample_kernel, layer=layer, n_alias=len(alias_args)),
        grid_spec=grid_spec,
        out_shape=[jax.ShapeDtypeStruct(merged.shape, merged.dtype)]
        + [jax.ShapeDtypeStruct(a.shape, a.dtype) for a in seq_args],
        input_output_aliases={1 + n: n for n in range(len(alias_args))},
        compiler_params=pltpu.CompilerParams(dimension_semantics=("arbitrary",),
                                             vmem_limit_bytes=VMEM_LIMIT),
        name="mix_sample",
    )(sinks, *alias_args, *proj, *seq_args, vecs, *const_args)


def _out_ffn_kernel(*refs, last):
    x_ref, m_ref, vec_ref, wo_ref, wg_ref, wu_ref, wd_ref = refs[:7]
    out_refs = refs[7:9] if last else refs[7:8]
    act_ref = refs[-1]
    x1 = x_ref[...] + _dot(m_ref[...], wo_ref[...])
    ms = jnp.mean(x1 * x1, axis=-1, keepdims=True)
    h = (x1 * lax.rsqrt(ms + RMS_EPS) * vec_ref[V_NORM2:V_NORM2 + 1, :]).astype(BF16)
    for c in range(D_FFN // FFN_CHUNK):
        cols = slice(c * FFN_CHUNK, (c + 1) * FFN_CHUNK)
        gate = _dot(h, wg_ref[:, cols])
        up = _dot(h, wu_ref[:, cols])
        act_ref[:, cols] = (gate * _sigmoid(gate) * up).astype(BF16)
    y = x1 + _dot(act_ref[...], wd_ref[...])
    if last:
        out_refs[0][...] = y.reshape(BATCH, BLOCK, D_MODEL)

        @pl.when(pl.program_id(0) < SAMPLE_TILES)
        def _():
            out_refs[1][...] = out_refs[0][...].reshape(ROW_TILE, D_MODEL)
    else:
        out_refs[0][...] = y


def _out_ffn(layer, x_all, merged, vecs, ffn_weights, last):
    resident = lambda a: pl.BlockSpec(a.shape, lambda i: (0, 0), pipeline_mode=pl.Buffered(1))
    if last:
        n_tiles = MAIN_TILES + SAMPLE_TILES
        tile_of = lambda i: jnp.where(i < SAMPLE_TILES, MAIN_TILES + i, i - SAMPLE_TILES)
        out_specs = [pl.BlockSpec((BATCH, BLOCK, D_MODEL), lambda i: (0, jnp.maximum(i - SAMPLE_TILES, 0), 0)),
                     pl.BlockSpec((ROW_TILE, D_MODEL), lambda i: (jnp.minimum(i, SAMPLE_TILES - 1), 0))]
        out_shape = [jax.ShapeDtypeStruct((BATCH, SEQ, D_MODEL), F32),
                     jax.ShapeDtypeStruct((SAMPLE_ROWS, D_MODEL), F32)]
    else:
        n_tiles = N_TILES
        tile_of = lambda i: i
        out_specs = [pl.BlockSpec((ROW_TILE, D_MODEL), lambda i: (i, 0))]
        out_shape = [jax.ShapeDtypeStruct((TOTAL_ROWS, D_MODEL), F32)]
    tile = lambda w_: pl.BlockSpec((ROW_TILE, w_), lambda i: (tile_of(i), 0))
    return pl.pallas_call(
        functools.partial(_out_ffn_kernel, last=last),
        grid=(n_tiles,),
        in_specs=[tile(D_MODEL), tile(D_MIX), pl.BlockSpec((V_ROWS, D_MODEL), lambda i: (layer, 0))]
        + [resident(w) for w in ffn_weights],
        out_specs=out_specs,
        out_shape=out_shape,
        scratch_shapes=[pltpu.VMEM((ROW_TILE, D_FFN), BF16)],
        compiler_params=pltpu.CompilerParams(dimension_semantics=("arbitrary",),
                                             vmem_limit_bytes=VMEM_LIMIT),
        name="out_ffn",
    )(x_all, merged, vecs, *ffn_weights)


def _vector_slab(norm1, norm2, q_norm, k_norm, b_g, gla_norm):
    pad = lambda a: jnp.pad(a.astype(F32), ((0, 0), (0, D_MODEL - a.shape[1])))
    rows = [norm1.astype(F32), norm2.astype(F32), pad(jnp.tile(q_norm, (1, N_HEADS)) * ATT_SCALE),
            pad(jnp.tile(k_norm, (1, N_KV_HEADS))), pad(b_g), pad(gla_norm)]
    rows += [jnp.zeros((DEPTH, D_MODEL), F32)] * (V_ROWS - len(rows))
    return jnp.stack(rows, axis=1).reshape(DEPTH * V_ROWS, D_MODEL)


def kernel(x_prompt, x_sample, cache_k, cache_v, state_gla, meta, norm1, w_in, q_norm, k_norm, sinks,
           w_g2, b_g, gla_norm, w_o, norm2, w_gate, w_up, w_down):
    consts = _constants()
    dt = x_prompt.dtype
    vecs = _vector_slab(norm1, norm2, q_norm, k_norm, b_g, gla_norm)
    sinks = sinks.astype(F32)
    lead = jnp.tile(jnp.concatenate([jnp.zeros((PAD_LEN, D_MODEL), dt), meta.astype(dt)], axis=0), (BATCH, 1))
    x_srcs = (x_prompt, x_sample.reshape(SAMPLE_ROWS, D_MODEL), lead)
    to_feature_major = lambda c: c.transpose(0, 1, 3, 4, 2).reshape(DEPTH, DEC_BATCH, ATT_KV_W, WINDOW)
    ck_in, cv_in = to_feature_major(cache_k), to_feature_major(cache_v)
    st_in = state_gla.reshape(DEPTH, DEC_BATCH, GLA_K_W, GLA_DV)
    w_in_t = jnp.swapaxes(w_in, 1, 2)

    prompt_outs, sample_outs = None, None
    for l in range(DEPTH):
        x_all, merged, proj_sample, prompt_outs, ffn_bf16 = _front(
            l, sinks, x_srcs, vecs, w_in_t, w_g2, (w_o, w_gate, w_up, w_down), consts, prompt_outs)
        merged, *sample_outs = _mix_sample(l, sinks, merged, proj_sample, ck_in, cv_in, st_in, vecs, consts,
                                           sample_outs)
        x_srcs = tuple(_out_ffn(l, x_all, merged, vecs, ffn_bf16, last=l == DEPTH - 1))

    y_main, y_sample = x_srcs
    ps, pk, pv = prompt_outs
    sk, sv, ss = sample_outs
    kv5 = lambda a, n: a.reshape(DEPTH, n, N_KV_HEADS, HEAD_DIM, WINDOW).transpose(0, 1, 4, 2, 3)
    st5 = lambda a, n: a.reshape(DEPTH, n, GLA_HEADS, GLA_DK, GLA_DV)
    return (y_main, y_sample.reshape(DEC_BATCH, DEC_SEQ, D_MODEL),
            kv5(pk, BATCH), kv5(pv, BATCH), st5(ps, BATCH), kv5(sk, DEC_BATCH), kv5(sv, DEC_BATCH),
            st5(ss, DEC_BATCH))
```

```python
import functools

import jax
import jax.numpy as jnp
import numpy as np
from jax import lax
from jax.experimental import pallas as pl
from jax.experimental.pallas import tpu as pltpu

F32 = jnp.float32
BF16 = jnp.bfloat16

D_MODEL = 1024
BATCH = 4
SEQ = 4096
DEPTH = 2
DEC_BATCH = 128
DEC_SEQ = 8
N_META = 16
WINDOW = 128
BLOCK = 128
PAD_LEN = BLOCK - N_META
N_HEADS = 8
N_KV_HEADS = 2
HEAD_DIM = 64
ATT_SCALE = HEAD_DIM ** -0.5
ATT_Q_W = N_HEADS * HEAD_DIM
ATT_KV_W = N_KV_HEADS * HEAD_DIM
GLA_HEADS = 4
GLA_DK = 64
GLA_DV = 128
GLA_K_W = GLA_HEADS * GLA_DK
GLA_V_W = GLA_HEADS * GLA_DV
GATE_RANK = 16
GATE_NORMALIZER = 16.0
D_MIX = ATT_Q_W + GLA_V_W
D_FFN = 2816
IN_W = 2320
RMS_EPS = 1e-6
MASK_VALUE = -1e30

LANES = 128
N_BLOCKS = 1 + SEQ // BLOCK
MAIN_ROWS = BATCH * SEQ
SAMPLE_ROWS = DEC_BATCH * DEC_SEQ
LEAD_ROWS = BATCH * BLOCK
TOTAL_ROWS = MAIN_ROWS + SAMPLE_ROWS + LEAD_ROWS
ROW_TILE = 512
MAIN_TILES = MAIN_ROWS // ROW_TILE
SAMPLE_TILES = SAMPLE_ROWS // ROW_TILE
N_TILES = TOTAL_ROWS // ROW_TILE
LEAD_TILE = N_TILES - 1
SEQ_GROUP = BLOCK // DEC_SEQ
N_LEVELS = 7
LOW_LEVELS = 3
LOG_DK = 6
N_PREP = 8
W_ROWS = D_MODEL // N_PREP
N_CONV = 16

SRC_LOW, SRC_OG = 1792, 1808
OFF_Q, OFF_K, OFF_V = 0, 512, 640
OFF_GQ, OFF_GK, OFF_GV, OFF_OG, OFF_LOW = 768, 1024, 1280, 1792, 2304
PROJ_W = OFF_LOW + LANES
PROJ_CHUNK = 512
FFN_CHUNK = 256
VMEM_LIMIT = 58 * 1024 * 1024

V_NORM1, V_NORM2, V_QG, V_KG, V_BG, V_GN, V_ROWS = 0, 1, 2, 3, 4, 5, 8

PF_K, PF_V, PF_GQ, PF_GK, PF_LD, PF_OG, PF_W = 0, 128, 256, 512, 768, 1024, 1536
PB_Q, PB_GV, PB_W = 0, 512, 1024


def _proj_views(pf_ref, pb_ref):
    f = lambda a, b: pf_ref.at[:, a:b]
    return (pb_ref.at[:, PB_Q:PB_GV], f(PF_K, PF_V), f(PF_V, PF_GQ), f(PF_GQ, PF_GK), f(PF_GK, PF_LD),
            f(PF_LD, PF_OG), pb_ref.at[:, PB_GV:PB_W], f(PF_OG, PF_W))


def _dot(a, b):
    return jnp.dot(a, b, preferred_element_type=F32)


def _dot_nt(a, b):
    return lax.dot_general(a, b, (((1,), (1,)), ((), ())), preferred_element_type=F32)


def _dot_tn(a, b):
    return lax.dot_general(a, b, (((0,), (0,)), ((), ())), preferred_element_type=F32)


def _split(x):
    hi = x.astype(BF16)
    lo = (x - hi.astype(F32)).astype(BF16)
    return hi, lo


def _sigmoid(x):
    return 1.0 / (1.0 + jnp.exp(-x))


def _level_matrix(levels, n=BLOCK):
    out = np.zeros((len(levels) * n, n), np.float32)
    for i, l in enumerate(levels):
        size = 2 << l
        for t in range(n):
            mid = (t // size) * size + size // 2 - 1
            if (t >> l) & 1:
                out[i * n + t, mid + 1:t + 1] = 1.0
            else:
                out[i * n + t, t + 1:mid + 1] = 1.0
    return out


def _constants():
    r = np.arange(BLOCK)
    tri = (r[None, :] <= r[:, None]).astype(np.float32)
    same_seq = (r[None, :] // DEC_SEQ) == (r[:, None] // DEC_SEQ)
    diff = np.maximum(r[:, None] ^ r[None, :], 1)
    pair_level = np.where(r[None, :] < r[:, None], np.floor(np.log2(diff)).astype(np.int32),
                          np.where(r[None, :] == r[:, None], N_LEVELS, N_LEVELS + 1)).astype(np.int32)
    return dict(
        tri=jnp.asarray(tri, BF16),
        seq_tri=jnp.asarray(tri * same_seq, BF16),
        seq_ones=jnp.asarray(same_seq.astype(np.float32), BF16),
        lev_low=jnp.asarray(_level_matrix(range(LOW_LEVELS)), BF16),
        pair_level=jnp.asarray(pair_level),
    )


def _head_masks(rows):
    lane = lax.broadcasted_iota(jnp.int32, (rows, GLA_K_W), 1)
    return [(lane >> LOG_DK) == h for h in range(GLA_HEADS)]


def _gla_intra(gq, gk, decay_levels, levels, pair_level):
    rows = gq[0].shape[0]
    streams = range(len(gq))
    low_half = lax.broadcasted_iota(jnp.int32, (rows, LANES), 1) < GLA_DK

    def pair_products(qh, kh):
        qh, kh = qh.astype(BF16), kh.astype(BF16)
        out = []
        for pair in range(GLA_HEADS // 2):
            kl = kh[:, pair * LANES:(pair + 1) * LANES]
            zero = jnp.zeros_like(kl)
            stacked = jnp.concatenate([jnp.where(low_half, kl, zero), jnp.where(low_half, zero, kl)], axis=0)
            p = _dot_nt(qh[:, pair * LANES:(pair + 1) * LANES], stacked)
            out += [p[:, :rows], p[:, rows:]]
        return out

    row = lax.broadcasted_iota(jnp.int32, (rows, 1), 0)

    def upper_q_lower_k(b, l):
        size = 1 << l
        if size % 8:
            return jnp.where(((row >> l) & 1) == 1, gq[b], gk[b])
        return jnp.concatenate([(gq[b] if i & 1 else gk[b])[i * size:(i + 1) * size]
                                for i in range(rows // size)], axis=0)

    on_diag = pair_level == N_LEVELS
    a = [[jnp.where(on_diag, p, 0.0) for p in pair_products(gq[b], gk[b])] for b in streams]
    for i, l in enumerate(levels):
        at_level = pair_level == l
        for b in streams:
            x = upper_q_lower_k(b, l) * decay_levels[b][i]
            a[b] = [jnp.where(at_level, p, a_h) for p, a_h in zip(pair_products(x, x), a[b])]
    return a


def _gla_merge(o, og, gn):
    outs = []
    for h in range(GLA_HEADS):
        oh = o[:, h * GLA_DV:(h + 1) * GLA_DV]
        gh = og[:, h * GLA_DV:(h + 1) * GLA_DV]
        ms = jnp.mean(oh * oh, axis=-1, keepdims=True)
        outs.append(oh * lax.rsqrt(ms + RMS_EPS) * gn * (gh * _sigmoid(gh)))
    return outs


def _dup_halves(x):
    low = lax.broadcasted_iota(jnp.int32, x.shape, 1) < HEAD_DIM
    rolled = pltpu.roll(x, HEAD_DIM, axis=1)
    return jnp.where(low, x, rolled), jnp.where(low, rolled, x)


def _convert_proj_weights(i, w_ref, wg2_ref, wbf_ref, wg2s_ref):
    rows = pl.ds(pl.multiple_of(i * W_ROWS, W_ROWS), W_ROWS)

    def put(dst, src, n=LANES):
        tile = w_ref[0, src:src + n, :]
        if n < LANES:
            tile = jnp.concatenate([tile, jnp.zeros((LANES - n, W_ROWS), F32)], axis=0)
        wbf_ref[rows, dst:dst + LANES] = tile.T.astype(BF16)

    for c in range(SRC_LOW // LANES):
        put(c * LANES, c * LANES)
    for c in range((IN_W - SRC_OG) // LANES):
        put(OFF_OG + c * LANES, SRC_OG + c * LANES)
    put(OFF_LOW, SRC_LOW, GATE_RANK)

    @pl.when(i == 0)
    def _():
        wg2s_ref[...] = jnp.concatenate(
            [wg2_ref[0], jnp.zeros((LANES - GATE_RANK, GLA_K_W), F32)], axis=0).astype(BF16)


def _project_tile(x, vec_ref, wbf_ref, wg2s_ref, z_ref, pf_ref, pb_ref):
    q_ref, k_ref, v_ref, gq_ref, gk_ref, ld_ref, gv_ref, og_ref = _proj_views(pf_ref, pb_ref)
    ms = jnp.mean(x * x, axis=-1, keepdims=True)
    h = (x * lax.rsqrt(ms + RMS_EPS) * vec_ref[V_NORM1:V_NORM1 + 1, :]).astype(BF16)

    low_half = lax.broadcasted_iota(jnp.int32, (ROW_TILE, LANES), 1) < HEAD_DIM

    def head_norm(t):
        t2 = t * t
        ms_lo = jnp.sum(jnp.where(low_half, t2, 0.0), axis=-1, keepdims=True) * (1.0 / HEAD_DIM)
        ms_hi = jnp.sum(jnp.where(low_half, 0.0, t2), axis=-1, keepdims=True) * (1.0 / HEAD_DIM)
        return t * jnp.where(low_half, lax.rsqrt(ms_lo + RMS_EPS), lax.rsqrt(ms_hi + RMS_EPS))

    def q_part():
        for j in range(ATT_Q_W // LANES):
            cols = slice(j * LANES, (j + 1) * LANES)
            q_ref[:, cols] = (head_norm(z_ref[:, cols]) * vec_ref[V_QG:V_QG + 1, cols]).astype(BF16)

    def kv_part():
        k_ref[...] = head_norm(z_ref[:, OFF_K:OFF_V]) * vec_ref[V_KG:V_KG + 1, :ATT_KV_W]
        v_ref[...] = z_ref[:, OFF_V:OFF_GQ]
        gq_ref[...] = z_ref[:, OFF_GQ:OFF_GK] * (GLA_DK ** -0.5)

    def gk_part():
        gk_ref[...] = z_ref[:, OFF_GK:OFF_GV]

    def gv_part():
        gv_ref[...] = z_ref[:, OFF_GV:OFF_OG].astype(BF16)

    def tail_part():
        og_ref[...] = z_ref[:, OFF_OG:OFF_LOW]
        logit = (_dot(z_ref[:, OFF_LOW:PROJ_W].astype(BF16), wg2s_ref[...])
                 + vec_ref[V_BG:V_BG + 1, :GLA_K_W])
        log_sig = jnp.minimum(logit, 0.0) - jnp.log1p(jnp.exp(-jnp.abs(logit)))
        ld_ref[...] = log_sig * (1.0 / GATE_NORMALIZER)

    pieces = [q_part, kv_part, gk_part, gv_part, tail_part]
    for c, piece in enumerate(pieces):
        cols = slice(c * PROJ_CHUNK, min((c + 1) * PROJ_CHUNK, PROJ_W))
        z_ref[:, cols] = _dot(h, wbf_ref[:, cols])
        piece()


def _attention_units(blk, layer, sinks_ref, pf_ref, pb_ref, m_ref, kprev_ref, vprev_ref):
    q_ref, k_ref, v_ref = _proj_views(pf_ref, pb_ref)[:3]
    per_group = N_HEADS // N_KV_HEADS
    grows = per_group * BLOCK
    row = lax.broadcasted_iota(jnp.int32, (grows, BLOCK), 0) & (BLOCK - 1)
    col = lax.broadcasted_iota(jnp.int32, (grows, BLOCK), 1)
    own = col <= row
    kpos = jnp.where(own, blk * BLOCK, (blk - 1) * BLOCK) + col - PAD_LEN
    live = kpos >= 0
    low_half = lax.broadcasted_iota(jnp.int32, (BLOCK, LANES), 1) < HEAD_DIM
    head_of_row = lax.broadcasted_iota(jnp.int32, (grows, 1), 0) >> N_LEVELS
    sink_cols = []
    for g in range(N_KV_HEADS):
        sink = jnp.zeros((grows, 1), F32)
        for r in range(per_group):
            sink = jnp.where(head_of_row == r, sinks_ref[layer, g * per_group + r], sink)
        sink_cols.append(sink)

    units =[(b, g) for b in range(BATCH) for g in range(N_KV_HEADS)]
    rows = [slice(b * BLOCK, (b + 1) * BLOCK) for b in range(BATCH)]
    tiles = [range(g * per_group // 2, (g + 1) * per_group // 2) for g in range(N_KV_HEADS)]
    kdup = [[x.astype(BF16) for x in _dup_halves(k_ref[rows[b], :])] for b in range(BATCH)]
    vdup = [[x.astype(BF16) for x in _dup_halves(v_ref[rows[b], :])] for b in range(BATCH)]
    kk = [jnp.concatenate([kprev_ref[b * N_KV_HEADS + g], kdup[b][g]], axis=0) for b, g in units]
    vv = [jnp.concatenate([vprev_ref[b * N_KV_HEADS + g], vdup[b][g]], axis=0) for b, g in units]
    for b, g in units:
        kprev_ref[b * N_KV_HEADS + g] = kdup[b][g]
        vprev_ref[b * N_KV_HEADS + g] = vdup[b][g]

    def stacked_queries(b, g):
        pieces = []
        for j in tiles[g]:
            qt = q_ref[rows[b], j * LANES:(j + 1) * LANES]
            zero = jnp.zeros_like(qt)
            pieces += [jnp.where(low_half, qt, zero), jnp.where(low_half, zero, qt)]
        return jnp.concatenate(pieces, axis=0)

    def attend(u, b, g):
        s = _dot_nt(stacked_queries(b, g), kk[u])
        s = jnp.where(live, jnp.where(own, s[:, BLOCK:], s[:, :BLOCK]), MASK_VALUE)
        m = jnp.maximum(jnp.max(s, axis=-1, keepdims=True), sink_cols[g])
        p = jnp.exp(s - m)
        denom = jnp.sum(p, axis=-1, keepdims=True) + jnp.exp(sink_cols[g] - m)
        p2 = jnp.concatenate([jnp.where(own, 0.0, p), jnp.where(own, p, 0.0)], axis=1).astype(BF16)
        o = _dot(p2, vv[u]) / denom
        for n, j in enumerate(tiles[g]):
            o_lo = o[2 * n * BLOCK:(2 * n + 1) * BLOCK]
            o_hi = o[(2 * n + 1) * BLOCK:(2 * n + 2) * BLOCK]
            m_ref[rows[b], j * LANES:(j + 1) * LANES] = jnp.where(low_half, o_lo, o_hi).astype(BF16)

    return [functools.partial(attend, u, b, g) for u, (b, g) in enumerate(units)]


def _gla_tile(blk, pf_ref, pb_ref, vec_ref, tri_ref, lev_ref, ones_ref, plev_ref, m_ref, state_ref, sbd_ref):
    gq_ref, gk_ref, ld_ref, gv_ref, og_ref = _proj_views(pf_ref, pb_ref)[3:]
    rpos = blk * BLOCK + lax.broadcasted_iota(jnp.int32, (BLOCK, 1), 0) - PAD_LEN
    valid = (rpos >= 0).astype(F32)
    ld_all = jnp.concatenate([ld_ref[b * BLOCK:(b + 1) * BLOCK, :] for b in range(BATCH)], axis=1)
    hi, lo = _split(ld_all)
    tri = tri_ref[...]
    g_cum_all = _dot(tri, hi) + _dot(tri, lo)
    lev = lev_ref[...]
    low_sums = _dot(lev, hi) + _dot(lev, lo)
    level_sums = [low_sums[l * BLOCK:(l + 1) * BLOCK] for l in range(LOW_LEVELS)]
    for l in range(LOW_LEVELS, N_LEVELS):
        half = 1 << l
        pieces = []
        for p in range(BLOCK // (2 * half)):
            mid = g_cum_all[p * 2 * half + half - 1:p * 2 * half + half, :]
            pieces += [mid - g_cum_all[p * 2 * half:p * 2 * half + half],
                       g_cum_all[p * 2 * half + half:(p + 1) * 2 * half] - mid]
        level_sums.append(jnp.concatenate(pieces, axis=0))
    decay_all = [jnp.exp(s) for s in level_sums]
    ones = ones_ref[...]
    decay_col_all = jnp.exp(_dot_tn(hi, ones) + _dot_tn(lo, ones))
    pair_level = plev_ref[...]
    gn = vec_ref[V_GN:V_GN + 1, :GLA_DV]
    streams = range(BATCH)
    rows = [slice(b * BLOCK, (b + 1) * BLOCK) for b in streams]
    cols = [slice(b * GLA_K_W, (b + 1) * GLA_K_W) for b in streams]
    gq = [gq_ref[rows[b], :] for b in streams]
    gk = [gk_ref[rows[b], :] * valid for b in streams]
    gv = [gv_ref[rows[b], :] for b in streams]
    g_cum = [g_cum_all[:, cols[b]] for b in streams]
    a = _gla_intra(gq, gk, [[d[:, cols[b]] for d in decay_all] for b in streams], range(N_LEVELS), pair_level)

    pairs = range(GLA_HEADS // 2)
    q_dec = [(gq[b] * jnp.exp(g_cum[b])).astype(BF16) for b in streams]
    o = [jnp.concatenate([_dot(q_dec[b][:, i * LANES:(i + 1) * LANES], sbd_ref[2 * b + i]) for i in pairs], axis=1)
         for b in streams]
    o = [o[b] + jnp.concatenate(
        [_dot(a[b][h].astype(BF16), gv[b][:, h * GLA_DV:(h + 1) * GLA_DV]) for h in range(GLA_HEADS)], axis=1)
        for b in streams]
    for b in streams:
        for h, gh in enumerate(_gla_merge(o[b], og_ref[rows[b], :], gn)):
            m_ref[rows[b], ATT_Q_W + h * GLA_DV:ATT_Q_W + (h + 1) * GLA_DV] = gh.astype(BF16)

    k_dec = [(gk[b] * jnp.exp(g_cum[b][BLOCK - 1:BLOCK, :] - g_cum[b])).astype(BF16) for b in streams]
    kv = [[_dot_tn(k_dec[b][:, i * LANES:(i + 1) * LANES], gv[b][:, 2 * i * GLA_DV:(2 * i + 2) * GLA_DV])
           for i in pairs] for b in streams]
    for b in streams:
        new_state = decay_col_all[cols[b], :] * state_ref[b] + jnp.concatenate(
            [kv[b][h // 2][(h % 2) * GLA_DK:(h % 2 + 1) * GLA_DK, (h % 2) * GLA_DV:(h % 2 + 1) * GLA_DV]
             for h in range(GLA_HEADS)], axis=0)
        state_ref[b] = new_state
        for h in range(GLA_HEADS):
            sbd_ref[2 * b + h // 2, (h % 2) * GLA_DK:(h % 2 + 1) * GLA_DK,
                    (h % 2) * GLA_DV:(h % 2 + 1) * GLA_DV] = new_state[h * GLA_DK:(h + 1) * GLA_DK].astype(BF16)


def _front_kernel(*refs, layer, n_src, chained):
    sinks_ref, refs = refs[0], refs[1:]
    if chained:
        refs = refs[3:]
    x_refs, refs = refs[:n_src], refs[n_src:]
    vec_ref, w_ref, wg2_ref, tri_ref, lev_ref, ones_ref, plev_ref = refs[:7]
    ffn_f32_refs, refs = refs[7:11], refs[11:]
    if n_src > 1:
        xo_ref, refs = refs[0], refs[1:]
    (m_ref, pfs_ref, pbs_ref, s_out_ref, k_out_ref, v_out_ref) = refs[:6]
    ffn_bf16_refs, refs = refs[6:10], refs[10:]
    wbf_ref, wg2s_ref, z_ref, pf_ref, pb_ref, kprev_ref, vprev_ref, state_ref, sbd_ref = refs
    i = pl.program_id(0)
    t = i - N_PREP

    @pl.when(i < N_PREP)
    def _():
        _convert_proj_weights(i, w_ref, wg2_ref, wbf_ref, wg2s_ref)

    @pl.when((t >= 1) & (t <= N_CONV))
    def _():
        for src, dst in zip(ffn_f32_refs, ffn_bf16_refs):
            dst[...] = src[0].astype(BF16)

    @pl.when(i == 0)
    def _():
        kprev_ref[...] = jnp.zeros_like(kprev_ref)
        vprev_ref[...] = jnp.zeros_like(vprev_ref)
        state_ref[...] = jnp.zeros_like(state_ref)
        sbd_ref[...] = jnp.zeros_like(sbd_ref)

    def load_x():
        if n_src == 1:
            return x_refs[0][...]
        x = jnp.where(t == 0, x_refs[2][...],
                      jnp.where(t < N_BLOCKS, x_refs[0][...].reshape(ROW_TILE, D_MODEL), x_refs[1][...]))
        xo_ref[...] = x
        return x

    @pl.when((t >= 0) & (t < N_BLOCKS))
    def _():
        _project_tile(load_x(), vec_ref, wbf_ref, wg2s_ref, z_ref, pf_ref, pb_ref)
        for attend in _attention_units(t, layer, sinks_ref, pf_ref, pb_ref, m_ref, kprev_ref, vprev_ref):
            attend()
        _gla_tile(t, pf_ref, pb_ref, vec_ref, tri_ref, lev_ref, ones_ref, plev_ref, m_ref, state_ref, sbd_ref)

    @pl.when(t == N_BLOCKS - 1)
    def _():
        s_out_ref[0] = state_ref[...]
        for b in range(BATCH):
            k_out_ref[0, b] = pf_ref[b * BLOCK:(b + 1) * BLOCK, PF_K:PF_V].T
            v_out_ref[0, b] = pf_ref[b * BLOCK:(b + 1) * BLOCK, PF_V:PF_GQ].T

    @pl.when(t >= N_BLOCKS)
    def _():
        _project_tile(load_x(), vec_ref, wbf_ref, wg2s_ref, z_ref, pfs_ref, pbs_ref)


def _front(layer, sinks, x_srcs, vecs, w_in, w_g2, ffn_f32, consts, prev):
    n_src = len(x_srcs)
    step = lambda i: i - N_PREP
    conv_of = lambda i: jnp.clip(step(i) - 1, 0, N_CONV - 1)
    tile_of = lambda i: jnp.where(step(i) <= 0, LEAD_TILE, jnp.minimum(step(i) - 1, LEAD_TILE - 1))
    prompt_tile_of = lambda i: jnp.where(step(i) <= 0, LEAD_TILE, jnp.minimum(step(i) - 1, MAIN_TILES - 1))
    sample_tile_of = lambda i: jnp.clip(step(i) - N_BLOCKS, 0, SAMPLE_TILES - 1)
    full = lambda a: pl.BlockSpec(a.shape, lambda i, *_: (0,) * a.ndim)
    per_batch = lambda r, c: pl.BlockSpec((1, BATCH, r, c), lambda i, *_: (layer, 0, 0, 0))
    if n_src == 1:
        x_specs = [pl.BlockSpec((ROW_TILE, D_MODEL), lambda i, *_: (tile_of(i), 0))]
    else:
        x_specs = [
            pl.BlockSpec((BATCH, BLOCK, D_MODEL), lambda i, *_: (0, jnp.clip(step(i) - 1, 0, MAIN_TILES - 1), 0)),
            pl.BlockSpec((ROW_TILE, D_MODEL), lambda i, *_: (sample_tile_of(i), 0)),
            pl.BlockSpec((ROW_TILE, D_MODEL), lambda i, *_: (0, 0)),
        ]
    ones = jnp.ones((BLOCK, LANES), BF16)
    const_args = (consts["tri"], consts["lev_low"], ones, consts["pair_level"])
    chained = prev is not None
    prev_args = tuple(prev) if chained else ()
    in_specs = ([pl.BlockSpec(memory_space=pl.ANY)] * len(prev_args) + x_specs + [
        pl.BlockSpec((V_ROWS, D_MODEL), lambda i, *_: (layer, 0)),
        pl.BlockSpec((1, IN_W, W_ROWS), lambda i, *_: (layer, 0, jnp.minimum(i, N_PREP - 1))),
        pl.BlockSpec((1, GATE_RANK, GLA_K_W), lambda i, *_: (layer, 0, 0)),
    ] + [full(a) for a in const_args] + [
        pl.BlockSpec((1, a.shape[1] // N_CONV, a.shape[2]), lambda i, *_: (layer, conv_of(i), 0)) for a in ffn_f32])
    out_specs = [pl.BlockSpec((ROW_TILE, D_MIX), lambda i, *_: (prompt_tile_of(i), 0)),
                 pl.BlockSpec((ROW_TILE, PF_W), lambda i, *_: (sample_tile_of(i), 0)),
                 pl.BlockSpec((ROW_TILE, PB_W), lambda i, *_: (sample_tile_of(i), 0)),
                 per_batch(GLA_K_W, GLA_DV), per_batch(BLOCK, ATT_KV_W), per_batch(BLOCK, ATT_KV_W)]
    out_specs += [pl.BlockSpec((a.shape[1] // N_CONV, a.shape[2]), lambda i, *_: (conv_of(i), 0)) for a in ffn_f32]
    out_shape = [jax.ShapeDtypeStruct((TOTAL_ROWS, D_MIX), BF16),
                 jax.ShapeDtypeStruct((SAMPLE_ROWS, PF_W), F32),
                 jax.ShapeDtypeStruct((SAMPLE_ROWS, PB_W), BF16),
                 jax.ShapeDtypeStruct((DEPTH, BATCH, GLA_K_W, GLA_DV), F32),
                 jax.ShapeDtypeStruct((DEPTH, BATCH, BLOCK, ATT_KV_W), F32),
                 jax.ShapeDtypeStruct((DEPTH, BATCH, BLOCK, ATT_KV_W), F32)]
    out_shape += [jax.ShapeDtypeStruct(a.shape[1:], BF16) for a in ffn_f32]
    n_lead_out = 0
    if n_src > 1:
        out_specs = [pl.BlockSpec((ROW_TILE, D_MODEL), lambda i, *_: (tile_of(i), 0))] + out_specs
        out_shape = [jax.ShapeDtypeStruct((TOTAL_ROWS, D_MODEL), F32)] + out_shape
        n_lead_out = 1
    grid_spec = pltpu.PrefetchScalarGridSpec(
        num_scalar_prefetch=1,
        grid=(N_PREP + N_BLOCKS + SAMPLE_TILES,),
        in_specs=in_specs,
        out_specs=out_specs,
        scratch_shapes=[pltpu.VMEM((D_MODEL, PROJ_W), BF16), pltpu.VMEM((LANES, GLA_K_W), BF16),
                        pltpu.VMEM((ROW_TILE, PROJ_W), F32),
                        pltpu.VMEM((ROW_TILE, PF_W), F32), pltpu.VMEM((ROW_TILE, PB_W), BF16),
                        pltpu.VMEM((BATCH * N_KV_HEADS, BLOCK, ATT_KV_W), BF16),
                        pltpu.VMEM((BATCH * N_KV_HEADS, BLOCK, ATT_KV_W), BF16),
                        pltpu.VMEM((BATCH, GLA_K_W, GLA_DV), F32),
                        pltpu.VMEM((BATCH * GLA_HEADS // 2, 2 * GLA_DK, 2 * GLA_DV), BF16)],
    )
    res = pl.pallas_call(
        functools.partial(_front_kernel, layer=layer, n_src=n_src, chained=chained),
        grid_spec=grid_spec,
        out_shape=out_shape,
        input_output_aliases={1 + n: n_lead_out + 3 + n for n in range(len(prev_args))},
        compiler_params=pltpu.CompilerParams(dimension_semantics=("arbitrary",),
                                             vmem_limit_bytes=VMEM_LIMIT),
        name="front",
    )(sinks, *prev_args, *x_srcs, vecs, w_in, w_g2, *const_args, *ffn_f32)
    x_all = res[0] if n_src > 1 else x_srcs[0]
    merged, pfs, pbs, ps, pk, pv = res[n_lead_out:n_lead_out + 6]
    return x_all, merged, (pfs, pbs), (ps, pk, pv), tuple(res[n_lead_out + 6:])


def _mix_sample_kernel(*refs, layer, n_alias):
    sinks_ref, refs = refs[0], refs[n_alias + 1:]
    (pf_ref, pb_ref, ck_ref, cv_ref, st_ref,
     vec_ref, tri_ref, sones_ref, lev_ref, ones_ref, plev_ref,
     m_ref, ck_out_ref, cv_out_ref, st_out_ref) = refs
    q_ref, k_ref, v_ref, gq_ref, gk_ref, ld_ref, gv_ref, og_ref = _proj_views(pf_ref, pb_ref)
    gq = gq_ref[...]
    gk = gk_ref[...]
    gv = gv_ref[...]
    gvf = gv.astype(F32)
    hi, lo = _split(ld_ref[...])
    tri = tri_ref[...]
    g_cum = _dot(tri, hi) + _dot(tri, lo)
    sones = sones_ref[...]
    g_tot = _dot(sones, hi) + _dot(sones, lo)
    lev = lev_ref[...]
    decay_levels = jnp.exp(_dot(lev, hi) + _dot(lev, lo))
    a = _gla_intra([gq], [gk], [[decay_levels[l * BLOCK:(l + 1) * BLOCK] for l in range(LOW_LEVELS)]],
                   range(LOW_LEVELS), plev_ref[...])[0]
    o_intra = jnp.concatenate(
        [_dot(a[h].astype(BF16), gv[:, h * GLA_DV:(h + 1) * GLA_DV]) for h in range(GLA_HEADS)], axis=1)
    q_dec = gq * jnp.exp(g_cum)
    k_dec = gk * jnp.exp(g_tot - g_cum)
    hm8 = _head_masks(DEC_SEQ)
    ones8 = ones_ref[...]
    hi_f, lo_f = hi.astype(F32), lo.astype(F32)

    qf = q_ref[...].astype(F32)
    kf = k_ref[...]
    vf = v_ref[...]
    low8 = lax.broadcasted_iota(jnp.int32, (DEC_SEQ, LANES), 1) < HEAD_DIM
    nkeys = WINDOW + DEC_SEQ
    srow = lax.broadcasted_iota(jnp.int32, (N_HEADS * DEC_SEQ, nkeys), 0)
    scol = lax.broadcasted_iota(jnp.int32, (N_HEADS * DEC_SEQ, nkeys), 1)
    t_of_row = srow & (DEC_SEQ - 1)
    amask = ((scol < WINDOW) & (scol > t_of_row)) | ((scol >= WINDOW) & (scol - WINDOW <= t_of_row))
    rid = lax.broadcasted_iota(jnp.int32, (N_HEADS * DEC_SEQ, 1), 0) >> LOW_LEVELS
    sink_col = jnp.zeros((N_HEADS * DEC_SEQ, 1), F32)
    for i in range(N_HEADS):
        sink_col = jnp.where(rid == i, sinks_ref[layer, i], sink_col)

    seqs = range(SEQ_GROUP)
    rows = [slice(b * DEC_SEQ, (b + 1) * DEC_SEQ) for b in seqs]

    def stacked_queries(b):
        pieces = []
        for j in range(N_HEADS // 2):
            g = (2 * j) // (N_HEADS // N_KV_HEADS)
            qt = qf[rows[b], j * LANES:(j + 1) * LANES]
            swapped = pltpu.roll(qt, HEAD_DIM, axis=1)
            own_lanes = low8 if g == 0 else ~low8
            even, odd = (qt, swapped) if g == 0 else (swapped, qt)
            pieces += [jnp.where(own_lanes, even, 0.0), jnp.where(own_lanes, odd, 0.0)]
        return jnp.concatenate(pieces, axis=0).astype(BF16)

    lane = lax.broadcasted_iota(jnp.int32, (WINDOW, WINDOW), 1)
    old_lanes = lane < WINDOW - DEC_SEQ

    def shifted_buffer(buf, new_rows):
        wide = jnp.concatenate([jnp.zeros((WINDOW - DEC_SEQ, LANES), F32), new_rows], axis=0)
        return jnp.where(old_lanes, pltpu.roll(buf, WINDOW - DEC_SEQ, axis=1), wide.T)

    qp = [stacked_queries(b) for b in seqs]
    ck = [ck_ref[0, b] for b in seqs]
    cv = [cv_ref[0, b] for b in seqs]
    for b in seqs:
        ck_out_ref[0, b] = shifted_buffer(ck[b], kf[rows[b]])
        cv_out_ref[0, b] = shifted_buffer(cv[b], vf[rows[b]])
    s = [jnp.concatenate([_dot(qp[b], ck[b].astype(BF16)), _dot_nt(qp[b], kf[rows[b]].astype(BF16))], axis=1)
         for b in seqs]
    s = [jnp.where(amask, s[b], MASK_VALUE) for b in seqs]
    m = [jnp.maximum(jnp.max(s[b], axis=-1, keepdims=True), sink_col) for b in seqs]
    p = [jnp.exp(s[b] - m[b]) for b in seqs]
    denom = [jnp.sum(p[b], axis=-1, keepdims=True) + jnp.exp(sink_col - m[b]) for b in seqs]
    pb = [p[b].astype(BF16) for b in seqs]
    ob = [(_dot_nt(pb[b][:, :WINDOW], cv[b].astype(BF16))
           + _dot(pb[b][:, WINDOW:], vf[rows[b]].astype(BF16))) / denom[b] for b in seqs]

    def head_tiles(o):
        tiles = []
        for j in range(N_HEADS // 2):
            g = (2 * j) // (N_HEADS // N_KV_HEADS)
            even = o[(2 * j) * DEC_SEQ:(2 * j + 1) * DEC_SEQ]
            odd = o[(2 * j + 1) * DEC_SEQ:(2 * j + 2) * DEC_SEQ]
            if g == 0:
                tiles.append(jnp.where(low8, even, pltpu.roll(odd, HEAD_DIM, axis=1)))
            else:
                tiles.append(jnp.where(low8, pltpu.roll(even, HEAD_DIM, axis=1), odd))
        return jnp.concatenate(tiles, axis=1)

    att_rows = [head_tiles(ob[b]) for b in seqs]

    head_stack = lambda x: jnp.concatenate([jnp.where(mk, x, 0.0) for mk in hm8], axis=0).astype(BF16)
    state = [st_ref[0, b] for b in seqs]
    oi = [_dot(head_stack(q_dec[rows[b]]), state[b].astype(BF16)) for b in seqs]
    inter_rows = [jnp.concatenate([oi[b][h * DEC_SEQ:(h + 1) * DEC_SEQ] for h in range(GLA_HEADS)], axis=1)
                  for b in seqs]
    vstack = [jnp.concatenate([gvf[rows[b]][:, h * GLA_DV:(h + 1) * GLA_DV] for h in range(GLA_HEADS)],
                              axis=0).astype(BF16) for b in seqs]
    kv = [_dot_tn(head_stack(k_dec[rows[b]]), vstack[b]) for b in seqs]
    decay_col = [jnp.exp(_dot_tn(hi_f[rows[b]], ones8) + _dot_tn(lo_f[rows[b]], ones8)) for b in seqs]
    for b in seqs:
        st_out_ref[0, b] = decay_col[b] * state[b] + kv[b]

    m_ref[:, :ATT_Q_W] = jnp.concatenate(att_rows, axis=0).astype(BF16)
    o = o_intra + jnp.concatenate(inter_rows, axis=0)
    gn = vec_ref[V_GN:V_GN + 1, :GLA_DV]
    for h, gh in enumerate(_gla_merge(o, og_ref[...], gn)):
        m_ref[:, ATT_Q_W + h * GLA_DV:ATT_Q_W + (h + 1) * GLA_DV] = gh.astype(BF16)


def _mix_sample(layer, sinks, merged, proj, cache_k, cache_v, state, vecs, consts, prev):
    tok = lambda w_: pl.BlockSpec((BLOCK, w_), lambda i, *_: (i, 0))
    merged_blk = pl.BlockSpec((BLOCK, D_MIX), lambda i, *_: (MAIN_ROWS // BLOCK + i, 0))
    full = lambda a: pl.BlockSpec(a.shape, lambda i, *_: (0,) * a.ndim)
    seq = lambda a: pl.BlockSpec((1, SEQ_GROUP) + a.shape[2:], lambda i, *_: (layer, i, 0, 0))
    ones8 = jnp.ones((DEC_SEQ, LANES), F32)
    const_args = (consts["seq_tri"], consts["seq_ones"], consts["lev_low"], ones8, consts["pair_level"])
    seq_args = (cache_k, cache_v, state)
    alias_args = (merged,) + (tuple(prev) if prev is not None else ())
    grid_spec = pltpu.PrefetchScalarGridSpec(
        num_scalar_prefetch=1,
        grid=(DEC_BATCH // SEQ_GROUP,),
        in_specs=([pl.BlockSpec(memory_space=pl.ANY)] * len(alias_args)
                  + [tok(a.shape[1]) for a in proj] + [seq(a) for a in seq_args]
                  + [pl.BlockSpec((V_ROWS, D_MODEL), lambda i, *_: (layer, 0))]
                  + [full(a) for a in const_args]),
        out_specs=[merged_blk] + [seq(a) for a in seq_args],
    )
    return pl.pallas_call(
        functools.partial(_mix_sample_kernel, layer=layer, n_alias=len(alias_args)),
        grid_spec=grid_spec,
        out_shape=[jax.ShapeDtypeStruct(merged.shape, merged.dtype)]
        + [jax.ShapeDtypeStruct(a.shape, a.dtype) for a in seq_args],
        input_output_aliases={1 + n: n for n in range(len(alias_args))},
        compiler_params=pltpu.CompilerParams(dimension_semantics=("arbitrary",),
                                             vmem_limit_bytes=VMEM_LIMIT),
        name="mix_sample",
    )(sinks, *alias_args, *proj, *seq_args, vecs, *const_args)


def _out_ffn_kernel(*refs, last):
    x_ref, m_ref, vec_ref, wo_ref, wg_ref, wu_ref, wd_ref = refs[:7]
    out_refs = refs[7:9] if last else refs[7:8]
    act_ref = refs[-1]
    x1 = x_ref[...] + _dot(m_ref[...], wo_ref[...])
    ms = jnp.mean(x1 * x1, axis=-1, keepdims=True)
    h = (x1 * lax.rsqrt(ms + RMS_EPS) * vec_ref[V_NORM2:V_NORM2 + 1, :]).astype(BF16)
    for c in range(D_FFN // FFN_CHUNK):
        cols = slice(c * FFN_CHUNK, (c + 1) * FFN_CHUNK)
        gate = _dot(h, wg_ref[:, cols])
        up = _dot(h, wu_ref[:, cols])
        act_ref[:, cols] = (gate * _sigmoid(gate) * up).astype(BF16)
    y = x1 + _dot(act_ref[...], wd_ref[...])
    if last:
        out_refs[0][...] = y.reshape(BATCH, BLOCK, D_MODEL)

        @pl.when(pl.program_id(0) < SAMPLE_TILES)
        def _():
            out_refs[1][...] = out_refs[0][...].reshape(ROW_TILE, D_MODEL)
    else:
        out_refs[0][...] = y


def _out_ffn(layer, x_all, merged, vecs, ffn_weights, last):
    resident = lambda a: pl.BlockSpec(a.shape, lambda i: (0, 0), pipeline_mode=pl.Buffered(1))
    if last:
        n_tiles = MAIN_TILES + SAMPLE_TILES
        tile_of = lambda i: jnp.where(i < SAMPLE_TILES, MAIN_TILES + i, i - SAMPLE_TILES)
        out_specs = [pl.BlockSpec((BATCH, BLOCK, D_MODEL), lambda i: (0, jnp.maximum(i - SAMPLE_TILES, 0), 0)),
                     pl.BlockSpec((ROW_TILE, D_MODEL), lambda i: (jnp.minimum(i, SAMPLE_TILES - 1), 0))]
        out_shape = [jax.ShapeDtypeStruct((BATCH, SEQ, D_MODEL), F32),
                     jax.ShapeDtypeStruct((SAMPLE_ROWS, D_MODEL), F32)]
    else:
        n_tiles = N_TILES
        tile_of = lambda i: i
        out_specs = [pl.BlockSpec((ROW_TILE, D_MODEL), lambda i: (i, 0))]
        out_shape = [jax.ShapeDtypeStruct((TOTAL_ROWS, D_MODEL), F32)]
    tile = lambda w_: pl.BlockSpec((ROW_TILE, w_), lambda i: (tile_of(i), 0))
    return pl.pallas_call(
        functools.partial(_out_ffn_kernel, last=last),
        grid=(n_tiles,),
        in_specs=[tile(D_MODEL), tile(D_MIX), pl.BlockSpec((V_ROWS, D_MODEL), lambda i: (layer, 0))]
        + [resident(w) for w in ffn_weights],
        out_specs=out_specs,
        out_shape=out_shape,
        scratch_shapes=[pltpu.VMEM((ROW_TILE, D_FFN), BF16)],
        compiler_params=pltpu.CompilerParams(dimension_semantics=("arbitrary",),
                                             vmem_limit_bytes=VMEM_LIMIT),
        name="out_ffn",
    )(x_all, merged, vecs, *ffn_weights)


def _vector_slab(norm1, norm2, q_norm, k_norm, b_g, gla_norm):
    pad = lambda a: jnp.pad(a.astype(F32), ((0, 0), (0, D_MODEL - a.shape[1])))
    rows = [norm1.astype(F32), norm2.astype(F32), pad(jnp.tile(q_norm, (1, N_HEADS)) * ATT_SCALE),
            pad(jnp.tile(k_norm, (1, N_KV_HEADS))), pad(b_g), pad(gla_norm)]
    rows += [jnp.zeros((DEPTH, D_MODEL), F32)] * (V_ROWS - len(rows))
    return jnp.stack(rows, axis=1).reshape(DEPTH * V_ROWS, D_MODEL)


def kernel(x_prompt, x_sample, cache_k, cache_v, state_gla, meta, norm1, w_in, q_norm, k_norm, sinks,
           w_g2, b_g, gla_norm, w_o, norm2, w_gate, w_up, w_down):
    consts = _constants()
    dt = x_prompt.dtype
    vecs = _vector_slab(norm1, norm2, q_norm, k_norm, b_g, gla_norm)
    sinks = sinks.astype(F32)
    lead = jnp.tile(jnp.concatenate([jnp.zeros((PAD_LEN, D_MODEL), dt), meta.astype(dt)], axis=0), (BATCH, 1))
    x_srcs = (x_prompt, x_sample.reshape(SAMPLE_ROWS, D_MODEL), lead)
    to_feature_major = lambda c: c.transpose(0, 1, 3, 4, 2).reshape(DEPTH, DEC_BATCH, ATT_KV_W, WINDOW)
    ck_in, cv_in = to_feature_major(cache_k), to_feature_major(cache_v)
    st_in = state_gla.reshape(DEPTH, DEC_BATCH, GLA_K_W, GLA_DV)
    w_in_t = jnp.swapaxes(w_in, 1, 2)

    prompt_outs, sample_outs = None, None
    for l in range(DEPTH):
        x_all, merged, proj_sample, prompt_outs, ffn_bf16 = _front(
            l, sinks, x_srcs, vecs, w_in_t, w_g2, (w_o, w_gate, w_up, w_down), consts, prompt_outs)
        merged, *sample_outs = _mix_sample(l, sinks, merged, proj_sample, ck_in, cv_in, st_in, vecs, consts,
                                           sample_outs)
        x_srcs = tuple(_out_ffn(l, x_all, merged, vecs, ffn_bf16, last=l == DEPTH - 1))

    y_main, y_sample = x_srcs
    ps, pk, pv = prompt_outs
    sk, sv, ss = sample_outs
    kv5 = lambda a, n: a.reshape(DEPTH, n, N_KV_HEADS, HEAD_DIM, WINDOW).transpose(0, 1, 4, 2, 3)
    st5 = lambda a, n: a.reshape(DEPTH, n, GLA_HEADS, GLA_DK, GLA_DV)
    return (y_main, y_sample.reshape(DEC_BATCH, DEC_SEQ, D_MODEL),
            kv5(pk, BATCH), kv5(pv, BATCH), st5(ps, BATCH), kv5(sk, DEC_BATCH), kv5(sv, DEC_BATCH),
            st5(ss, DEC_BATCH))
```

```python
import functools

import jax
import jax.numpy as jnp
import numpy as np
from jax import lax
from jax.experimental import pallas as pl
from jax.experimental.pallas import tpu as pltpu

F32 = jnp.float32
BF16 = jnp.bfloat16

D_MODEL = 1024
BATCH = 4
SEQ = 4096
DEPTH = 2
DEC_BATCH = 128
DEC_SEQ = 8
N_META = 16
WINDOW = 128
BLOCK = 128
PAD_LEN = BLOCK - N_META
N_HEADS = 8
N_KV_HEADS = 2
HEAD_DIM = 64
ATT_SCALE = HEAD_DIM ** -0.5
ATT_Q_W = N_HEADS * HEAD_DIM
ATT_KV_W = N_KV_HEADS * HEAD_DIM
GLA_HEADS = 4
GLA_DK = 64
GLA_DV = 128
GLA_K_W = GLA_HEADS * GLA_DK
GLA_V_W = GLA_HEADS * GLA_DV
GATE_RANK = 16
GATE_NORMALIZER = 16.0
D_MIX = ATT_Q_W + GLA_V_W
D_FFN = 2816
IN_W = 2320
RMS_EPS = 1e-6
MASK_VALUE = -1e30

LANES = 128
N_BLOCKS = 1 + SEQ // BLOCK
MAIN_ROWS = BATCH * SEQ
SAMPLE_ROWS = DEC_BATCH * DEC_SEQ
LEAD_ROWS = BATCH * BLOCK
TOTAL_ROWS = MAIN_ROWS + SAMPLE_ROWS + LEAD_ROWS
ROW_TILE = 512
MAIN_TILES = MAIN_ROWS // ROW_TILE
SAMPLE_TILES = SAMPLE_ROWS // ROW_TILE
N_TILES = TOTAL_ROWS // ROW_TILE
LEAD_TILE = N_TILES - 1
SEQ_GROUP = BLOCK // DEC_SEQ
N_LEVELS = 7
LOW_LEVELS = 3
LOG_DK = 6
N_PREP = 8
W_ROWS = D_MODEL // N_PREP
N_CONV = 16

SRC_LOW, SRC_OG = 1792, 1808
OFF_Q, OFF_K, OFF_V = 0, 512, 640
OFF_GQ, OFF_GK, OFF_GV, OFF_OG, OFF_LOW = 768, 1024, 1280, 1792, 2304
PROJ_W = OFF_LOW + LANES
PROJ_CHUNK = 512
FFN_CHUNK = 256
VMEM_LIMIT = 58 * 1024 * 1024

V_NORM1, V_NORM2, V_QG, V_KG, V_BG, V_GN, V_ROWS = 0, 1, 2, 3, 4, 5, 8

PF_K, PF_V, PF_GQ, PF_GK, PF_LD, PF_OG, PF_W = 0, 128, 256, 512, 768, 1024, 1536
PB_Q, PB_GV, PB_W = 0, 512, 1024


def _proj_views(pf_ref, pb_ref):
    f = lambda a, b: pf_ref.at[:, a:b]
    return (pb_ref.at[:, PB_Q:PB_GV], f(PF_K, PF_V), f(PF_V, PF_GQ), f(PF_GQ, PF_GK), f(PF_GK, PF_LD),
            f(PF_LD, PF_OG), pb_ref.at[:, PB_GV:PB_W], f(PF_OG, PF_W))


def _dot(a, b):
    return jnp.dot(a, b, preferred_element_type=F32)


def _dot_nt(a, b):
    return lax.dot_general(a, b, (((1,), (1,)), ((), ())), preferred_element_type=F32)


def _dot_tn(a, b):
    return lax.dot_general(a, b, (((0,), (0,)), ((), ())), preferred_element_type=F32)


def _split(x):
    hi = x.astype(BF16)
    lo = (x - hi.astype(F32)).astype(BF16)
    return hi, lo


def _sigmoid(x):
    return 1.0 / (1.0 + jnp.exp(-x))


def _level_matrix(levels, n=BLOCK):
    out = np.zeros((len(levels) * n, n), np.float32)
    for i, l in enumerate(levels):
        size = 2 << l
        for t in range(n):
            mid = (t // size) * size + size // 2 - 1
            if (t >> l) & 1:
                out[i * n + t, mid + 1:t + 1] = 1.0
            else:
                out[i * n + t, t + 1:mid + 1] = 1.0
    return out


def _constants():
    r = np.arange(BLOCK)
    tri = (r[None, :] <= r[:, None]).astype(np.float32)
    same_seq = (r[None, :] // DEC_SEQ) == (r[:, None] // DEC_SEQ)
    diff = np.maximum(r[:, None] ^ r[None, :], 1)
    pair_level = np.where(r[None, :] < r[:, None], np.floor(np.log2(diff)).astype(np.int32),
                          np.where(r[None, :] == r[:, None], N_LEVELS, N_LEVELS + 1)).astype(np.int32)
    return dict(
        tri=jnp.asarray(tri, BF16),
        seq_tri=jnp.asarray(tri * same_seq, BF16),
        seq_ones=jnp.asarray(same_seq.astype(np.float32), BF16),
        lev_low=jnp.asarray(_level_matrix(range(LOW_LEVELS)), BF16),
        pair_level=jnp.asarray(pair_level),
    )


def _head_masks(rows):
    lane = lax.broadcasted_iota(jnp.int32, (rows, GLA_K_W), 1)
    return [(lane >> LOG_DK) == h for h in range(GLA_HEADS)]


def _gla_intra(gq, gk, decay_levels, levels, pair_level):
    rows = gq[0].shape[0]
    streams = range(len(gq))
    low_half = lax.broadcasted_iota(jnp.int32, (rows, LANES), 1) < GLA_DK

    def pair_products(qh, kh):
        qh, kh = qh.astype(BF16), kh.astype(BF16)
        out = []
        for pair in range(GLA_HEADS // 2):
            kl = kh[:, pair * LANES:(pair + 1) * LANES]
            zero = jnp.zeros_like(kl)
            stacked = jnp.concatenate([jnp.where(low_half, kl, zero), jnp.where(low_half, zero, kl)], axis=0)
            p = _dot_nt(qh[:, pair * LANES:(pair + 1) * LANES], stacked)
            out += [p[:, :rows], p[:, rows:]]
        return out

    row = lax.broadcasted_iota(jnp.int32, (rows, 1), 0)

    def upper_q_lower_k(b, l):
        size = 1 << l
        if size % 8:
            return jnp.where(((row >> l) & 1) == 1, gq[b], gk[b])
        return jnp.concatenate([(gq[b] if i & 1 else gk[b])[i * size:(i + 1) * size]
                                for i in range(rows // size)], axis=0)

    def diagonal(b):
        out = []
        for pair in range(GLA_HEADS // 2):
            prod = (gq[b] * gk[b])[:, pair * LANES:(pair + 1) * LANES]
            out += [jnp.sum(jnp.where(low_half, prod, 0.0), axis=-1, keepdims=True),
                    jnp.sum(jnp.where(low_half, 0.0, prod), axis=-1, keepdims=True)]
        return out

    def blocks(v, size, parity):
        return jnp.concatenate([v[i * size:(i + 1) * size] for i in range(parity, rows // size, 2)], axis=0)

    def interleave(lower, upper, size):
        pieces = []
        for i in range(rows // (2 * size)):
            pieces += [lower[i * size:(i + 1) * size], upper[i * size:(i + 1) * size]]
        return jnp.concatenate(pieces, axis=0)

    on_diag = pair_level == N_LEVELS
    a = [[jnp.where(on_diag, d, 0.0) for d in diagonal(b)] for b in streams]
    for i, l in enumerate(levels):
        at_level = pair_level == l
        size = 1 << l
        for b in streams:
            x = upper_q_lower_k(b, l) * decay_levels[b][i]
            if size % 8:
                a[b] = [jnp.where(at_level, p, a_h) for p, a_h in zip(pair_products(x, x), a[b])]
            else:
                mask_up = blocks(pair_level, size, 1) == l
                p_up = pair_products(blocks(x, size, 1), x)
                a[b] = [interleave(blocks(a_h, size, 0), jnp.where(mask_up, p, blocks(a_h, size, 1)), size)
                        for p, a_h in zip(p_up, a[b])]
    return a


def _gla_merge(o, og, gn):
    outs = []
    for h in range(GLA_HEADS):
        oh = o[:, h * GLA_DV:(h + 1) * GLA_DV]
        gh = og[:, h * GLA_DV:(h + 1) * GLA_DV]
        ms = jnp.mean(oh * oh, axis=-1, keepdims=True)
        outs.append(oh * lax.rsqrt(ms + RMS_EPS) * gn * (gh * _sigmoid(gh)))
    return outs


def _dup_halves(x):
    low = lax.broadcasted_iota(jnp.int32, x.shape, 1) < HEAD_DIM
    rolled = pltpu.roll(x, HEAD_DIM, axis=1)
    return jnp.where(low, x, rolled), jnp.where(low, rolled, x)


def _convert_proj_weights(i, w_ref, wg2_ref, wbf_ref, wg2s_ref):
    rows = pl.ds(pl.multiple_of(i * W_ROWS, W_ROWS), W_ROWS)

    def put(dst, src, n=LANES):
        tile = w_ref[0, src:src + n, :]
        if n < LANES:
            tile = jnp.concatenate([tile, jnp.zeros((LANES - n, W_ROWS), F32)], axis=0)
        wbf_ref[rows, dst:dst + LANES] = tile.T.astype(BF16)

    for c in range(SRC_LOW // LANES):
        put(c * LANES, c * LANES)
    for c in range((IN_W - SRC_OG) // LANES):
        put(OFF_OG + c * LANES, SRC_OG + c * LANES)
    put(OFF_LOW, SRC_LOW, GATE_RANK)

    @pl.when(i == 0)
    def _():
        wg2s_ref[...] = jnp.concatenate(
            [wg2_ref[0], jnp.zeros((LANES - GATE_RANK, GLA_K_W), F32)], axis=0).astype(BF16)


def _project_tile(x, vec_ref, wbf_ref, wg2s_ref, z_ref, pf_ref, pb_ref):
    q_ref, k_ref, v_ref, gq_ref, gk_ref, ld_ref, gv_ref, og_ref = _proj_views(pf_ref, pb_ref)
    ms = jnp.mean(x * x, axis=-1, keepdims=True)
    h = (x * lax.rsqrt(ms + RMS_EPS) * vec_ref[V_NORM1:V_NORM1 + 1, :]).astype(BF16)

    low_half = lax.broadcasted_iota(jnp.int32, (ROW_TILE, LANES), 1) < HEAD_DIM

    def head_norm(t):
        t2 = t * t
        ms_lo = jnp.sum(jnp.where(low_half, t2, 0.0), axis=-1, keepdims=True) * (1.0 / HEAD_DIM)
        ms_hi = jnp.sum(jnp.where(low_half, 0.0, t2), axis=-1, keepdims=True) * (1.0 / HEAD_DIM)
        return t * jnp.where(low_half, lax.rsqrt(ms_lo + RMS_EPS), lax.rsqrt(ms_hi + RMS_EPS))

    def q_part():
        for j in range(ATT_Q_W // LANES):
            cols = slice(j * LANES, (j + 1) * LANES)
            q_ref[:, cols] = (head_norm(z_ref[:, cols]) * vec_ref[V_QG:V_QG + 1, cols]).astype(BF16)

    def kv_part():
        k_ref[...] = head_norm(z_ref[:, OFF_K:OFF_V]) * vec_ref[V_KG:V_KG + 1, :ATT_KV_W]
        v_ref[...] = z_ref[:, OFF_V:OFF_GQ]
        gq_ref[...] = z_ref[:, OFF_GQ:OFF_GK] * (GLA_DK ** -0.5)

    def gk_part():
        gk_ref[...] = z_ref[:, OFF_GK:OFF_GV]

    def gv_part():
        gv_ref[...] = z_ref[:, OFF_GV:OFF_OG].astype(BF16)

    def tail_part():
        og_ref[...] = z_ref[:, OFF_OG:OFF_LOW]
        logit = (_dot(z_ref[:, OFF_LOW:PROJ_W].astype(BF16), wg2s_ref[...])
                 + vec_ref[V_BG:V_BG + 1, :GLA_K_W])
        log_sig = jnp.minimum(logit, 0.0) - jnp.log1p(jnp.exp(-jnp.abs(logit)))
        ld_ref[...] = log_sig * (1.0 / GATE_NORMALIZER)

    pieces = [q_part, kv_part, gk_part, gv_part, tail_part]
    for c, piece in enumerate(pieces):
        cols = slice(c * PROJ_CHUNK, min((c + 1) * PROJ_CHUNK, PROJ_W))
        z_ref[:, cols] = _dot(h, wbf_ref[:, cols])
        piece()


def _attention_units(blk, layer, sinks_ref, pf_ref, pb_ref, m_ref, kprev_ref, vprev_ref):
    q_ref, k_ref, v_ref = _proj_views(pf_ref, pb_ref)[:3]
    per_group = N_HEADS // N_KV_HEADS
    grows = per_group * BLOCK
    row = lax.broadcasted_iota(jnp.int32, (grows, BLOCK), 0) & (BLOCK - 1)
    col = lax.broadcasted_iota(jnp.int32, (grows, BLOCK), 1)
    own = col <= row
    kpos = jnp.where(own, blk * BLOCK, (blk - 1) * BLOCK) + col - PAD_LEN
    live = kpos >= 0
    low_half = lax.broadcasted_iota(jnp.int32, (BLOCK, LANES), 1) < HEAD_DIM
    head_of_row = lax.broadcasted_iota(jnp.int32, (grows, 1), 0) >> N_LEVELS
    sink_cols = []
    for g in range(N_KV_HEADS):
        sink = jnp.zeros((grows, 1), F32)
        for r in range(per_group):
            sink = jnp.where(head_of_row == r, sinks_ref[layer, g * per_group + r], sink)
        sink_cols.append(sink)

    units =[(b, g) for b in range(BATCH) for g in range(N_KV_HEADS)]
    rows = [slice(b * BLOCK, (b + 1) * BLOCK) for b in range(BATCH)]
    tiles = [range(g * per_group // 2, (g + 1) * per_group // 2) for g in range(N_KV_HEADS)]
    kdup = [[x.astype(BF16) for x in _dup_halves(k_ref[rows[b], :])] for b in range(BATCH)]
    vdup = [[x.astype(BF16) for x in _dup_halves(v_ref[rows[b], :])] for b in range(BATCH)]
    kk = [jnp.concatenate([kprev_ref[b * N_KV_HEADS + g], kdup[b][g]], axis=0) for b, g in units]
    vv = [jnp.concatenate([vprev_ref[b * N_KV_HEADS + g], vdup[b][g]], axis=0) for b, g in units]
    for b, g in units:
        kprev_ref[b * N_KV_HEADS + g] = kdup[b][g]
        vprev_ref[b * N_KV_HEADS + g] = vdup[b][g]

    def stacked_queries(b, g):
        pieces = []
        for j in tiles[g]:
            qt = q_ref[rows[b], j * LANES:(j + 1) * LANES]
            zero = jnp.zeros_like(qt)
            pieces += [jnp.where(low_half, qt, zero), jnp.where(low_half, zero, qt)]
        return jnp.concatenate(pieces, axis=0)

    def attend(u, b, g):
        s = _dot_nt(stacked_queries(b, g), kk[u])
        s = jnp.where(live, jnp.where(own, s[:, BLOCK:], s[:, :BLOCK]), MASK_VALUE)
        m = jnp.maximum(jnp.max(s, axis=-1, keepdims=True), sink_cols[g])
        p = jnp.exp(s - m)
        denom = jnp.sum(p, axis=-1, keepdims=True) + jnp.exp(sink_cols[g] - m)
        p2 = jnp.concatenate([jnp.where(own, 0.0, p), jnp.where(own, p, 0.0)], axis=1).astype(BF16)
        o = _dot(p2, vv[u]) / denom
        for n, j in enumerate(tiles[g]):
            o_lo = o[2 * n * BLOCK:(2 * n + 1) * BLOCK]
            o_hi = o[(2 * n + 1) * BLOCK:(2 * n + 2) * BLOCK]
            m_ref[rows[b], j * LANES:(j + 1) * LANES] = jnp.where(low_half, o_lo, o_hi).astype(BF16)

    return [functools.partial(attend, u, b, g) for u, (b, g) in enumerate(units)]


def _gla_tile(blk, pf_ref, pb_ref, vec_ref, tri_ref, lev_ref, plev_ref, m_ref, state_ref, sbd_ref):
    gq_ref, gk_ref, ld_ref, gv_ref, og_ref = _proj_views(pf_ref, pb_ref)[3:]
    rpos = blk * BLOCK + lax.broadcasted_iota(jnp.int32, (BLOCK, 1), 0) - PAD_LEN
    valid = (rpos >= 0).astype(F32)
    ld_all = jnp.concatenate([ld_ref[b * BLOCK:(b + 1) * BLOCK, :] for b in range(BATCH)], axis=1)
    hi, lo = _split(ld_all)
    tri = tri_ref[...]
    g_cum_all = _dot(tri, hi) + _dot(tri, lo)
    lev = lev_ref[...]
    low_sums = _dot(lev, hi) + _dot(lev, lo)
    level_sums = [low_sums[l * BLOCK:(l + 1) * BLOCK] for l in range(LOW_LEVELS)]
    for l in range(LOW_LEVELS, N_LEVELS):
        half = 1 << l
        pieces = []
        for p in range(BLOCK // (2 * half)):
            mid = g_cum_all[p * 2 * half + half - 1:p * 2 * half + half, :]
            pieces += [mid - g_cum_all[p * 2 * half:p * 2 * half + half],
                       g_cum_all[p * 2 * half + half:(p + 1) * 2 * half] - mid]
        level_sums.append(jnp.concatenate(pieces, axis=0))
    decay_all = [jnp.exp(s) for s in level_sums]
    g_last_all = jnp.broadcast_to(g_cum_all[BLOCK - 1:BLOCK, :], g_cum_all.shape)
    decay_col_all = jnp.exp(jnp.concatenate(
        [g_last_all[:, c * LANES:(c + 1) * LANES].T for c in range(BATCH * GLA_K_W // LANES)], axis=0))
    pair_level = plev_ref[...]
    gn = vec_ref[V_GN:V_GN + 1, :GLA_DV]
    streams = range(BATCH)
    rows = [slice(b * BLOCK, (b + 1) * BLOCK) for b in streams]
    cols = [slice(b * GLA_K_W, (b + 1) * GLA_K_W) for b in streams]
    gq = [gq_ref[rows[b], :] for b in streams]
    gk = [gk_ref[rows[b], :] * valid for b in streams]
    gv = [gv_ref[rows[b], :] for b in streams]
    g_cum = [g_cum_all[:, cols[b]] for b in streams]
    a = _gla_intra(gq, gk, [[d[:, cols[b]] for d in decay_all] for b in streams], range(N_LEVELS), pair_level)

    pairs = range(GLA_HEADS // 2)
    q_dec = [(gq[b] * jnp.exp(g_cum[b])).astype(BF16) for b in streams]
    o = [jnp.concatenate([_dot(q_dec[b][:, i * LANES:(i + 1) * LANES], sbd_ref[2 * b + i]) for i in pairs], axis=1)
         for b in streams]
    o = [o[b] + jnp.concatenate(
        [_dot(a[b][h].astype(BF16), gv[b][:, h * GLA_DV:(h + 1) * GLA_DV]) for h in range(GLA_HEADS)], axis=1)
        for b in streams]
    for b in streams:
        for h, gh in enumerate(_gla_merge(o[b], og_ref[rows[b], :], gn)):
            m_ref[rows[b], ATT_Q_W + h * GLA_DV:ATT_Q_W + (h + 1) * GLA_DV] = gh.astype(BF16)

    k_dec = [(gk[b] * jnp.exp(g_cum[b][BLOCK - 1:BLOCK, :] - g_cum[b])).astype(BF16) for b in streams]
    kv = [[_dot_tn(k_dec[b][:, i * LANES:(i + 1) * LANES], gv[b][:, 2 * i * GLA_DV:(2 * i + 2) * GLA_DV])
           for i in pairs] for b in streams]
    for b in streams:
        new_state = decay_col_all[cols[b], :] * state_ref[b] + jnp.concatenate(
            [kv[b][h // 2][(h % 2) * GLA_DK:(h % 2 + 1) * GLA_DK, (h % 2) * GLA_DV:(h % 2 + 1) * GLA_DV]
             for h in range(GLA_HEADS)], axis=0)
        state_ref[b] = new_state
        for h in range(GLA_HEADS):
            sbd_ref[2 * b + h // 2, (h % 2) * GLA_DK:(h % 2 + 1) * GLA_DK,
                    (h % 2) * GLA_DV:(h % 2 + 1) * GLA_DV] = new_state[h * GLA_DK:(h + 1) * GLA_DK].astype(BF16)


def _front_kernel(*refs, layer, n_src, chained):
    sinks_ref, refs = refs[0], refs[1:]
    if chained:
        refs = refs[3:]
    x_refs, refs = refs[:n_src], refs[n_src:]
    vec_ref, w_ref, wg2_ref, tri_ref, lev_ref, plev_ref = refs[:6]
    ffn_f32_refs, refs = refs[6:10], refs[10:]
    if n_src > 1:
        xo_ref, refs = refs[0], refs[1:]
    (m_ref, pfs_ref, pbs_ref, s_out_ref, k_out_ref, v_out_ref) = refs[:6]
    ffn_bf16_refs, refs = refs[6:10], refs[10:]
    wbf_ref, wg2s_ref, z_ref, pf_ref, pb_ref, kprev_ref, vprev_ref, state_ref, sbd_ref = refs
    i = pl.program_id(0)
    t = i - N_PREP

    @pl.when(i < N_PREP)
    def _():
        _convert_proj_weights(i, w_ref, wg2_ref, wbf_ref, wg2s_ref)

    @pl.when((t >= 1) & (t <= N_CONV))
    def _():
        for src, dst in zip(ffn_f32_refs, ffn_bf16_refs):
            dst[...] = src[0].astype(BF16)

    @pl.when(i == 0)
    def _():
        kprev_ref[...] = jnp.zeros_like(kprev_ref)
        vprev_ref[...] = jnp.zeros_like(vprev_ref)
        state_ref[...] = jnp.zeros_like(state_ref)
        sbd_ref[...] = jnp.zeros_like(sbd_ref)

    def load_x():
        if n_src == 1:
            return x_refs[0][...]
        x = jnp.where(t == 0, x_refs[2][...],
                      jnp.where(t < N_BLOCKS, x_refs[0][...].reshape(ROW_TILE, D_MODEL), x_refs[1][...]))
        xo_ref[...] = x
        return x

    @pl.when((t >= 0) & (t < N_BLOCKS))
    def _():
        _project_tile(load_x(), vec_ref, wbf_ref, wg2s_ref, z_ref, pf_ref, pb_ref)
        for attend in _attention_units(t, layer, sinks_ref, pf_ref, pb_ref, m_ref, kprev_ref, vprev_ref):
            attend()
        _gla_tile(t, pf_ref, pb_ref, vec_ref, tri_ref, lev_ref, plev_ref, m_ref, state_ref, sbd_ref)

    @pl.when(t == N_BLOCKS - 1)
    def _():
        s_out_ref[0] = state_ref[...]
        for b in range(BATCH):
            k_out_ref[0, b] = pf_ref[b * BLOCK:(b + 1) * BLOCK, PF_K:PF_V].T
            v_out_ref[0, b] = pf_ref[b * BLOCK:(b + 1) * BLOCK, PF_V:PF_GQ].T

    @pl.when(t >= N_BLOCKS)
    def _():
        _project_tile(load_x(), vec_ref, wbf_ref, wg2s_ref, z_ref, pfs_ref, pbs_ref)


def _front(layer, sinks, x_srcs, vecs, w_in, w_g2, ffn_f32, consts, prev):
    n_src = len(x_srcs)
    step = lambda i: i - N_PREP
    conv_of = lambda i: jnp.clip(step(i) - 1, 0, N_CONV - 1)
    tile_of = lambda i: jnp.where(step(i) <= 0, LEAD_TILE, jnp.minimum(step(i) - 1, LEAD_TILE - 1))
    prompt_tile_of = lambda i: jnp.where(step(i) <= 0, LEAD_TILE, jnp.minimum(step(i) - 1, MAIN_TILES - 1))
    sample_tile_of = lambda i: jnp.clip(step(i) - N_BLOCKS, 0, SAMPLE_TILES - 1)
    full = lambda a: pl.BlockSpec(a.shape, lambda i, *_: (0,) * a.ndim)
    per_batch = lambda r, c: pl.BlockSpec((1, BATCH, r, c), lambda i, *_: (layer, 0, 0, 0))
    if n_src == 1:
        x_specs = [pl.BlockSpec((ROW_TILE, D_MODEL), lambda i, *_: (tile_of(i), 0))]
    else:
        x_specs = [
            pl.BlockSpec((BATCH, BLOCK, D_MODEL), lambda i, *_: (0, jnp.clip(step(i) - 1, 0, MAIN_TILES - 1), 0)),
            pl.BlockSpec((ROW_TILE, D_MODEL), lambda i, *_: (sample_tile_of(i), 0)),
            pl.BlockSpec((ROW_TILE, D_MODEL), lambda i, *_: (0, 0)),
        ]
    const_args = (consts["tri"], consts["lev_low"], consts["pair_level"])
    chained = prev is not None
    prev_args = tuple(prev) if chained else ()
    in_specs = ([pl.BlockSpec(memory_space=pl.ANY)] * len(prev_args) + x_specs + [
        pl.BlockSpec((V_ROWS, D_MODEL), lambda i, *_: (layer, 0)),
        pl.BlockSpec((1, IN_W, W_ROWS), lambda i, *_: (layer, 0, jnp.minimum(i, N_PREP - 1))),
        pl.BlockSpec((1, GATE_RANK, GLA_K_W), lambda i, *_: (layer, 0, 0)),
    ] + [full(a) for a in const_args] + [
        pl.BlockSpec((1, a.shape[1] // N_CONV, a.shape[2]), lambda i, *_: (layer, conv_of(i), 0)) for a in ffn_f32])
    out_specs = [pl.BlockSpec((ROW_TILE, D_MIX), lambda i, *_: (prompt_tile_of(i), 0)),
                 pl.BlockSpec((ROW_TILE, PF_W), lambda i, *_: (sample_tile_of(i), 0)),
                 pl.BlockSpec((ROW_TILE, PB_W), lambda i, *_: (sample_tile_of(i), 0)),
                 per_batch(GLA_K_W, GLA_DV), per_batch(BLOCK, ATT_KV_W), per_batch(BLOCK, ATT_KV_W)]
    out_specs += [pl.BlockSpec((a.shape[1] // N_CONV, a.shape[2]), lambda i, *_: (conv_of(i), 0)) for a in ffn_f32]
    out_shape = [jax.ShapeDtypeStruct((TOTAL_ROWS, D_MIX), BF16),
                 jax.ShapeDtypeStruct((SAMPLE_ROWS, PF_W), F32),
                 jax.ShapeDtypeStruct((SAMPLE_ROWS, PB_W), BF16),
                 jax.ShapeDtypeStruct((DEPTH, BATCH, GLA_K_W, GLA_DV), F32),
                 jax.ShapeDtypeStruct((DEPTH, BATCH, BLOCK, ATT_KV_W), F32),
                 jax.ShapeDtypeStruct((DEPTH, BATCH, BLOCK, ATT_KV_W), F32)]
    out_shape += [jax.ShapeDtypeStruct(a.shape[1:], BF16) for a in ffn_f32]
    n_lead_out = 0
    if n_src > 1:
        out_specs = [pl.BlockSpec((ROW_TILE, D_MODEL), lambda i, *_: (tile_of(i), 0))] + out_specs
        out_shape = [jax.ShapeDtypeStruct((TOTAL_ROWS, D_MODEL), F32)] + out_shape
        n_lead_out = 1
    grid_spec = pltpu.PrefetchScalarGridSpec(
        num_scalar_prefetch=1,
        grid=(N_PREP + N_BLOCKS + SAMPLE_TILES,),
        in_specs=in_specs,
        out_specs=out_specs,
        scratch_shapes=[pltpu.VMEM((D_MODEL, PROJ_W), BF16), pltpu.VMEM((LANES, GLA_K_W), BF16),
                        pltpu.VMEM((ROW_TILE, PROJ_W), F32),
                        pltpu.VMEM((ROW_TILE, PF_W), F32), pltpu.VMEM((ROW_TILE, PB_W), BF16),
                        pltpu.VMEM((BATCH * N_KV_HEADS, BLOCK, ATT_KV_W), BF16),
                        pltpu.VMEM((BATCH * N_KV_HEADS, BLOCK, ATT_KV_W), BF16),
                        pltpu.VMEM((BATCH, GLA_K_W, GLA_DV), F32),
                        pltpu.VMEM((BATCH * GLA_HEADS // 2, 2 * GLA_DK, 2 * GLA_DV), BF16)],
    )
    res = pl.pallas_call(
        functools.partial(_front_kernel, layer=layer, n_src=n_src, chained=chained),
        grid_spec=grid_spec,
        out_shape=out_shape,
        input_output_aliases={1 + n: n_lead_out + 3 + n for n in range(len(prev_args))},
        compiler_params=pltpu.CompilerParams(dimension_semantics=("arbitrary",),
                                             vmem_limit_bytes=VMEM_LIMIT),
        name="front",
    )(sinks, *prev_args, *x_srcs, vecs, w_in, w_g2, *const_args, *ffn_f32)
    x_all = res[0] if n_src > 1 else x_srcs[0]
    merged, pfs, pbs, ps, pk, pv = res[n_lead_out:n_lead_out + 6]
    return x_all, merged, (pfs, pbs), (ps, pk, pv), tuple(res[n_lead_out + 6:])


def _mix_sample_kernel(*refs, layer, n_alias):
    sinks_ref, refs = refs[0], refs[n_alias + 1:]
    (pf_ref, pb_ref, ck_ref, cv_ref, st_ref,
     vec_ref, tri_ref, sones_ref, lev_ref, ones_ref, plev_ref,
     m_ref, ck_out_ref, cv_out_ref, st_out_ref) = refs
    q_ref, k_ref, v_ref, gq_ref, gk_ref, ld_ref, gv_ref, og_ref = _proj_views(pf_ref, pb_ref)
    gq = gq_ref[...]
    gk = gk_ref[...]
    gv = gv_ref[...]
    gvf = gv.astype(F32)
    hi, lo = _split(ld_ref[...])
    tri = tri_ref[...]
    g_cum = _dot(tri, hi) + _dot(tri, lo)
    sones = sones_ref[...]
    g_tot = _dot(sones, hi) + _dot(sones, lo)
    lev = lev_ref[...]
    decay_levels = jnp.exp(_dot(lev, hi) + _dot(lev, lo))
    a = _gla_intra([gq], [gk], [[decay_levels[l * BLOCK:(l + 1) * BLOCK] for l in range(LOW_LEVELS)]],
                   range(LOW_LEVELS), plev_ref[...])[0]
    o_intra = jnp.concatenate(
        [_dot(a[h].astype(BF16), gv[:, h * GLA_DV:(h + 1) * GLA_DV]) for h in range(GLA_HEADS)], axis=1)
    q_dec = gq * jnp.exp(g_cum)
    k_dec = gk * jnp.exp(g_tot - g_cum)
    hm8 = _head_masks(DEC_SEQ)
    ones8 = ones_ref[...]
    hi_f, lo_f = hi.astype(F32), lo.astype(F32)

    qf = q_ref[...].astype(F32)
    kf = k_ref[...]
    vf = v_ref[...]
    low8 = lax.broadcasted_iota(jnp.int32, (DEC_SEQ, LANES), 1) < HEAD_DIM
    nkeys = WINDOW + DEC_SEQ
    srow = lax.broadcasted_iota(jnp.int32, (N_HEADS * DEC_SEQ, nkeys), 0)
    scol = lax.broadcasted_iota(jnp.int32, (N_HEADS * DEC_SEQ, nkeys), 1)
    t_of_row = srow & (DEC_SEQ - 1)
    amask = ((scol < WINDOW) & (scol > t_of_row)) | ((scol >= WINDOW) & (scol - WINDOW <= t_of_row))
    rid = lax.broadcasted_iota(jnp.int32, (N_HEADS * DEC_SEQ, 1), 0) >> LOW_LEVELS
    sink_col = jnp.zeros((N_HEADS * DEC_SEQ, 1), F32)
    for i in range(N_HEADS):
        sink_col = jnp.where(rid == i, sinks_ref[layer, i], sink_col)

    seqs = range(SEQ_GROUP)
    rows = [slice(b * DEC_SEQ, (b + 1) * DEC_SEQ) for b in seqs]

    def stacked_queries(b):
        pieces = []
        for j in range(N_HEADS // 2):
            g = (2 * j) // (N_HEADS // N_KV_HEADS)
            qt = qf[rows[b], j * LANES:(j + 1) * LANES]
            swapped = pltpu.roll(qt, HEAD_DIM, axis=1)
            own_lanes = low8 if g == 0 else ~low8
            even, odd = (qt, swapped) if g == 0 else (swapped, qt)
            pieces += [jnp.where(own_lanes, even, 0.0), jnp.where(own_lanes, odd, 0.0)]
        return jnp.concatenate(pieces, axis=0).astype(BF16)

    lane = lax.broadcasted_iota(jnp.int32, (WINDOW, WINDOW), 1)
    old_lanes = lane < WINDOW - DEC_SEQ

    def shifted_buffer(buf, new_rows):
        wide = jnp.concatenate([jnp.zeros((WINDOW - DEC_SEQ, LANES), F32), new_rows], axis=0)
        return jnp.where(old_lanes, pltpu.roll(buf, WINDOW - DEC_SEQ, axis=1), wide.T)

    qp = [stacked_queries(b) for b in seqs]
    ck = [ck_ref[0, b] for b in seqs]
    cv = [cv_ref[0, b] for b in seqs]
    for b in seqs:
        ck_out_ref[0, b] = shifted_buffer(ck[b], kf[rows[b]])
        cv_out_ref[0, b] = shifted_buffer(cv[b], vf[rows[b]])
    s = [jnp.concatenate([_dot(qp[b], ck[b].astype(BF16)), _dot_nt(qp[b], kf[rows[b]].astype(BF16))], axis=1)
         for b in seqs]
    s = [jnp.where(amask, s[b], MASK_VALUE) for b in seqs]
    m = [jnp.maximum(jnp.max(s[b], axis=-1, keepdims=True), sink_col) for b in seqs]
    p = [jnp.exp(s[b] - m[b]) for b in seqs]
    denom = [jnp.sum(p[b], axis=-1, keepdims=True) + jnp.exp(sink_col - m[b]) for b in seqs]
    pb = [p[b].astype(BF16) for b in seqs]
    ob = [(_dot_nt(pb[b][:, :WINDOW], cv[b].astype(BF16))
           + _dot(pb[b][:, WINDOW:], vf[rows[b]].astype(BF16))) / denom[b] for b in seqs]

    def head_tiles(o):
        tiles = []
        for j in range(N_HEADS // 2):
            g = (2 * j) // (N_HEADS // N_KV_HEADS)
            even = o[(2 * j) * DEC_SEQ:(2 * j + 1) * DEC_SEQ]
            odd = o[(2 * j + 1) * DEC_SEQ:(2 * j + 2) * DEC_SEQ]
            if g == 0:
                tiles.append(jnp.where(low8, even, pltpu.roll(odd, HEAD_DIM, axis=1)))
            else:
                tiles.append(jnp.where(low8, pltpu.roll(even, HEAD_DIM, axis=1), odd))
        return jnp.concatenate(tiles, axis=1)

    att_rows = [head_tiles(ob[b]) for b in seqs]

    head_stack = lambda x: jnp.concatenate([jnp.where(mk, x, 0.0) for mk in hm8], axis=0).astype(BF16)
    state = [st_ref[0, b] for b in seqs]
    oi = [_dot(head_stack(q_dec[rows[b]]), state[b].astype(BF16)) for b in seqs]
    inter_rows = [jnp.concatenate([oi[b][h * DEC_SEQ:(h + 1) * DEC_SEQ] for h in range(GLA_HEADS)], axis=1)
                  for b in seqs]
    vstack = [jnp.concatenate([gvf[rows[b]][:, h * GLA_DV:(h + 1) * GLA_DV] for h in range(GLA_HEADS)],
                              axis=0).astype(BF16) for b in seqs]
    kv = [_dot_tn(head_stack(k_dec[rows[b]]), vstack[b]) for b in seqs]
    decay_col = [jnp.exp(_dot_tn(hi_f[rows[b]], ones8) + _dot_tn(lo_f[rows[b]], ones8)) for b in seqs]
    for b in seqs:
        st_out_ref[0, b] = decay_col[b] * state[b] + kv[b]

    m_ref[:, :ATT_Q_W] = jnp.concatenate(att_rows, axis=0).astype(BF16)
    o = o_intra + jnp.concatenate(inter_rows, axis=0)
    gn = vec_ref[V_GN:V_GN + 1, :GLA_DV]
    for h, gh in enumerate(_gla_merge(o, og_ref[...], gn)):
        m_ref[:, ATT_Q_W + h * GLA_DV:ATT_Q_W + (h + 1) * GLA_DV] = gh.astype(BF16)


def _mix_sample(layer, sinks, merged, proj, cache_k, cache_v, state, vecs, consts, prev):
    tok = lambda w_: pl.BlockSpec((BLOCK, w_), lambda i, *_: (i, 0))
    merged_blk = pl.BlockSpec((BLOCK, D_MIX), lambda i, *_: (MAIN_ROWS // BLOCK + i, 0))
    full = lambda a: pl.BlockSpec(a.shape, lambda i, *_: (0,) * a.ndim)
    seq = lambda a: pl.BlockSpec((1, SEQ_GROUP) + a.shape[2:], lambda i, *_: (layer, i, 0, 0))
    ones8 = jnp.ones((DEC_SEQ, LANES), F32)
    const_args = (consts["seq_tri"], consts["seq_ones"], consts["lev_low"], ones8, consts["pair_level"])
    seq_args = (cache_k, cache_v, state)
    alias_args = (merged,) + (tuple(prev) if prev is not None else ())
    grid_spec = pltpu.PrefetchScalarGridSpec(
        num_scalar_prefetch=1,
        grid=(DEC_BATCH // SEQ_GROUP,),
        in_specs=([pl.BlockSpec(memory_space=pl.ANY)] * len(alias_args)
                  + [tok(a.shape[1]) for a in proj] + [seq(a) for a in seq_args]
                  + [pl.BlockSpec((V_ROWS, D_MODEL), lambda i, *_: (layer, 0))]
                  + [full(a) for a in const_args]),
        out_specs=[merged_blk] + [seq(a) for a in seq_args],
    )
    return pl.pallas_call(
        functools.partial(_mix_sample_kernel, layer=layer, n_alias=len(alias_args)),
        grid_spec=grid_spec,
        out_shape=[jax.ShapeDtypeStruct(merged.shape, merged.dtype)]
        + [jax.ShapeDtypeStruct(a.shape, a.dtype) for a in seq_args],
        input_output_aliases={1 + n: n for n in range(len(alias_args))},
        compiler_params=pltpu.CompilerParams(dimension_semantics=("arbitrary",),
                                             vmem_limit_bytes=VMEM_LIMIT),
        name="mix_sample",
    )(sinks, *alias_args, *proj, *seq_args, vecs, *const_args)


def _out_ffn_kernel(*refs, last):
    x_ref, m_ref, vec_ref, wo_ref, wg_ref, wu_ref, wd_ref = refs[:7]
    out_refs = refs[7:9] if last else refs[7:8]
    act_ref = refs[-1]
    x1 = x_ref[...] + _dot(m_ref[...], wo_ref[...])
    ms = jnp.mean(x1 * x1, axis=-1, keepdims=True)
    h = (x1 * lax.rsqrt(ms + RMS_EPS) * vec_ref[V_NORM2:V_NORM2 + 1, :]).astype(BF16)
    for c in range(D_FFN // FFN_CHUNK):
        cols = slice(c * FFN_CHUNK, (c + 1) * FFN_CHUNK)
        gate = _dot(h, wg_ref[:, cols])
        up = _dot(h, wu_ref[:, cols])
        act_ref[:, cols] = (gate * _sigmoid(gate) * up).astype(BF16)
    y = x1 + _dot(act_ref[...], wd_ref[...])
    if last:
        out_refs[0][...] = y.reshape(BATCH, BLOCK, D_MODEL)

        @pl.when(pl.program_id(0) < SAMPLE_TILES)
        def _():
            out_refs[1][...] = out_refs[0][...].reshape(ROW_TILE, D_MODEL)
    else:
        out_refs[0][...] = y


def _out_ffn(layer, x_all, merged, vecs, ffn_weights, last):
    resident = lambda a: pl.BlockSpec(a.shape, lambda i: (0, 0), pipeline_mode=pl.Buffered(1))
    if last:
        n_tiles = MAIN_TILES + SAMPLE_TILES
        tile_of = lambda i: jnp.where(i < SAMPLE_TILES, MAIN_TILES + i, i - SAMPLE_TILES)
        out_specs = [pl.BlockSpec((BATCH, BLOCK, D_MODEL), lambda i: (0, jnp.maximum(i - SAMPLE_TILES, 0), 0)),
                     pl.BlockSpec((ROW_TILE, D_MODEL), lambda i: (jnp.minimum(i, SAMPLE_TILES - 1), 0))]
        out_shape = [jax.ShapeDtypeStruct((BATCH, SEQ, D_MODEL), F32),
                     jax.ShapeDtypeStruct((SAMPLE_ROWS, D_MODEL), F32)]
    else:
        n_tiles = N_TILES
        tile_of = lambda i: i
        out_specs = [pl.BlockSpec((ROW_TILE, D_MODEL), lambda i: (i, 0))]
        out_shape = [jax.ShapeDtypeStruct((TOTAL_ROWS, D_MODEL), F32)]
    tile = lambda w_: pl.BlockSpec((ROW_TILE, w_), lambda i: (tile_of(i), 0))
    return pl.pallas_call(
        functools.partial(_out_ffn_kernel, last=last),
        grid=(n_tiles,),
        in_specs=[tile(D_MODEL), tile(D_MIX), pl.BlockSpec((V_ROWS, D_MODEL), lambda i: (layer, 0))]
        + [resident(w) for w in ffn_weights],
        out_specs=out_specs,
        out_shape=out_shape,
        scratch_shapes=[pltpu.VMEM((ROW_TILE, D_FFN), BF16)],
        compiler_params=pltpu.CompilerParams(dimension_semantics=("arbitrary",),
                                             vmem_limit_bytes=VMEM_LIMIT),
        name="out_ffn",
    )(x_all, merged, vecs, *ffn_weights)


def _vector_slab(norm1, norm2, q_norm, k_norm, b_g, gla_norm):
    pad = lambda a: jnp.pad(a.astype(F32), ((0, 0), (0, D_MODEL - a.shape[1])))
    rows = [norm1.astype(F32), norm2.astype(F32), pad(jnp.tile(q_norm, (1, N_HEADS)) * ATT_SCALE),
            pad(jnp.tile(k_norm, (1, N_KV_HEADS))), pad(b_g), pad(gla_norm)]
    rows += [jnp.zeros((DEPTH, D_MODEL), F32)] * (V_ROWS - len(rows))
    return jnp.stack(rows, axis=1).reshape(DEPTH * V_ROWS, D_MODEL)


def kernel(x_prompt, x_sample, cache_k, cache_v, state_gla, meta, norm1, w_in, q_norm, k_norm, sinks,
           w_g2, b_g, gla_norm, w_o, norm2, w_gate, w_up, w_down):
    consts = _constants()
    dt = x_prompt.dtype
    vecs = _vector_slab(norm1, norm2, q_norm, k_norm, b_g, gla_norm)
    sinks = sinks.astype(F32)
    lead = jnp.tile(jnp.concatenate([jnp.zeros((PAD_LEN, D_MODEL), dt), meta.astype(dt)], axis=0), (BATCH, 1))
    x_srcs = (x_prompt, x_sample.reshape(SAMPLE_ROWS, D_MODEL), lead)
    to_feature_major = lambda c: c.transpose(0, 1, 3, 4, 2).reshape(DEPTH, DEC_BATCH, ATT_KV_W, WINDOW)
    ck_in, cv_in = to_feature_major(cache_k), to_feature_major(cache_v)
    st_in = state_gla.reshape(DEPTH, DEC_BATCH, GLA_K_W, GLA_DV)
    w_in_t = jnp.swapaxes(w_in, 1, 2)

    prompt_outs, sample_outs = None, None
    for l in range(DEPTH):
        x_all, merged, proj_sample, prompt_outs, ffn_bf16 = _front(
            l, sinks, x_srcs, vecs, w_in_t, w_g2, (w_o, w_gate, w_up, w_down), consts, prompt_outs)
        merged, *sample_outs = _mix_sample(l, sinks, merged, proj_sample, ck_in, cv_in, st_in, vecs, consts,
                                           sample_outs)
        x_srcs = tuple(_out_ffn(l, x_all, merged, vecs, ffn_bf16, last=l == DEPTH - 1))

    y_main, y_sample = x_srcs
    ps, pk, pv = prompt_outs
    sk, sv, ss = sample_outs
    kv5 = lambda a, n: a.reshape(DEPTH, n, N_KV_HEADS, HEAD_DIM, WINDOW).transpose(0, 1, 4, 2, 3)
    st5 = lambda a, n: a.reshape(DEPTH, n, GLA_HEADS, GLA_DK, GLA_DV)
    return (y_main, y_sample.reshape(DEC_BATCH, DEC_SEQ, D_MODEL),
            kv5(pk, BATCH), kv5(pv, BATCH), st5(ps, BATCH), kv5(sk, DEC_BATCH), kv5(sv, DEC_BATCH),
            st5(ss, DEC_BATCH))
```

```python
import functools

import jax
import jax.numpy as jnp
import numpy as np
from jax import lax
from jax.experimental import pallas as pl
from jax.experimental.pallas import tpu as pltpu

F32 = jnp.float32
BF16 = jnp.bfloat16

D_MODEL = 1024
BATCH = 4
SEQ = 4096
DEPTH = 2
DEC_BATCH = 128
DEC_SEQ = 8
N_META = 16
WINDOW = 128
BLOCK = 128
PAD_LEN = BLOCK - N_META
N_HEADS = 8
N_KV_HEADS = 2
HEAD_DIM = 64
ATT_SCALE = HEAD_DIM ** -0.5
ATT_Q_W = N_HEADS * HEAD_DIM
ATT_KV_W = N_KV_HEADS * HEAD_DIM
GLA_HEADS = 4
GLA_DK = 64
GLA_DV = 128
GLA_K_W = GLA_HEADS * GLA_DK
GLA_V_W = GLA_HEADS * GLA_DV
GATE_RANK = 16
GATE_NORMALIZER = 16.0
D_MIX = ATT_Q_W + GLA_V_W
D_FFN = 2816
IN_W = 2320
RMS_EPS = 1e-6
MASK_VALUE = -1e30

LANES = 128
N_BLOCKS = 1 + SEQ // BLOCK
MAIN_ROWS = BATCH * SEQ
SAMPLE_ROWS = DEC_BATCH * DEC_SEQ
LEAD_ROWS = BATCH * BLOCK
TOTAL_ROWS = MAIN_ROWS + SAMPLE_ROWS + LEAD_ROWS
ROW_TILE = 512
MAIN_TILES = MAIN_ROWS // ROW_TILE
SAMPLE_TILES = SAMPLE_ROWS // ROW_TILE
N_TILES = TOTAL_ROWS // ROW_TILE
LEAD_TILE = N_TILES - 1
SEQ_GROUP = BLOCK // DEC_SEQ
N_LEVELS = 7
LOW_LEVELS = 3
LOG_DK = 6
N_PREP = 8
W_ROWS = D_MODEL // N_PREP
N_CONV = 16

SRC_LOW, SRC_OG = 1792, 1808
OFF_Q, OFF_K, OFF_V = 0, 512, 640
OFF_GQ, OFF_GK, OFF_GV, OFF_OG, OFF_LOW = 768, 1024, 1280, 1792, 2304
PROJ_W = OFF_LOW + LANES
PROJ_CHUNK = 512
FFN_CHUNK = 256
VMEM_LIMIT = 58 * 1024 * 1024

V_NORM1, V_NORM2, V_QG, V_KG, V_BG, V_GN, V_ROWS = 0, 1, 2, 3, 4, 5, 8

PF_K, PF_V, PF_GQ, PF_GK, PF_LD, PF_OG, PF_W = 0, 128, 256, 512, 768, 1024, 1536
PB_Q, PB_GV, PB_W = 0, 512, 1024


def _proj_views(pf_ref, pb_ref):
    f = lambda a, b: pf_ref.at[:, a:b]
    return (pb_ref.at[:, PB_Q:PB_GV], f(PF_K, PF_V), f(PF_V, PF_GQ), f(PF_GQ, PF_GK), f(PF_GK, PF_LD),
            f(PF_LD, PF_OG), pb_ref.at[:, PB_GV:PB_W], f(PF_OG, PF_W))


def _dot(a, b):
    return jnp.dot(a, b, preferred_element_type=F32)


def _dot_nt(a, b):
    return lax.dot_general(a, b, (((1,), (1,)), ((), ())), preferred_element_type=F32)


def _dot_tn(a, b):
    return lax.dot_general(a, b, (((0,), (0,)), ((), ())), preferred_element_type=F32)


def _split(x):
    hi = x.astype(BF16)
    lo = (x - hi.astype(F32)).astype(BF16)
    return hi, lo


def _sigmoid(x):
    return 1.0 / (1.0 + jnp.exp(-x))


def _level_matrix(levels, n=BLOCK):
    out = np.zeros((len(levels) * n, n), np.float32)
    for i, l in enumerate(levels):
        size = 2 << l
        for t in range(n):
            mid = (t // size) * size + size // 2 - 1
            if (t >> l) & 1:
                out[i * n + t, mid + 1:t + 1] = 1.0
            else:
                out[i * n + t, t + 1:mid + 1] = 1.0
    return out


def _constants():
    r = np.arange(BLOCK)
    tri = (r[None, :] <= r[:, None]).astype(np.float32)
    same_seq = (r[None, :] // DEC_SEQ) == (r[:, None] // DEC_SEQ)
    diff = np.maximum(r[:, None] ^ r[None, :], 1)
    pair_level = np.where(r[None, :] < r[:, None], np.floor(np.log2(diff)).astype(np.int32),
                          np.where(r[None, :] == r[:, None], N_LEVELS, N_LEVELS + 1)).astype(np.int32)
    return dict(
        tri=jnp.asarray(tri, BF16),
        seq_tri=jnp.asarray(tri * same_seq, BF16),
        seq_ones=jnp.asarray(same_seq.astype(np.float32), BF16),
        lev_low=jnp.asarray(_level_matrix(range(LOW_LEVELS)), BF16),
        pair_level=jnp.asarray(pair_level),
    )


def _head_masks(rows):
    lane = lax.broadcasted_iota(jnp.int32, (rows, GLA_K_W), 1)
    return [(lane >> LOG_DK) == h for h in range(GLA_HEADS)]


def _gla_intra(gq, gk, decay_levels, levels, pair_level):
    rows = gq[0].shape[0]
    streams = range(len(gq))
    low_half = lax.broadcasted_iota(jnp.int32, (rows, LANES), 1) < GLA_DK

    def pair_products(qh, kh):
        qh, kh = qh.astype(BF16), kh.astype(BF16)
        out = []
        for pair in range(GLA_HEADS // 2):
            kl = kh[:, pair * LANES:(pair + 1) * LANES]
            zero = jnp.zeros_like(kl)
            stacked = jnp.concatenate([jnp.where(low_half, kl, zero), jnp.where(low_half, zero, kl)], axis=0)
            p = _dot_nt(qh[:, pair * LANES:(pair + 1) * LANES], stacked)
            out += [p[:, :rows], p[:, rows:]]
        return out

    row = lax.broadcasted_iota(jnp.int32, (rows, 1), 0)

    def upper_q_lower_k(b, l):
        size = 1 << l
        if size % 8:
            return jnp.where(((row >> l) & 1) == 1, gq[b], gk[b])
        return jnp.concatenate([(gq[b] if i & 1 else gk[b])[i * size:(i + 1) * size]
                                for i in range(rows // size)], axis=0)

    def diagonal(b):
        out = []
        for pair in range(GLA_HEADS // 2):
            prod = (gq[b] * gk[b])[:, pair * LANES:(pair + 1) * LANES]
            out += [jnp.sum(jnp.where(low_half, prod, 0.0), axis=-1, keepdims=True),
                    jnp.sum(jnp.where(low_half, 0.0, prod), axis=-1, keepdims=True)]
        return out

    def blocks(v, size, parity):
        return jnp.concatenate([v[i * size:(i + 1) * size] for i in range(parity, rows // size, 2)], axis=0)

    def interleave(lower, upper, size):
        pieces = []
        for i in range(rows // (2 * size)):
            pieces += [lower[i * size:(i + 1) * size], upper[i * size:(i + 1) * size]]
        return jnp.concatenate(pieces, axis=0)

    on_diag = pair_level == N_LEVELS
    a = [[jnp.where(on_diag, d, 0.0) for d in diagonal(b)] for b in streams]
    for i, l in enumerate(levels):
        at_level = pair_level == l
        size = 1 << l
        for b in streams:
            x = upper_q_lower_k(b, l) * decay_levels[b][i]
            if size % 8:
                a[b] = [jnp.where(at_level, p, a_h) for p, a_h in zip(pair_products(x, x), a[b])]
            else:
                mask_up = blocks(pair_level, size, 1) == l
                p_up = pair_products(blocks(x, size, 1), x)
                a[b] = [interleave(blocks(a_h, size, 0), jnp.where(mask_up, p, blocks(a_h, size, 1)), size)
                        for p, a_h in zip(p_up, a[b])]
    return a


def _gla_merge(o, og, gn):
    outs = []
    for h in range(GLA_HEADS):
        oh = o[:, h * GLA_DV:(h + 1) * GLA_DV]
        gh = og[:, h * GLA_DV:(h + 1) * GLA_DV]
        ms = jnp.mean(oh * oh, axis=-1, keepdims=True)
        outs.append(oh * lax.rsqrt(ms + RMS_EPS) * gn * (gh * _sigmoid(gh)))
    return outs


def _dup_halves(x):
    low = lax.broadcasted_iota(jnp.int32, x.shape, 1) < HEAD_DIM
    rolled = pltpu.roll(x, HEAD_DIM, axis=1)
    return jnp.where(low, x, rolled), jnp.where(low, rolled, x)


def _convert_proj_weights(i, w_ref, wg2_ref, wbf_ref, wg2s_ref):
    rows = pl.ds(pl.multiple_of(i * W_ROWS, W_ROWS), W_ROWS)

    def put(dst, src, n=LANES):
        tile = w_ref[0, src:src + n, :]
        if n < LANES:
            tile = jnp.concatenate([tile, jnp.zeros((LANES - n, W_ROWS), F32)], axis=0)
        wbf_ref[rows, dst:dst + LANES] = tile.T.astype(BF16)

    for c in range(SRC_LOW // LANES):
        put(c * LANES, c * LANES)
    for c in range((IN_W - SRC_OG) // LANES):
        put(OFF_OG + c * LANES, SRC_OG + c * LANES)
    put(OFF_LOW, SRC_LOW, GATE_RANK)

    @pl.when(i == 0)
    def _():
        wg2s_ref[...] = jnp.concatenate(
            [wg2_ref[0], jnp.zeros((LANES - GATE_RANK, GLA_K_W), F32)], axis=0).astype(BF16)


def _project_tile(x, vec_ref, wbf_ref, wg2s_ref, z_ref, pf_ref, pb_ref):
    q_ref, k_ref, v_ref, gq_ref, gk_ref, ld_ref, gv_ref, og_ref = _proj_views(pf_ref, pb_ref)
    ms = jnp.mean(x * x, axis=-1, keepdims=True)
    h = (x * lax.rsqrt(ms + RMS_EPS) * vec_ref[V_NORM1:V_NORM1 + 1, :]).astype(BF16)

    low_half = lax.broadcasted_iota(jnp.int32, (ROW_TILE, LANES), 1) < HEAD_DIM

    def head_norm(t):
        t2 = t * t
        ms_lo = jnp.sum(jnp.where(low_half, t2, 0.0), axis=-1, keepdims=True) * (1.0 / HEAD_DIM)
        ms_hi = jnp.sum(jnp.where(low_half, 0.0, t2), axis=-1, keepdims=True) * (1.0 / HEAD_DIM)
        return t * jnp.where(low_half, lax.rsqrt(ms_lo + RMS_EPS), lax.rsqrt(ms_hi + RMS_EPS))

    def q_part():
        for j in range(ATT_Q_W // LANES):
            cols = slice(j * LANES, (j + 1) * LANES)
            q_ref[:, cols] = (head_norm(z_ref[:, cols]) * vec_ref[V_QG:V_QG + 1, cols]).astype(BF16)

    def kv_part():
        k_ref[...] = head_norm(z_ref[:, OFF_K:OFF_V]) * vec_ref[V_KG:V_KG + 1, :ATT_KV_W]
        v_ref[...] = z_ref[:, OFF_V:OFF_GQ]
        gq_ref[...] = z_ref[:, OFF_GQ:OFF_GK] * (GLA_DK ** -0.5)

    def gk_part():
        gk_ref[...] = z_ref[:, OFF_GK:OFF_GV]

    def gv_part():
        gv_ref[...] = z_ref[:, OFF_GV:OFF_OG].astype(BF16)
        og_ref[...] = z_ref[:, OFF_OG:OFF_LOW]

    def gate_part():
        logit =(_dot(z_ref[:, OFF_LOW:PROJ_W].astype(BF16), wg2s_ref[...])
                 + vec_ref[V_BG:V_BG + 1, :GLA_K_W])
        log_sig = jnp.minimum(logit, 0.0) - jnp.log1p(jnp.exp(-jnp.abs(logit)))
        ld_ref[...] = log_sig * (1.0 / GATE_NORMALIZER)

    pieces = [q_part, kv_part, gk_part, gv_part, gate_part]
    for c in [len(pieces) - 1] + list(range(len(pieces) - 1)):
        cols = slice(c * PROJ_CHUNK, min((c + 1) * PROJ_CHUNK, PROJ_W))
        z_ref[:, cols] = _dot(h, wbf_ref[:, cols])
        pieces[c]()


def _attention_units(blk, layer, sinks_ref, pf_ref, pb_ref, m_ref, kprev_ref, vprev_ref):
    q_ref, k_ref, v_ref = _proj_views(pf_ref, pb_ref)[:3]
    per_group = N_HEADS // N_KV_HEADS
    grows = per_group * BLOCK
    row = lax.broadcasted_iota(jnp.int32, (grows, BLOCK), 0) & (BLOCK - 1)
    col = lax.broadcasted_iota(jnp.int32, (grows, BLOCK), 1)
    own = col <= row
    kpos = jnp.where(own, blk * BLOCK, (blk - 1) * BLOCK) + col - PAD_LEN
    live = kpos >= 0
    low_half = lax.broadcasted_iota(jnp.int32, (BLOCK, LANES), 1) < HEAD_DIM
    head_of_row = lax.broadcasted_iota(jnp.int32, (grows, 1), 0) >> N_LEVELS
    sink_cols = []
    for g in range(N_KV_HEADS):
        sink = jnp.zeros((grows, 1), F32)
        for r in range(per_group):
            sink = jnp.where(head_of_row == r, sinks_ref[layer, g * per_group + r], sink)
        sink_cols.append(sink)

    units =[(b, g) for b in range(BATCH) for g in range(N_KV_HEADS)]
    rows = [slice(b * BLOCK, (b + 1) * BLOCK) for b in range(BATCH)]
    tiles = [range(g * per_group // 2, (g + 1) * per_group // 2) for g in range(N_KV_HEADS)]
    kdup = [[x.astype(BF16) for x in _dup_halves(k_ref[rows[b], :])] for b in range(BATCH)]
    vdup = [[x.astype(BF16) for x in _dup_halves(v_ref[rows[b], :])] for b in range(BATCH)]
    kk = [jnp.concatenate([kprev_ref[b * N_KV_HEADS + g], kdup[b][g]], axis=0) for b, g in units]
    vv = [jnp.concatenate([vprev_ref[b * N_KV_HEADS + g], vdup[b][g]], axis=0) for b, g in units]
    for b, g in units:
        kprev_ref[b * N_KV_HEADS + g] = kdup[b][g]
        vprev_ref[b * N_KV_HEADS + g] = vdup[b][g]

    def stacked_queries(b, g):
        pieces = []
        for j in tiles[g]:
            qt = q_ref[rows[b], j * LANES:(j + 1) * LANES]
            zero = jnp.zeros_like(qt)
            pieces += [jnp.where(low_half, qt, zero), jnp.where(low_half, zero, qt)]
        return jnp.concatenate(pieces, axis=0)

    def attend(u, b, g):
        s = _dot_nt(stacked_queries(b, g), kk[u])
        s = jnp.where(live, jnp.where(own, s[:, BLOCK:], s[:, :BLOCK]), MASK_VALUE)
        m = jnp.maximum(jnp.max(s, axis=-1, keepdims=True), sink_cols[g])
        p = jnp.exp(s - m)
        denom = jnp.sum(p, axis=-1, keepdims=True) + jnp.exp(sink_cols[g] - m)
        p2 = jnp.concatenate([jnp.where(own, 0.0, p), jnp.where(own, p, 0.0)], axis=1).astype(BF16)
        o = _dot(p2, vv[u]) / denom
        for n, j in enumerate(tiles[g]):
            o_lo = o[2 * n * BLOCK:(2 * n + 1) * BLOCK]
            o_hi = o[(2 * n + 1) * BLOCK:(2 * n + 2) * BLOCK]
            m_ref[rows[b], j * LANES:(j + 1) * LANES] = jnp.where(low_half, o_lo, o_hi).astype(BF16)

    return [functools.partial(attend, u, b, g) for u, (b, g) in enumerate(units)]


def _gla_decays(pf_ref, tri_ref, lev_ref):
    ld_ref = pf_ref.at[:, PF_LD:PF_OG]
    ld_all = jnp.concatenate([ld_ref[b * BLOCK:(b + 1) * BLOCK, :] for b in range(BATCH)], axis=1)
    hi, lo = _split(ld_all)
    tri = tri_ref[...]
    g_cum_all = _dot(tri, hi) + _dot(tri, lo)
    lev = lev_ref[...]
    low_sums = _dot(lev, hi) + _dot(lev, lo)
    level_sums = [low_sums[l * BLOCK:(l + 1) * BLOCK] for l in range(LOW_LEVELS)]
    for l in range(LOW_LEVELS, N_LEVELS):
        half = 1 << l
        pieces = []
        for p in range(BLOCK // (2 * half)):
            mid = g_cum_all[p * 2 * half + half - 1:p * 2 * half + half, :]
            pieces += [mid - g_cum_all[p * 2 * half:p * 2 * half + half],
                       g_cum_all[p * 2 * half + half:(p + 1) * 2 * half] - mid]
        level_sums.append(jnp.concatenate(pieces, axis=0))
    decay_all = [jnp.exp(s) for s in level_sums]
    g_last_all = jnp.broadcast_to(g_cum_all[BLOCK - 1:BLOCK, :], g_cum_all.shape)
    decay_col_all = jnp.exp(jnp.concatenate(
        [g_last_all[:, c * LANES:(c + 1) * LANES].T for c in range(BATCH * GLA_K_W // LANES)], axis=0))
    return g_cum_all, decay_all, decay_col_all


def _gla_tile(blk, decays, pf_ref, pb_ref, vec_ref, plev_ref, m_ref, state_ref, sbd_ref):
    g_cum_all, decay_all, decay_col_all = decays
    gq_ref, gk_ref, _, gv_ref, og_ref = _proj_views(pf_ref, pb_ref)[3:]
    rpos = blk * BLOCK + lax.broadcasted_iota(jnp.int32, (BLOCK, 1), 0) - PAD_LEN
    valid = (rpos >= 0).astype(F32)
    pair_level = plev_ref[...]
    gn = vec_ref[V_GN:V_GN + 1, :GLA_DV]
    streams = range(BATCH)
    rows = [slice(b * BLOCK, (b + 1) * BLOCK) for b in streams]
    cols = [slice(b * GLA_K_W, (b + 1) * GLA_K_W) for b in streams]
    gq = [gq_ref[rows[b], :] for b in streams]
    gk = [gk_ref[rows[b], :] * valid for b in streams]
    gv = [gv_ref[rows[b], :] for b in streams]
    g_cum = [g_cum_all[:, cols[b]] for b in streams]
    a = _gla_intra(gq, gk, [[d[:, cols[b]] for d in decay_all] for b in streams], range(N_LEVELS), pair_level)

    pairs = range(GLA_HEADS // 2)
    q_dec = [(gq[b] * jnp.exp(g_cum[b])).astype(BF16) for b in streams]
    o = [jnp.concatenate([_dot(q_dec[b][:, i * LANES:(i + 1) * LANES], sbd_ref[2 * b + i]) for i in pairs], axis=1)
         for b in streams]
    o = [o[b] + jnp.concatenate(
        [_dot(a[b][h].astype(BF16), gv[b][:, h * GLA_DV:(h + 1) * GLA_DV]) for h in range(GLA_HEADS)], axis=1)
        for b in streams]
    for b in streams:
        for h, gh in enumerate(_gla_merge(o[b], og_ref[rows[b], :], gn)):
            m_ref[rows[b], ATT_Q_W + h * GLA_DV:ATT_Q_W + (h + 1) * GLA_DV] = gh.astype(BF16)

    k_dec = [(gk[b] * jnp.exp(g_cum[b][BLOCK - 1:BLOCK, :] - g_cum[b])).astype(BF16) for b in streams]
    kv = [[_dot_tn(k_dec[b][:, i * LANES:(i + 1) * LANES], gv[b][:, 2 * i * GLA_DV:(2 * i + 2) * GLA_DV])
           for i in pairs] for b in streams]
    for b in streams:
        new_state = decay_col_all[cols[b], :] * state_ref[b] + jnp.concatenate(
            [kv[b][h // 2][(h % 2) * GLA_DK:(h % 2 + 1) * GLA_DK, (h % 2) * GLA_DV:(h % 2 + 1) * GLA_DV]
             for h in range(GLA_HEADS)], axis=0)
        state_ref[b] = new_state
        for h in range(GLA_HEADS):
            sbd_ref[2 * b + h // 2, (h % 2) * GLA_DK:(h % 2 + 1) * GLA_DK,
                    (h % 2) * GLA_DV:(h % 2 + 1) * GLA_DV] = new_state[h * GLA_DK:(h + 1) * GLA_DK].astype(BF16)


def _front_kernel(*refs, layer, n_src, chained):
    sinks_ref, refs = refs[0], refs[1:]
    if chained:
        refs = refs[3:]
    x_refs, refs = refs[:n_src], refs[n_src:]
    vec_ref, w_ref, wg2_ref, tri_ref, lev_ref, plev_ref = refs[:6]
    ffn_f32_refs, refs = refs[6:10], refs[10:]
    if n_src > 1:
        xo_ref, refs = refs[0], refs[1:]
    (m_ref, pfs_ref, pbs_ref, s_out_ref, k_out_ref, v_out_ref) = refs[:6]
    ffn_bf16_refs, refs = refs[6:10], refs[10:]
    wbf_ref, wg2s_ref, z_ref, pf_ref, pb_ref, kprev_ref, vprev_ref, state_ref, sbd_ref = refs
    i = pl.program_id(0)
    t = i - N_PREP

    @pl.when(i < N_PREP)
    def _():
        _convert_proj_weights(i, w_ref, wg2_ref, wbf_ref, wg2s_ref)

    @pl.when((t >= 1) & (t <= N_CONV))
    def _():
        for src, dst in zip(ffn_f32_refs, ffn_bf16_refs):
            dst[...] = src[0].astype(BF16)

    @pl.when(i == 0)
    def _():
        kprev_ref[...] = jnp.zeros_like(kprev_ref)
        vprev_ref[...] = jnp.zeros_like(vprev_ref)
        state_ref[...] = jnp.zeros_like(state_ref)
        sbd_ref[...] = jnp.zeros_like(sbd_ref)

    def load_x():
        if n_src == 1:
            return x_refs[0][...]
        x = jnp.where(t == 0, x_refs[2][...],
                      jnp.where(t < N_BLOCKS, x_refs[0][...].reshape(ROW_TILE, D_MODEL), x_refs[1][...]))
        xo_ref[...] = x
        return x

    @pl.when((t >= 0) & (t < N_BLOCKS))
    def _():
        _project_tile(load_x(), vec_ref, wbf_ref, wg2s_ref, z_ref, pf_ref, pb_ref)
        for attend in _attention_units(t, layer, sinks_ref, pf_ref, pb_ref, m_ref, kprev_ref, vprev_ref):
            attend()
        _gla_tile(t, _gla_decays(pf_ref, tri_ref, lev_ref), pf_ref, pb_ref, vec_ref, plev_ref, m_ref,
                  state_ref, sbd_ref)

    @pl.when(t == N_BLOCKS - 1)
    def _():
        s_out_ref[0] = state_ref[...]
        for b in range(BATCH):
            k_out_ref[0, b] = pf_ref[b * BLOCK:(b + 1) * BLOCK, PF_K:PF_V].T
            v_out_ref[0, b] = pf_ref[b * BLOCK:(b + 1) * BLOCK, PF_V:PF_GQ].T

    @pl.when(t >= N_BLOCKS)
    def _():
        _project_tile(load_x(), vec_ref, wbf_ref, wg2s_ref, z_ref, pfs_ref, pbs_ref)


def _front(layer, sinks, x_srcs, vecs, w_in, w_g2, ffn_f32, consts, prev):
    n_src = len(x_srcs)
    step = lambda i: i - N_PREP
    conv_of = lambda i: jnp.clip(step(i) - 1, 0, N_CONV - 1)
    tile_of = lambda i: jnp.where(step(i) <= 0, LEAD_TILE, jnp.minimum(step(i) - 1, LEAD_TILE - 1))
    prompt_tile_of = lambda i: jnp.where(step(i) <= 0, LEAD_TILE, jnp.minimum(step(i) - 1, MAIN_TILES - 1))
    sample_tile_of = lambda i: jnp.clip(step(i) - N_BLOCKS, 0, SAMPLE_TILES - 1)
    full = lambda a: pl.BlockSpec(a.shape, lambda i, *_: (0,) * a.ndim)
    per_batch = lambda r, c: pl.BlockSpec((1, BATCH, r, c), lambda i, *_: (layer, 0, 0, 0))
    if n_src == 1:
        x_specs = [pl.BlockSpec((ROW_TILE, D_MODEL), lambda i, *_: (tile_of(i), 0))]
    else:
        x_specs = [
            pl.BlockSpec((BATCH, BLOCK, D_MODEL), lambda i, *_: (0, jnp.clip(step(i) - 1, 0, MAIN_TILES - 1), 0)),
            pl.BlockSpec((ROW_TILE, D_MODEL), lambda i, *_: (sample_tile_of(i), 0), pipeline_mode=pl.Buffered(1)),
            pl.BlockSpec((ROW_TILE, D_MODEL), lambda i, *_: (0, 0), pipeline_mode=pl.Buffered(1)),
        ]
    const_args = (consts["tri"], consts["lev_low"], consts["pair_level"])
    chained = prev is not None
    prev_args = tuple(prev) if chained else ()
    in_specs = ([pl.BlockSpec(memory_space=pl.ANY)] * len(prev_args) + x_specs + [
        pl.BlockSpec((V_ROWS, D_MODEL), lambda i, *_: (layer, 0)),
        pl.BlockSpec((1, IN_W, W_ROWS), lambda i, *_: (layer, 0, jnp.minimum(i, N_PREP - 1))),
        pl.BlockSpec((1, GATE_RANK, GLA_K_W), lambda i, *_: (layer, 0, 0)),
    ] + [full(a) for a in const_args] + [
        pl.BlockSpec((1, a.shape[1] // N_CONV, a.shape[2]), lambda i, *_: (layer, conv_of(i), 0)) for a in ffn_f32])
    out_specs = [pl.BlockSpec((ROW_TILE, D_MIX), lambda i, *_: (prompt_tile_of(i), 0)),
                 pl.BlockSpec((ROW_TILE, PF_W), lambda i, *_: (sample_tile_of(i), 0)),
                 pl.BlockSpec((ROW_TILE, PB_W), lambda i, *_: (sample_tile_of(i), 0)),
                 per_batch(GLA_K_W, GLA_DV), per_batch(BLOCK, ATT_KV_W), per_batch(BLOCK, ATT_KV_W)]
    out_specs += [pl.BlockSpec((a.shape[1] // N_CONV, a.shape[2]), lambda i, *_: (conv_of(i), 0)) for a in ffn_f32]
    out_shape = [jax.ShapeDtypeStruct((TOTAL_ROWS, D_MIX), BF16),
                 jax.ShapeDtypeStruct((SAMPLE_ROWS, PF_W), F32),
                 jax.ShapeDtypeStruct((SAMPLE_ROWS, PB_W), BF16),
                 jax.ShapeDtypeStruct((DEPTH, BATCH, GLA_K_W, GLA_DV), F32),
                 jax.ShapeDtypeStruct((DEPTH, BATCH, BLOCK, ATT_KV_W), F32),
                 jax.ShapeDtypeStruct((DEPTH, BATCH, BLOCK, ATT_KV_W), F32)]
    out_shape += [jax.ShapeDtypeStruct(a.shape[1:], BF16) for a in ffn_f32]
    n_lead_out = 0
    if n_src > 1:
        out_specs = [pl.BlockSpec((ROW_TILE, D_MODEL), lambda i, *_: (tile_of(i), 0))] + out_specs
        out_shape = [jax.ShapeDtypeStruct((TOTAL_ROWS, D_MODEL), F32)] + out_shape
        n_lead_out = 1
    grid_spec = pltpu.PrefetchScalarGridSpec(
        num_scalar_prefetch=1,
        grid=(N_PREP + N_BLOCKS + SAMPLE_TILES,),
        in_specs=in_specs,
        out_specs=out_specs,
        scratch_shapes=[pltpu.VMEM((D_MODEL, PROJ_W), BF16), pltpu.VMEM((LANES, GLA_K_W), BF16),
                        pltpu.VMEM((ROW_TILE, PROJ_W), F32),
                        pltpu.VMEM((ROW_TILE, PF_W), F32), pltpu.VMEM((ROW_TILE, PB_W), BF16),
                        pltpu.VMEM((BATCH * N_KV_HEADS, BLOCK, ATT_KV_W), BF16),
                        pltpu.VMEM((BATCH * N_KV_HEADS, BLOCK, ATT_KV_W), BF16),
                        pltpu.VMEM((BATCH, GLA_K_W, GLA_DV), F32),
                        pltpu.VMEM((BATCH * GLA_HEADS // 2, 2 * GLA_DK, 2 * GLA_DV), BF16)],
    )
    res = pl.pallas_call(
        functools.partial(_front_kernel, layer=layer, n_src=n_src, chained=chained),
        grid_spec=grid_spec,
        out_shape=out_shape,
        input_output_aliases={1 + n: n_lead_out + 3 + n for n in range(len(prev_args))},
        compiler_params=pltpu.CompilerParams(dimension_semantics=("arbitrary",),
                                             vmem_limit_bytes=VMEM_LIMIT),
        name="front",
    )(sinks, *prev_args, *x_srcs, vecs, w_in, w_g2, *const_args, *ffn_f32)
    x_all = res[0] if n_src > 1 else x_srcs[0]
    merged, pfs, pbs, ps, pk, pv = res[n_lead_out:n_lead_out + 6]
    return x_all, merged, (pfs, pbs), (ps, pk, pv), tuple(res[n_lead_out + 6:])


def _mix_sample_kernel(*refs, layer, n_alias):
    sinks_ref, refs = refs[0], refs[n_alias + 1:]
    (pf_ref, pb_ref, ck_ref, cv_ref, st_ref,
     vec_ref, tri_ref, sones_ref, lev_ref, ones_ref, plev_ref,
     m_ref, ck_out_ref, cv_out_ref, st_out_ref) = refs
    q_ref, k_ref, v_ref, gq_ref, gk_ref, ld_ref, gv_ref, og_ref = _proj_views(pf_ref, pb_ref)
    gq = gq_ref[...]
    gk = gk_ref[...]
    gv = gv_ref[...]
    gvf = gv.astype(F32)
    hi, lo = _split(ld_ref[...])
    tri = tri_ref[...]
    g_cum = _dot(tri, hi) + _dot(tri, lo)
    sones = sones_ref[...]
    g_tot = _dot(sones, hi) + _dot(sones, lo)
    lev = lev_ref[...]
    decay_levels = jnp.exp(_dot(lev, hi) + _dot(lev, lo))
    a = _gla_intra([gq], [gk], [[decay_levels[l * BLOCK:(l + 1) * BLOCK] for l in range(LOW_LEVELS)]],
                   range(LOW_LEVELS), plev_ref[...])[0]
    o_intra = jnp.concatenate(
        [_dot(a[h].astype(BF16), gv[:, h * GLA_DV:(h + 1) * GLA_DV]) for h in range(GLA_HEADS)], axis=1)
    q_dec = gq * jnp.exp(g_cum)
    k_dec = gk * jnp.exp(g_tot - g_cum)
    hm8 = _head_masks(DEC_SEQ)
    ones8 = ones_ref[...]
    hi_f, lo_f = hi.astype(F32), lo.astype(F32)

    qf = q_ref[...].astype(F32)
    kf = k_ref[...]
    vf = v_ref[...]
    low8 = lax.broadcasted_iota(jnp.int32, (DEC_SEQ, LANES), 1) < HEAD_DIM
    nkeys = WINDOW + DEC_SEQ
    srow = lax.broadcasted_iota(jnp.int32, (N_HEADS * DEC_SEQ, nkeys), 0)
    scol = lax.broadcasted_iota(jnp.int32, (N_HEADS * DEC_SEQ, nkeys), 1)
    t_of_row = srow & (DEC_SEQ - 1)
    amask = ((scol < WINDOW) & (scol > t_of_row)) | ((scol >= WINDOW) & (scol - WINDOW <= t_of_row))
    rid = lax.broadcasted_iota(jnp.int32, (N_HEADS * DEC_SEQ, 1), 0) >> LOW_LEVELS
    sink_col = jnp.zeros((N_HEADS * DEC_SEQ, 1), F32)
    for i in range(N_HEADS):
        sink_col = jnp.where(rid == i, sinks_ref[layer, i], sink_col)

    seqs = range(SEQ_GROUP)
    rows = [slice(b * DEC_SEQ, (b + 1) * DEC_SEQ) for b in seqs]

    def stacked_queries(b):
        pieces = []
        for j in range(N_HEADS // 2):
            g = (2 * j) // (N_HEADS // N_KV_HEADS)
            qt = qf[rows[b], j * LANES:(j + 1) * LANES]
            swapped = pltpu.roll(qt, HEAD_DIM, axis=1)
            own_lanes = low8 if g == 0 else ~low8
            even, odd = (qt, swapped) if g == 0 else (swapped, qt)
            pieces += [jnp.where(own_lanes, even, 0.0), jnp.where(own_lanes, odd, 0.0)]
        return jnp.concatenate(pieces, axis=0).astype(BF16)

    lane = lax.broadcasted_iota(jnp.int32, (WINDOW, WINDOW), 1)
    old_lanes = lane < WINDOW - DEC_SEQ

    def shifted_buffer(buf, new_rows):
        wide = jnp.concatenate([jnp.zeros((WINDOW - DEC_SEQ, LANES), F32), new_rows], axis=0)
        return jnp.where(old_lanes, pltpu.roll(buf, WINDOW - DEC_SEQ, axis=1), wide.T)

    qp = [stacked_queries(b) for b in seqs]
    ck = [ck_ref[0, b] for b in seqs]
    cv = [cv_ref[0, b] for b in seqs]
    for b in seqs:
        ck_out_ref[0, b] = shifted_buffer(ck[b], kf[rows[b]])
        cv_out_ref[0, b] = shifted_buffer(cv[b], vf[rows[b]])
    s = [jnp.concatenate([_dot(qp[b], ck[b].astype(BF16)), _dot_nt(qp[b], kf[rows[b]].astype(BF16))], axis=1)
         for b in seqs]
    s = [jnp.where(amask, s[b], MASK_VALUE) for b in seqs]
    m = [jnp.maximum(jnp.max(s[b], axis=-1, keepdims=True), sink_col) for b in seqs]
    p = [jnp.exp(s[b] - m[b]) for b in seqs]
    denom = [jnp.sum(p[b], axis=-1, keepdims=True) + jnp.exp(sink_col - m[b]) for b in seqs]
    pb = [p[b].astype(BF16) for b in seqs]
    ob = [(_dot_nt(pb[b][:, :WINDOW], cv[b].astype(BF16))
           + _dot(pb[b][:, WINDOW:], vf[rows[b]].astype(BF16))) / denom[b] for b in seqs]

    def head_tiles(o):
        tiles = []
        for j in range(N_HEADS // 2):
            g = (2 * j) // (N_HEADS // N_KV_HEADS)
            even = o[(2 * j) * DEC_SEQ:(2 * j + 1) * DEC_SEQ]
            odd = o[(2 * j + 1) * DEC_SEQ:(2 * j + 2) * DEC_SEQ]
            if g == 0:
                tiles.append(jnp.where(low8, even, pltpu.roll(odd, HEAD_DIM, axis=1)))
            else:
                tiles.append(jnp.where(low8, pltpu.roll(even, HEAD_DIM, axis=1), odd))
        return jnp.concatenate(tiles, axis=1)

    att_rows = [head_tiles(ob[b]) for b in seqs]

    head_stack = lambda x: jnp.concatenate([jnp.where(mk, x, 0.0) for mk in hm8], axis=0).astype(BF16)
    state = [st_ref[0, b] for b in seqs]
    oi = [_dot(head_stack(q_dec[rows[b]]), state[b].astype(BF16)) for b in seqs]
    inter_rows = [jnp.concatenate([oi[b][h * DEC_SEQ:(h + 1) * DEC_SEQ] for h in range(GLA_HEADS)], axis=1)
                  for b in seqs]
    vstack = [jnp.concatenate([gvf[rows[b]][:, h * GLA_DV:(h + 1) * GLA_DV] for h in range(GLA_HEADS)],
                              axis=0).astype(BF16) for b in seqs]
    kv = [_dot_tn(head_stack(k_dec[rows[b]]), vstack[b]) for b in seqs]
    decay_col = [jnp.exp(_dot_tn(hi_f[rows[b]], ones8) + _dot_tn(lo_f[rows[b]], ones8)) for b in seqs]
    for b in seqs:
        st_out_ref[0, b] = decay_col[b] * state[b] + kv[b]

    m_ref[:, :ATT_Q_W] = jnp.concatenate(att_rows, axis=0).astype(BF16)
    o = o_intra + jnp.concatenate(inter_rows, axis=0)
    gn = vec_ref[V_GN:V_GN + 1, :GLA_DV]
    for h, gh in enumerate(_gla_merge(o, og_ref[...], gn)):
        m_ref[:, ATT_Q_W + h * GLA_DV:ATT_Q_W + (h + 1) * GLA_DV] = gh.astype(BF16)


def _mix_sample(layer, sinks, merged, proj, cache_k, cache_v, state, vecs, consts, prev):
    tok = lambda w_: pl.BlockSpec((BLOCK, w_), lambda i, *_: (i, 0))
    merged_blk = pl.BlockSpec((BLOCK, D_MIX), lambda i, *_: (MAIN_ROWS // BLOCK + i, 0))
    full = lambda a: pl.BlockSpec(a.shape, lambda i, *_: (0,) * a.ndim)
    seq = lambda a: pl.BlockSpec((1, SEQ_GROUP) + a.shape[2:], lambda i, *_: (layer, i, 0, 0))
    ones8 = jnp.ones((DEC_SEQ, LANES), F32)
    const_args = (consts["seq_tri"], consts["seq_ones"], consts["lev_low"], ones8, consts["pair_level"])
    seq_args = (cache_k, cache_v, state)
    alias_args = (merged,) + (tuple(prev) if prev is not None else ())
    grid_spec = pltpu.PrefetchScalarGridSpec(
        num_scalar_prefetch=1,
        grid=(DEC_BATCH // SEQ_GROUP,),
        in_specs=([pl.BlockSpec(memory_space=pl.ANY)] * len(alias_args)
                  + [tok(a.shape[1]) for a in proj] + [seq(a) for a in seq_args]
                  + [pl.BlockSpec((V_ROWS, D_MODEL), lambda i, *_: (layer, 0))]
                  + [full(a) for a in const_args]),
        out_specs=[merged_blk] + [seq(a) for a in seq_args],
    )
    return pl.pallas_call(
        functools.partial(_mix_sample_kernel, layer=layer, n_alias=len(alias_args)),
        grid_spec=grid_spec,
        out_shape=[jax.ShapeDtypeStruct(merged.shape, merged.dtype)]
        + [jax.ShapeDtypeStruct(a.shape, a.dtype) for a in seq_args],
        input_output_aliases={1 + n: n for n in range(len(alias_args))},
        compiler_params=pltpu.CompilerParams(dimension_semantics=("arbitrary",),
                                             vmem_limit_bytes=VMEM_LIMIT),
        name="mix_sample",
    )(sinks, *alias_args, *proj, *seq_args, vecs, *const_args)


def _out_ffn_kernel(*refs, last):
    x_ref, m_ref, vec_ref, wo_ref, wg_ref, wu_ref, wd_ref = refs[:7]
    out_refs = refs[7:9] if last else refs[7:8]
    act_ref = refs[-1]
    x1 = x_ref[...] + _dot(m_ref[...], wo_ref[...])
    ms = jnp.mean(x1 * x1, axis=-1, keepdims=True)
    h = (x1 * lax.rsqrt(ms + RMS_EPS) * vec_ref[V_NORM2:V_NORM2 + 1, :]).astype(BF16)
    for c in range(D_FFN // FFN_CHUNK):
        cols = slice(c * FFN_CHUNK, (c + 1) * FFN_CHUNK)
        gate = _dot(h, wg_ref[:, cols])
        up = _dot(h, wu_ref[:, cols])
        act_ref[:, cols] = (gate * _sigmoid(gate) * up).astype(BF16)
    y = x1 + _dot(act_ref[...], wd_ref[...])
    if last:
        out_refs[0][...] = y.reshape(BATCH, BLOCK, D_MODEL)

        @pl.when(pl.program_id(0) < SAMPLE_TILES)
        def _():
            out_refs[1][...] = out_refs[0][...].reshape(ROW_TILE, D_MODEL)
    else:
        out_refs[0][...] = y


def _out_ffn(layer, x_all, merged, vecs, ffn_weights, last):
    resident = lambda a: pl.BlockSpec(a.shape, lambda i: (0, 0), pipeline_mode=pl.Buffered(1))
    if last:
        n_tiles = MAIN_TILES + SAMPLE_TILES
        tile_of = lambda i: jnp.where(i < SAMPLE_TILES, MAIN_TILES + i, i - SAMPLE_TILES)
        out_specs = [pl.BlockSpec((BATCH, BLOCK, D_MODEL), lambda i: (0, jnp.maximum(i - SAMPLE_TILES, 0), 0)),
                     pl.BlockSpec((ROW_TILE, D_MODEL), lambda i: (jnp.minimum(i, SAMPLE_TILES - 1), 0))]
        out_shape = [jax.ShapeDtypeStruct((BATCH, SEQ, D_MODEL), F32),
                     jax.ShapeDtypeStruct((SAMPLE_ROWS, D_MODEL), F32)]
    else:
        n_tiles = N_TILES
        tile_of = lambda i: i
        out_specs = [pl.BlockSpec((ROW_TILE, D_MODEL), lambda i: (i, 0))]
        out_shape = [jax.ShapeDtypeStruct((TOTAL_ROWS, D_MODEL), F32)]
    tile = lambda w_: pl.BlockSpec((ROW_TILE, w_), lambda i: (tile_of(i), 0))
    return pl.pallas_call(
        functools.partial(_out_ffn_kernel, last=last),
        grid=(n_tiles,),
        in_specs=[tile(D_MODEL), tile(D_MIX), pl.BlockSpec((V_ROWS, D_MODEL), lambda i: (layer, 0))]
        + [resident(w) for w in ffn_weights],
        out_specs=out_specs,
        out_shape=out_shape,
        scratch_shapes=[pltpu.VMEM((ROW_TILE, D_FFN), BF16)],
        compiler_params=pltpu.CompilerParams(dimension_semantics=("arbitrary",),
                                             vmem_limit_bytes=VMEM_LIMIT),
        name="out_ffn",
    )(x_all, merged, vecs, *ffn_weights)


def _vector_slab(norm1, norm2, q_norm, k_norm, b_g, gla_norm):
    pad = lambda a: jnp.pad(a.astype(F32), ((0, 0), (0, D_MODEL - a.shape[1])))
    rows = [norm1.astype(F32), norm2.astype(F32), pad(jnp.tile(q_norm, (1, N_HEADS)) * ATT_SCALE),
            pad(jnp.tile(k_norm, (1, N_KV_HEADS))), pad(b_g), pad(gla_norm)]
    rows += [jnp.zeros((DEPTH, D_MODEL), F32)] * (V_ROWS - len(rows))
    return jnp.stack(rows, axis=1).reshape(DEPTH * V_ROWS, D_MODEL)


def kernel(x_prompt, x_sample, cache_k, cache_v, state_gla, meta, norm1, w_in, q_norm, k_norm, sinks,
           w_g2, b_g, gla_norm, w_o, norm2, w_gate, w_up, w_down):
    consts = _constants()
    dt = x_prompt.dtype
    vecs = _vector_slab(norm1, norm2, q_norm, k_norm, b_g, gla_norm)
    sinks = sinks.astype(F32)
    lead = jnp.tile(jnp.concatenate([jnp.zeros((PAD_LEN, D_MODEL), dt), meta.astype(dt)], axis=0), (BATCH, 1))
    x_srcs = (x_prompt, x_sample.reshape(SAMPLE_ROWS, D_MODEL), lead)
    to_feature_major = lambda c: c.transpose(0, 1, 3, 4, 2).reshape(DEPTH, DEC_BATCH, ATT_KV_W, WINDOW)
    ck_in, cv_in = to_feature_major(cache_k), to_feature_major(cache_v)
    st_in = state_gla.reshape(DEPTH, DEC_BATCH, GLA_K_W, GLA_DV)
    w_in_t = jnp.swapaxes(w_in, 1, 2)

    prompt_outs, sample_outs = None, None
    for l in range(DEPTH):
        x_all, merged, proj_sample, prompt_outs, ffn_bf16 = _front(
            l, sinks, x_srcs, vecs, w_in_t, w_g2, (w_o, w_gate, w_up, w_down), consts, prompt_outs)
        merged, *sample_outs = _mix_sample(l, sinks, merged, proj_sample, ck_in, cv_in, st_in, vecs, consts,
                                           sample_outs)
        x_srcs = tuple(_out_ffn(l, x_all, merged, vecs, ffn_bf16, last=l == DEPTH - 1))

    y_main, y_sample = x_srcs
    ps, pk, pv = prompt_outs
    sk, sv, ss = sample_outs
    kv5 = lambda a, n: a.reshape(DEPTH, n, N_KV_HEADS, HEAD_DIM, WINDOW).transpose(0, 1, 4, 2, 3)
    st5 = lambda a, n: a.reshape(DEPTH, n, GLA_HEADS, GLA_DK, GLA_DV)
    return (y_main, y_sample.reshape(DEC_BATCH, DEC_SEQ, D_MODEL),
            kv5(pk, BATCH), kv5(pv, BATCH), st5(ps, BATCH), kv5(sk, DEC_BATCH), kv5(sv, DEC_BATCH),
            st5(ss, DEC_BATCH))
```

```python
import functools

import jax
import jax.numpy as jnp
import numpy as np
from jax import lax
from jax.experimental import pallas as pl
from jax.experimental.pallas import tpu as pltpu

F32 = jnp.float32
BF16 = jnp.bfloat16

D_MODEL = 1024
BATCH = 4
SEQ = 4096
DEPTH = 2
DEC_BATCH = 128
DEC_SEQ = 8
N_META = 16
WINDOW = 128
BLOCK = 128
PAD_LEN = BLOCK - N_META
N_HEADS = 8
N_KV_HEADS = 2
HEAD_DIM = 64
ATT_SCALE = HEAD_DIM ** -0.5
ATT_Q_W = N_HEADS * HEAD_DIM
ATT_KV_W = N_KV_HEADS * HEAD_DIM
GLA_HEADS = 4
GLA_DK = 64
GLA_DV = 128
GLA_K_W = GLA_HEADS * GLA_DK
GLA_V_W = GLA_HEADS * GLA_DV
GATE_RANK = 16
GATE_NORMALIZER = 16.0
D_MIX = ATT_Q_W + GLA_V_W
D_FFN = 2816
IN_W = 2320
RMS_EPS = 1e-6
MASK_VALUE = -1e30

LANES = 128
N_BLOCKS = 1 + SEQ // BLOCK
MAIN_ROWS = BATCH * SEQ
SAMPLE_ROWS = DEC_BATCH * DEC_SEQ
LEAD_ROWS = BATCH * BLOCK
TOTAL_ROWS = MAIN_ROWS + SAMPLE_ROWS + LEAD_ROWS
ROW_TILE = 512
MAIN_TILES = MAIN_ROWS // ROW_TILE
SAMPLE_TILES = SAMPLE_ROWS // ROW_TILE
N_TILES = TOTAL_ROWS // ROW_TILE
LEAD_TILE = N_TILES - 1
SEQ_GROUP = BLOCK // DEC_SEQ
N_LEVELS = 7
LOW_LEVELS = 3
LOG_DK = 6
N_PREP = 8
W_ROWS = D_MODEL // N_PREP
N_CONV = 16

SRC_LOW, SRC_OG = 1792, 1808
OFF_Q, OFF_K, OFF_V = 0, 512, 640
OFF_GQ, OFF_GK, OFF_GV, OFF_OG, OFF_LOW = 768, 1024, 1280, 1792, 2304
PROJ_W = OFF_LOW + LANES
PROJ_CHUNK = 512
FFN_CHUNK = 256
VMEM_LIMIT = 58 * 1024 * 1024

V_NORM1, V_NORM2, V_QG, V_KG, V_BG, V_GN, V_ROWS = 0, 1, 2, 3, 4, 5, 8

PF_K, PF_V, PF_GQ, PF_GK, PF_LD, PF_OG, PF_W = 0, 128, 256, 512, 768, 1024, 1536
PB_Q, PB_GV, PB_W = 0, 512, 1024


def _proj_views(pf_ref, pb_ref):
    f = lambda a, b: pf_ref.at[:, a:b]
    return (pb_ref.at[:, PB_Q:PB_GV], f(PF_K, PF_V), f(PF_V, PF_GQ), f(PF_GQ, PF_GK), f(PF_GK, PF_LD),
            f(PF_LD, PF_OG), pb_ref.at[:, PB_GV:PB_W], f(PF_OG, PF_W))


def _dot(a, b):
    return jnp.dot(a, b, preferred_element_type=F32)


def _dot_nt(a, b):
    return lax.dot_general(a, b, (((1,), (1,)), ((), ())), preferred_element_type=F32)


def _dot_tn(a, b):
    return lax.dot_general(a, b, (((0,), (0,)), ((), ())), preferred_element_type=F32)


def _split(x):
    hi = x.astype(BF16)
    lo = (x - hi.astype(F32)).astype(BF16)
    return hi, lo


def _sigmoid(x):
    return 1.0 / (1.0 + jnp.exp(-x))


def _level_matrix(levels, n=BLOCK):
    out = np.zeros((len(levels) * n, n), np.float32)
    for i, l in enumerate(levels):
        size = 2 << l
        for t in range(n):
            mid = (t // size) * size + size // 2 - 1
            if (t >> l) & 1:
                out[i * n + t, mid + 1:t + 1] = 1.0
            else:
                out[i * n + t, t + 1:mid + 1] = 1.0
    return out


def _constants():
    r = np.arange(BLOCK)
    tri = (r[None, :] <= r[:, None]).astype(np.float32)
    same_seq = (r[None, :] // DEC_SEQ) == (r[:, None] // DEC_SEQ)
    diff = np.maximum(r[:, None] ^ r[None, :], 1)
    pair_level = np.where(r[None, :] < r[:, None], np.floor(np.log2(diff)).astype(np.int32),
                          np.where(r[None, :] == r[:, None], N_LEVELS, N_LEVELS + 1)).astype(np.int32)
    return dict(
        tri=jnp.asarray(tri, BF16),
        seq_tri=jnp.asarray(tri * same_seq, BF16),
        seq_ones=jnp.asarray(same_seq.astype(np.float32), BF16),
        lev_low=jnp.asarray(_level_matrix(range(LOW_LEVELS)), BF16),
        pair_level=jnp.asarray(pair_level),
    )


def _head_masks(rows):
    lane = lax.broadcasted_iota(jnp.int32, (rows, GLA_K_W), 1)
    return [(lane >> LOG_DK) == h for h in range(GLA_HEADS)]


def _gla_intra(gq, gk, decay_levels, levels, pair_level):
    rows = gq[0].shape[0]
    streams = range(len(gq))
    low_half = lax.broadcasted_iota(jnp.int32, (rows, LANES), 1) < GLA_DK

    def pair_products(qh, kh):
        qh, kh = qh.astype(BF16), kh.astype(BF16)
        out = []
        for pair in range(GLA_HEADS // 2):
            kl = kh[:, pair * LANES:(pair + 1) * LANES]
            zero = jnp.zeros_like(kl)
            stacked = jnp.concatenate([jnp.where(low_half, kl, zero), jnp.where(low_half, zero, kl)], axis=0)
            p = _dot_nt(qh[:, pair * LANES:(pair + 1) * LANES], stacked)
            out += [p[:, :rows], p[:, rows:]]
        return out

    row = lax.broadcasted_iota(jnp.int32, (rows, 1), 0)

    def upper_q_lower_k(b, l):
        size = 1 << l
        if size % 8:
            return jnp.where(((row >> l) & 1) == 1, gq[b], gk[b])
        return jnp.concatenate([(gq[b] if i & 1 else gk[b])[i * size:(i + 1) * size]
                                for i in range(rows // size)], axis=0)

    def diagonal(b):
        out = []
        for pair in range(GLA_HEADS // 2):
            prod = (gq[b] * gk[b])[:, pair * LANES:(pair + 1) * LANES]
            out += [jnp.sum(jnp.where(low_half, prod, 0.0), axis=-1, keepdims=True),
                    jnp.sum(jnp.where(low_half, 0.0, prod), axis=-1, keepdims=True)]
        return out

    def blocks(v, size, parity):
        return jnp.concatenate([v[i * size:(i + 1) * size] for i in range(parity, rows // size, 2)], axis=0)

    def interleave(lower, upper, size):
        pieces = []
        for i in range(rows // (2 * size)):
            pieces += [lower[i * size:(i + 1) * size], upper[i * size:(i + 1) * size]]
        return jnp.concatenate(pieces, axis=0)

    on_diag = pair_level == N_LEVELS
    a = [[jnp.where(on_diag, d, 0.0) for d in diagonal(b)] for b in streams]
    for i, l in enumerate(levels):
        at_level = pair_level == l
        size = 1 << l
        for b in streams:
            x = upper_q_lower_k(b, l) * decay_levels[b][i]
            if size % 8:
                a[b] = [jnp.where(at_level, p, a_h) for p, a_h in zip(pair_products(x, x), a[b])]
            else:
                mask_up = blocks(pair_level, size, 1) == l
                p_up = pair_products(blocks(x, size, 1), x)
                a[b] = [interleave(blocks(a_h, size, 0), jnp.where(mask_up, p, blocks(a_h, size, 1)), size)
                        for p, a_h in zip(p_up, a[b])]
    return a


def _gla_merge(o, og, gn):
    outs = []
    for h in range(GLA_HEADS):
        oh = o[:, h * GLA_DV:(h + 1) * GLA_DV]
        gh = og[:, h * GLA_DV:(h + 1) * GLA_DV]
        ms = jnp.mean(oh * oh, axis=-1, keepdims=True)
        outs.append(oh * lax.rsqrt(ms + RMS_EPS) * gn * (gh * _sigmoid(gh)))
    return outs


def _dup_halves(x):
    low = lax.broadcasted_iota(jnp.int32, x.shape, 1) < HEAD_DIM
    rolled = pltpu.roll(x, HEAD_DIM, axis=1)
    return jnp.where(low, x, rolled), jnp.where(low, rolled, x)


def _convert_proj_weights(i, w_ref, wg2_ref, wbf_ref, wg2s_ref):
    rows = pl.ds(pl.multiple_of(i * W_ROWS, W_ROWS), W_ROWS)

    def put(dst, src, n=LANES):
        tile = w_ref[0, src:src + n, :]
        if n < LANES:
            tile = jnp.concatenate([tile, jnp.zeros((LANES - n, W_ROWS), F32)], axis=0)
        wbf_ref[rows, dst:dst + LANES] = tile.T.astype(BF16)

    for c in range(SRC_LOW // LANES):
        put(c * LANES, c * LANES)
    for c in range((IN_W - SRC_OG) // LANES):
        put(OFF_OG + c * LANES, SRC_OG + c * LANES)
    put(OFF_LOW, SRC_LOW, GATE_RANK)

    @pl.when(i == 0)
    def _():
        wg2s_ref[...] = jnp.concatenate(
            [wg2_ref[0], jnp.zeros((LANES - GATE_RANK, GLA_K_W), F32)], axis=0).astype(BF16)


def _project_tile(x, vec_ref, wbf_ref, wg2s_ref, z_ref, pf_ref, pb_ref):
    q_ref, k_ref, v_ref, gq_ref, gk_ref, ld_ref, gv_ref, og_ref = _proj_views(pf_ref, pb_ref)
    ms = jnp.mean(x * x, axis=-1, keepdims=True)
    h = (x * lax.rsqrt(ms + RMS_EPS) * vec_ref[V_NORM1:V_NORM1 + 1, :]).astype(BF16)

    low_half = lax.broadcasted_iota(jnp.int32, (ROW_TILE, LANES), 1) < HEAD_DIM

    def head_norm(t):
        t2 = t * t
        ms_lo = jnp.sum(jnp.where(low_half, t2, 0.0), axis=-1, keepdims=True) * (1.0 / HEAD_DIM)
        ms_hi = jnp.sum(jnp.where(low_half, 0.0, t2), axis=-1, keepdims=True) * (1.0 / HEAD_DIM)
        return t * jnp.where(low_half, lax.rsqrt(ms_lo + RMS_EPS), lax.rsqrt(ms_hi + RMS_EPS))

    def q_part():
        for j in range(ATT_Q_W // LANES):
            cols = slice(j * LANES, (j + 1) * LANES)
            q_ref[:, cols] = (head_norm(z_ref[:, cols]) * vec_ref[V_QG:V_QG + 1, cols]).astype(BF16)

    def kv_part():
        k_ref[...] = head_norm(z_ref[:, OFF_K:OFF_V]) * vec_ref[V_KG:V_KG + 1, :ATT_KV_W]
        v_ref[...] = z_ref[:, OFF_V:OFF_GQ]
        gq_ref[...] = z_ref[:, OFF_GQ:OFF_GK] * (GLA_DK ** -0.5)

    def gk_part():
        gk_ref[...] = z_ref[:, OFF_GK:OFF_GV]

    def gv_part():
        gv_ref[...] = z_ref[:, OFF_GV:OFF_OG].astype(BF16)
        og_ref[...] = z_ref[:, OFF_OG:OFF_LOW]

    def gate_part():
        logit =(_dot(z_ref[:, OFF_LOW:PROJ_W].astype(BF16), wg2s_ref[...])
                 + vec_ref[V_BG:V_BG + 1, :GLA_K_W])
        log_sig = jnp.minimum(logit, 0.0) - jnp.log1p(jnp.exp(-jnp.abs(logit)))
        ld_ref[...] = log_sig * (1.0 / GATE_NORMALIZER)

    pieces = [q_part, kv_part, gk_part, gv_part, gate_part]
    for c in [len(pieces) - 1] + list(range(len(pieces) - 1)):
        cols = slice(c * PROJ_CHUNK, min((c + 1) * PROJ_CHUNK, PROJ_W))
        z_ref[:, cols] = _dot(h, wbf_ref[:, cols])
        pieces[c]()


def _attention_units(blk, layer, sinks_ref, pf_ref, pb_ref, m_ref, kprev_ref, vprev_ref):
    q_ref, k_ref, v_ref = _proj_views(pf_ref, pb_ref)[:3]
    per_group = N_HEADS // N_KV_HEADS
    grows = per_group * BLOCK
    row = lax.broadcasted_iota(jnp.int32, (grows, BLOCK), 0) & (BLOCK - 1)
    col = lax.broadcasted_iota(jnp.int32, (grows, BLOCK), 1)
    own = col <= row
    kpos = jnp.where(own, blk * BLOCK, (blk - 1) * BLOCK) + col - PAD_LEN
    live = kpos >= 0
    low_half = lax.broadcasted_iota(jnp.int32, (BLOCK, LANES), 1) < HEAD_DIM
    head_of_row = lax.broadcasted_iota(jnp.int32, (grows, 1), 0) >> N_LEVELS
    sink_cols = []
    for g in range(N_KV_HEADS):
        sink = jnp.zeros((grows, 1), F32)
        for r in range(per_group):
            sink = jnp.where(head_of_row == r, sinks_ref[layer, g * per_group + r], sink)
        sink_cols.append(sink)

    units =[(b, g) for b in range(BATCH) for g in range(N_KV_HEADS)]
    rows = [slice(b * BLOCK, (b + 1) * BLOCK) for b in range(BATCH)]
    tiles = [range(g * per_group // 2, (g + 1) * per_group // 2) for g in range(N_KV_HEADS)]
    kdup = [[x.astype(BF16) for x in _dup_halves(k_ref[rows[b], :])] for b in range(BATCH)]
    vdup = [[x.astype(BF16) for x in _dup_halves(v_ref[rows[b], :])] for b in range(BATCH)]
    kk = [jnp.concatenate([kprev_ref[b * N_KV_HEADS + g], kdup[b][g]], axis=0) for b, g in units]
    vv = [jnp.concatenate([vprev_ref[b * N_KV_HEADS + g], vdup[b][g]], axis=0) for b, g in units]
    for b, g in units:
        kprev_ref[b * N_KV_HEADS + g] = kdup[b][g]
        vprev_ref[b * N_KV_HEADS + g] = vdup[b][g]

    def stacked_queries(b, g):
        pieces = []
        for j in tiles[g]:
            qt = q_ref[rows[b], j * LANES:(j + 1) * LANES]
            zero = jnp.zeros_like(qt)
            pieces += [jnp.where(low_half, qt, zero), jnp.where(low_half, zero, qt)]
        return jnp.concatenate(pieces, axis=0)

    def attend(u, b, g):
        s = _dot_nt(stacked_queries(b, g), kk[u])
        s = jnp.where(live, jnp.where(own, s[:, BLOCK:], s[:, :BLOCK]), MASK_VALUE)
        m = jnp.maximum(jnp.max(s, axis=-1, keepdims=True), sink_cols[g])
        p = jnp.exp(s - m)
        denom = jnp.sum(p, axis=-1, keepdims=True) + jnp.exp(sink_cols[g] - m)
        p2 = jnp.concatenate([jnp.where(own, 0.0, p), jnp.where(own, p, 0.0)], axis=1).astype(BF16)
        o = _dot(p2, vv[u]) / denom
        for n, j in enumerate(tiles[g]):
            o_lo = o[2 * n * BLOCK:(2 * n + 1) * BLOCK]
            o_hi = o[(2 * n + 1) * BLOCK:(2 * n + 2) * BLOCK]
            m_ref[rows[b], j * LANES:(j + 1) * LANES] = jnp.where(low_half, o_lo, o_hi).astype(BF16)

    return [functools.partial(attend, u, b, g) for u, (b, g) in enumerate(units)]


def _gla_decays(pf_ref, tri_ref, lev_ref):
    ld_ref = pf_ref.at[:, PF_LD:PF_OG]
    ld_all = jnp.concatenate([ld_ref[b * BLOCK:(b + 1) * BLOCK, :] for b in range(BATCH)], axis=1)
    hi, lo = _split(ld_all)
    tri = tri_ref[...]
    g_cum_all = _dot(tri, hi) + _dot(tri, lo)
    lev = lev_ref[...]
    low_sums = _dot(lev, hi) + _dot(lev, lo)
    level_sums = [low_sums[l * BLOCK:(l + 1) * BLOCK] for l in range(LOW_LEVELS)]
    for l in range(LOW_LEVELS, N_LEVELS):
        half = 1 << l
        pieces = []
        for p in range(BLOCK // (2 * half)):
            mid = g_cum_all[p * 2 * half + half - 1:p * 2 * half + half, :]
            pieces += [mid - g_cum_all[p * 2 * half:p * 2 * half + half],
                       g_cum_all[p * 2 * half + half:(p + 1) * 2 * half] - mid]
        level_sums.append(jnp.concatenate(pieces, axis=0))
    decay_all = [jnp.exp(s) for s in level_sums]
    g_last_all = jnp.broadcast_to(g_cum_all[BLOCK - 1:BLOCK, :], g_cum_all.shape)
    decay_col_all = jnp.exp(jnp.concatenate(
        [g_last_all[:, c * LANES:(c + 1) * LANES].T for c in range(BATCH * GLA_K_W // LANES)], axis=0))
    return g_cum_all, decay_all, decay_col_all


def _gla_tile(blk, decays, pf_ref, pb_ref, vec_ref, plev_ref, m_ref, state_ref, sbd_ref):
    g_cum_all, decay_all, decay_col_all = decays
    gq_ref, gk_ref, _, gv_ref, og_ref = _proj_views(pf_ref, pb_ref)[3:]
    rpos = blk * BLOCK + lax.broadcasted_iota(jnp.int32, (BLOCK, 1), 0) - PAD_LEN
    valid = (rpos >= 0).astype(F32)
    pair_level = plev_ref[...]
    gn = vec_ref[V_GN:V_GN + 1, :GLA_DV]
    streams = range(BATCH)
    rows = [slice(b * BLOCK, (b + 1) * BLOCK) for b in streams]
    cols = [slice(b * GLA_K_W, (b + 1) * GLA_K_W) for b in streams]
    gq = [gq_ref[rows[b], :] for b in streams]
    gk = [gk_ref[rows[b], :] * valid for b in streams]
    gv = [gv_ref[rows[b], :] for b in streams]
    g_cum = [g_cum_all[:, cols[b]] for b in streams]
    a = _gla_intra(gq, gk, [[d[:, cols[b]] for d in decay_all] for b in streams], range(N_LEVELS), pair_level)

    pairs = range(GLA_HEADS // 2)
    q_dec = [(gq[b] * jnp.exp(g_cum[b])).astype(BF16) for b in streams]
    o = [jnp.concatenate([_dot(q_dec[b][:, i * LANES:(i + 1) * LANES], sbd_ref[2 * b + i]) for i in pairs], axis=1)
         for b in streams]
    o = [o[b] + jnp.concatenate(
        [_dot(a[b][h].astype(BF16), gv[b][:, h * GLA_DV:(h + 1) * GLA_DV]) for h in range(GLA_HEADS)], axis=1)
        for b in streams]
    for b in streams:
        for h, gh in enumerate(_gla_merge(o[b], og_ref[rows[b], :], gn)):
            m_ref[rows[b], ATT_Q_W + h * GLA_DV:ATT_Q_W + (h + 1) * GLA_DV] = gh.astype(BF16)

    k_dec = [(gk[b] * jnp.exp(g_cum[b][BLOCK - 1:BLOCK, :] - g_cum[b])).astype(BF16) for b in streams]
    kv = [[_dot_tn(k_dec[b][:, i * LANES:(i + 1) * LANES], gv[b][:, 2 * i * GLA_DV:(2 * i + 2) * GLA_DV])
           for i in pairs] for b in streams]
    for b in streams:
        new_state = decay_col_all[cols[b], :] * state_ref[b] + jnp.concatenate(
            [kv[b][h // 2][(h % 2) * GLA_DK:(h % 2 + 1) * GLA_DK, (h % 2) * GLA_DV:(h % 2 + 1) * GLA_DV]
             for h in range(GLA_HEADS)], axis=0)
        state_ref[b] = new_state
        for h in range(GLA_HEADS):
            sbd_ref[2 * b + h // 2, (h % 2) * GLA_DK:(h % 2 + 1) * GLA_DK,
                    (h % 2) * GLA_DV:(h % 2 + 1) * GLA_DV] = new_state[h * GLA_DK:(h + 1) * GLA_DK].astype(BF16)


def _front_kernel(*refs, layer, n_src, chained):
    sinks_ref, refs = refs[0], refs[1:]
    if chained:
        refs = refs[3:]
    x_refs, refs = refs[:n_src], refs[n_src:]
    vec_ref, w_ref, wg2_ref, tri_ref, lev_ref, plev_ref = refs[:6]
    ffn_f32_refs, refs = refs[6:10], refs[10:]
    if n_src > 1:
        xo_ref, refs = refs[0], refs[1:]
    (m_ref, pfs_ref, pbs_ref, s_out_ref, k_out_ref, v_out_ref) = refs[:6]
    ffn_bf16_refs, refs = refs[6:10], refs[10:]
    wbf_ref, wg2s_ref, z_ref, pf_ref, pb_ref, kprev_ref, vprev_ref, state_ref, sbd_ref = refs
    i = pl.program_id(0)
    t = i - N_PREP

    @pl.when(i < N_PREP)
    def _():
        _convert_proj_weights(i, w_ref, wg2_ref, wbf_ref, wg2s_ref)

    @pl.when((t >= 1) & (t <= N_CONV))
    def _():
        for src, dst in zip(ffn_f32_refs, ffn_bf16_refs):
            dst[...] = src[0].astype(BF16)

    @pl.when(i == 0)
    def _():
        kprev_ref[...] = jnp.zeros_like(kprev_ref)
        vprev_ref[...] = jnp.zeros_like(vprev_ref)
        state_ref[...] = jnp.zeros_like(state_ref)
        sbd_ref[...] = jnp.zeros_like(sbd_ref)

    def load_x():
        if n_src == 1:
            return x_refs[0][...]
        x = jnp.where(t == 0, x_refs[2][...],
                      jnp.where(t < N_BLOCKS, x_refs[0][...].reshape(ROW_TILE, D_MODEL), x_refs[1][...]))
        xo_ref[...] = x
        return x

    @pl.when((t >= 0) & (t < N_BLOCKS))
    def _():
        _project_tile(load_x(), vec_ref, wbf_ref, wg2s_ref, z_ref, pf_ref, pb_ref)
        for attend in _attention_units(t, layer, sinks_ref, pf_ref, pb_ref, m_ref, kprev_ref, vprev_ref):
            attend()
        _gla_tile(t, _gla_decays(pf_ref, tri_ref, lev_ref), pf_ref, pb_ref, vec_ref, plev_ref, m_ref,
                  state_ref, sbd_ref)

    @pl.when(t == N_BLOCKS - 1)
    def _():
        own = 0 if chained else layer
        if not chained:
            for ref in (s_out_ref, k_out_ref, v_out_ref):
                ref[...] = jnp.zeros_like(ref)
        s_out_ref[own] = state_ref[...]
        for b in range(BATCH):
            k_out_ref[own, b] = pf_ref[b * BLOCK:(b + 1) * BLOCK, PF_K:PF_V].T
            v_out_ref[own, b] = pf_ref[b * BLOCK:(b + 1) * BLOCK, PF_V:PF_GQ].T

    @pl.when(t >= N_BLOCKS)
    def _():
        _project_tile(load_x(), vec_ref, wbf_ref, wg2s_ref, z_ref, pfs_ref, pbs_ref)
        m_ref[...] = jnp.zeros_like(m_ref)


def _front(layer, sinks, x_srcs, vecs, w_in, w_g2, ffn_f32, consts, prev):
    n_src = len(x_srcs)
    step = lambda i: i - N_PREP
    conv_of = lambda i: jnp.clip(step(i) - 1, 0, N_CONV - 1)
    tile_of = lambda i: jnp.where(step(i) <= 0, LEAD_TILE, jnp.minimum(step(i) - 1, LEAD_TILE - 1))
    sample_tile_of = lambda i: jnp.clip(step(i) - N_BLOCKS, 0, SAMPLE_TILES - 1)
    full = lambda a: pl.BlockSpec(a.shape, lambda i, *_: (0,) * a.ndim)
    if prev is None:
        per_batch = lambda r, c: pl.BlockSpec((DEPTH, BATCH, r, c), lambda i, *_: (0, 0, 0, 0))
    else:
        per_batch = lambda r, c: pl.BlockSpec((1, BATCH, r, c), lambda i, *_: (layer, 0, 0, 0))
    if n_src == 1:
        x_specs = [pl.BlockSpec((ROW_TILE, D_MODEL), lambda i, *_: (tile_of(i), 0))]
    else:
        x_specs = [
            pl.BlockSpec((BATCH, BLOCK, D_MODEL), lambda i, *_: (0, jnp.clip(step(i) - 1, 0, MAIN_TILES - 1), 0)),
            pl.BlockSpec((ROW_TILE, D_MODEL), lambda i, *_: (sample_tile_of(i), 0), pipeline_mode=pl.Buffered(1)),
            pl.BlockSpec((ROW_TILE, D_MODEL), lambda i, *_: (0, 0), pipeline_mode=pl.Buffered(1)),
        ]
    const_args = (consts["tri"], consts["lev_low"], consts["pair_level"])
    chained = prev is not None
    prev_args = tuple(prev) if chained else ()
    in_specs = ([pl.BlockSpec(memory_space=pl.ANY)] * len(prev_args) + x_specs + [
        pl.BlockSpec((V_ROWS, D_MODEL), lambda i, *_: (layer, 0)),
        pl.BlockSpec((1, IN_W, W_ROWS), lambda i, *_: (layer, 0, jnp.minimum(i, N_PREP - 1))),
        pl.BlockSpec((1, GATE_RANK, GLA_K_W), lambda i, *_: (layer, 0, 0)),
    ] + [full(a) for a in const_args] + [
        pl.BlockSpec((1, a.shape[1] // N_CONV, a.shape[2]), lambda i, *_: (layer, conv_of(i), 0)) for a in ffn_f32])
    out_specs = [pl.BlockSpec((ROW_TILE, D_MIX), lambda i, *_: (tile_of(i), 0)),
                 pl.BlockSpec((ROW_TILE, PF_W), lambda i, *_: (sample_tile_of(i), 0)),
                 pl.BlockSpec((ROW_TILE, PB_W), lambda i, *_: (sample_tile_of(i), 0)),
                 per_batch(GLA_K_W, GLA_DV), per_batch(BLOCK, ATT_KV_W), per_batch(BLOCK, ATT_KV_W)]
    out_specs += [pl.BlockSpec((a.shape[1] // N_CONV, a.shape[2]), lambda i, *_: (conv_of(i), 0)) for a in ffn_f32]
    out_shape = [jax.ShapeDtypeStruct((TOTAL_ROWS, D_MIX), BF16),
                 jax.ShapeDtypeStruct((SAMPLE_ROWS, PF_W), F32),
                 jax.ShapeDtypeStruct((SAMPLE_ROWS, PB_W), BF16),
                 jax.ShapeDtypeStruct((DEPTH, BATCH, GLA_K_W, GLA_DV), F32),
                 jax.ShapeDtypeStruct((DEPTH, BATCH, BLOCK, ATT_KV_W), F32),
                 jax.ShapeDtypeStruct((DEPTH, BATCH, BLOCK, ATT_KV_W), F32)]
    out_shape += [jax.ShapeDtypeStruct(a.shape[1:], BF16) for a in ffn_f32]
    n_lead_out = 0
    if n_src > 1:
        out_specs = [pl.BlockSpec((ROW_TILE, D_MODEL), lambda i, *_: (tile_of(i), 0))] + out_specs
        out_shape = [jax.ShapeDtypeStruct((TOTAL_ROWS, D_MODEL), F32)] + out_shape
        n_lead_out = 1
    grid_spec = pltpu.PrefetchScalarGridSpec(
        num_scalar_prefetch=1,
        grid=(N_PREP + N_BLOCKS + SAMPLE_TILES,),
        in_specs=in_specs,
        out_specs=out_specs,
        scratch_shapes=[pltpu.VMEM((D_MODEL, PROJ_W), BF16), pltpu.VMEM((LANES, GLA_K_W), BF16),
                        pltpu.VMEM((ROW_TILE, PROJ_W), F32),
                        pltpu.VMEM((ROW_TILE, PF_W), F32), pltpu.VMEM((ROW_TILE, PB_W), BF16),
                        pltpu.VMEM((BATCH * N_KV_HEADS, BLOCK, ATT_KV_W), BF16),
                        pltpu.VMEM((BATCH * N_KV_HEADS, BLOCK, ATT_KV_W), BF16),
                        pltpu.VMEM((BATCH, GLA_K_W, GLA_DV), F32),
                        pltpu.VMEM((BATCH * GLA_HEADS // 2, 2 * GLA_DK, 2 * GLA_DV), BF16)],
    )
    res = pl.pallas_call(
        functools.partial(_front_kernel, layer=layer, n_src=n_src, chained=chained),
        grid_spec=grid_spec,
        out_shape=out_shape,
        input_output_aliases={1 + n: n_lead_out + 3 + n for n in range(len(prev_args))},
        compiler_params=pltpu.CompilerParams(dimension_semantics=("arbitrary",),
                                             vmem_limit_bytes=VMEM_LIMIT),
        name="front",
    )(sinks, *prev_args, *x_srcs, vecs, w_in, w_g2, *const_args, *ffn_f32)
    x_all = res[0] if n_src > 1 else x_srcs[0]
    merged, pfs, pbs, ps, pk, pv = res[n_lead_out:n_lead_out + 6]
    return x_all, merged, (pfs, pbs), (ps, pk, pv), tuple(res[n_lead_out + 6:])


def _mix_sample_kernel(*refs, layer, n_alias):
    sinks_ref, refs = refs[0], refs[n_alias + 1:]
    (pf_ref, pb_ref, ck_ref, cv_ref, st_ref,
     vec_ref, tri_ref, sones_ref, lev_ref, ones_ref, plev_ref,
     m_ref, ck_out_ref, cv_out_ref, st_out_ref) = refs
    q_ref, k_ref, v_ref, gq_ref, gk_ref, ld_ref, gv_ref, og_ref = _proj_views(pf_ref, pb_ref)
    gq = gq_ref[...]
    gk = gk_ref[...]
    gv = gv_ref[...]
    gvf = gv.astype(F32)
    hi, lo = _split(ld_ref[...])
    tri = tri_ref[...]
    g_cum = _dot(tri, hi) + _dot(tri, lo)
    sones = sones_ref[...]
    g_tot = _dot(sones, hi) + _dot(sones, lo)
    lev = lev_ref[...]
    decay_levels = jnp.exp(_dot(lev, hi) + _dot(lev, lo))
    a = _gla_intra([gq], [gk], [[decay_levels[l * BLOCK:(l + 1) * BLOCK] for l in range(LOW_LEVELS)]],
                   range(LOW_LEVELS), plev_ref[...])[0]
    o_intra = jnp.concatenate(
        [_dot(a[h].astype(BF16), gv[:, h * GLA_DV:(h + 1) * GLA_DV]) for h in range(GLA_HEADS)], axis=1)
    q_dec = gq * jnp.exp(g_cum)
    k_dec = gk * jnp.exp(g_tot - g_cum)
    hm8 = _head_masks(DEC_SEQ)
    ones8 = ones_ref[...]
    hi_f, lo_f = hi.astype(F32), lo.astype(F32)

    qf = q_ref[...].astype(F32)
    kf = k_ref[...]
    vf = v_ref[...]
    low8 = lax.broadcasted_iota(jnp.int32, (DEC_SEQ, LANES), 1) < HEAD_DIM
    nkeys = WINDOW + DEC_SEQ
    srow = lax.broadcasted_iota(jnp.int32, (N_HEADS * DEC_SEQ, nkeys), 0)
    scol = lax.broadcasted_iota(jnp.int32, (N_HEADS * DEC_SEQ, nkeys), 1)
    t_of_row = srow & (DEC_SEQ - 1)
    amask = ((scol < WINDOW) & (scol > t_of_row)) | ((scol >= WINDOW) & (scol - WINDOW <= t_of_row))
    rid = lax.broadcasted_iota(jnp.int32, (N_HEADS * DEC_SEQ, 1), 0) >> LOW_LEVELS
    sink_col = jnp.zeros((N_HEADS * DEC_SEQ, 1), F32)
    for i in range(N_HEADS):
        sink_col = jnp.where(rid == i, sinks_ref[layer, i], sink_col)

    seqs = range(SEQ_GROUP)
    rows = [slice(b * DEC_SEQ, (b + 1) * DEC_SEQ) for b in seqs]

    def stacked_queries(b):
        pieces = []
        for j in range(N_HEADS // 2):
            g = (2 * j) // (N_HEADS // N_KV_HEADS)
            qt = qf[rows[b], j * LANES:(j + 1) * LANES]
            swapped = pltpu.roll(qt, HEAD_DIM, axis=1)
            own_lanes = low8 if g == 0 else ~low8
            even, odd = (qt, swapped) if g == 0 else (swapped, qt)
            pieces += [jnp.where(own_lanes, even, 0.0), jnp.where(own_lanes, odd, 0.0)]
        return jnp.concatenate(pieces, axis=0).astype(BF16)

    lane = lax.broadcasted_iota(jnp.int32, (WINDOW, WINDOW), 1)
    old_lanes = lane < WINDOW - DEC_SEQ

    def shifted_buffer(buf, new_rows):
        wide = jnp.concatenate([jnp.zeros((WINDOW - DEC_SEQ, LANES), F32), new_rows], axis=0)
        return jnp.where(old_lanes, pltpu.roll(buf, WINDOW - DEC_SEQ, axis=1), wide.T)

    qp = [stacked_queries(b) for b in seqs]
    ck = [ck_ref[0, b] for b in seqs]
    cv = [cv_ref[0, b] for b in seqs]
    for b in seqs:
        ck_out_ref[0, b] = shifted_buffer(ck[b], kf[rows[b]])
        cv_out_ref[0, b] = shifted_buffer(cv[b], vf[rows[b]])
    s = [jnp.concatenate([_dot(qp[b], ck[b].astype(BF16)), _dot_nt(qp[b], kf[rows[b]].astype(BF16))], axis=1)
         for b in seqs]
    s = [jnp.where(amask, s[b], MASK_VALUE) for b in seqs]
    m = [jnp.maximum(jnp.max(s[b], axis=-1, keepdims=True), sink_col) for b in seqs]
    p = [jnp.exp(s[b] - m[b]) for b in seqs]
    denom = [jnp.sum(p[b], axis=-1, keepdims=True) + jnp.exp(sink_col - m[b]) for b in seqs]
    pb = [p[b].astype(BF16) for b in seqs]
    ob = [(_dot_nt(pb[b][:, :WINDOW], cv[b].astype(BF16))
           + _dot(pb[b][:, WINDOW:], vf[rows[b]].astype(BF16))) / denom[b] for b in seqs]

    def head_tiles(o):
        tiles = []
        for j in range(N_HEADS // 2):
            g = (2 * j) // (N_HEADS // N_KV_HEADS)
            even = o[(2 * j) * DEC_SEQ:(2 * j + 1) * DEC_SEQ]
            odd = o[(2 * j + 1) * DEC_SEQ:(2 * j + 2) * DEC_SEQ]
            if g == 0:
                tiles.append(jnp.where(low8, even, pltpu.roll(odd, HEAD_DIM, axis=1)))
            else:
                tiles.append(jnp.where(low8, pltpu.roll(even, HEAD_DIM, axis=1), odd))
        return jnp.concatenate(tiles, axis=1)

    att_rows = [head_tiles(ob[b]) for b in seqs]

    head_stack = lambda x: jnp.concatenate([jnp.where(mk, x, 0.0) for mk in hm8], axis=0).astype(BF16)
    state = [st_ref[0, b] for b in seqs]
    oi = [_dot(head_stack(q_dec[rows[b]]), state[b].astype(BF16)) for b in seqs]
    inter_rows = [jnp.concatenate([oi[b][h * DEC_SEQ:(h + 1) * DEC_SEQ] for h in range(GLA_HEADS)], axis=1)
                  for b in seqs]
    vstack = [jnp.concatenate([gvf[rows[b]][:, h * GLA_DV:(h + 1) * GLA_DV] for h in range(GLA_HEADS)],
                              axis=0).astype(BF16) for b in seqs]
    kv = [_dot_tn(head_stack(k_dec[rows[b]]), vstack[b]) for b in seqs]
    decay_col = [jnp.exp(_dot_tn(hi_f[rows[b]], ones8) + _dot_tn(lo_f[rows[b]], ones8)) for b in seqs]
    for b in seqs:
        st_out_ref[0, b] = decay_col[b] * state[b] + kv[b]

    m_ref[:, :ATT_Q_W] = jnp.concatenate(att_rows, axis=0).astype(BF16)
    o = o_intra + jnp.concatenate(inter_rows, axis=0)
    gn = vec_ref[V_GN:V_GN + 1, :GLA_DV]
    for h, gh in enumerate(_gla_merge(o, og_ref[...], gn)):
        m_ref[:, ATT_Q_W + h * GLA_DV:ATT_Q_W + (h + 1) * GLA_DV] = gh.astype(BF16)


def _mix_sample(layer, sinks, merged, proj, cache_k, cache_v, state, vecs, consts, prev):
    tok = lambda w_: pl.BlockSpec((BLOCK, w_), lambda i, *_: (i, 0))
    merged_blk = pl.BlockSpec((BLOCK, D_MIX), lambda i, *_: (MAIN_ROWS // BLOCK + i, 0))
    full = lambda a: pl.BlockSpec(a.shape, lambda i, *_: (0,) * a.ndim)
    seq = lambda a: pl.BlockSpec((1, SEQ_GROUP) + a.shape[2:], lambda i, *_: (layer, i, 0, 0))
    ones8 = jnp.ones((DEC_SEQ, LANES), F32)
    const_args = (consts["seq_tri"], consts["seq_ones"], consts["lev_low"], ones8, consts["pair_level"])
    seq_args = (cache_k, cache_v, state)
    alias_args = (merged,) + (tuple(prev) if prev is not None else ())
    grid_spec = pltpu.PrefetchScalarGridSpec(
        num_scalar_prefetch=1,
        grid=(DEC_BATCH // SEQ_GROUP,),
        in_specs=([pl.BlockSpec(memory_space=pl.ANY)] * len(alias_args)
                  + [tok(a.shape[1]) for a in proj] + [seq(a) for a in seq_args]
                  + [pl.BlockSpec((V_ROWS, D_MODEL), lambda i, *_: (layer, 0))]
                  + [full(a) for a in const_args]),
        out_specs=[merged_blk] + [seq(a) for a in seq_args],
    )
    return pl.pallas_call(
        functools.partial(_mix_sample_kernel, layer=layer, n_alias=len(alias_args)),
        grid_spec=grid_spec,
        out_shape=[jax.ShapeDtypeStruct(merged.shape, merged.dtype)]
        + [jax.ShapeDtypeStruct(a.shape, a.dtype) for a in seq_args],
        input_output_aliases={1 + n: n for n in range(len(alias_args))},
        compiler_params=pltpu.CompilerParams(dimension_semantics=("arbitrary",),
                                             vmem_limit_bytes=VMEM_LIMIT),
        name="mix_sample",
    )(sinks, *alias_args, *proj, *seq_args, vecs, *const_args)


def _out_ffn_kernel(*refs, last):
    x_ref, m_ref, vec_ref, wo_ref, wg_ref, wu_ref, wd_ref = refs[:7]
    out_refs = refs[7:9] if last else refs[7:8]
    act_ref = refs[-1]
    x1 = x_ref[...] + _dot(m_ref[...], wo_ref[...])
    ms = jnp.mean(x1 * x1, axis=-1, keepdims=True)
    h = (x1 * lax.rsqrt(ms + RMS_EPS) * vec_ref[V_NORM2:V_NORM2 + 1, :]).astype(BF16)
    for c in range(D_FFN // FFN_CHUNK):
        cols = slice(c * FFN_CHUNK, (c + 1) * FFN_CHUNK)
        gate = _dot(h, wg_ref[:, cols])
        up = _dot(h, wu_ref[:, cols])
        act_ref[:, cols] = (gate * _sigmoid(gate) * up).astype(BF16)
    y = x1 + _dot(act_ref[...], wd_ref[...])
    if last:
        out_refs[0][...] = y.reshape(BATCH, BLOCK, D_MODEL)

        @pl.when(pl.program_id(0) < SAMPLE_TILES)
        def _():
            out_refs[1][...] = out_refs[0][...].reshape(ROW_TILE, D_MODEL)
    else:
        out_refs[0][...] = y


def _out_ffn(layer, x_all, merged, vecs, ffn_weights, last):
    resident = lambda a: pl.BlockSpec(a.shape, lambda i: (0, 0), pipeline_mode=pl.Buffered(1))
    if last:
        n_tiles = MAIN_TILES + SAMPLE_TILES
        tile_of = lambda i: jnp.where(i < SAMPLE_TILES, MAIN_TILES + i, i - SAMPLE_TILES)
        out_specs = [pl.BlockSpec((BATCH, BLOCK, D_MODEL), lambda i: (0, jnp.maximum(i - SAMPLE_TILES, 0), 0)),
                     pl.BlockSpec((ROW_TILE, D_MODEL), lambda i: (jnp.minimum(i, SAMPLE_TILES - 1), 0))]
        out_shape = [jax.ShapeDtypeStruct((BATCH, SEQ, D_MODEL), F32),
                     jax.ShapeDtypeStruct((SAMPLE_ROWS, D_MODEL), F32)]
    else:
        n_tiles = N_TILES
        tile_of = lambda i: i
        out_specs = [pl.BlockSpec((ROW_TILE, D_MODEL), lambda i: (i, 0))]
        out_shape = [jax.ShapeDtypeStruct((TOTAL_ROWS, D_MODEL), F32)]
    tile = lambda w_: pl.BlockSpec((ROW_TILE, w_), lambda i: (tile_of(i), 0))
    return pl.pallas_call(
        functools.partial(_out_ffn_kernel, last=last),
        grid=(n_tiles,),
        in_specs=[tile(D_MODEL), tile(D_MIX), pl.BlockSpec((V_ROWS, D_MODEL), lambda i: (layer, 0))]
        + [resident(w) for w in ffn_weights],
        out_specs=out_specs,
        out_shape=out_shape,
        scratch_shapes=[pltpu.VMEM((ROW_TILE, D_FFN), BF16)],
        compiler_params=pltpu.CompilerParams(dimension_semantics=("arbitrary",),
                                             vmem_limit_bytes=VMEM_LIMIT),
        name="out_ffn",
    )(x_all, merged, vecs, *ffn_weights)


def _vector_slab(norm1, norm2, q_norm, k_norm, b_g, gla_norm):
    pad = lambda a: jnp.pad(a.astype(F32), ((0, 0), (0, D_MODEL - a.shape[1])))
    rows = [norm1.astype(F32), norm2.astype(F32), pad(jnp.tile(q_norm, (1, N_HEADS)) * ATT_SCALE),
            pad(jnp.tile(k_norm, (1, N_KV_HEADS))), pad(b_g), pad(gla_norm)]
    rows += [jnp.zeros((DEPTH, D_MODEL), F32)] * (V_ROWS - len(rows))
    return jnp.stack(rows, axis=1).reshape(DEPTH * V_ROWS, D_MODEL)


def kernel(x_prompt, x_sample, cache_k, cache_v, state_gla, meta, norm1, w_in, q_norm, k_norm, sinks,
           w_g2, b_g, gla_norm, w_o, norm2, w_gate, w_up, w_down):
    consts = _constants()
    dt = x_prompt.dtype
    vecs = _vector_slab(norm1, norm2, q_norm, k_norm, b_g, gla_norm)
    sinks = sinks.astype(F32)
    lead = jnp.tile(jnp.concatenate([jnp.zeros((PAD_LEN, D_MODEL), dt), meta.astype(dt)], axis=0), (BATCH, 1))
    x_srcs = (x_prompt, x_sample.reshape(SAMPLE_ROWS, D_MODEL), lead)
    to_feature_major = lambda c: c.transpose(0, 1, 3, 4, 2).reshape(DEPTH, DEC_BATCH, ATT_KV_W, WINDOW)
    ck_in, cv_in = to_feature_major(cache_k), to_feature_major(cache_v)
    st_in = state_gla.reshape(DEPTH, DEC_BATCH, GLA_K_W, GLA_DV)
    w_in_t = jnp.swapaxes(w_in, 1, 2)

    prompt_outs, sample_outs = None, None
    for l in range(DEPTH):
        x_all, merged, proj_sample, prompt_outs, ffn_bf16 = _front(
            l, sinks, x_srcs, vecs, w_in_t, w_g2, (w_o, w_gate, w_up, w_down), consts, prompt_outs)
        merged, *sample_outs = _mix_sample(l, sinks, merged, proj_sample, ck_in, cv_in, st_in, vecs, consts,
                                           sample_outs)
        x_srcs = tuple(_out_ffn(l, x_all, merged, vecs, ffn_bf16, last=l == DEPTH - 1))

    y_main, y_sample = x_srcs
    ps, pk, pv = prompt_outs
    sk, sv, ss = sample_outs
    kv5 = lambda a, n: a.reshape(DEPTH, n, N_KV_HEADS, HEAD_DIM, WINDOW).transpose(0, 1, 4, 2, 3)
    st5 = lambda a, n: a.reshape(DEPTH, n, GLA_HEADS, GLA_DK, GLA_DV)
    return (y_main, y_sample.reshape(DEC_BATCH, DEC_SEQ, D_MODEL),
            kv5(pk, BATCH), kv5(pv, BATCH), st5(ps, BATCH), kv5(sk, DEC_BATCH), kv5(sv, DEC_BATCH),
            st5(ss, DEC_BATCH))
```

```python
import functools

import jax
import jax.numpy as jnp
import numpy as np
from jax import lax
from jax.experimental import pallas as pl
from jax.experimental.pallas import tpu as pltpu

F32 = jnp.float32
BF16 = jnp.bfloat16

D_MODEL = 1024
BATCH = 4
SEQ = 4096
DEPTH = 2
DEC_BATCH = 128
DEC_SEQ = 8
N_META = 16
WINDOW = 128
BLOCK = 128
PAD_LEN = BLOCK - N_META
N_HEADS = 8
N_KV_HEADS = 2
HEAD_DIM = 64
ATT_SCALE = HEAD_DIM ** -0.5
ATT_Q_W = N_HEADS * HEAD_DIM
ATT_KV_W = N_KV_HEADS * HEAD_DIM
GLA_HEADS = 4
GLA_DK = 64
GLA_DV = 128
GLA_K_W = GLA_HEADS * GLA_DK
GLA_V_W = GLA_HEADS * GLA_DV
GATE_RANK = 16
GATE_NORMALIZER = 16.0
D_MIX = ATT_Q_W + GLA_V_W
D_FFN = 2816
IN_W = 2320
RMS_EPS = 1e-6
MASK_VALUE = -1e30

LANES = 128
N_BLOCKS = 1 + SEQ // BLOCK
MAIN_ROWS = BATCH * SEQ
SAMPLE_ROWS = DEC_BATCH * DEC_SEQ
LEAD_ROWS = BATCH * BLOCK
TOTAL_ROWS = MAIN_ROWS + SAMPLE_ROWS + LEAD_ROWS
ROW_TILE = 512
MAIN_TILES = MAIN_ROWS // ROW_TILE
SAMPLE_TILES = SAMPLE_ROWS // ROW_TILE
N_TILES = TOTAL_ROWS // ROW_TILE
LEAD_TILE = N_TILES - 1
SEQ_GROUP = BLOCK // DEC_SEQ
N_LEVELS = 7
LOW_LEVELS = 3
LOG_DK = 6
N_PREP = 8
W_ROWS = D_MODEL // N_PREP
N_CONV = 16

SRC_LOW, SRC_OG = 1792, 1808
OFF_Q, OFF_K, OFF_V = 0, 512, 640
OFF_GQ, OFF_GK, OFF_GV, OFF_OG, OFF_LOW = 768, 1024, 1280, 1792, 2304
PROJ_W = OFF_LOW + LANES
PROJ_CHUNK = 512
FFN_CHUNK = 256
VMEM_LIMIT = 58 * 1024 * 1024

V_NORM1, V_NORM2, V_QG, V_KG, V_BG, V_GN, V_ROWS = 0, 1, 2, 3, 4, 5, 8

PF_K, PF_V, PF_GQ, PF_GK, PF_LD, PF_OG, PF_W = 0, 128, 256, 512, 768, 1024, 1536
PB_Q, PB_GV, PB_W = 0, 512, 1024


def _proj_views(pf_ref, pb_ref):
    f = lambda a, b: pf_ref.at[:, a:b]
    return (pb_ref.at[:, PB_Q:PB_GV], f(PF_K, PF_V), f(PF_V, PF_GQ), f(PF_GQ, PF_GK), f(PF_GK, PF_LD),
            f(PF_LD, PF_OG), pb_ref.at[:, PB_GV:PB_W], f(PF_OG, PF_W))


def _dot(a, b):
    return jnp.dot(a, b, preferred_element_type=F32)


def _dot_nt(a, b):
    return lax.dot_general(a, b, (((1,), (1,)), ((), ())), preferred_element_type=F32)


def _dot_tn(a, b):
    return lax.dot_general(a, b, (((0,), (0,)), ((), ())), preferred_element_type=F32)


def _split(x):
    hi = x.astype(BF16)
    lo = (x - hi.astype(F32)).astype(BF16)
    return hi, lo


def _sigmoid(x):
    return 1.0 / (1.0 + jnp.exp(-x))


def _level_matrix(levels, n=BLOCK):
    out = np.zeros((len(levels) * n, n), np.float32)
    for i, l in enumerate(levels):
        size = 2 << l
        for t in range(n):
            mid = (t // size) * size + size // 2 - 1
            if (t >> l) & 1:
                out[i * n + t, mid + 1:t + 1] = 1.0
            else:
                out[i * n + t, t + 1:mid + 1] = 1.0
    return out


def _constants():
    r = np.arange(BLOCK)
    tri = (r[None, :] <= r[:, None]).astype(np.float32)
    same_seq = (r[None, :] // DEC_SEQ) == (r[:, None] // DEC_SEQ)
    diff = np.maximum(r[:, None] ^ r[None, :], 1)
    pair_level = np.where(r[None, :] < r[:, None], np.floor(np.log2(diff)).astype(np.int32),
                          np.where(r[None, :] == r[:, None], N_LEVELS, N_LEVELS + 1)).astype(np.int32)
    return dict(
        tri=jnp.asarray(tri, BF16),
        seq_tri=jnp.asarray(tri * same_seq, BF16),
        seq_ones=jnp.asarray(same_seq.astype(np.float32), BF16),
        lev_low=jnp.asarray(_level_matrix(range(LOW_LEVELS)), BF16),
        pair_level=jnp.asarray(pair_level),
    )


def _head_masks(rows):
    lane = lax.broadcasted_iota(jnp.int32, (rows, GLA_K_W), 1)
    return [(lane >> LOG_DK) == h for h in range(GLA_HEADS)]


def _gla_intra(gq, gk, decay_levels, levels, pair_level):
    rows = gq[0].shape[0]
    streams = range(len(gq))
    low_half = lax.broadcasted_iota(jnp.int32, (rows, LANES), 1) < GLA_DK

    def pair_products(qh, kh):
        qh, kh = qh.astype(BF16), kh.astype(BF16)
        out = []
        for pair in range(GLA_HEADS // 2):
            kl = kh[:, pair * LANES:(pair + 1) * LANES]
            zero = jnp.zeros_like(kl)
            stacked = jnp.concatenate([jnp.where(low_half, kl, zero), jnp.where(low_half, zero, kl)], axis=0)
            p = _dot_nt(qh[:, pair * LANES:(pair + 1) * LANES], stacked)
            out += [p[:, :rows], p[:, rows:]]
        return out

    row = lax.broadcasted_iota(jnp.int32, (rows, 1), 0)

    def upper_q_lower_k(b, l):
        size = 1 << l
        if size % 8:
            return jnp.where(((row >> l) & 1) == 1, gq[b], gk[b])
        return jnp.concatenate([(gq[b] if i & 1 else gk[b])[i * size:(i + 1) * size]
                                for i in range(rows // size)], axis=0)

    def diagonal(b):
        out = []
        for pair in range(GLA_HEADS // 2):
            prod = (gq[b] * gk[b])[:, pair * LANES:(pair + 1) * LANES]
            out += [jnp.sum(jnp.where(low_half, prod, 0.0), axis=-1, keepdims=True),
                    jnp.sum(jnp.where(low_half, 0.0, prod), axis=-1, keepdims=True)]
        return out

    def blocks(v, size, parity):
        return jnp.concatenate([v[i * size:(i + 1) * size] for i in range(parity, rows // size, 2)], axis=0)

    def interleave(lower, upper, size):
        pieces = []
        for i in range(rows // (2 * size)):
            pieces += [lower[i * size:(i + 1) * size], upper[i * size:(i + 1) * size]]
        return jnp.concatenate(pieces, axis=0)

    on_diag = pair_level == N_LEVELS
    a = [[jnp.where(on_diag, d, 0.0) for d in diagonal(b)] for b in streams]
    for i, l in enumerate(levels):
        at_level = pair_level == l
        size = 1 << l
        for b in streams:
            x = upper_q_lower_k(b, l) * decay_levels[b][i]
            if size % 8:
                a[b] = [jnp.where(at_level, p, a_h) for p, a_h in zip(pair_products(x, x), a[b])]
            else:
                mask_up = blocks(pair_level, size, 1) == l
                p_up = pair_products(blocks(x, size, 1), x)
                a[b] = [interleave(blocks(a_h, size, 0), jnp.where(mask_up, p, blocks(a_h, size, 1)), size)
                        for p, a_h in zip(p_up, a[b])]
    return a


def _gla_merge(o, og, gn):
    outs = []
    for h in range(GLA_HEADS):
        oh = o[:, h * GLA_DV:(h + 1) * GLA_DV]
        gh = og[:, h * GLA_DV:(h + 1) * GLA_DV]
        ms = jnp.mean(oh * oh, axis=-1, keepdims=True)
        outs.append(oh * lax.rsqrt(ms + RMS_EPS) * gn * (gh * _sigmoid(gh)))
    return outs


def _dup_halves(x):
    low = lax.broadcasted_iota(jnp.int32, x.shape, 1) < HEAD_DIM
    rolled = pltpu.roll(x, HEAD_DIM, axis=1)
    return jnp.where(low, x, rolled), jnp.where(low, rolled, x)


def _convert_proj_weights(i, w_ref, wg2_ref, wbf_ref, wg2s_ref):
    rows = pl.ds(pl.multiple_of(i * W_ROWS, W_ROWS), W_ROWS)

    def put(dst, src, n=LANES):
        tile = w_ref[0, src:src + n, :]
        if n < LANES:
            tile = jnp.concatenate([tile, jnp.zeros((LANES - n, W_ROWS), F32)], axis=0)
        wbf_ref[rows, dst:dst + LANES] = tile.T.astype(BF16)

    for c in range(SRC_LOW // LANES):
        put(c * LANES, c * LANES)
    for c in range((IN_W - SRC_OG) // LANES):
        put(OFF_OG + c * LANES, SRC_OG + c * LANES)
    put(OFF_LOW, SRC_LOW, GATE_RANK)

    @pl.when(i == 0)
    def _():
        wg2s_ref[...] = jnp.concatenate(
            [wg2_ref[0], jnp.zeros((LANES - GATE_RANK, GLA_K_W), F32)], axis=0).astype(BF16)


def _project_tile(x, vec_ref, wbf_ref, wg2s_ref, z_ref, pf_ref, pb_ref):
    q_ref, k_ref, v_ref, gq_ref, gk_ref, ld_ref, gv_ref, og_ref = _proj_views(pf_ref, pb_ref)
    ms = jnp.mean(x * x, axis=-1, keepdims=True)
    h = (x * lax.rsqrt(ms + RMS_EPS) * vec_ref[V_NORM1:V_NORM1 + 1, :]).astype(BF16)

    low_half = lax.broadcasted_iota(jnp.int32, (ROW_TILE, LANES), 1) < HEAD_DIM

    def head_norm(t):
        t2 = t * t
        ms_lo = jnp.sum(jnp.where(low_half, t2, 0.0), axis=-1, keepdims=True) * (1.0 / HEAD_DIM)
        ms_hi = jnp.sum(jnp.where(low_half, 0.0, t2), axis=-1, keepdims=True) * (1.0 / HEAD_DIM)
        return t * jnp.where(low_half, lax.rsqrt(ms_lo + RMS_EPS), lax.rsqrt(ms_hi + RMS_EPS))

    def q_part():
        for j in range(ATT_Q_W // LANES):
            cols = slice(j * LANES, (j + 1) * LANES)
            q_ref[:, cols] = (head_norm(z_ref[:, cols]) * vec_ref[V_QG:V_QG + 1, cols]).astype(BF16)

    def kv_part():
        k_ref[...] = head_norm(z_ref[:, OFF_K:OFF_V]) * vec_ref[V_KG:V_KG + 1, :ATT_KV_W]
        v_ref[...] = z_ref[:, OFF_V:OFF_GQ]
        gq_ref[...] = z_ref[:, OFF_GQ:OFF_GK] * (GLA_DK ** -0.5)

    def gk_part():
        gk_ref[...] = z_ref[:, OFF_GK:OFF_GV]

    def gv_part():
        gv_ref[...] = z_ref[:, OFF_GV:OFF_OG].astype(BF16)
        og_ref[...] = z_ref[:, OFF_OG:OFF_LOW]

    def gate_part():
        logit =(_dot(z_ref[:, OFF_LOW:PROJ_W].astype(BF16), wg2s_ref[...])
                 + vec_ref[V_BG:V_BG + 1, :GLA_K_W])
        log_sig = jnp.minimum(logit, 0.0) - jnp.log1p(jnp.exp(-jnp.abs(logit)))
        ld_ref[...] = log_sig * (1.0 / GATE_NORMALIZER)

    pieces = [q_part, kv_part, gk_part, gv_part, gate_part]
    for c in [len(pieces) - 1] + list(range(len(pieces) - 1)):
        cols = slice(c * PROJ_CHUNK, min((c + 1) * PROJ_CHUNK, PROJ_W))
        z_ref[:, cols] = _dot(h, wbf_ref[:, cols])
        pieces[c]()


def _attention_units(blk, layer, sinks_ref, pf_ref, pb_ref, m_ref, kprev_ref, vprev_ref):
    q_ref, k_ref, v_ref = _proj_views(pf_ref, pb_ref)[:3]
    per_group = N_HEADS // N_KV_HEADS
    grows = per_group * BLOCK
    row = lax.broadcasted_iota(jnp.int32, (grows, BLOCK), 0) & (BLOCK - 1)
    col = lax.broadcasted_iota(jnp.int32, (grows, BLOCK), 1)
    own = col <= row
    kpos = jnp.where(own, blk * BLOCK, (blk - 1) * BLOCK) + col - PAD_LEN
    live = kpos >= 0
    low_half = lax.broadcasted_iota(jnp.int32, (BLOCK, LANES), 1) < HEAD_DIM
    head_of_row = lax.broadcasted_iota(jnp.int32, (grows, 1), 0) >> N_LEVELS
    sink_cols = []
    for g in range(N_KV_HEADS):
        sink = jnp.zeros((grows, 1), F32)
        for r in range(per_group):
            sink = jnp.where(head_of_row == r, sinks_ref[layer, g * per_group + r], sink)
        sink_cols.append(sink)

    units =[(b, g) for b in range(BATCH) for g in range(N_KV_HEADS)]
    rows = [slice(b * BLOCK, (b + 1) * BLOCK) for b in range(BATCH)]
    tiles = [range(g * per_group // 2, (g + 1) * per_group // 2) for g in range(N_KV_HEADS)]
    kdup = [_dup_halves(k_ref[rows[b], :].astype(BF16)) for b in range(BATCH)]
    vdup = [_dup_halves(v_ref[rows[b], :].astype(BF16)) for b in range(BATCH)]
    kk = [jnp.concatenate([kprev_ref[b * N_KV_HEADS + g], kdup[b][g]], axis=0) for b, g in units]
    vv = [jnp.concatenate([vprev_ref[b * N_KV_HEADS + g], vdup[b][g]], axis=0) for b, g in units]
    for b, g in units:
        kprev_ref[b * N_KV_HEADS + g] = kdup[b][g]
        vprev_ref[b * N_KV_HEADS + g] = vdup[b][g]

    def stacked_queries(b, g):
        pieces = []
        for j in tiles[g]:
            qt = q_ref[rows[b], j * LANES:(j + 1) * LANES]
            zero = jnp.zeros_like(qt)
            pieces += [jnp.where(low_half, qt, zero), jnp.where(low_half, zero, qt)]
        return jnp.concatenate(pieces, axis=0)

    def attend(u, b, g):
        s = _dot_nt(stacked_queries(b, g), kk[u])
        s = jnp.where(live, jnp.where(own, s[:, BLOCK:], s[:, :BLOCK]), MASK_VALUE)
        m = jnp.maximum(jnp.max(s, axis=-1, keepdims=True), sink_cols[g])
        p = jnp.exp(s - m)
        denom = jnp.sum(p, axis=-1, keepdims=True) + jnp.exp(sink_cols[g] - m)
        p2 = jnp.concatenate([jnp.where(own, 0.0, p), jnp.where(own, p, 0.0)], axis=1).astype(BF16)
        o = _dot(p2, vv[u]) / denom
        for n, j in enumerate(tiles[g]):
            o_lo = o[2 * n * BLOCK:(2 * n + 1) * BLOCK]
            o_hi = o[(2 * n + 1) * BLOCK:(2 * n + 2) * BLOCK]
            m_ref[rows[b], j * LANES:(j + 1) * LANES] = jnp.where(low_half, o_lo, o_hi).astype(BF16)

    return [functools.partial(attend, u, b, g) for u, (b, g) in enumerate(units)]


def _gla_decays(pf_ref, tri_ref, lev_ref):
    ld_ref = pf_ref.at[:, PF_LD:PF_OG]
    ld_all = jnp.concatenate([ld_ref[b * BLOCK:(b + 1) * BLOCK, :] for b in range(BATCH)], axis=1)
    hi, lo = _split(ld_all)
    tri = tri_ref[...]
    g_cum_all = _dot(tri, hi) + _dot(tri, lo)
    lev = lev_ref[...]
    low_sums = _dot(lev, hi) + _dot(lev, lo)
    level_sums = [low_sums[l * BLOCK:(l + 1) * BLOCK] for l in range(LOW_LEVELS)]
    for l in range(LOW_LEVELS, N_LEVELS):
        half = 1 << l
        pieces = []
        for p in range(BLOCK // (2 * half)):
            mid = g_cum_all[p * 2 * half + half - 1:p * 2 * half + half, :]
            pieces += [mid - g_cum_all[p * 2 * half:p * 2 * half + half],
                       g_cum_all[p * 2 * half + half:(p + 1) * 2 * half] - mid]
        level_sums.append(jnp.concatenate(pieces, axis=0))
    decay_all = [jnp.exp(s) for s in level_sums]
    g_last_all = jnp.broadcast_to(g_cum_all[BLOCK - 1:BLOCK, :], g_cum_all.shape)
    decay_col_all = jnp.exp(jnp.concatenate(
        [g_last_all[:, c * LANES:(c + 1) * LANES].T for c in range(BATCH * GLA_K_W // LANES)], axis=0))
    return g_cum_all, decay_all, decay_col_all


def _gla_tile(blk, decays, pf_ref, pb_ref, vec_ref, plev_ref, m_ref, state_ref, sbd_ref):
    g_cum_all, decay_all, decay_col_all = decays
    gq_ref, gk_ref, _, gv_ref, og_ref = _proj_views(pf_ref, pb_ref)[3:]
    rpos = blk * BLOCK + lax.broadcasted_iota(jnp.int32, (BLOCK, 1), 0) - PAD_LEN
    valid = (rpos >= 0).astype(F32)
    pair_level = plev_ref[...]
    gn = vec_ref[V_GN:V_GN + 1, :GLA_DV]
    streams = range(BATCH)
    rows = [slice(b * BLOCK, (b + 1) * BLOCK) for b in streams]
    cols = [slice(b * GLA_K_W, (b + 1) * GLA_K_W) for b in streams]
    gq = [gq_ref[rows[b], :] for b in streams]
    gk = [gk_ref[rows[b], :] * valid for b in streams]
    gv = [gv_ref[rows[b], :] for b in streams]
    g_cum = [g_cum_all[:, cols[b]] for b in streams]
    a = _gla_intra(gq, gk, [[d[:, cols[b]] for d in decay_all] for b in streams], range(N_LEVELS), pair_level)

    pairs = range(GLA_HEADS // 2)
    q_dec = [(gq[b] * jnp.exp(g_cum[b])).astype(BF16) for b in streams]
    o = [jnp.concatenate([_dot(q_dec[b][:, i * LANES:(i + 1) * LANES], sbd_ref[2 * b + i]) for i in pairs], axis=1)
         for b in streams]
    o = [o[b] + jnp.concatenate(
        [_dot(a[b][h].astype(BF16), gv[b][:, h * GLA_DV:(h + 1) * GLA_DV]) for h in range(GLA_HEADS)], axis=1)
        for b in streams]
    for b in streams:
        for h, gh in enumerate(_gla_merge(o[b], og_ref[rows[b], :], gn)):
            m_ref[rows[b], ATT_Q_W + h * GLA_DV:ATT_Q_W + (h + 1) * GLA_DV] = gh.astype(BF16)

    k_dec = [(gk[b] * jnp.exp(g_cum[b][BLOCK - 1:BLOCK, :] - g_cum[b])).astype(BF16) for b in streams]
    kv = [[_dot_tn(k_dec[b][:, i * LANES:(i + 1) * LANES], gv[b][:, 2 * i * GLA_DV:(2 * i + 2) * GLA_DV])
           for i in pairs] for b in streams]
    for b in streams:
        new_state = decay_col_all[cols[b], :] * state_ref[b] + jnp.concatenate(
            [kv[b][h // 2][(h % 2) * GLA_DK:(h % 2 + 1) * GLA_DK, (h % 2) * GLA_DV:(h % 2 + 1) * GLA_DV]
             for h in range(GLA_HEADS)], axis=0)
        state_ref[b] = new_state
        for h in range(GLA_HEADS):
            sbd_ref[2 * b + h // 2, (h % 2) * GLA_DK:(h % 2 + 1) * GLA_DK,
                    (h % 2) * GLA_DV:(h % 2 + 1) * GLA_DV] = new_state[h * GLA_DK:(h + 1) * GLA_DK].astype(BF16)


def _front_kernel(*refs, layer, n_src, chained):
    sinks_ref, refs = refs[0], refs[1:]
    if chained:
        refs = refs[3:]
    x_refs, refs = refs[:n_src], refs[n_src:]
    vec_ref, w_ref, wg2_ref, tri_ref, lev_ref, plev_ref = refs[:6]
    ffn_f32_refs, refs = refs[6:10], refs[10:]
    if n_src > 1:
        xo_ref, refs = refs[0], refs[1:]
    (m_ref, pfs_ref, pbs_ref, s_out_ref, k_out_ref, v_out_ref) = refs[:6]
    ffn_bf16_refs, refs = refs[6:10], refs[10:]
    wbf_ref, wg2s_ref, z_ref, pf_ref, pb_ref, kprev_ref, vprev_ref, state_ref, sbd_ref = refs
    i = pl.program_id(0)
    t = i - N_PREP

    @pl.when(i < N_PREP)
    def _():
        _convert_proj_weights(i, w_ref, wg2_ref, wbf_ref, wg2s_ref)

    @pl.when((t >= 1) & (t <= N_CONV))
    def _():
        for src, dst in zip(ffn_f32_refs, ffn_bf16_refs):
            dst[...] = src[0].astype(BF16)

    @pl.when(i == 0)
    def _():
        kprev_ref[...] = jnp.zeros_like(kprev_ref)
        vprev_ref[...] = jnp.zeros_like(vprev_ref)
        state_ref[...] = jnp.zeros_like(state_ref)
        sbd_ref[...] = jnp.zeros_like(sbd_ref)

    def load_x():
        if n_src == 1:
            return x_refs[0][...]
        x = jnp.where(t == 0, x_refs[2][...],
                      jnp.where(t < N_BLOCKS, x_refs[0][...].reshape(ROW_TILE, D_MODEL), x_refs[1][...]))
        xo_ref[...] = x
        return x

    @pl.when((t >= 0) & (t < N_BLOCKS))
    def _():
        _project_tile(load_x(), vec_ref, wbf_ref, wg2s_ref, z_ref, pf_ref, pb_ref)
        for attend in _attention_units(t, layer, sinks_ref, pf_ref, pb_ref, m_ref, kprev_ref, vprev_ref):
            attend()
        _gla_tile(t, _gla_decays(pf_ref, tri_ref, lev_ref), pf_ref, pb_ref, vec_ref, plev_ref, m_ref,
                  state_ref, sbd_ref)

    @pl.when(t == N_BLOCKS - 1)
    def _():
        own = 0 if chained else layer
        if not chained:
            for ref in (s_out_ref, k_out_ref, v_out_ref):
                ref[...] = jnp.zeros_like(ref)
        s_out_ref[own] = state_ref[...]
        for b in range(BATCH):
            k_out_ref[own, b] = pf_ref[b * BLOCK:(b + 1) * BLOCK, PF_K:PF_V].T
            v_out_ref[own, b] = pf_ref[b * BLOCK:(b + 1) * BLOCK, PF_V:PF_GQ].T

    @pl.when(t >= N_BLOCKS)
    def _():
        _project_tile(load_x(), vec_ref, wbf_ref, wg2s_ref, z_ref, pfs_ref, pbs_ref)
        m_ref[...] = jnp.zeros_like(m_ref)


def _front(layer, sinks, x_srcs, vecs, w_in, w_g2, ffn_f32, consts, prev):
    n_src = len(x_srcs)
    step = lambda i: i - N_PREP
    conv_of = lambda i: jnp.clip(step(i) - 1, 0, N_CONV - 1)
    tile_of = lambda i: jnp.where(step(i) <= 0, LEAD_TILE, jnp.minimum(step(i) - 1, LEAD_TILE - 1))
    sample_tile_of = lambda i: jnp.clip(step(i) - N_BLOCKS, 0, SAMPLE_TILES - 1)
    full = lambda a: pl.BlockSpec(a.shape, lambda i, *_: (0,) * a.ndim)
    if prev is None:
        per_batch = lambda r, c: pl.BlockSpec((DEPTH, BATCH, r, c), lambda i, *_: (0, 0, 0, 0))
    else:
        per_batch = lambda r, c: pl.BlockSpec((1, BATCH, r, c), lambda i, *_: (layer, 0, 0, 0))
    if n_src == 1:
        x_specs = [pl.BlockSpec((ROW_TILE, D_MODEL), lambda i, *_: (tile_of(i), 0))]
    else:
        x_specs = [
            pl.BlockSpec((BATCH, BLOCK, D_MODEL), lambda i, *_: (0, jnp.clip(step(i) - 1, 0, MAIN_TILES - 1), 0)),
            pl.BlockSpec((ROW_TILE, D_MODEL), lambda i, *_: (sample_tile_of(i), 0), pipeline_mode=pl.Buffered(1)),
            pl.BlockSpec((ROW_TILE, D_MODEL), lambda i, *_: (0, 0), pipeline_mode=pl.Buffered(1)),
        ]
    const_args = (consts["tri"], consts["lev_low"], consts["pair_level"])
    chained = prev is not None
    prev_args = tuple(prev) if chained else ()
    in_specs = ([pl.BlockSpec(memory_space=pl.ANY)] * len(prev_args) + x_specs + [
        pl.BlockSpec((V_ROWS, D_MODEL), lambda i, *_: (layer, 0)),
        pl.BlockSpec((1, IN_W, W_ROWS), lambda i, *_: (layer, 0, jnp.minimum(i, N_PREP - 1))),
        pl.BlockSpec((1, GATE_RANK, GLA_K_W), lambda i, *_: (layer, 0, 0)),
    ] + [full(a) for a in const_args] + [
        pl.BlockSpec((1, a.shape[1] // N_CONV, a.shape[2]), lambda i, *_: (layer, conv_of(i), 0)) for a in ffn_f32])
    out_specs = [pl.BlockSpec((ROW_TILE, D_MIX), lambda i, *_: (tile_of(i), 0)),
                 pl.BlockSpec((ROW_TILE, PF_W), lambda i, *_: (sample_tile_of(i), 0)),
                 pl.BlockSpec((ROW_TILE, PB_W), lambda i, *_: (sample_tile_of(i), 0)),
                 per_batch(GLA_K_W, GLA_DV), per_batch(BLOCK, ATT_KV_W), per_batch(BLOCK, ATT_KV_W)]
    out_specs += [pl.BlockSpec((a.shape[1] // N_CONV, a.shape[2]), lambda i, *_: (conv_of(i), 0)) for a in ffn_f32]
    out_shape = [jax.ShapeDtypeStruct((TOTAL_ROWS, D_MIX), BF16),
                 jax.ShapeDtypeStruct((SAMPLE_ROWS, PF_W), F32),
                 jax.ShapeDtypeStruct((SAMPLE_ROWS, PB_W), BF16),
                 jax.ShapeDtypeStruct((DEPTH, BATCH, GLA_K_W, GLA_DV), F32),
                 jax.ShapeDtypeStruct((DEPTH, BATCH, BLOCK, ATT_KV_W), F32),
                 jax.ShapeDtypeStruct((DEPTH, BATCH, BLOCK, ATT_KV_W), F32)]
    out_shape += [jax.ShapeDtypeStruct(a.shape[1:], BF16) for a in ffn_f32]
    n_lead_out = 0
    if n_src > 1:
        out_specs = [pl.BlockSpec((ROW_TILE, D_MODEL), lambda i, *_: (tile_of(i), 0))] + out_specs
        out_shape = [jax.ShapeDtypeStruct((TOTAL_ROWS, D_MODEL), F32)] + out_shape
        n_lead_out = 1
    grid_spec = pltpu.PrefetchScalarGridSpec(
        num_scalar_prefetch=1,
        grid=(N_PREP + N_BLOCKS + SAMPLE_TILES,),
        in_specs=in_specs,
        out_specs=out_specs,
        scratch_shapes=[pltpu.VMEM((D_MODEL, PROJ_W), BF16), pltpu.VMEM((LANES, GLA_K_W), BF16),
                        pltpu.VMEM((ROW_TILE, PROJ_W), F32),
                        pltpu.VMEM((ROW_TILE, PF_W), F32), pltpu.VMEM((ROW_TILE, PB_W), BF16),
                        pltpu.VMEM((BATCH * N_KV_HEADS, BLOCK, ATT_KV_W), BF16),
                        pltpu.VMEM((BATCH * N_KV_HEADS, BLOCK, ATT_KV_W), BF16),
                        pltpu.VMEM((BATCH, GLA_K_W, GLA_DV), F32),
                        pltpu.VMEM((BATCH * GLA_HEADS // 2, 2 * GLA_DK, 2 * GLA_DV), BF16)],
    )
    res = pl.pallas_call(
        functools.partial(_front_kernel, layer=layer, n_src=n_src, chained=chained),
        grid_spec=grid_spec,
        out_shape=out_shape,
        input_output_aliases={1 + n: n_lead_out + 3 + n for n in range(len(prev_args))},
        compiler_params=pltpu.CompilerParams(dimension_semantics=("arbitrary",),
                                             vmem_limit_bytes=VMEM_LIMIT),
        name="front",
    )(sinks, *prev_args, *x_srcs, vecs, w_in, w_g2, *const_args, *ffn_f32)
    x_all = res[0] if n_src > 1 else x_srcs[0]
    merged, pfs, pbs, ps, pk, pv = res[n_lead_out:n_lead_out + 6]
    return x_all, merged, (pfs, pbs), (ps, pk, pv), tuple(res[n_lead_out + 6:])


def _mix_sample_kernel(*refs, layer, n_alias):
    sinks_ref, refs = refs[0], refs[n_alias + 1:]
    (pf_ref, pb_ref, ck_ref, cv_ref, st_ref,
     vec_ref, tri_ref, sones_ref, lev_ref, ones_ref, plev_ref,
     m_ref, ck_out_ref, cv_out_ref, st_out_ref) = refs
    q_ref, k_ref, v_ref, gq_ref, gk_ref, ld_ref, gv_ref, og_ref = _proj_views(pf_ref, pb_ref)
    gq = gq_ref[...]
    gk = gk_ref[...]
    gv = gv_ref[...]
    gvf = gv.astype(F32)
    hi, lo = _split(ld_ref[...])
    tri = tri_ref[...]
    g_cum = _dot(tri, hi) + _dot(tri, lo)
    sones = sones_ref[...]
    g_tot = _dot(sones, hi) + _dot(sones, lo)
    lev = lev_ref[...]
    decay_levels = jnp.exp(_dot(lev, hi) + _dot(lev, lo))
    a = _gla_intra([gq], [gk], [[decay_levels[l * BLOCK:(l + 1) * BLOCK] for l in range(LOW_LEVELS)]],
                   range(LOW_LEVELS), plev_ref[...])[0]
    o_intra = jnp.concatenate(
        [_dot(a[h].astype(BF16), gv[:, h * GLA_DV:(h + 1) * GLA_DV]) for h in range(GLA_HEADS)], axis=1)
    q_dec = gq * jnp.exp(g_cum)
    k_dec = gk * jnp.exp(g_tot - g_cum)
    hm8 = _head_masks(DEC_SEQ)
    ones8 = ones_ref[...]
    hi_f, lo_f = hi.astype(F32), lo.astype(F32)

    qf = q_ref[...].astype(F32)
    kf = k_ref[...]
    vf = v_ref[...]
    low8 = lax.broadcasted_iota(jnp.int32, (DEC_SEQ, LANES), 1) < HEAD_DIM
    nkeys = WINDOW + DEC_SEQ
    srow = lax.broadcasted_iota(jnp.int32, (N_HEADS * DEC_SEQ, nkeys), 0)
    scol = lax.broadcasted_iota(jnp.int32, (N_HEADS * DEC_SEQ, nkeys), 1)
    t_of_row = srow & (DEC_SEQ - 1)
    amask = ((scol < WINDOW) & (scol > t_of_row)) | ((scol >= WINDOW) & (scol - WINDOW <= t_of_row))
    rid = lax.broadcasted_iota(jnp.int32, (N_HEADS * DEC_SEQ, 1), 0) >> LOW_LEVELS
    sink_col = jnp.zeros((N_HEADS * DEC_SEQ, 1), F32)
    for i in range(N_HEADS):
        sink_col = jnp.where(rid == i, sinks_ref[layer, i], sink_col)

    seqs = range(SEQ_GROUP)
    rows = [slice(b * DEC_SEQ, (b + 1) * DEC_SEQ) for b in seqs]

    def stacked_queries(b):
        pieces = []
        for j in range(N_HEADS // 2):
            g = (2 * j) // (N_HEADS // N_KV_HEADS)
            qt = qf[rows[b], j * LANES:(j + 1) * LANES]
            swapped = pltpu.roll(qt, HEAD_DIM, axis=1)
            own_lanes = low8 if g == 0 else ~low8
            even, odd = (qt, swapped) if g == 0 else (swapped, qt)
            pieces += [jnp.where(own_lanes, even, 0.0), jnp.where(own_lanes, odd, 0.0)]
        return jnp.concatenate(pieces, axis=0).astype(BF16)

    lane = lax.broadcasted_iota(jnp.int32, (WINDOW, WINDOW), 1)
    old_lanes = lane < WINDOW - DEC_SEQ

    def shifted_buffer(buf, new_rows):
        wide = jnp.concatenate([jnp.zeros((WINDOW - DEC_SEQ, LANES), F32), new_rows], axis=0)
        return jnp.where(old_lanes, pltpu.roll(buf, WINDOW - DEC_SEQ, axis=1), wide.T)

    qp = [stacked_queries(b) for b in seqs]
    ck = [ck_ref[0, b] for b in seqs]
    cv = [cv_ref[0, b] for b in seqs]
    for b in seqs:
        ck_out_ref[0, b] = shifted_buffer(ck[b], kf[rows[b]])
        cv_out_ref[0, b] = shifted_buffer(cv[b], vf[rows[b]])
    s = [jnp.concatenate([_dot(qp[b], ck[b].astype(BF16)), _dot_nt(qp[b], kf[rows[b]].astype(BF16))], axis=1)
         for b in seqs]
    s = [jnp.where(amask, s[b], MASK_VALUE) for b in seqs]
    m = [jnp.maximum(jnp.max(s[b], axis=-1, keepdims=True), sink_col) for b in seqs]
    p = [jnp.exp(s[b] - m[b]) for b in seqs]
    denom = [jnp.sum(p[b], axis=-1, keepdims=True) + jnp.exp(sink_col - m[b]) for b in seqs]
    pb = [p[b].astype(BF16) for b in seqs]
    ob = [(_dot_nt(pb[b][:, :WINDOW], cv[b].astype(BF16))
           + _dot(pb[b][:, WINDOW:], vf[rows[b]].astype(BF16))) / denom[b] for b in seqs]

    def head_tiles(o):
        tiles = []
        for j in range(N_HEADS // 2):
            g = (2 * j) // (N_HEADS // N_KV_HEADS)
            even = o[(2 * j) * DEC_SEQ:(2 * j + 1) * DEC_SEQ]
            odd = o[(2 * j + 1) * DEC_SEQ:(2 * j + 2) * DEC_SEQ]
            if g == 0:
                tiles.append(jnp.where(low8, even, pltpu.roll(odd, HEAD_DIM, axis=1)))
            else:
                tiles.append(jnp.where(low8, pltpu.roll(even, HEAD_DIM, axis=1), odd))
        return jnp.concatenate(tiles, axis=1)

    att_rows = [head_tiles(ob[b]) for b in seqs]

    head_stack = lambda x: jnp.concatenate([jnp.where(mk, x, 0.0) for mk in hm8], axis=0).astype(BF16)
    state = [st_ref[0, b] for b in seqs]
    oi = [_dot(head_stack(q_dec[rows[b]]), state[b].astype(BF16)) for b in seqs]
    inter_rows = [jnp.concatenate([oi[b][h * DEC_SEQ:(h + 1) * DEC_SEQ] for h in range(GLA_HEADS)], axis=1)
                  for b in seqs]
    vstack = [jnp.concatenate([gvf[rows[b]][:, h * GLA_DV:(h + 1) * GLA_DV] for h in range(GLA_HEADS)],
                              axis=0).astype(BF16) for b in seqs]
    kv = [_dot_tn(head_stack(k_dec[rows[b]]), vstack[b]) for b in seqs]
    decay_col = [jnp.exp(_dot_tn(hi_f[rows[b]], ones8) + _dot_tn(lo_f[rows[b]], ones8)) for b in seqs]
    for b in seqs:
        st_out_ref[0, b] = decay_col[b] * state[b] + kv[b]

    m_ref[:, :ATT_Q_W] = jnp.concatenate(att_rows, axis=0).astype(BF16)
    o = o_intra + jnp.concatenate(inter_rows, axis=0)
    gn = vec_ref[V_GN:V_GN + 1, :GLA_DV]
    for h, gh in enumerate(_gla_merge(o, og_ref[...], gn)):
        m_ref[:, ATT_Q_W + h * GLA_DV:ATT_Q_W + (h + 1) * GLA_DV] = gh.astype(BF16)


def _mix_sample(layer, sinks, merged, proj, cache_k, cache_v, state, vecs, consts, prev):
    tok = lambda w_: pl.BlockSpec((BLOCK, w_), lambda i, *_: (i, 0))
    merged_blk = pl.BlockSpec((BLOCK, D_MIX), lambda i, *_: (MAIN_ROWS // BLOCK + i, 0))
    full = lambda a: pl.BlockSpec(a.shape, lambda i, *_: (0,) * a.ndim)
    seq = lambda a: pl.BlockSpec((1, SEQ_GROUP) + a.shape[2:], lambda i, *_: (layer, i, 0, 0))
    ones8 = jnp.ones((DEC_SEQ, LANES), F32)
    const_args = (consts["seq_tri"], consts["seq_ones"], consts["lev_low"], ones8, consts["pair_level"])
    seq_args = (cache_k, cache_v, state)
    alias_args = (merged,) + (tuple(prev) if prev is not None else ())
    grid_spec = pltpu.PrefetchScalarGridSpec(
        num_scalar_prefetch=1,
        grid=(DEC_BATCH // SEQ_GROUP,),
        in_specs=([pl.BlockSpec(memory_space=pl.ANY)] * len(alias_args)
                  + [tok(a.shape[1]) for a in proj] + [seq(a) for a in seq_args]
                  + [pl.BlockSpec((V_ROWS, D_MODEL), lambda i, *_: (layer, 0))]
                  + [full(a) for a in const_args]),
        out_specs=[merged_blk] + [seq(a) for a in seq_args],
    )
    return pl.pallas_call(
        functools.partial(_mix_sample_kernel, layer=layer, n_alias=len(alias_args)),
        grid_spec=grid_spec,
        out_shape=[jax.ShapeDtypeStruct(merged.shape, merged.dtype)]
        + [jax.ShapeDtypeStruct(a.shape, a.dtype) for a in seq_args],
        input_output_aliases={1 + n: n for n in range(len(alias_args))},
        compiler_params=pltpu.CompilerParams(dimension_semantics=("arbitrary",),
                                             vmem_limit_bytes=VMEM_LIMIT),
        name="mix_sample",
    )(sinks, *alias_args, *proj, *seq_args, vecs, *const_args)


def _out_ffn_kernel(*refs, last):
    x_ref, m_ref, vec_ref, wo_ref, wg_ref, wu_ref, wd_ref = refs[:7]
    out_refs = refs[7:9] if last else refs[7:8]
    act_ref = refs[-1]
    x1 = x_ref[...] + _dot(m_ref[...], wo_ref[...])
    ms = jnp.mean(x1 * x1, axis=-1, keepdims=True)
    h = (x1 * lax.rsqrt(ms + RMS_EPS) * vec_ref[V_NORM2:V_NORM2 + 1, :]).astype(BF16)
    for c in range(D_FFN // FFN_CHUNK):
        cols = slice(c * FFN_CHUNK, (c + 1) * FFN_CHUNK)
        gate = _dot(h, wg_ref[:, cols])
        up = _dot(h, wu_ref[:, cols])
        act_ref[:, cols] = (gate * _sigmoid(gate) * up).astype(BF16)
    y = x1 + _dot(act_ref[...], wd_ref[...])
    if last:
        out_refs[0][...] = y.reshape(BATCH, BLOCK, D_MODEL)

        @pl.when(pl.program_id(0) < SAMPLE_TILES)
        def _():
            out_refs[1][...] = out_refs[0][...].reshape(ROW_TILE, D_MODEL)
    else:
        out_refs[0][...] = y


def _out_ffn(layer, x_all, merged, vecs, ffn_weights, last):
    resident = lambda a: pl.BlockSpec(a.shape, lambda i: (0, 0), pipeline_mode=pl.Buffered(1))
    if last:
        n_tiles = MAIN_TILES + SAMPLE_TILES
        tile_of = lambda i: jnp.where(i < SAMPLE_TILES, MAIN_TILES + i, i - SAMPLE_TILES)
        out_specs = [pl.BlockSpec((BATCH, BLOCK, D_MODEL), lambda i: (0, jnp.maximum(i - SAMPLE_TILES, 0), 0)),
                     pl.BlockSpec((ROW_TILE, D_MODEL), lambda i: (jnp.minimum(i, SAMPLE_TILES - 1), 0))]
        out_shape = [jax.ShapeDtypeStruct((BATCH, SEQ, D_MODEL), F32),
                     jax.ShapeDtypeStruct((SAMPLE_ROWS, D_MODEL), F32)]
    else:
        n_tiles = N_TILES
        tile_of = lambda i: i
        out_specs = [pl.BlockSpec((ROW_TILE, D_MODEL), lambda i: (i, 0))]
        out_shape = [jax.ShapeDtypeStruct((TOTAL_ROWS, D_MODEL), F32)]
    tile = lambda w_: pl.BlockSpec((ROW_TILE, w_), lambda i: (tile_of(i), 0))
    return pl.pallas_call(
        functools.partial(_out_ffn_kernel, last=last),
        grid=(n_tiles,),
        in_specs=[tile(D_MODEL), tile(D_MIX), pl.BlockSpec((V_ROWS, D_MODEL), lambda i: (layer, 0))]
        + [resident(w) for w in ffn_weights],
        out_specs=out_specs,
        out_shape=out_shape,
        scratch_shapes=[pltpu.VMEM((ROW_TILE, D_FFN), BF16)],
        compiler_params=pltpu.CompilerParams(dimension_semantics=("arbitrary",),
                                             vmem_limit_bytes=VMEM_LIMIT),
        name="out_ffn",
    )(x_all, merged, vecs, *ffn_weights)


def _vector_slab(norm1, norm2, q_norm, k_norm, b_g, gla_norm):
    pad = lambda a: jnp.pad(a.astype(F32), ((0, 0), (0, D_MODEL - a.shape[1])))
    rows = [norm1.astype(F32), norm2.astype(F32), pad(jnp.tile(q_norm, (1, N_HEADS)) * ATT_SCALE),
            pad(jnp.tile(k_norm, (1, N_KV_HEADS))), pad(b_g), pad(gla_norm)]
    rows += [jnp.zeros((DEPTH, D_MODEL), F32)] * (V_ROWS - len(rows))
    return jnp.stack(rows, axis=1).reshape(DEPTH * V_ROWS, D_MODEL)


def kernel(x_prompt, x_sample, cache_k, cache_v, state_gla, meta, norm1, w_in, q_norm, k_norm, sinks,
           w_g2, b_g, gla_norm, w_o, norm2, w_gate, w_up, w_down):
    consts = _constants()
    dt = x_prompt.dtype
    vecs = _vector_slab(norm1, norm2, q_norm, k_norm, b_g, gla_norm)
    sinks = sinks.astype(F32)
    lead = jnp.tile(jnp.concatenate([jnp.zeros((PAD_LEN, D_MODEL), dt), meta.astype(dt)], axis=0), (BATCH, 1))
    x_srcs = (x_prompt, x_sample.reshape(SAMPLE_ROWS, D_MODEL), lead)
    to_feature_major = lambda c: c.transpose(0, 1, 3, 4, 2).reshape(DEPTH, DEC_BATCH, ATT_KV_W, WINDOW)
    ck_in, cv_in = to_feature_major(cache_k), to_feature_major(cache_v)
    st_in = state_gla.reshape(DEPTH, DEC_BATCH, GLA_K_W, GLA_DV)
    w_in_t = jnp.swapaxes(w_in, 1, 2)

    prompt_outs, sample_outs = None, None
    for l in range(DEPTH):
        x_all, merged, proj_sample, prompt_outs, ffn_bf16 = _front(
            l, sinks, x_srcs, vecs, w_in_t, w_g2, (w_o, w_gate, w_up, w_down), consts, prompt_outs)
        merged, *sample_outs = _mix_sample(l, sinks, merged, proj_sample, ck_in, cv_in, st_in, vecs, consts,
                                           sample_outs)
        x_srcs = tuple(_out_ffn(l, x_all, merged, vecs, ffn_bf16, last=l == DEPTH - 1))

    y_main, y_sample = x_srcs
    ps, pk, pv = prompt_outs
    sk, sv, ss = sample_outs
    kv5 = lambda a, n: a.reshape(DEPTH, n, N_KV_HEADS, HEAD_DIM, WINDOW).transpose(0, 1, 4, 2, 3)
    st5 = lambda a, n: a.reshape(DEPTH, n, GLA_HEADS, GLA_DK, GLA_DV)
    return (y_main, y_sample.reshape(DEC_BATCH, DEC_SEQ, D_MODEL),
            kv5(pk, BATCH), kv5(pv, BATCH), st5(ps, BATCH), kv5(sk, DEC_BATCH), kv5(sv, DEC_BATCH),
            st5(ss, DEC_BATCH))
```

```python
import functools

import jax
import jax.numpy as jnp
import numpy as np
from jax import lax
from jax.experimental import pallas as pl
from jax.experimental.pallas import tpu as pltpu

F32 = jnp.float32
BF16 = jnp.bfloat16

D_MODEL = 1024
BATCH = 4
SEQ = 4096
DEPTH = 2
DEC_BATCH = 128
DEC_SEQ = 8
N_META = 16
WINDOW = 128
BLOCK = 128
PAD_LEN = BLOCK - N_META
N_HEADS = 8
N_KV_HEADS = 2
HEAD_DIM = 64
ATT_SCALE = HEAD_DIM ** -0.5
ATT_Q_W = N_HEADS * HEAD_DIM
ATT_KV_W = N_KV_HEADS * HEAD_DIM
GLA_HEADS = 4
GLA_DK = 64
GLA_DV = 128
GLA_K_W = GLA_HEADS * GLA_DK
GLA_V_W = GLA_HEADS * GLA_DV
GATE_RANK = 16
GATE_NORMALIZER = 16.0
D_MIX = ATT_Q_W + GLA_V_W
D_FFN = 2816
IN_W = 2320
RMS_EPS = 1e-6
MASK_VALUE = -1e30

LANES = 128
N_BLOCKS = 1 + SEQ // BLOCK
MAIN_ROWS = BATCH * SEQ
SAMPLE_ROWS = DEC_BATCH * DEC_SEQ
LEAD_ROWS = BATCH * BLOCK
TOTAL_ROWS = MAIN_ROWS + SAMPLE_ROWS + LEAD_ROWS
ROW_TILE = 512
MAIN_TILES = MAIN_ROWS // ROW_TILE
SAMPLE_TILES = SAMPLE_ROWS // ROW_TILE
N_TILES = TOTAL_ROWS // ROW_TILE
LEAD_TILE = N_TILES - 1
SEQ_GROUP = BLOCK // DEC_SEQ
N_LEVELS = 7
LOW_LEVELS = 3
LOG_DK = 6
N_PREP = 8
W_ROWS = D_MODEL // N_PREP
N_CONV = 16

SRC_LOW, SRC_OG = 1792, 1808
OFF_Q, OFF_K, OFF_V = 0, 512, 640
OFF_GQ, OFF_GK, OFF_GV, OFF_OG, OFF_LOW = 768, 1024, 1280, 1792, 2304
PROJ_W = OFF_LOW + LANES
PROJ_CHUNK = 512
FFN_CHUNK = 256
SCORE_LOOKAHEAD = 1
VMEM_LIMIT = 58 * 1024 * 1024

V_NORM1, V_NORM2, V_QG, V_KG, V_BG, V_GN, V_ROWS = 0, 1, 2, 3, 4, 5, 8

PF_K, PF_V, PF_GQ, PF_GK, PF_LD, PF_OG, PF_W = 0, 128, 256, 512, 768, 1024, 1536
PB_Q, PB_GV, PB_W = 0, 512, 1024


def _proj_views(pf_ref, pb_ref):
    f = lambda a, b: pf_ref.at[:, a:b]
    return (pb_ref.at[:, PB_Q:PB_GV], f(PF_K, PF_V), f(PF_V, PF_GQ), f(PF_GQ, PF_GK), f(PF_GK, PF_LD),
            f(PF_LD, PF_OG), pb_ref.at[:, PB_GV:PB_W], f(PF_OG, PF_W))


def _dot(a, b):
    return jnp.dot(a, b, preferred_element_type=F32)


def _dot_nt(a, b):
    return lax.dot_general(a, b, (((1,), (1,)), ((), ())), preferred_element_type=F32)


def _dot_tn(a, b):
    return lax.dot_general(a, b, (((0,), (0,)), ((), ())), preferred_element_type=F32)


def _split(x):
    hi = x.astype(BF16)
    lo = (x - hi.astype(F32)).astype(BF16)
    return hi, lo


def _sigmoid(x):
    return 1.0 / (1.0 + jnp.exp(-x))


def _level_matrix(levels, n=BLOCK):
    out = np.zeros((len(levels) * n, n), np.float32)
    for i, l in enumerate(levels):
        size = 2 << l
        for t in range(n):
            mid = (t // size) * size + size // 2 - 1
            if (t >> l) & 1:
                out[i * n + t, mid + 1:t + 1] = 1.0
            else:
                out[i * n + t, t + 1:mid + 1] = 1.0
    return out


def _constants():
    r = np.arange(BLOCK)
    tri = (r[None, :] <= r[:, None]).astype(np.float32)
    same_seq = (r[None, :] // DEC_SEQ) == (r[:, None] // DEC_SEQ)
    diff = np.maximum(r[:, None] ^ r[None, :], 1)
    pair_level = np.where(r[None, :] < r[:, None], np.floor(np.log2(diff)).astype(np.int32),
                          np.where(r[None, :] == r[:, None], N_LEVELS, N_LEVELS + 1)).astype(np.int32)
    return dict(
        tri=jnp.asarray(tri, BF16),
        seq_tri=jnp.asarray(tri * same_seq, BF16),
        seq_ones=jnp.asarray(same_seq.astype(np.float32), BF16),
        lev_low=jnp.asarray(_level_matrix(range(LOW_LEVELS)), BF16),
        pair_level=jnp.asarray(pair_level),
    )


def _head_masks(rows):
    lane = lax.broadcasted_iota(jnp.int32, (rows, GLA_K_W), 1)
    return [(lane >> LOG_DK) == h for h in range(GLA_HEADS)]


def _gla_intra(gq, gk, level_decay, levels, pair_level):
    rows = gq[0].shape[0]
    streams = range(len(gq))
    low_half = lax.broadcasted_iota(jnp.int32, (rows, LANES), 1) < GLA_DK

    def pair_products(qh, kh):
        qh, kh = qh.astype(BF16), kh.astype(BF16)
        out = []
        for pair in range(GLA_HEADS // 2):
            kl = kh[:, pair * LANES:(pair + 1) * LANES]
            zero = jnp.zeros_like(kl)
            stacked = jnp.concatenate([jnp.where(low_half, kl, zero), jnp.where(low_half, zero, kl)], axis=0)
            p = _dot_nt(qh[:, pair * LANES:(pair + 1) * LANES], stacked)
            out += [p[:, :rows], p[:, rows:]]
        return out

    row = lax.broadcasted_iota(jnp.int32, (rows, 1), 0)

    def upper_q_lower_k(b, l):
        size = 1 << l
        if size % 8:
            return jnp.where(((row >> l) & 1) == 1, gq[b], gk[b])
        return jnp.concatenate([(gq[b] if i & 1 else gk[b])[i * size:(i + 1) * size]
                                for i in range(rows // size)], axis=0)

    def diagonal(b):
        out = []
        for pair in range(GLA_HEADS // 2):
            prod = (gq[b] * gk[b])[:, pair * LANES:(pair + 1) * LANES]
            out += [jnp.sum(jnp.where(low_half, prod, 0.0), axis=-1, keepdims=True),
                    jnp.sum(jnp.where(low_half, 0.0, prod), axis=-1, keepdims=True)]
        return out

    def blocks(v, size, parity):
        return jnp.concatenate([v[i * size:(i + 1) * size] for i in range(parity, rows // size, 2)], axis=0)

    def interleave(lower, upper, size):
        pieces = []
        for i in range(rows // (2 * size)):
            pieces += [lower[i * size:(i + 1) * size], upper[i * size:(i + 1) * size]]
        return jnp.concatenate(pieces, axis=0)

    on_diag = pair_level == N_LEVELS
    a = [[jnp.where(on_diag, d, 0.0).astype(BF16) for d in diagonal(b)] for b in streams]
    for l in levels:
        at_level = pair_level == l
        size = 1 << l
        for b in streams:
            x = upper_q_lower_k(b, l) * level_decay(b, l)
            if size % 16:
                a[b] = [jnp.where(at_level, p.astype(BF16), a_h) for p, a_h in zip(pair_products(x, x), a[b])]
            else:
                mask_up = blocks(pair_level, size, 1) == l
                p_up = pair_products(blocks(x, size, 1), x)
                a[b] = [interleave(blocks(a_h, size, 0),
                                   jnp.where(mask_up, p.astype(BF16), blocks(a_h, size, 1)), size)
                        for p, a_h in zip(p_up, a[b])]
    return a


def _gla_merge(o, og, gn):
    outs = []
    for h in range(GLA_HEADS):
        oh = o[:, h * GLA_DV:(h + 1) * GLA_DV]
        gh = og[:, h * GLA_DV:(h + 1) * GLA_DV]
        ms = jnp.mean(oh * oh, axis=-1, keepdims=True)
        outs.append(oh * lax.rsqrt(ms + RMS_EPS) * gn * (gh * _sigmoid(gh)))
    return outs


def _dup_halves(x):
    low = lax.broadcasted_iota(jnp.int32, x.shape, 1) < HEAD_DIM
    rolled = pltpu.roll(x, HEAD_DIM, axis=1)
    return jnp.where(low, x, rolled), jnp.where(low, rolled, x)


def _convert_proj_weights(i, w_ref, wg2_ref, wbf_ref, wg2s_ref):
    rows = pl.ds(pl.multiple_of(i * W_ROWS, W_ROWS), W_ROWS)

    def put(dst, src, n=LANES):
        tile = w_ref[0, src:src + n, :]
        if n < LANES:
            tile = jnp.concatenate([tile, jnp.zeros((LANES - n, W_ROWS), F32)], axis=0)
        wbf_ref[rows, dst:dst + LANES] = tile.T.astype(BF16)

    for c in range(SRC_LOW // LANES):
        put(c * LANES, c * LANES)
    for c in range((IN_W - SRC_OG) // LANES):
        put(OFF_OG + c * LANES, SRC_OG + c * LANES)
    put(OFF_LOW, SRC_LOW, GATE_RANK)

    @pl.when(i == 0)
    def _():
        wg2s_ref[...] = jnp.concatenate(
            [wg2_ref[0], jnp.zeros((LANES - GATE_RANK, GLA_K_W), F32)], axis=0).astype(BF16)


def _project_tile(x, vec_ref, wbf_ref, wg2s_ref, z_ref, pf_ref, pb_ref):
    q_ref, k_ref, v_ref, gq_ref, gk_ref, ld_ref, gv_ref, og_ref = _proj_views(pf_ref, pb_ref)
    ms = jnp.mean(x * x, axis=-1, keepdims=True)
    h = (x * lax.rsqrt(ms + RMS_EPS) * vec_ref[V_NORM1:V_NORM1 + 1, :]).astype(BF16)

    low_half = lax.broadcasted_iota(jnp.int32, (ROW_TILE, LANES), 1) < HEAD_DIM

    def head_norm(t):
        t2 = t * t
        ms_lo = jnp.sum(jnp.where(low_half, t2, 0.0), axis=-1, keepdims=True) * (1.0 / HEAD_DIM)
        ms_hi = jnp.sum(jnp.where(low_half, 0.0, t2), axis=-1, keepdims=True) * (1.0 / HEAD_DIM)
        return t * jnp.where(low_half, lax.rsqrt(ms_lo + RMS_EPS), lax.rsqrt(ms_hi + RMS_EPS))

    def q_part():
        for j in range(ATT_Q_W // LANES):
            cols = slice(j * LANES, (j + 1) * LANES)
            q_ref[:, cols] = (head_norm(z_ref[:, cols]) * vec_ref[V_QG:V_QG + 1, cols]).astype(BF16)

    def kv_part():
        k_ref[...] = head_norm(z_ref[:, OFF_K:OFF_V]) * vec_ref[V_KG:V_KG + 1, :ATT_KV_W]
        v_ref[...] = z_ref[:, OFF_V:OFF_GQ]
        gq_ref[...] = z_ref[:, OFF_GQ:OFF_GK] * (GLA_DK ** -0.5)

    def gk_part():
        gk_ref[...] = z_ref[:, OFF_GK:OFF_GV]

    def gv_part():
        gv_ref[...] = z_ref[:, OFF_GV:OFF_OG].astype(BF16)
        og_ref[...] = z_ref[:, OFF_OG:OFF_LOW]

    def gate_part():
        logit =(_dot(z_ref[:, OFF_LOW:PROJ_W].astype(BF16), wg2s_ref[...])
                 + vec_ref[V_BG:V_BG + 1, :GLA_K_W])
        log_sig = jnp.minimum(logit, 0.0) - jnp.log1p(jnp.exp(-jnp.abs(logit)))
        ld_ref[...] = log_sig * (1.0 / GATE_NORMALIZER)

    pieces = [q_part, kv_part, gk_part, gv_part, gate_part]
    for c in [len(pieces) - 1] + list(range(len(pieces) - 1)):
        cols = slice(c * PROJ_CHUNK, min((c + 1) * PROJ_CHUNK, PROJ_W))
        z_ref[:, cols] = _dot(h, wbf_ref[:, cols])
        pieces[c]()


def _attention_tile(blk, layer, sinks_ref, pf_ref, pb_ref, m_ref, kprev_ref, vprev_ref):
    q_ref, k_ref, v_ref = _proj_views(pf_ref, pb_ref)[:3]
    per_group = N_HEADS // N_KV_HEADS
    grows = per_group * BLOCK
    row = lax.broadcasted_iota(jnp.int32, (grows, BLOCK), 0) & (BLOCK - 1)
    col = lax.broadcasted_iota(jnp.int32, (grows, BLOCK), 1)
    own = col <= row
    kpos = jnp.where(own, blk * BLOCK, (blk - 1) * BLOCK) + col - PAD_LEN
    live = kpos >= 0
    low_half = lax.broadcasted_iota(jnp.int32, (BLOCK, LANES), 1) < HEAD_DIM
    head_of_row = lax.broadcasted_iota(jnp.int32, (grows, 1), 0) >> N_LEVELS
    sink_cols = []
    for g in range(N_KV_HEADS):
        sink = jnp.zeros((grows, 1), F32)
        for r in range(per_group):
            sink = jnp.where(head_of_row == r, sinks_ref[layer, g * per_group + r], sink)
        sink_cols.append(sink)

    units =[(b, g) for b in range(BATCH) for g in range(N_KV_HEADS)]
    rows = [slice(b * BLOCK, (b + 1) * BLOCK) for b in range(BATCH)]
    tiles = [range(g * per_group // 2, (g + 1) * per_group // 2) for g in range(N_KV_HEADS)]
    kdup = [_dup_halves(k_ref[rows[b], :].astype(BF16)) for b in range(BATCH)]
    vdup = [_dup_halves(v_ref[rows[b], :].astype(BF16)) for b in range(BATCH)]
    kk = [jnp.concatenate([kprev_ref[b * N_KV_HEADS + g], kdup[b][g]], axis=0) for b, g in units]
    vv = [jnp.concatenate([vprev_ref[b * N_KV_HEADS + g], vdup[b][g]], axis=0) for b, g in units]
    for b, g in units:
        kprev_ref[b * N_KV_HEADS + g] = kdup[b][g]
        vprev_ref[b * N_KV_HEADS + g] = vdup[b][g]

    def stacked_queries(b, g):
        pieces = []
        for j in tiles[g]:
            qt = q_ref[rows[b], j * LANES:(j + 1) * LANES]
            zero = jnp.zeros_like(qt)
            pieces += [jnp.where(low_half, qt, zero), jnp.where(low_half, zero, qt)]
        return jnp.concatenate(pieces, axis=0)

    def scores(u):
        b, g = units[u]
        return _dot_nt(stacked_queries(b, g), kk[u])

    def attend(u, s):
        b, g = units[u]
        s = jnp.where(live, jnp.where(own, s[:, BLOCK:], s[:, :BLOCK]), MASK_VALUE)
        m = jnp.maximum(jnp.max(s, axis=-1, keepdims=True), sink_cols[g])
        p = jnp.exp(s - m)
        denom = jnp.sum(p, axis=-1, keepdims=True) + jnp.exp(sink_cols[g] - m)
        p2 = jnp.concatenate([jnp.where(own, 0.0, p), jnp.where(own, p, 0.0)], axis=1).astype(BF16)
        o = _dot(p2, vv[u]) / denom
        for n, j in enumerate(tiles[g]):
            o_lo = o[2 * n * BLOCK:(2 * n + 1) * BLOCK]
            o_hi = o[(2 * n + 1) * BLOCK:(2 * n + 2) * BLOCK]
            m_ref[rows[b], j * LANES:(j + 1) * LANES] = jnp.where(low_half, o_lo, o_hi).astype(BF16)

    pending = [scores(u) for u in range(SCORE_LOOKAHEAD)]
    for u in range(len(units)):
        if u + SCORE_LOOKAHEAD < len(units):
            pending.append(scores(u + SCORE_LOOKAHEAD))
        attend(u, pending.pop(0))


def _gla_decays(pf_ref, tri_ref, lev_ref):
    ld_ref = pf_ref.at[:, PF_LD:PF_OG]
    ld_all = jnp.concatenate([ld_ref[b * BLOCK:(b + 1) * BLOCK, :] for b in range(BATCH)], axis=1)
    hi, lo = _split(ld_all)
    tri = tri_ref[...]
    g_cum_all = _dot(tri, hi) + _dot(tri, lo)
    lev = lev_ref[...]
    low_sums = _dot(lev, hi) + _dot(lev, lo)

    def level_decay(l, cols):
        if l < LOW_LEVELS:
            return jnp.exp(low_sums[l * BLOCK:(l + 1) * BLOCK, cols])
        half = 1 << l
        g = g_cum_all[:, cols]
        pieces = []
        for p in range(BLOCK // (2 * half)):
            mid = g[p * 2 * half + half - 1:p * 2 * half + half, :]
            pieces += [mid - g[p * 2 * half:p * 2 * half + half], g[p * 2 * half + half:(p + 1) * 2 * half] - mid]
        return jnp.exp(jnp.concatenate(pieces, axis=0))

    g_last_all = jnp.broadcast_to(g_cum_all[BLOCK - 1:BLOCK, :], g_cum_all.shape)
    decay_col_all = jnp.exp(jnp.concatenate(
        [g_last_all[:, c * LANES:(c + 1) * LANES].T for c in range(BATCH * GLA_K_W // LANES)], axis=0))
    return g_cum_all, level_decay, decay_col_all


def _gla_tile(blk, decays, pf_ref, pb_ref, vec_ref, plev_ref, m_ref, state_ref, sbd_ref):
    g_cum_all, level_decay, decay_col_all = decays
    gq_ref, gk_ref, _, gv_ref, og_ref = _proj_views(pf_ref, pb_ref)[3:]
    rpos = blk * BLOCK + lax.broadcasted_iota(jnp.int32, (BLOCK, 1), 0) - PAD_LEN
    valid = (rpos >= 0).astype(F32)
    pair_level = plev_ref[...]
    gn = vec_ref[V_GN:V_GN + 1, :GLA_DV]
    streams = range(BATCH)
    rows = [slice(b * BLOCK, (b + 1) * BLOCK) for b in streams]
    cols = [slice(b * GLA_K_W, (b + 1) * GLA_K_W) for b in streams]
    gq = [gq_ref[rows[b], :] for b in streams]
    gk = [gk_ref[rows[b], :] * valid for b in streams]
    gv = [gv_ref[rows[b], :] for b in streams]
    g_cum = [g_cum_all[:, cols[b]] for b in streams]
    a = _gla_intra(gq, gk, lambda b, l: level_decay(l, cols[b]), range(N_LEVELS), pair_level)

    pairs = range(GLA_HEADS // 2)
    q_dec = [(gq[b] * jnp.exp(g_cum[b])).astype(BF16) for b in streams]
    o = [jnp.concatenate([_dot(q_dec[b][:, i * LANES:(i + 1) * LANES], sbd_ref[2 * b + i]) for i in pairs], axis=1)
         for b in streams]
    o = [o[b] + jnp.concatenate(
        [_dot(a[b][h].astype(BF16), gv[b][:, h * GLA_DV:(h + 1) * GLA_DV]) for h in range(GLA_HEADS)], axis=1)
        for b in streams]
    for b in streams:
        for h, gh in enumerate(_gla_merge(o[b], og_ref[rows[b], :], gn)):
            m_ref[rows[b], ATT_Q_W + h * GLA_DV:ATT_Q_W + (h + 1) * GLA_DV] = gh.astype(BF16)

    k_dec = [(gk[b] * jnp.exp(g_cum[b][BLOCK - 1:BLOCK, :] - g_cum[b])).astype(BF16) for b in streams]
    kv = [[_dot_tn(k_dec[b][:, i * LANES:(i + 1) * LANES], gv[b][:, 2 * i * GLA_DV:(2 * i + 2) * GLA_DV])
           for i in pairs] for b in streams]
    for b in streams:
        new_state = decay_col_all[cols[b], :] * state_ref[b] + jnp.concatenate(
            [kv[b][h // 2][(h % 2) * GLA_DK:(h % 2 + 1) * GLA_DK, (h % 2) * GLA_DV:(h % 2 + 1) * GLA_DV]
             for h in range(GLA_HEADS)], axis=0)
        state_ref[b] = new_state
        for h in range(GLA_HEADS):
            sbd_ref[2 * b + h // 2, (h % 2) * GLA_DK:(h % 2 + 1) * GLA_DK,
                    (h % 2) * GLA_DV:(h % 2 + 1) * GLA_DV] = new_state[h * GLA_DK:(h + 1) * GLA_DK].astype(BF16)


def _front_kernel(*refs, layer, n_src, chained):
    sinks_ref, refs = refs[0], refs[1:]
    if chained:
        refs = refs[3:]
    x_refs, refs = refs[:n_src], refs[n_src:]
    vec_ref, w_ref, wg2_ref, tri_ref, lev_ref, plev_ref = refs[:6]
    ffn_f32_refs, refs = refs[6:10], refs[10:]
    if n_src > 1:
        xo_ref, refs = refs[0], refs[1:]
    (m_ref, pfs_ref, pbs_ref, s_out_ref, k_out_ref, v_out_ref) = refs[:6]
    ffn_bf16_refs, refs = refs[6:10], refs[10:]
    wbf_ref, wg2s_ref, z_ref, pf_ref, pb_ref, kprev_ref, vprev_ref, state_ref, sbd_ref = refs
    i = pl.program_id(0)
    t = i - N_PREP

    @pl.when(i < N_PREP)
    def _():
        _convert_proj_weights(i, w_ref, wg2_ref, wbf_ref, wg2s_ref)

    @pl.when((t >= 1) & (t <= N_CONV))
    def _():
        for src, dst in zip(ffn_f32_refs, ffn_bf16_refs):
            dst[...] = src[0].astype(BF16)

    @pl.when(i == 0)
    def _():
        kprev_ref[...] = jnp.zeros_like(kprev_ref)
        vprev_ref[...] = jnp.zeros_like(vprev_ref)
        state_ref[...] = jnp.zeros_like(state_ref)
        sbd_ref[...] = jnp.zeros_like(sbd_ref)

    def load_x():
        if n_src == 1:
            return x_refs[0][...]
        x = jnp.where(t == 0, x_refs[2][...],
                      jnp.where(t < N_BLOCKS, x_refs[0][...].reshape(ROW_TILE, D_MODEL), x_refs[1][...]))
        xo_ref[...] = x
        return x

    @pl.when((t >= 0) & (t < N_BLOCKS))
    def _():
        _project_tile(load_x(), vec_ref, wbf_ref, wg2s_ref, z_ref, pf_ref, pb_ref)
        _attention_tile(t, layer, sinks_ref, pf_ref, pb_ref, m_ref, kprev_ref, vprev_ref)
        _gla_tile(t, _gla_decays(pf_ref, tri_ref, lev_ref), pf_ref, pb_ref, vec_ref, plev_ref, m_ref,
                  state_ref, sbd_ref)

    @pl.when(t == N_BLOCKS - 1)
    def _():
        own = 0 if chained else layer
        if not chained:
            for ref in (s_out_ref, k_out_ref, v_out_ref):
                ref[...] = jnp.zeros_like(ref)
        s_out_ref[own] = state_ref[...]
        for b in range(BATCH):
            k_out_ref[own, b] = pf_ref[b * BLOCK:(b + 1) * BLOCK, PF_K:PF_V].T
            v_out_ref[own, b] = pf_ref[b * BLOCK:(b + 1) * BLOCK, PF_V:PF_GQ].T

    @pl.when(t >= N_BLOCKS)
    def _():
        _project_tile(load_x(), vec_ref, wbf_ref, wg2s_ref, z_ref, pfs_ref, pbs_ref)
        m_ref[...] = jnp.zeros_like(m_ref)


def _front(layer, sinks, x_srcs, vecs, w_in, w_g2, ffn_f32, consts, prev):
    n_src = len(x_srcs)
    step = lambda i: i - N_PREP
    conv_of = lambda i: jnp.clip(step(i) - 1, 0, N_CONV - 1)
    tile_of = lambda i: jnp.where(step(i) <= 0, LEAD_TILE, jnp.minimum(step(i) - 1, LEAD_TILE - 1))
    sample_tile_of = lambda i: jnp.clip(step(i) - N_BLOCKS, 0, SAMPLE_TILES - 1)
    full = lambda a: pl.BlockSpec(a.shape, lambda i, *_: (0,) * a.ndim)
    if prev is None:
        per_batch = lambda r, c: pl.BlockSpec((DEPTH, BATCH, r, c), lambda i, *_: (0, 0, 0, 0))
    else:
        per_batch = lambda r, c: pl.BlockSpec((1, BATCH, r, c), lambda i, *_: (layer, 0, 0, 0))
    if n_src == 1:
        x_specs = [pl.BlockSpec((ROW_TILE, D_MODEL), lambda i, *_: (tile_of(i), 0))]
    else:
        x_specs = [
            pl.BlockSpec((BATCH, BLOCK, D_MODEL), lambda i, *_: (0, jnp.clip(step(i) - 1, 0, MAIN_TILES - 1), 0)),
            pl.BlockSpec((ROW_TILE, D_MODEL), lambda i, *_: (sample_tile_of(i), 0), pipeline_mode=pl.Buffered(1)),
            pl.BlockSpec((ROW_TILE, D_MODEL), lambda i, *_: (0, 0), pipeline_mode=pl.Buffered(1)),
        ]
    const_args = (consts["tri"], consts["lev_low"], consts["pair_level"])
    chained = prev is not None
    prev_args = tuple(prev) if chained else ()
    in_specs = ([pl.BlockSpec(memory_space=pl.ANY)] * len(prev_args) + x_specs + [
        pl.BlockSpec((V_ROWS, D_MODEL), lambda i, *_: (layer, 0)),
        pl.BlockSpec((1, IN_W, W_ROWS), lambda i, *_: (layer, 0, jnp.minimum(i, N_PREP - 1))),
        pl.BlockSpec((1, GATE_RANK, GLA_K_W), lambda i, *_: (layer, 0, 0)),
    ] + [full(a) for a in const_args] + [
        pl.BlockSpec((1, a.shape[1] // N_CONV, a.shape[2]), lambda i, *_: (layer, conv_of(i), 0)) for a in ffn_f32])
    out_specs = [pl.BlockSpec((ROW_TILE, D_MIX), lambda i, *_: (tile_of(i), 0)),
                 pl.BlockSpec((ROW_TILE, PF_W), lambda i, *_: (sample_tile_of(i), 0)),
                 pl.BlockSpec((ROW_TILE, PB_W), lambda i, *_: (sample_tile_of(i), 0)),
                 per_batch(GLA_K_W, GLA_DV), per_batch(BLOCK, ATT_KV_W), per_batch(BLOCK, ATT_KV_W)]
    out_specs += [pl.BlockSpec((a.shape[1] // N_CONV, a.shape[2]), lambda i, *_: (conv_of(i), 0)) for a in ffn_f32]
    out_shape = [jax.ShapeDtypeStruct((TOTAL_ROWS, D_MIX), BF16),
                 jax.ShapeDtypeStruct((SAMPLE_ROWS, PF_W), F32),
                 jax.ShapeDtypeStruct((SAMPLE_ROWS, PB_W), BF16),
                 jax.ShapeDtypeStruct((DEPTH, BATCH, GLA_K_W, GLA_DV), F32),
                 jax.ShapeDtypeStruct((DEPTH, BATCH, BLOCK, ATT_KV_W), F32),
                 jax.ShapeDtypeStruct((DEPTH, BATCH, BLOCK, ATT_KV_W), F32)]
    out_shape += [jax.ShapeDtypeStruct(a.shape[1:], BF16) for a in ffn_f32]
    n_lead_out = 0
    if n_src > 1:
        out_specs = [pl.BlockSpec((ROW_TILE, D_MODEL), lambda i, *_: (tile_of(i), 0))] + out_specs
        out_shape = [jax.ShapeDtypeStruct((TOTAL_ROWS, D_MODEL), F32)] + out_shape
        n_lead_out = 1
    grid_spec = pltpu.PrefetchScalarGridSpec(
        num_scalar_prefetch=1,
        grid=(N_PREP + N_BLOCKS + SAMPLE_TILES,),
        in_specs=in_specs,
        out_specs=out_specs,
        scratch_shapes=[pltpu.VMEM((D_MODEL, PROJ_W), BF16), pltpu.VMEM((LANES, GLA_K_W), BF16),
                        pltpu.VMEM((ROW_TILE, PROJ_W), F32),
                        pltpu.VMEM((ROW_TILE, PF_W), F32), pltpu.VMEM((ROW_TILE, PB_W), BF16),
                        pltpu.VMEM((BATCH * N_KV_HEADS, BLOCK, ATT_KV_W), BF16),
                        pltpu.VMEM((BATCH * N_KV_HEADS, BLOCK, ATT_KV_W), BF16),
                        pltpu.VMEM((BATCH, GLA_K_W, GLA_DV), F32),
                        pltpu.VMEM((BATCH * GLA_HEADS // 2, 2 * GLA_DK, 2 * GLA_DV), BF16)],
    )
    res = pl.pallas_call(
        functools.partial(_front_kernel, layer=layer, n_src=n_src, chained=chained),
        grid_spec=grid_spec,
        out_shape=out_shape,
        input_output_aliases={1 + n: n_lead_out + 3 + n for n in range(len(prev_args))},
        compiler_params=pltpu.CompilerParams(dimension_semantics=("arbitrary",),
                                             vmem_limit_bytes=VMEM_LIMIT),
        name="front",
    )(sinks, *prev_args, *x_srcs, vecs, w_in, w_g2, *const_args, *ffn_f32)
    x_all = res[0] if n_src > 1 else x_srcs[0]
    merged, pfs, pbs, ps, pk, pv = res[n_lead_out:n_lead_out + 6]
    return x_all, merged, (pfs, pbs), (ps, pk, pv), tuple(res[n_lead_out + 6:])


def _mix_sample_kernel(*refs, layer, n_alias):
    sinks_ref, refs = refs[0], refs[n_alias + 1:]
    (pf_ref, pb_ref, ck_ref, cv_ref, st_ref,
     vec_ref, tri_ref, sones_ref, lev_ref, ones_ref, plev_ref,
     m_ref, ck_out_ref, cv_out_ref, st_out_ref) = refs
    q_ref, k_ref, v_ref, gq_ref, gk_ref, ld_ref, gv_ref, og_ref = _proj_views(pf_ref, pb_ref)
    gq = gq_ref[...]
    gk = gk_ref[...]
    gv = gv_ref[...]
    gvf = gv.astype(F32)
    hi, lo = _split(ld_ref[...])
    tri = tri_ref[...]
    g_cum = _dot(tri, hi) + _dot(tri, lo)
    sones = sones_ref[...]
    g_tot = _dot(sones, hi) + _dot(sones, lo)
    lev = lev_ref[...]
    level_sums = _dot(lev, hi) + _dot(lev, lo)
    a = _gla_intra([gq], [gk], lambda b, l: jnp.exp(level_sums[l * BLOCK:(l + 1) * BLOCK]),
                   range(LOW_LEVELS), plev_ref[...])[0]
    o_intra = jnp.concatenate(
        [_dot(a[h].astype(BF16), gv[:, h * GLA_DV:(h + 1) * GLA_DV]) for h in range(GLA_HEADS)], axis=1)
    q_dec = gq * jnp.exp(g_cum)
    k_dec = gk * jnp.exp(g_tot - g_cum)
    hm8 = _head_masks(DEC_SEQ)
    ones8 = ones_ref[...]
    hi_f, lo_f = hi.astype(F32), lo.astype(F32)

    qf = q_ref[...].astype(F32)
    kf = k_ref[...]
    vf = v_ref[...]
    low8 = lax.broadcasted_iota(jnp.int32, (DEC_SEQ, LANES), 1) < HEAD_DIM
    nkeys = WINDOW + DEC_SEQ
    srow = lax.broadcasted_iota(jnp.int32, (N_HEADS * DEC_SEQ, nkeys), 0)
    scol = lax.broadcasted_iota(jnp.int32, (N_HEADS * DEC_SEQ, nkeys), 1)
    t_of_row = srow & (DEC_SEQ - 1)
    amask = ((scol < WINDOW) & (scol > t_of_row)) | ((scol >= WINDOW) & (scol - WINDOW <= t_of_row))
    rid = lax.broadcasted_iota(jnp.int32, (N_HEADS * DEC_SEQ, 1), 0) >> LOW_LEVELS
    sink_col = jnp.zeros((N_HEADS * DEC_SEQ, 1), F32)
    for i in range(N_HEADS):
        sink_col = jnp.where(rid == i, sinks_ref[layer, i], sink_col)

    seqs = range(SEQ_GROUP)
    rows = [slice(b * DEC_SEQ, (b + 1) * DEC_SEQ) for b in seqs]

    def stacked_queries(b):
        pieces = []
        for j in range(N_HEADS // 2):
            g = (2 * j) // (N_HEADS // N_KV_HEADS)
            qt = qf[rows[b], j * LANES:(j + 1) * LANES]
            swapped = pltpu.roll(qt, HEAD_DIM, axis=1)
            own_lanes = low8 if g == 0 else ~low8
            even, odd = (qt, swapped) if g == 0 else (swapped, qt)
            pieces += [jnp.where(own_lanes, even, 0.0), jnp.where(own_lanes, odd, 0.0)]
        return jnp.concatenate(pieces, axis=0).astype(BF16)

    lane = lax.broadcasted_iota(jnp.int32, (WINDOW, WINDOW), 1)
    old_lanes = lane < WINDOW - DEC_SEQ

    def shifted_buffer(buf, new_rows):
        wide = jnp.concatenate([jnp.zeros((WINDOW - DEC_SEQ, LANES), F32), new_rows], axis=0)
        return jnp.where(old_lanes, pltpu.roll(buf, WINDOW - DEC_SEQ, axis=1), wide.T)

    qp = [stacked_queries(b) for b in seqs]
    ck = [ck_ref[0, b] for b in seqs]
    cv = [cv_ref[0, b] for b in seqs]
    for b in seqs:
        ck_out_ref[0, b] = shifted_buffer(ck[b], kf[rows[b]])
        cv_out_ref[0, b] = shifted_buffer(cv[b], vf[rows[b]])
    s = [jnp.concatenate([_dot(qp[b], ck[b].astype(BF16)), _dot_nt(qp[b], kf[rows[b]].astype(BF16))], axis=1)
         for b in seqs]
    s = [jnp.where(amask, s[b], MASK_VALUE) for b in seqs]
    m = [jnp.maximum(jnp.max(s[b], axis=-1, keepdims=True), sink_col) for b in seqs]
    p = [jnp.exp(s[b] - m[b]) for b in seqs]
    denom = [jnp.sum(p[b], axis=-1, keepdims=True) + jnp.exp(sink_col - m[b]) for b in seqs]
    pb = [p[b].astype(BF16) for b in seqs]
    ob = [(_dot_nt(pb[b][:, :WINDOW], cv[b].astype(BF16))
           + _dot(pb[b][:, WINDOW:], vf[rows[b]].astype(BF16))) / denom[b] for b in seqs]

    def head_tiles(o):
        tiles = []
        for j in range(N_HEADS // 2):
            g = (2 * j) // (N_HEADS // N_KV_HEADS)
            even = o[(2 * j) * DEC_SEQ:(2 * j + 1) * DEC_SEQ]
            odd = o[(2 * j + 1) * DEC_SEQ:(2 * j + 2) * DEC_SEQ]
            if g == 0:
                tiles.append(jnp.where(low8, even, pltpu.roll(odd, HEAD_DIM, axis=1)))
            else:
                tiles.append(jnp.where(low8, pltpu.roll(even, HEAD_DIM, axis=1), odd))
        return jnp.concatenate(tiles, axis=1)

    att_rows = [head_tiles(ob[b]) for b in seqs]

    head_stack = lambda x: jnp.concatenate([jnp.where(mk, x, 0.0) for mk in hm8], axis=0).astype(BF16)
    state = [st_ref[0, b] for b in seqs]
    oi = [_dot(head_stack(q_dec[rows[b]]), state[b].astype(BF16)) for b in seqs]
    inter_rows = [jnp.concatenate([oi[b][h * DEC_SEQ:(h + 1) * DEC_SEQ] for h in range(GLA_HEADS)], axis=1)
                  for b in seqs]
    vstack = [jnp.concatenate([gvf[rows[b]][:, h * GLA_DV:(h + 1) * GLA_DV] for h in range(GLA_HEADS)],
                              axis=0).astype(BF16) for b in seqs]
    kv = [_dot_tn(head_stack(k_dec[rows[b]]), vstack[b]) for b in seqs]
    decay_col = [jnp.exp(_dot_tn(hi_f[rows[b]], ones8) + _dot_tn(lo_f[rows[b]], ones8)) for b in seqs]
    for b in seqs:
        st_out_ref[0, b] = decay_col[b] * state[b] + kv[b]

    m_ref[:, :ATT_Q_W] = jnp.concatenate(att_rows, axis=0).astype(BF16)
    o = o_intra + jnp.concatenate(inter_rows, axis=0)
    gn = vec_ref[V_GN:V_GN + 1, :GLA_DV]
    for h, gh in enumerate(_gla_merge(o, og_ref[...], gn)):
        m_ref[:, ATT_Q_W + h * GLA_DV:ATT_Q_W + (h + 1) * GLA_DV] = gh.astype(BF16)


def _mix_sample(layer, sinks, merged, proj, cache_k, cache_v, state, vecs, consts, prev):
    tok = lambda w_: pl.BlockSpec((BLOCK, w_), lambda i, *_: (i, 0))
    merged_blk = pl.BlockSpec((BLOCK, D_MIX), lambda i, *_: (MAIN_ROWS // BLOCK + i, 0))
    full = lambda a: pl.BlockSpec(a.shape, lambda i, *_: (0,) * a.ndim)
    seq = lambda a: pl.BlockSpec((1, SEQ_GROUP) + a.shape[2:], lambda i, *_: (layer, i, 0, 0))
    ones8 = jnp.ones((DEC_SEQ, LANES), F32)
    const_args = (consts["seq_tri"], consts["seq_ones"], consts["lev_low"], ones8, consts["pair_level"])
    seq_args = (cache_k, cache_v, state)
    alias_args = (merged,) + (tuple(prev) if prev is not None else ())
    grid_spec = pltpu.PrefetchScalarGridSpec(
        num_scalar_prefetch=1,
        grid=(DEC_BATCH // SEQ_GROUP,),
        in_specs=([pl.BlockSpec(memory_space=pl.ANY)] * len(alias_args)
                  + [tok(a.shape[1]) for a in proj] + [seq(a) for a in seq_args]
                  + [pl.BlockSpec((V_ROWS, D_MODEL), lambda i, *_: (layer, 0))]
                  + [full(a) for a in const_args]),
        out_specs=[merged_blk] + [seq(a) for a in seq_args],
    )
    return pl.pallas_call(
        functools.partial(_mix_sample_kernel, layer=layer, n_alias=len(alias_args)),
        grid_spec=grid_spec,
        out_shape=[jax.ShapeDtypeStruct(merged.shape, merged.dtype)]
        + [jax.ShapeDtypeStruct(a.shape, a.dtype) for a in seq_args],
        input_output_aliases={1 + n: n for n in range(len(alias_args))},
        compiler_params=pltpu.CompilerParams(dimension_semantics=("arbitrary",),
                                             vmem_limit_bytes=VMEM_LIMIT),
        name="mix_sample",
    )(sinks, *alias_args, *proj, *seq_args, vecs, *const_args)


def _out_ffn_kernel(*refs, last):
    x_ref, m_ref, vec_ref, wo_ref, wg_ref, wu_ref, wd_ref = refs[:7]
    out_refs = refs[7:9] if last else refs[7:8]
    act_ref = refs[-1]
    x1 = x_ref[...] + _dot(m_ref[...], wo_ref[...])
    ms = jnp.mean(x1 * x1, axis=-1, keepdims=True)
    h = (x1 * lax.rsqrt(ms + RMS_EPS) * vec_ref[V_NORM2:V_NORM2 + 1, :]).astype(BF16)
    for c in range(D_FFN // FFN_CHUNK):
        cols = slice(c * FFN_CHUNK, (c + 1) * FFN_CHUNK)
        gate = _dot(h, wg_ref[:, cols])
        up = _dot(h, wu_ref[:, cols])
        act_ref[:, cols] = (gate * _sigmoid(gate) * up).astype(BF16)
    y = x1 + _dot(act_ref[...], wd_ref[...])
    if last:
        out_refs[0][...] = y.reshape(BATCH, BLOCK, D_MODEL)

        @pl.when(pl.program_id(0) < SAMPLE_TILES)
        def _():
            out_refs[1][...] = out_refs[0][...].reshape(ROW_TILE, D_MODEL)
    else:
        out_refs[0][...] = y


def _out_ffn(layer, x_all, merged, vecs, ffn_weights, last):
    resident = lambda a: pl.BlockSpec(a.shape, lambda i: (0, 0), pipeline_mode=pl.Buffered(1))
    if last:
        n_tiles = MAIN_TILES + SAMPLE_TILES
        tile_of = lambda i: jnp.where(i < SAMPLE_TILES, MAIN_TILES + i, i - SAMPLE_TILES)
        out_specs = [pl.BlockSpec((BATCH, BLOCK, D_MODEL), lambda i: (0, jnp.maximum(i - SAMPLE_TILES, 0), 0)),
                     pl.BlockSpec((ROW_TILE, D_MODEL), lambda i: (jnp.minimum(i, SAMPLE_TILES - 1), 0))]
        out_shape = [jax.ShapeDtypeStruct((BATCH, SEQ, D_MODEL), F32),
                     jax.ShapeDtypeStruct((SAMPLE_ROWS, D_MODEL), F32)]
    else:
        n_tiles = N_TILES
        tile_of = lambda i: i
        out_specs = [pl.BlockSpec((ROW_TILE, D_MODEL), lambda i: (i, 0))]
        out_shape = [jax.ShapeDtypeStruct((TOTAL_ROWS, D_MODEL), F32)]
    tile = lambda w_: pl.BlockSpec((ROW_TILE, w_), lambda i: (tile_of(i), 0))
    return pl.pallas_call(
        functools.partial(_out_ffn_kernel, last=last),
        grid=(n_tiles,),
        in_specs=[tile(D_MODEL), tile(D_MIX), pl.BlockSpec((V_ROWS, D_MODEL), lambda i: (layer, 0))]
        + [resident(w) for w in ffn_weights],
        out_specs=out_specs,
        out_shape=out_shape,
        scratch_shapes=[pltpu.VMEM((ROW_TILE, D_FFN), BF16)],
        compiler_params=pltpu.CompilerParams(dimension_semantics=("arbitrary",),
                                             vmem_limit_bytes=VMEM_LIMIT),
        name="out_ffn",
    )(x_all, merged, vecs, *ffn_weights)


def _vector_slab(norm1, norm2, q_norm, k_norm, b_g, gla_norm):
    pad = lambda a: jnp.pad(a.astype(F32), ((0, 0), (0, D_MODEL - a.shape[1])))
    rows = [norm1.astype(F32), norm2.astype(F32), pad(jnp.tile(q_norm, (1, N_HEADS)) * ATT_SCALE),
            pad(jnp.tile(k_norm, (1, N_KV_HEADS))), pad(b_g), pad(gla_norm)]
    rows += [jnp.zeros((DEPTH, D_MODEL), F32)] * (V_ROWS - len(rows))
    return jnp.stack(rows, axis=1).reshape(DEPTH * V_ROWS, D_MODEL)


def kernel(x_prompt, x_sample, cache_k, cache_v, state_gla, meta, norm1, w_in, q_norm, k_norm, sinks,
           w_g2, b_g, gla_norm, w_o, norm2, w_gate, w_up, w_down):
    consts = _constants()
    dt = x_prompt.dtype
    vecs = _vector_slab(norm1, norm2, q_norm, k_norm, b_g, gla_norm)
    sinks = sinks.astype(F32)
    lead = jnp.tile(jnp.concatenate([jnp.zeros((PAD_LEN, D_MODEL), dt), meta.astype(dt)], axis=0), (BATCH, 1))
    x_srcs = (x_prompt, x_sample.reshape(SAMPLE_ROWS, D_MODEL), lead)
    to_feature_major = lambda c: c.transpose(0, 1, 3, 4, 2).reshape(DEPTH, DEC_BATCH, ATT_KV_W, WINDOW)
    ck_in, cv_in = to_feature_major(cache_k), to_feature_major(cache_v)
    st_in = state_gla.reshape(DEPTH, DEC_BATCH, GLA_K_W, GLA_DV)
    w_in_t = jnp.swapaxes(w_in, 1, 2)

    prompt_outs, sample_outs = None, None
    for l in range(DEPTH):
        x_all, merged, proj_sample, prompt_outs, ffn_bf16 = _front(
            l, sinks, x_srcs, vecs, w_in_t, w_g2, (w_o, w_gate, w_up, w_down), consts, prompt_outs)
        merged, *sample_outs = _mix_sample(l, sinks, merged, proj_sample, ck_in, cv_in, st_in, vecs, consts,
                                           sample_outs)
        x_srcs = tuple(_out_ffn(l, x_all, merged, vecs, ffn_bf16, last=l == DEPTH - 1))

    y_main, y_sample = x_srcs
    ps, pk, pv = prompt_outs
    sk, sv, ss = sample_outs
    kv5 = lambda a, n: a.reshape(DEPTH, n, N_KV_HEADS, HEAD_DIM, WINDOW).transpose(0, 1, 4, 2, 3)
    st5 = lambda a, n: a.reshape(DEPTH, n, GLA_HEADS, GLA_DK, GLA_DV)
    return (y_main, y_sample.reshape(DEC_BATCH, DEC_SEQ, D_MODEL),
            kv5(pk, BATCH), kv5(pv, BATCH), st5(ps, BATCH), kv5(sk, DEC_BATCH), kv5(sv, DEC_BATCH),
            st5(ss, DEC_BATCH))
```

```python
import functools

import jax
import jax.numpy as jnp
import numpy as np
from jax import lax
from jax.experimental import pallas as pl
from jax.experimental.pallas import tpu as pltpu

F32 = jnp.float32
BF16 = jnp.bfloat16

D_MODEL = 1024
BATCH = 4
SEQ = 4096
DEPTH = 2
DEC_BATCH = 128
DEC_SEQ = 8
N_META = 16
WINDOW = 128
BLOCK = 128
PAD_LEN = BLOCK - N_META
N_HEADS = 8
N_KV_HEADS = 2
HEAD_DIM = 64
ATT_SCALE = HEAD_DIM ** -0.5
ATT_Q_W = N_HEADS * HEAD_DIM
ATT_KV_W = N_KV_HEADS * HEAD_DIM
GLA_HEADS = 4
GLA_DK = 64
GLA_DV = 128
GLA_K_W = GLA_HEADS * GLA_DK
GLA_V_W = GLA_HEADS * GLA_DV
GATE_RANK = 16
GATE_NORMALIZER = 16.0
D_MIX = ATT_Q_W + GLA_V_W
D_FFN = 2816
IN_W = 2320
RMS_EPS = 1e-6
MASK_VALUE = -1e30

LANES = 128
N_BLOCKS = 1 + SEQ // BLOCK
MAIN_ROWS = BATCH * SEQ
SAMPLE_ROWS = DEC_BATCH * DEC_SEQ
LEAD_ROWS = BATCH * BLOCK
TOTAL_ROWS = MAIN_ROWS + SAMPLE_ROWS + LEAD_ROWS
ROW_TILE = 512
MAIN_TILES = MAIN_ROWS // ROW_TILE
SAMPLE_TILES = SAMPLE_ROWS // ROW_TILE
N_TILES = TOTAL_ROWS // ROW_TILE
LEAD_TILE = N_TILES - 1
SEQ_GROUP = BLOCK // DEC_SEQ
N_LEVELS = 7
LOW_LEVELS = 3
LOG_DK = 6
N_PREP = 8
W_ROWS = D_MODEL // N_PREP
N_CONV = 16

SRC_LOW, SRC_OG = 1792, 1808
OFF_Q, OFF_K, OFF_V = 0, 512, 640
OFF_GQ, OFF_GK, OFF_GV, OFF_OG, OFF_LOW = 768, 1024, 1280, 1792, 2304
PROJ_W = OFF_LOW + LANES
PROJ_CHUNK = 512
FFN_CHUNK = 256
NORM_PIECES = 4
SCORE_LOOKAHEAD = 1
VMEM_LIMIT = 58 * 1024 * 1024

V_NORM1, V_NORM2, V_QG, V_KG, V_BG, V_GN, V_ROWS = 0, 1, 2, 3, 4, 5, 8

PF_K, PF_V, PF_GQ, PF_GK, PF_LD, PF_OG, PF_W = 0, 128, 256, 512, 768, 1024, 1536
PB_Q, PB_GV, PB_W = 0, 512, 1024


def _proj_views(pf_ref, pb_ref):
    f = lambda a, b: pf_ref.at[:, a:b]
    return (pb_ref.at[:, PB_Q:PB_GV], f(PF_K, PF_V), f(PF_V, PF_GQ), f(PF_GQ, PF_GK), f(PF_GK, PF_LD),
            f(PF_LD, PF_OG), pb_ref.at[:, PB_GV:PB_W], f(PF_OG, PF_W))


def _dot(a, b):
    return jnp.dot(a, b, preferred_element_type=F32)


def _dot_nt(a, b):
    return lax.dot_general(a, b, (((1,), (1,)), ((), ())), preferred_element_type=F32)


def _dot_tn(a, b):
    return lax.dot_general(a, b, (((0,), (0,)), ((), ())), preferred_element_type=F32)


def _split(x):
    hi = x.astype(BF16)
    lo = (x - hi.astype(F32)).astype(BF16)
    return hi, lo


def _sigmoid(x):
    return 1.0 / (1.0 + jnp.exp(-x))


def _level_matrix(levels, n=BLOCK):
    out = np.zeros((len(levels) * n, n), np.float32)
    for i, l in enumerate(levels):
        size = 2 << l
        for t in range(n):
            mid = (t // size) * size + size // 2 - 1
            if (t >> l) & 1:
                out[i * n + t, mid + 1:t + 1] = 1.0
            else:
                out[i * n + t, t + 1:mid + 1] = 1.0
    return out


def _constants():
    r = np.arange(BLOCK)
    tri = (r[None, :] <= r[:, None]).astype(np.float32)
    same_seq = (r[None, :] // DEC_SEQ) == (r[:, None] // DEC_SEQ)
    diff = np.maximum(r[:, None] ^ r[None, :], 1)
    pair_level = np.where(r[None, :] < r[:, None], np.floor(np.log2(diff)).astype(np.int32),
                          np.where(r[None, :] == r[:, None], N_LEVELS, N_LEVELS + 1)).astype(np.int32)
    return dict(
        tri=jnp.asarray(tri, BF16),
        seq_tri=jnp.asarray(tri * same_seq, BF16),
        seq_ones=jnp.asarray(same_seq.astype(np.float32), BF16),
        lev_low=jnp.asarray(_level_matrix(range(LOW_LEVELS)), BF16),
        pair_level=jnp.asarray(pair_level),
    )


def _head_masks(rows):
    lane = lax.broadcasted_iota(jnp.int32, (rows, GLA_K_W), 1)
    return [(lane >> LOG_DK) == h for h in range(GLA_HEADS)]


def _gla_intra(gq, gk, level_decay, levels, pair_level):
    rows = gq[0].shape[0]
    streams = range(len(gq))
    low_half = lax.broadcasted_iota(jnp.int32, (rows, LANES), 1) < GLA_DK

    def pair_products(qh, kh):
        qh, kh = qh.astype(BF16), kh.astype(BF16)
        out = []
        for pair in range(GLA_HEADS // 2):
            kl = kh[:, pair * LANES:(pair + 1) * LANES]
            zero = jnp.zeros_like(kl)
            stacked = jnp.concatenate([jnp.where(low_half, kl, zero), jnp.where(low_half, zero, kl)], axis=0)
            p = _dot_nt(qh[:, pair * LANES:(pair + 1) * LANES], stacked)
            out += [p[:, :rows], p[:, rows:]]
        return out

    row = lax.broadcasted_iota(jnp.int32, (rows, 1), 0)

    def upper_q_lower_k(b, l):
        size = 1 << l
        if size % 8:
            return jnp.where(((row >> l) & 1) == 1, gq[b], gk[b])
        return jnp.concatenate([(gq[b] if i & 1 else gk[b])[i * size:(i + 1) * size]
                                for i in range(rows // size)], axis=0)

    def diagonal(b):
        out = []
        for pair in range(GLA_HEADS // 2):
            prod = (gq[b] * gk[b])[:, pair * LANES:(pair + 1) * LANES]
            out += [jnp.sum(jnp.where(low_half, prod, 0.0), axis=-1, keepdims=True),
                    jnp.sum(jnp.where(low_half, 0.0, prod), axis=-1, keepdims=True)]
        return out

    def blocks(v, size, parity):
        return jnp.concatenate([v[i * size:(i + 1) * size] for i in range(parity, rows // size, 2)], axis=0)

    def interleave(lower, upper, size):
        pieces = []
        for i in range(rows // (2 * size)):
            pieces += [lower[i * size:(i + 1) * size], upper[i * size:(i + 1) * size]]
        return jnp.concatenate(pieces, axis=0)

    on_diag = pair_level == N_LEVELS
    a = [[jnp.where(on_diag, d, 0.0).astype(BF16) for d in diagonal(b)] for b in streams]
    for l in levels:
        at_level = pair_level == l
        size = 1 << l
        for b in streams:
            x = upper_q_lower_k(b, l) * level_decay(b, l)
            if size % 16:
                a[b] = [jnp.where(at_level, p.astype(BF16), a_h) for p, a_h in zip(pair_products(x, x), a[b])]
            else:
                mask_up = blocks(pair_level, size, 1) == l
                p_up = pair_products(blocks(x, size, 1), x)
                a[b] = [interleave(blocks(a_h, size, 0),
                                   jnp.where(mask_up, p.astype(BF16), blocks(a_h, size, 1)), size)
                        for p, a_h in zip(p_up, a[b])]
    return a


def _gla_merge(o, og, gn):
    outs = []
    for h in range(GLA_HEADS):
        oh = o[:, h * GLA_DV:(h + 1) * GLA_DV]
        gh = og[:, h * GLA_DV:(h + 1) * GLA_DV]
        ms = jnp.mean(oh * oh, axis=-1, keepdims=True)
        outs.append(oh * lax.rsqrt(ms + RMS_EPS) * gn * (gh * _sigmoid(gh)))
    return outs


def _dup_halves(x):
    low = lax.broadcasted_iota(jnp.int32, x.shape, 1) < HEAD_DIM
    rolled = pltpu.roll(x, HEAD_DIM, axis=1)
    return jnp.where(low, x, rolled), jnp.where(low, rolled, x)


def _convert_proj_weights(i, w_ref, wg2_ref, wbf_ref, wg2s_ref):
    rows = pl.ds(pl.multiple_of(i * W_ROWS, W_ROWS), W_ROWS)

    def put(dst, src, n=LANES):
        tile = w_ref[0, src:src + n, :]
        if n < LANES:
            tile = jnp.concatenate([tile, jnp.zeros((LANES - n, W_ROWS), F32)], axis=0)
        wbf_ref[rows, dst:dst + LANES] = tile.T.astype(BF16)

    for c in range(SRC_LOW // LANES):
        put(c * LANES, c * LANES)
    for c in range((IN_W - SRC_OG) // LANES):
        put(OFF_OG + c * LANES, SRC_OG + c * LANES)
    put(OFF_LOW, SRC_LOW, GATE_RANK)

    @pl.when(i == 0)
    def _():
        wg2s_ref[...] = jnp.concatenate(
            [wg2_ref[0], jnp.zeros((LANES - GATE_RANK, GLA_K_W), F32)], axis=0).astype(BF16)


def _project_tile(x, vec_ref, wbf_ref, wg2s_ref, z_ref, pf_ref, pb_ref):
    q_ref, k_ref, v_ref, gq_ref, gk_ref, ld_ref, gv_ref, og_ref = _proj_views(pf_ref, pb_ref)
    ms = jnp.mean(x * x, axis=-1, keepdims=True)
    h = (x * lax.rsqrt(ms + RMS_EPS) * vec_ref[V_NORM1:V_NORM1 + 1, :]).astype(BF16)

    low_half = lax.broadcasted_iota(jnp.int32, (ROW_TILE, LANES), 1) < HEAD_DIM

    def head_norm(t):
        t2 = t * t
        ms_lo = jnp.sum(jnp.where(low_half, t2, 0.0), axis=-1, keepdims=True) * (1.0 / HEAD_DIM)
        ms_hi = jnp.sum(jnp.where(low_half, 0.0, t2), axis=-1, keepdims=True) * (1.0 / HEAD_DIM)
        return t * jnp.where(low_half, lax.rsqrt(ms_lo + RMS_EPS), lax.rsqrt(ms_hi + RMS_EPS))

    def q_part():
        for j in range(ATT_Q_W // LANES):
            cols = slice(j * LANES, (j + 1) * LANES)
            q_ref[:, cols] = (head_norm(z_ref[:, cols]) * vec_ref[V_QG:V_QG + 1, cols]).astype(BF16)

    def kv_part():
        k_ref[...] = head_norm(z_ref[:, OFF_K:OFF_V]) * vec_ref[V_KG:V_KG + 1, :ATT_KV_W]
        v_ref[...] = z_ref[:, OFF_V:OFF_GQ]
        gq_ref[...] = z_ref[:, OFF_GQ:OFF_GK] * (GLA_DK ** -0.5)

    def gk_part():
        gk_ref[...] = z_ref[:, OFF_GK:OFF_GV]

    def gv_part():
        gv_ref[...] = z_ref[:, OFF_GV:OFF_OG].astype(BF16)
        og_ref[...] = z_ref[:, OFF_OG:OFF_LOW]

    def gate_part():
        logit =(_dot(z_ref[:, OFF_LOW:PROJ_W].astype(BF16), wg2s_ref[...])
                 + vec_ref[V_BG:V_BG + 1, :GLA_K_W])
        log_sig = jnp.minimum(logit, 0.0) - jnp.log1p(jnp.exp(-jnp.abs(logit)))
        ld_ref[...] = log_sig * (1.0 / GATE_NORMALIZER)

    pieces = [q_part, kv_part, gk_part, gv_part, gate_part]
    for c in [len(pieces) - 1] + list(range(len(pieces) - 1)):
        cols = slice(c * PROJ_CHUNK, min((c + 1) * PROJ_CHUNK, PROJ_W))
        z_ref[:, cols] = _dot(h, wbf_ref[:, cols])
        pieces[c]()


def _attention_tile(blk, layer, sinks_ref, pf_ref, pb_ref, m_ref, kprev_ref, vprev_ref):
    q_ref, k_ref, v_ref = _proj_views(pf_ref, pb_ref)[:3]
    per_group = N_HEADS // N_KV_HEADS
    grows = per_group * BLOCK
    row = lax.broadcasted_iota(jnp.int32, (grows, BLOCK), 0) & (BLOCK - 1)
    col = lax.broadcasted_iota(jnp.int32, (grows, BLOCK), 1)
    own = col <= row
    kpos = jnp.where(own, blk * BLOCK, (blk - 1) * BLOCK) + col - PAD_LEN
    live = kpos >= 0
    low_half = lax.broadcasted_iota(jnp.int32, (BLOCK, LANES), 1) < HEAD_DIM
    head_of_row = lax.broadcasted_iota(jnp.int32, (grows, 1), 0) >> N_LEVELS
    sink_cols = []
    for g in range(N_KV_HEADS):
        sink = jnp.zeros((grows, 1), F32)
        for r in range(per_group):
            sink = jnp.where(head_of_row == r, sinks_ref[layer, g * per_group + r], sink)
        sink_cols.append(sink)

    units =[(b, g) for b in range(BATCH) for g in range(N_KV_HEADS)]
    rows = [slice(b * BLOCK, (b + 1) * BLOCK) for b in range(BATCH)]
    tiles = [range(g * per_group // 2, (g + 1) * per_group // 2) for g in range(N_KV_HEADS)]
    kdup = [_dup_halves(k_ref[rows[b], :].astype(BF16)) for b in range(BATCH)]
    vdup = [_dup_halves(v_ref[rows[b], :].astype(BF16)) for b in range(BATCH)]
    kk = [jnp.concatenate([kprev_ref[b * N_KV_HEADS + g], kdup[b][g]], axis=0) for b, g in units]
    vv = [jnp.concatenate([vprev_ref[b * N_KV_HEADS + g], vdup[b][g]], axis=0) for b, g in units]
    for b, g in units:
        kprev_ref[b * N_KV_HEADS + g] = kdup[b][g]
        vprev_ref[b * N_KV_HEADS + g] = vdup[b][g]

    def stacked_queries(b, g):
        pieces = []
        for j in tiles[g]:
            qt = q_ref[rows[b], j * LANES:(j + 1) * LANES]
            zero = jnp.zeros_like(qt)
            pieces += [jnp.where(low_half, qt, zero), jnp.where(low_half, zero, qt)]
        return jnp.concatenate(pieces, axis=0)

    def scores(u):
        b, g = units[u]
        return _dot_nt(stacked_queries(b, g), kk[u])

    def attend(u, s):
        b, g = units[u]
        s = jnp.where(live, jnp.where(own, s[:, BLOCK:], s[:, :BLOCK]), MASK_VALUE)
        m = jnp.maximum(jnp.max(s, axis=-1, keepdims=True), sink_cols[g])
        p = jnp.exp(s - m)
        denom = jnp.sum(p, axis=-1, keepdims=True) + jnp.exp(sink_cols[g] - m)
        p2 = jnp.concatenate([jnp.where(own, 0.0, p), jnp.where(own, p, 0.0)], axis=1).astype(BF16)
        o = _dot(p2, vv[u]) / denom
        for n, j in enumerate(tiles[g]):
            o_lo = o[2 * n * BLOCK:(2 * n + 1) * BLOCK]
            o_hi = o[(2 * n + 1) * BLOCK:(2 * n + 2) * BLOCK]
            m_ref[rows[b], j * LANES:(j + 1) * LANES] = jnp.where(low_half, o_lo, o_hi).astype(BF16)

    pending = [scores(u) for u in range(SCORE_LOOKAHEAD)]
    for u in range(len(units)):
        if u + SCORE_LOOKAHEAD < len(units):
            pending.append(scores(u + SCORE_LOOKAHEAD))
        attend(u, pending.pop(0))


def _gla_decays(pf_ref, tri_ref, lev_ref):
    ld_ref = pf_ref.at[:, PF_LD:PF_OG]
    ld_all = jnp.concatenate([ld_ref[b * BLOCK:(b + 1) * BLOCK, :] for b in range(BATCH)], axis=1)
    hi, lo = _split(ld_all)
    tri = tri_ref[...]
    g_cum_all = _dot(tri, hi) + _dot(tri, lo)
    lev = lev_ref[...]
    low_sums = _dot(lev, hi) + _dot(lev, lo)

    def level_decay(l, cols):
        if l < LOW_LEVELS:
            return jnp.exp(low_sums[l * BLOCK:(l + 1) * BLOCK, cols])
        half = 1 << l
        g = g_cum_all[:, cols]
        pieces = []
        for p in range(BLOCK // (2 * half)):
            mid = g[p * 2 * half + half - 1:p * 2 * half + half, :]
            pieces += [mid - g[p * 2 * half:p * 2 * half + half], g[p * 2 * half + half:(p + 1) * 2 * half] - mid]
        return jnp.exp(jnp.concatenate(pieces, axis=0))

    g_last_all = jnp.broadcast_to(g_cum_all[BLOCK - 1:BLOCK, :], g_cum_all.shape)
    decay_col_all = jnp.exp(jnp.concatenate(
        [g_last_all[:, c * LANES:(c + 1) * LANES].T for c in range(BATCH * GLA_K_W // LANES)], axis=0))
    return g_cum_all, level_decay, decay_col_all


def _gla_tile(blk, decays, pf_ref, pb_ref, vec_ref, plev_ref, m_ref, state_ref, sbd_ref):
    g_cum_all, level_decay, decay_col_all = decays
    gq_ref, gk_ref, _, gv_ref, og_ref = _proj_views(pf_ref, pb_ref)[3:]
    rpos = blk * BLOCK + lax.broadcasted_iota(jnp.int32, (BLOCK, 1), 0) - PAD_LEN
    valid = (rpos >= 0).astype(F32)
    pair_level = plev_ref[...]
    gn = vec_ref[V_GN:V_GN + 1, :GLA_DV]
    streams = range(BATCH)
    rows = [slice(b * BLOCK, (b + 1) * BLOCK) for b in streams]
    cols = [slice(b * GLA_K_W, (b + 1) * GLA_K_W) for b in streams]
    gq = [gq_ref[rows[b], :] for b in streams]
    gk = [gk_ref[rows[b], :] * valid for b in streams]
    gv = [gv_ref[rows[b], :] for b in streams]
    g_cum = [g_cum_all[:, cols[b]] for b in streams]
    a = _gla_intra(gq, gk, lambda b, l: level_decay(l, cols[b]), range(N_LEVELS), pair_level)

    pairs = range(GLA_HEADS // 2)
    q_dec = [(gq[b] * jnp.exp(g_cum[b])).astype(BF16) for b in streams]
    o = [jnp.concatenate([_dot(q_dec[b][:, i * LANES:(i + 1) * LANES], sbd_ref[2 * b + i]) for i in pairs], axis=1)
         for b in streams]
    o = [o[b] + jnp.concatenate(
        [_dot(a[b][h].astype(BF16), gv[b][:, h * GLA_DV:(h + 1) * GLA_DV]) for h in range(GLA_HEADS)], axis=1)
        for b in streams]
    for b in streams:
        for h, gh in enumerate(_gla_merge(o[b], og_ref[rows[b], :], gn)):
            m_ref[rows[b], ATT_Q_W + h * GLA_DV:ATT_Q_W + (h + 1) * GLA_DV] = gh.astype(BF16)

    k_dec = [(gk[b] * jnp.exp(g_cum[b][BLOCK - 1:BLOCK, :] - g_cum[b])).astype(BF16) for b in streams]
    kv = [[_dot_tn(k_dec[b][:, i * LANES:(i + 1) * LANES], gv[b][:, 2 * i * GLA_DV:(2 * i + 2) * GLA_DV])
           for i in pairs] for b in streams]
    for b in streams:
        new_state = decay_col_all[cols[b], :] * state_ref[b] + jnp.concatenate(
            [kv[b][h // 2][(h % 2) * GLA_DK:(h % 2 + 1) * GLA_DK, (h % 2) * GLA_DV:(h % 2 + 1) * GLA_DV]
             for h in range(GLA_HEADS)], axis=0)
        state_ref[b] = new_state
        for h in range(GLA_HEADS):
            sbd_ref[2 * b + h // 2, (h % 2) * GLA_DK:(h % 2 + 1) * GLA_DK,
                    (h % 2) * GLA_DV:(h % 2 + 1) * GLA_DV] = new_state[h * GLA_DK:(h + 1) * GLA_DK].astype(BF16)


def _front_kernel(*refs, layer, n_src, chained):
    sinks_ref, refs = refs[0], refs[1:]
    if chained:
        refs = refs[3:]
    x_refs, refs = refs[:n_src], refs[n_src:]
    vec_ref, w_ref, wg2_ref, tri_ref, lev_ref, plev_ref = refs[:6]
    ffn_f32_refs, refs = refs[6:10], refs[10:]
    if n_src > 1:
        xo_ref, refs = refs[0], refs[1:]
    (m_ref, pfs_ref, pbs_ref, s_out_ref, k_out_ref, v_out_ref) = refs[:6]
    ffn_bf16_refs, refs = refs[6:10], refs[10:]
    wbf_ref, wg2s_ref, z_ref, pf_ref, pb_ref, kprev_ref, vprev_ref, state_ref, sbd_ref = refs
    i = pl.program_id(0)
    t = i - N_PREP

    @pl.when(i < N_PREP)
    def _():
        _convert_proj_weights(i, w_ref, wg2_ref, wbf_ref, wg2s_ref)

    @pl.when((t >= 1) & (t <= N_CONV))
    def _():
        for src, dst in zip(ffn_f32_refs, ffn_bf16_refs):
            dst[...] = src[0].astype(BF16)

    @pl.when(i == 0)
    def _():
        kprev_ref[...] = jnp.zeros_like(kprev_ref)
        vprev_ref[...] = jnp.zeros_like(vprev_ref)
        state_ref[...] = jnp.zeros_like(state_ref)
        sbd_ref[...] = jnp.zeros_like(sbd_ref)

    def load_x():
        if n_src == 1:
            return x_refs[0][...]
        x = jnp.where(t == 0, x_refs[2][...],
                      jnp.where(t < N_BLOCKS, x_refs[0][...].reshape(ROW_TILE, D_MODEL), x_refs[1][...]))
        xo_ref[...] = x
        return x

    @pl.when((t >= 0) & (t < N_BLOCKS))
    def _():
        _project_tile(load_x(), vec_ref, wbf_ref, wg2s_ref, z_ref, pf_ref, pb_ref)
        _attention_tile(t, layer, sinks_ref, pf_ref, pb_ref, m_ref, kprev_ref, vprev_ref)
        _gla_tile(t, _gla_decays(pf_ref, tri_ref, lev_ref), pf_ref, pb_ref, vec_ref, plev_ref, m_ref,
                  state_ref, sbd_ref)

    @pl.when(t == N_BLOCKS - 1)
    def _():
        own = 0 if chained else layer
        if not chained:
            for ref in (s_out_ref, k_out_ref, v_out_ref):
                ref[...] = jnp.zeros_like(ref)
        s_out_ref[own] = state_ref[...]
        for b in range(BATCH):
            k_out_ref[own, b] = pf_ref[b * BLOCK:(b + 1) * BLOCK, PF_K:PF_V].T
            v_out_ref[own, b] = pf_ref[b * BLOCK:(b + 1) * BLOCK, PF_V:PF_GQ].T

    @pl.when(t >= N_BLOCKS)
    def _():
        _project_tile(load_x(), vec_ref, wbf_ref, wg2s_ref, z_ref, pfs_ref, pbs_ref)
        m_ref[...] = jnp.zeros_like(m_ref)


def _front(layer, sinks, x_srcs, vecs, w_in, w_g2, ffn_f32, consts, prev):
    n_src = len(x_srcs)
    step = lambda i: i - N_PREP
    conv_of = lambda i: jnp.clip(step(i) - 1, 0, N_CONV - 1)
    tile_of = lambda i: jnp.where(step(i) <= 0, LEAD_TILE, jnp.minimum(step(i) - 1, LEAD_TILE - 1))
    sample_tile_of = lambda i: jnp.clip(step(i) - N_BLOCKS, 0, SAMPLE_TILES - 1)
    full = lambda a: pl.BlockSpec(a.shape, lambda i, *_: (0,) * a.ndim)
    if prev is None:
        per_batch = lambda r, c: pl.BlockSpec((DEPTH, BATCH, r, c), lambda i, *_: (0, 0, 0, 0))
    else:
        per_batch = lambda r, c: pl.BlockSpec((1, BATCH, r, c), lambda i, *_: (layer, 0, 0, 0))
    if n_src == 1:
        x_specs = [pl.BlockSpec((ROW_TILE, D_MODEL), lambda i, *_: (tile_of(i), 0))]
    else:
        x_specs = [
            pl.BlockSpec((BATCH, BLOCK, D_MODEL), lambda i, *_: (0, jnp.clip(step(i) - 1, 0, MAIN_TILES - 1), 0)),
            pl.BlockSpec((ROW_TILE, D_MODEL), lambda i, *_: (sample_tile_of(i), 0), pipeline_mode=pl.Buffered(1)),
            pl.BlockSpec((ROW_TILE, D_MODEL), lambda i, *_: (0, 0), pipeline_mode=pl.Buffered(1)),
        ]
    const_args = (consts["tri"], consts["lev_low"], consts["pair_level"])
    chained = prev is not None
    prev_args = tuple(prev) if chained else ()
    in_specs = ([pl.BlockSpec(memory_space=pl.ANY)] * len(prev_args) + x_specs + [
        pl.BlockSpec((V_ROWS, D_MODEL), lambda i, *_: (layer, 0)),
        pl.BlockSpec((1, IN_W, W_ROWS), lambda i, *_: (layer, 0, jnp.minimum(i, N_PREP - 1))),
        pl.BlockSpec((1, GATE_RANK, GLA_K_W), lambda i, *_: (layer, 0, 0)),
    ] + [full(a) for a in const_args] + [
        pl.BlockSpec((1, a.shape[1] // N_CONV, a.shape[2]), lambda i, *_: (layer, conv_of(i), 0)) for a in ffn_f32])
    out_specs = [pl.BlockSpec((ROW_TILE, D_MIX), lambda i, *_: (tile_of(i), 0)),
                 pl.BlockSpec((ROW_TILE, PF_W), lambda i, *_: (sample_tile_of(i), 0)),
                 pl.BlockSpec((ROW_TILE, PB_W), lambda i, *_: (sample_tile_of(i), 0)),
                 per_batch(GLA_K_W, GLA_DV), per_batch(BLOCK, ATT_KV_W), per_batch(BLOCK, ATT_KV_W)]
    out_specs += [pl.BlockSpec((a.shape[1] // N_CONV, a.shape[2]), lambda i, *_: (conv_of(i), 0)) for a in ffn_f32]
    out_shape = [jax.ShapeDtypeStruct((TOTAL_ROWS, D_MIX), BF16),
                 jax.ShapeDtypeStruct((SAMPLE_ROWS, PF_W), F32),
                 jax.ShapeDtypeStruct((SAMPLE_ROWS, PB_W), BF16),
                 jax.ShapeDtypeStruct((DEPTH, BATCH, GLA_K_W, GLA_DV), F32),
                 jax.ShapeDtypeStruct((DEPTH, BATCH, BLOCK, ATT_KV_W), F32),
                 jax.ShapeDtypeStruct((DEPTH, BATCH, BLOCK, ATT_KV_W), F32)]
    out_shape += [jax.ShapeDtypeStruct(a.shape[1:], BF16) for a in ffn_f32]
    n_lead_out = 0
    if n_src > 1:
        out_specs = [pl.BlockSpec((ROW_TILE, D_MODEL), lambda i, *_: (tile_of(i), 0))] + out_specs
        out_shape = [jax.ShapeDtypeStruct((TOTAL_ROWS, D_MODEL), F32)] + out_shape
        n_lead_out = 1
    grid_spec = pltpu.PrefetchScalarGridSpec(
        num_scalar_prefetch=1,
        grid=(N_PREP + N_BLOCKS + SAMPLE_TILES,),
        in_specs=in_specs,
        out_specs=out_specs,
        scratch_shapes=[pltpu.VMEM((D_MODEL, PROJ_W), BF16), pltpu.VMEM((LANES, GLA_K_W), BF16),
                        pltpu.VMEM((ROW_TILE, PROJ_W), F32),
                        pltpu.VMEM((ROW_TILE, PF_W), F32), pltpu.VMEM((ROW_TILE, PB_W), BF16),
                        pltpu.VMEM((BATCH * N_KV_HEADS, BLOCK, ATT_KV_W), BF16),
                        pltpu.VMEM((BATCH * N_KV_HEADS, BLOCK, ATT_KV_W), BF16),
                        pltpu.VMEM((BATCH, GLA_K_W, GLA_DV), F32),
                        pltpu.VMEM((BATCH * GLA_HEADS // 2, 2 * GLA_DK, 2 * GLA_DV), BF16)],
    )
    res = pl.pallas_call(
        functools.partial(_front_kernel, layer=layer, n_src=n_src, chained=chained),
        grid_spec=grid_spec,
        out_shape=out_shape,
        input_output_aliases={1 + n: n_lead_out + 3 + n for n in range(len(prev_args))},
        compiler_params=pltpu.CompilerParams(dimension_semantics=("arbitrary",),
                                             vmem_limit_bytes=VMEM_LIMIT),
        name="front",
    )(sinks, *prev_args, *x_srcs, vecs, w_in, w_g2, *const_args, *ffn_f32)
    x_all = res[0] if n_src > 1 else x_srcs[0]
    merged, pfs, pbs, ps, pk, pv = res[n_lead_out:n_lead_out + 6]
    return x_all, merged, (pfs, pbs), (ps, pk, pv), tuple(res[n_lead_out + 6:])


def _mix_sample_kernel(*refs, layer, n_alias):
    sinks_ref, refs = refs[0], refs[n_alias + 1:]
    (pf_ref, pb_ref, ck_ref, cv_ref, st_ref,
     vec_ref, tri_ref, sones_ref, lev_ref, ones_ref, plev_ref,
     m_ref, ck_out_ref, cv_out_ref, st_out_ref) = refs
    q_ref, k_ref, v_ref, gq_ref, gk_ref, ld_ref, gv_ref, og_ref = _proj_views(pf_ref, pb_ref)
    gq = gq_ref[...]
    gk = gk_ref[...]
    gv = gv_ref[...]
    gvf = gv.astype(F32)
    hi, lo = _split(ld_ref[...])
    tri = tri_ref[...]
    g_cum = _dot(tri, hi) + _dot(tri, lo)
    sones = sones_ref[...]
    g_tot = _dot(sones, hi) + _dot(sones, lo)
    lev = lev_ref[...]
    level_sums = _dot(lev, hi) + _dot(lev, lo)
    a = _gla_intra([gq], [gk], lambda b, l: jnp.exp(level_sums[l * BLOCK:(l + 1) * BLOCK]),
                   range(LOW_LEVELS), plev_ref[...])[0]
    o_intra = jnp.concatenate(
        [_dot(a[h].astype(BF16), gv[:, h * GLA_DV:(h + 1) * GLA_DV]) for h in range(GLA_HEADS)], axis=1)
    q_dec = gq * jnp.exp(g_cum)
    k_dec = gk * jnp.exp(g_tot - g_cum)
    hm8 = _head_masks(DEC_SEQ)
    ones8 = ones_ref[...]
    hi_f, lo_f = hi.astype(F32), lo.astype(F32)

    qf = q_ref[...].astype(F32)
    kf = k_ref[...]
    vf = v_ref[...]
    low8 = lax.broadcasted_iota(jnp.int32, (DEC_SEQ, LANES), 1) < HEAD_DIM
    nkeys = WINDOW + DEC_SEQ
    srow = lax.broadcasted_iota(jnp.int32, (N_HEADS * DEC_SEQ, nkeys), 0)
    scol = lax.broadcasted_iota(jnp.int32, (N_HEADS * DEC_SEQ, nkeys), 1)
    t_of_row = srow & (DEC_SEQ - 1)
    amask = ((scol < WINDOW) & (scol > t_of_row)) | ((scol >= WINDOW) & (scol - WINDOW <= t_of_row))
    rid = lax.broadcasted_iota(jnp.int32, (N_HEADS * DEC_SEQ, 1), 0) >> LOW_LEVELS
    sink_col = jnp.zeros((N_HEADS * DEC_SEQ, 1), F32)
    for i in range(N_HEADS):
        sink_col = jnp.where(rid == i, sinks_ref[layer, i], sink_col)

    seqs = range(SEQ_GROUP)
    rows = [slice(b * DEC_SEQ, (b + 1) * DEC_SEQ) for b in seqs]

    def stacked_queries(b):
        pieces = []
        for j in range(N_HEADS // 2):
            g = (2 * j) // (N_HEADS // N_KV_HEADS)
            qt = qf[rows[b], j * LANES:(j + 1) * LANES]
            swapped = pltpu.roll(qt, HEAD_DIM, axis=1)
            own_lanes = low8 if g == 0 else ~low8
            even, odd = (qt, swapped) if g == 0 else (swapped, qt)
            pieces += [jnp.where(own_lanes, even, 0.0), jnp.where(own_lanes, odd, 0.0)]
        return jnp.concatenate(pieces, axis=0).astype(BF16)

    lane = lax.broadcasted_iota(jnp.int32, (WINDOW, WINDOW), 1)
    old_lanes = lane < WINDOW - DEC_SEQ

    def shifted_buffer(buf, new_rows):
        wide = jnp.concatenate([jnp.zeros((WINDOW - DEC_SEQ, LANES), F32), new_rows], axis=0)
        return jnp.where(old_lanes, pltpu.roll(buf, WINDOW - DEC_SEQ, axis=1), wide.T)

    qp = [stacked_queries(b) for b in seqs]
    ck = [ck_ref[0, b] for b in seqs]
    cv = [cv_ref[0, b] for b in seqs]
    for b in seqs:
        ck_out_ref[0, b] = shifted_buffer(ck[b], kf[rows[b]])
        cv_out_ref[0, b] = shifted_buffer(cv[b], vf[rows[b]])
    s = [jnp.concatenate([_dot(qp[b], ck[b].astype(BF16)), _dot_nt(qp[b], kf[rows[b]].astype(BF16))], axis=1)
         for b in seqs]
    s = [jnp.where(amask, s[b], MASK_VALUE) for b in seqs]
    m = [jnp.maximum(jnp.max(s[b], axis=-1, keepdims=True), sink_col) for b in seqs]
    p = [jnp.exp(s[b] - m[b]) for b in seqs]
    denom = [jnp.sum(p[b], axis=-1, keepdims=True) + jnp.exp(sink_col - m[b]) for b in seqs]
    pb = [p[b].astype(BF16) for b in seqs]
    ob = [(_dot_nt(pb[b][:, :WINDOW], cv[b].astype(BF16))
           + _dot(pb[b][:, WINDOW:], vf[rows[b]].astype(BF16))) / denom[b] for b in seqs]

    def head_tiles(o):
        tiles = []
        for j in range(N_HEADS // 2):
            g = (2 * j) // (N_HEADS // N_KV_HEADS)
            even = o[(2 * j) * DEC_SEQ:(2 * j + 1) * DEC_SEQ]
            odd = o[(2 * j + 1) * DEC_SEQ:(2 * j + 2) * DEC_SEQ]
            if g == 0:
                tiles.append(jnp.where(low8, even, pltpu.roll(odd, HEAD_DIM, axis=1)))
            else:
                tiles.append(jnp.where(low8, pltpu.roll(even, HEAD_DIM, axis=1), odd))
        return jnp.concatenate(tiles, axis=1)

    att_rows = [head_tiles(ob[b]) for b in seqs]

    head_stack = lambda x: jnp.concatenate([jnp.where(mk, x, 0.0) for mk in hm8], axis=0).astype(BF16)
    state = [st_ref[0, b] for b in seqs]
    oi = [_dot(head_stack(q_dec[rows[b]]), state[b].astype(BF16)) for b in seqs]
    inter_rows = [jnp.concatenate([oi[b][h * DEC_SEQ:(h + 1) * DEC_SEQ] for h in range(GLA_HEADS)], axis=1)
                  for b in seqs]
    vstack = [jnp.concatenate([gvf[rows[b]][:, h * GLA_DV:(h + 1) * GLA_DV] for h in range(GLA_HEADS)],
                              axis=0).astype(BF16) for b in seqs]
    kv = [_dot_tn(head_stack(k_dec[rows[b]]), vstack[b]) for b in seqs]
    decay_col = [jnp.exp(_dot_tn(hi_f[rows[b]], ones8) + _dot_tn(lo_f[rows[b]], ones8)) for b in seqs]
    for b in seqs:
        st_out_ref[0, b] = decay_col[b] * state[b] + kv[b]

    m_ref[:, :ATT_Q_W] = jnp.concatenate(att_rows, axis=0).astype(BF16)
    o = o_intra + jnp.concatenate(inter_rows, axis=0)
    gn = vec_ref[V_GN:V_GN + 1, :GLA_DV]
    for h, gh in enumerate(_gla_merge(o, og_ref[...], gn)):
        m_ref[:, ATT_Q_W + h * GLA_DV:ATT_Q_W + (h + 1) * GLA_DV] = gh.astype(BF16)


def _mix_sample(layer, sinks, merged, proj, cache_k, cache_v, state, vecs, consts, prev):
    tok = lambda w_: pl.BlockSpec((BLOCK, w_), lambda i, *_: (i, 0))
    merged_blk = pl.BlockSpec((BLOCK, D_MIX), lambda i, *_: (MAIN_ROWS // BLOCK + i, 0))
    full = lambda a: pl.BlockSpec(a.shape, lambda i, *_: (0,) * a.ndim)
    seq = lambda a: pl.BlockSpec((1, SEQ_GROUP) + a.shape[2:], lambda i, *_: (layer, i, 0, 0))
    ones8 = jnp.ones((DEC_SEQ, LANES), F32)
    const_args = (consts["seq_tri"], consts["seq_ones"], consts["lev_low"], ones8, consts["pair_level"])
    seq_args = (cache_k, cache_v, state)
    alias_args = (merged,) + (tuple(prev) if prev is not None else ())
    grid_spec = pltpu.PrefetchScalarGridSpec(
        num_scalar_prefetch=1,
        grid=(DEC_BATCH // SEQ_GROUP,),
        in_specs=([pl.BlockSpec(memory_space=pl.ANY)] * len(alias_args)
                  + [tok(a.shape[1]) for a in proj] + [seq(a) for a in seq_args]
                  + [pl.BlockSpec((V_ROWS, D_MODEL), lambda i, *_: (layer, 0))]
                  + [full(a) for a in const_args]),
        out_specs=[merged_blk] + [seq(a) for a in seq_args],
    )
    return pl.pallas_call(
        functools.partial(_mix_sample_kernel, layer=layer, n_alias=len(alias_args)),
        grid_spec=grid_spec,
        out_shape=[jax.ShapeDtypeStruct(merged.shape, merged.dtype)]
        + [jax.ShapeDtypeStruct(a.shape, a.dtype) for a in seq_args],
        input_output_aliases={1 + n: n for n in range(len(alias_args))},
        compiler_params=pltpu.CompilerParams(dimension_semantics=("arbitrary",),
                                             vmem_limit_bytes=VMEM_LIMIT),
        name="mix_sample",
    )(sinks, *alias_args, *proj, *seq_args, vecs, *const_args)


def _out_ffn_kernel(*refs, last, n_tiles):
    x_ref, m_ref, vec_ref, wo_ref, wg_ref, wu_ref, wd_ref = refs[:7]
    out_refs = refs[7:9] if last else refs[7:8]
    x1_bufs, h_bufs, act_ref = refs[-5:-3], refs[-3:-1], refs[-1]
    s = pl.program_id(0)
    n_chunks = D_FFN // FFN_CHUNK
    norm_rows = ROW_TILE // NORM_PIECES

    def prepare(slot):
        x1_ref, h_ref = x1_bufs[slot], h_bufs[slot]
        x1_ref[...] = x_ref[...] + _dot(m_ref[...], wo_ref[...])

        def norm_piece(r):
            rows = slice(r * norm_rows, (r + 1) * norm_rows)
            x1 = x1_ref[rows, :]
            ms = jnp.mean(x1 * x1, axis=-1, keepdims=True)
            h_ref[rows, :] = (x1 * lax.rsqrt(ms + RMS_EPS) * vec_ref[V_NORM2:V_NORM2 + 1, :]).astype(BF16)

        return [functools.partial(norm_piece, r) for r in range(NORM_PIECES)]

    def finish(slot, between=()):
        x1_ref, h_ref = x1_bufs[slot], h_bufs[slot]
        between = list(between)
        h = h_ref[...]
        for c in range(n_chunks):
            cols = slice(c * FFN_CHUNK, (c + 1) * FFN_CHUNK)
            gate = _dot(h, wg_ref[:, cols])
            up = _dot(h, wu_ref[:, cols])
            act_ref[:, cols] = (gate * _sigmoid(gate) * up).astype(BF16)
            if between:
                between.pop(0)()
        y = x1_ref[...] + _dot(act_ref[...], wd_ref[...])
        if last:
            out_refs[0][...] = y.reshape(BATCH, BLOCK, D_MODEL)

            @pl.when(s - 1 < SAMPLE_TILES)
            def _():
                out_refs[1][...] = out_refs[0][...].reshape(ROW_TILE, D_MODEL)
        else:
            out_refs[0][...] = y

    @pl.when(s == 0)
    def _():
        for piece in prepare(0):
            piece()

    inner = (s >= 1) & (s < n_tiles)
    for parity in range(2):
        @pl.when(inner & (s % 2 == parity))
        def _():
            finish(1 - parity, between=prepare(parity))

    @pl.when(s == n_tiles)
    def _():
        finish((n_tiles - 1) % 2)


def _out_ffn(layer, x_all, merged, vecs, ffn_weights, last):
    resident = lambda a: pl.BlockSpec(a.shape, lambda i: (0, 0), pipeline_mode=pl.Buffered(1))
    if last:
        n_tiles = MAIN_TILES + SAMPLE_TILES
        row_tile_of = lambda u: jnp.where(u < SAMPLE_TILES, MAIN_TILES + u, u - SAMPLE_TILES)
        u_out = lambda s: jnp.maximum(s - 1, 0)
        out_specs = [pl.BlockSpec((BATCH, BLOCK, D_MODEL),
                                  lambda s: (0, jnp.maximum(u_out(s) - SAMPLE_TILES, 0), 0)),
                     pl.BlockSpec((ROW_TILE, D_MODEL), lambda s: (jnp.minimum(u_out(s), SAMPLE_TILES - 1), 0))]
        out_shape = [jax.ShapeDtypeStruct((BATCH, SEQ, D_MODEL), F32),
                     jax.ShapeDtypeStruct((SAMPLE_ROWS, D_MODEL), F32)]
    else:
        n_tiles = N_TILES
        row_tile_of = lambda u: u
        out_specs = [pl.BlockSpec((ROW_TILE, D_MODEL), lambda s: (jnp.maximum(s - 1, 0), 0))]
        out_shape = [jax.ShapeDtypeStruct((TOTAL_ROWS, D_MODEL), F32)]
    tile = lambda w_: pl.BlockSpec((ROW_TILE, w_), lambda s: (row_tile_of(jnp.minimum(s, n_tiles - 1)), 0))
    return pl.pallas_call(
        functools.partial(_out_ffn_kernel, last=last, n_tiles=n_tiles),
        grid=(n_tiles + 1,),
        in_specs=[tile(D_MODEL), tile(D_MIX), pl.BlockSpec((V_ROWS, D_MODEL), lambda s: (layer, 0))]
        + [resident(w) for w in ffn_weights],
        out_specs=out_specs,
        out_shape=out_shape,
        scratch_shapes=[pltpu.VMEM((ROW_TILE, D_MODEL), F32), pltpu.VMEM((ROW_TILE, D_MODEL), F32),
                        pltpu.VMEM((ROW_TILE, D_MODEL), BF16), pltpu.VMEM((ROW_TILE, D_MODEL), BF16),
                        pltpu.VMEM((ROW_TILE, D_FFN), BF16)],
        compiler_params=pltpu.CompilerParams(dimension_semantics=("arbitrary",),
                                             vmem_limit_bytes=VMEM_LIMIT),
        name="out_ffn",
    )(x_all, merged, vecs, *ffn_weights)


def _vector_slab(norm1, norm2, q_norm, k_norm, b_g, gla_norm):
    pad = lambda a: jnp.pad(a.astype(F32), ((0, 0), (0, D_MODEL - a.shape[1])))
    rows = [norm1.astype(F32), norm2.astype(F32), pad(jnp.tile(q_norm, (1, N_HEADS)) * ATT_SCALE),
            pad(jnp.tile(k_norm, (1, N_KV_HEADS))), pad(b_g), pad(gla_norm)]
    rows += [jnp.zeros((DEPTH, D_MODEL), F32)] * (V_ROWS - len(rows))
    return jnp.stack(rows, axis=1).reshape(DEPTH * V_ROWS, D_MODEL)


def kernel(x_prompt, x_sample, cache_k, cache_v, state_gla, meta, norm1, w_in, q_norm, k_norm, sinks,
           w_g2, b_g, gla_norm, w_o, norm2, w_gate, w_up, w_down):
    consts = _constants()
    dt = x_prompt.dtype
    vecs = _vector_slab(norm1, norm2, q_norm, k_norm, b_g, gla_norm)
    sinks = sinks.astype(F32)
    lead = jnp.tile(jnp.concatenate([jnp.zeros((PAD_LEN, D_MODEL), dt), meta.astype(dt)], axis=0), (BATCH, 1))
    x_srcs = (x_prompt, x_sample.reshape(SAMPLE_ROWS, D_MODEL), lead)
    to_feature_major = lambda c: c.transpose(0, 1, 3, 4, 2).reshape(DEPTH, DEC_BATCH, ATT_KV_W, WINDOW)
    ck_in, cv_in = to_feature_major(cache_k), to_feature_major(cache_v)
    st_in = state_gla.reshape(DEPTH, DEC_BATCH, GLA_K_W, GLA_DV)
    w_in_t = jnp.swapaxes(w_in, 1, 2)

    prompt_outs, sample_outs = None, None
    for l in range(DEPTH):
        x_all, merged, proj_sample, prompt_outs, ffn_bf16 = _front(
            l, sinks, x_srcs, vecs, w_in_t, w_g2, (w_o, w_gate, w_up, w_down), consts, prompt_outs)
        merged, *sample_outs = _mix_sample(l, sinks, merged, proj_sample, ck_in, cv_in, st_in, vecs, consts,
                                           sample_outs)
        x_srcs = tuple(_out_ffn(l, x_all, merged, vecs, ffn_bf16, last=l == DEPTH - 1))

    y_main, y_sample = x_srcs
    ps, pk, pv = prompt_outs
    sk, sv, ss = sample_outs
    kv5 = lambda a, n: a.reshape(DEPTH, n, N_KV_HEADS, HEAD_DIM, WINDOW).transpose(0, 1, 4, 2, 3)
    st5 = lambda a, n: a.reshape(DEPTH, n, GLA_HEADS, GLA_DK, GLA_DV)
    return (y_main, y_sample.reshape(DEC_BATCH, DEC_SEQ, D_MODEL),
            kv5(pk, BATCH), kv5(pv, BATCH), st5(ps, BATCH), kv5(sk, DEC_BATCH), kv5(sv, DEC_BATCH),
            st5(ss, DEC_BATCH))
```

```python
import functools

import jax
import jax.numpy as jnp
import numpy as np
from jax import lax
from jax.experimental import pallas as pl
from jax.experimental.pallas import tpu as pltpu

F32 = jnp.float32
BF16 = jnp.bfloat16

D_MODEL = 1024
BATCH = 4
SEQ = 4096
DEPTH = 2
DEC_BATCH = 128
DEC_SEQ = 8
N_META = 16
WINDOW = 128
BLOCK = 128
PAD_LEN = BLOCK - N_META
N_HEADS = 8
N_KV_HEADS = 2
HEAD_DIM = 64
ATT_SCALE = HEAD_DIM ** -0.5
ATT_Q_W = N_HEADS * HEAD_DIM
ATT_KV_W = N_KV_HEADS * HEAD_DIM
GLA_HEADS = 4
GLA_DK = 64
GLA_DV = 128
GLA_K_W = GLA_HEADS * GLA_DK
GLA_V_W = GLA_HEADS * GLA_DV
GATE_RANK = 16
GATE_NORMALIZER = 16.0
D_MIX = ATT_Q_W + GLA_V_W
D_FFN = 2816
IN_W = 2320
RMS_EPS = 1e-6
MASK_VALUE = -1e30

LANES = 128
N_BLOCKS = 1 + SEQ // BLOCK
MAIN_ROWS = BATCH * SEQ
SAMPLE_ROWS = DEC_BATCH * DEC_SEQ
LEAD_ROWS = BATCH * BLOCK
TOTAL_ROWS = MAIN_ROWS + SAMPLE_ROWS + LEAD_ROWS
ROW_TILE = 512
MAIN_TILES = MAIN_ROWS // ROW_TILE
SAMPLE_TILES = SAMPLE_ROWS // ROW_TILE
N_TILES = TOTAL_ROWS // ROW_TILE
LEAD_TILE = N_TILES - 1
SEQ_GROUP = BLOCK // DEC_SEQ
N_LEVELS = 7
LOW_LEVELS = 3
LOG_DK = 6
N_PREP = 8
W_ROWS = D_MODEL // N_PREP
N_CONV = 16

SRC_LOW, SRC_OG = 1792, 1808
OFF_Q, OFF_K, OFF_V = 0, 512, 640
OFF_GQ, OFF_GK, OFF_GV, OFF_OG, OFF_LOW = 768, 1024, 1280, 1792, 2304
PROJ_W = OFF_LOW + LANES
PROJ_CHUNK = 512
FFN_CHUNK = 256
SCORE_LOOKAHEAD = 1
VMEM_LIMIT = 58 * 1024 * 1024

V_NORM1, V_NORM2, V_QG, V_KG, V_BG, V_GN, V_ROWS = 0, 1, 2, 3, 4, 5, 8

PF_K, PF_V, PF_GQ, PF_GK, PF_LD, PF_OG, PF_W = 0, 128, 256, 512, 768, 1024, 1536
PB_Q, PB_GV, PB_W = 0, 512, 1024


def _proj_views(pf_ref, pb_ref):
    f = lambda a, b: pf_ref.at[:, a:b]
    return (pb_ref.at[:, PB_Q:PB_GV], f(PF_K, PF_V), f(PF_V, PF_GQ), f(PF_GQ, PF_GK), f(PF_GK, PF_LD),
            f(PF_LD, PF_OG), pb_ref.at[:, PB_GV:PB_W], f(PF_OG, PF_W))


def _dot(a, b):
    return jnp.dot(a, b, preferred_element_type=F32)


def _dot_nt(a, b):
    return lax.dot_general(a, b, (((1,), (1,)), ((), ())), preferred_element_type=F32)


def _dot_tn(a, b):
    return lax.dot_general(a, b, (((0,), (0,)), ((), ())), preferred_element_type=F32)


def _split(x):
    hi = x.astype(BF16)
    lo = (x - hi.astype(F32)).astype(BF16)
    return hi, lo


def _sigmoid(x):
    return 1.0 / (1.0 + jnp.exp(-x))


def _level_matrix(levels, n=BLOCK):
    out = np.zeros((len(levels) * n, n), np.float32)
    for i, l in enumerate(levels):
        size = 2 << l
        for t in range(n):
            mid = (t // size) * size + size // 2 - 1
            if (t >> l) & 1:
                out[i * n + t, mid + 1:t + 1] = 1.0
            else:
                out[i * n + t, t + 1:mid + 1] = 1.0
    return out


def _constants():
    r = np.arange(BLOCK)
    tri = (r[None, :] <= r[:, None]).astype(np.float32)
    same_seq = (r[None, :] // DEC_SEQ) == (r[:, None] // DEC_SEQ)
    diff = np.maximum(r[:, None] ^ r[None, :], 1)
    pair_level = np.where(r[None, :] < r[:, None], np.floor(np.log2(diff)).astype(np.int32),
                          np.where(r[None, :] == r[:, None], N_LEVELS, N_LEVELS + 1)).astype(np.int32)
    return dict(
        tri=jnp.asarray(tri, BF16),
        seq_tri=jnp.asarray(tri * same_seq, BF16),
        seq_ones=jnp.asarray(same_seq.astype(np.float32), BF16),
        lev_low=jnp.asarray(_level_matrix(range(LOW_LEVELS)), BF16),
        pair_level=jnp.asarray(pair_level),
    )


def _head_masks(rows):
    lane = lax.broadcasted_iota(jnp.int32, (rows, GLA_K_W), 1)
    return [(lane >> LOG_DK) == h for h in range(GLA_HEADS)]


def _gla_intra(gq, gk, level_decay, levels, pair_level):
    rows = gq[0].shape[0]
    streams = range(len(gq))
    low_half = lax.broadcasted_iota(jnp.int32, (rows, LANES), 1) < GLA_DK

    def pair_products(qh, kh):
        qh, kh = qh.astype(BF16), kh.astype(BF16)
        out = []
        for pair in range(GLA_HEADS // 2):
            kl = kh[:, pair * LANES:(pair + 1) * LANES]
            zero = jnp.zeros_like(kl)
            stacked = jnp.concatenate([jnp.where(low_half, kl, zero), jnp.where(low_half, zero, kl)], axis=0)
            p = _dot_nt(qh[:, pair * LANES:(pair + 1) * LANES], stacked)
            out += [p[:, :rows], p[:, rows:]]
        return out

    row = lax.broadcasted_iota(jnp.int32, (rows, 1), 0)

    def upper_q_lower_k(b, l):
        size = 1 << l
        if size % 8:
            return jnp.where(((row >> l) & 1) == 1, gq[b], gk[b])
        return jnp.concatenate([(gq[b] if i & 1 else gk[b])[i * size:(i + 1) * size]
                                for i in range(rows // size)], axis=0)

    def diagonal(b):
        out = []
        for pair in range(GLA_HEADS // 2):
            prod = (gq[b] * gk[b])[:, pair * LANES:(pair + 1) * LANES]
            out += [jnp.sum(jnp.where(low_half, prod, 0.0), axis=-1, keepdims=True),
                    jnp.sum(jnp.where(low_half, 0.0, prod), axis=-1, keepdims=True)]
        return out

    def blocks(v, size, parity):
        return jnp.concatenate([v[i * size:(i + 1) * size] for i in range(parity, rows // size, 2)], axis=0)

    def interleave(lower, upper, size):
        pieces = []
        for i in range(rows // (2 * size)):
            pieces += [lower[i * size:(i + 1) * size], upper[i * size:(i + 1) * size]]
        return jnp.concatenate(pieces, axis=0)

    on_diag = pair_level == N_LEVELS
    a = [[jnp.where(on_diag, d, 0.0).astype(BF16) for d in diagonal(b)] for b in streams]
    for l in levels:
        at_level = pair_level == l
        size = 1 << l
        for b in streams:
            x = upper_q_lower_k(b, l) * level_decay(b, l)
            if size % 16:
                a[b] = [jnp.where(at_level, p.astype(BF16), a_h) for p, a_h in zip(pair_products(x, x), a[b])]
            else:
                mask_up = blocks(pair_level, size, 1) == l
                p_up = pair_products(blocks(x, size, 1), x)
                a[b] = [interleave(blocks(a_h, size, 0),
                                   jnp.where(mask_up, p.astype(BF16), blocks(a_h, size, 1)), size)
                        for p, a_h in zip(p_up, a[b])]
    return a


def _gla_merge(o, og, gn):
    outs = []
    for h in range(GLA_HEADS):
        oh = o[:, h * GLA_DV:(h + 1) * GLA_DV]
        gh = og[:, h * GLA_DV:(h + 1) * GLA_DV]
        ms = jnp.mean(oh * oh, axis=-1, keepdims=True)
        outs.append(oh * lax.rsqrt(ms + RMS_EPS) * gn * (gh * _sigmoid(gh)))
    return outs


def _dup_halves(x):
    low = lax.broadcasted_iota(jnp.int32, x.shape, 1) < HEAD_DIM
    rolled = pltpu.roll(x, HEAD_DIM, axis=1)
    return jnp.where(low, x, rolled), jnp.where(low, rolled, x)


def _convert_proj_weights(i, w_ref, wg2_ref, wbf_ref, wg2s_ref):
    rows = pl.ds(pl.multiple_of(i * W_ROWS, W_ROWS), W_ROWS)

    def put(dst, src, n=LANES):
        tile = w_ref[0, src:src + n, :]
        if n < LANES:
            tile = jnp.concatenate([tile, jnp.zeros((LANES - n, W_ROWS), F32)], axis=0)
        wbf_ref[rows, dst:dst + LANES] = tile.T.astype(BF16)

    for c in range(SRC_LOW // LANES):
        put(c * LANES, c * LANES)
    for c in range((IN_W - SRC_OG) // LANES):
        put(OFF_OG + c * LANES, SRC_OG + c * LANES)
    put(OFF_LOW, SRC_LOW, GATE_RANK)

    @pl.when(i == 0)
    def _():
        wg2s_ref[...] = jnp.concatenate(
            [wg2_ref[0], jnp.zeros((LANES - GATE_RANK, GLA_K_W), F32)], axis=0).astype(BF16)


def _project_tile(x, vec_ref, wbf_ref, wg2s_ref, z_ref, pf_ref, pb_ref):
    q_ref, k_ref, v_ref, gq_ref, gk_ref, ld_ref, gv_ref, og_ref = _proj_views(pf_ref, pb_ref)
    ms = jnp.mean(x * x, axis=-1, keepdims=True)
    h = (x * lax.rsqrt(ms + RMS_EPS) * vec_ref[V_NORM1:V_NORM1 + 1, :]).astype(BF16)

    low_half = lax.broadcasted_iota(jnp.int32, (ROW_TILE, LANES), 1) < HEAD_DIM

    def head_norm(t):
        t2 = t * t
        ms_lo = jnp.sum(jnp.where(low_half, t2, 0.0), axis=-1, keepdims=True) * (1.0 / HEAD_DIM)
        ms_hi = jnp.sum(jnp.where(low_half, 0.0, t2), axis=-1, keepdims=True) * (1.0 / HEAD_DIM)
        return t * jnp.where(low_half, lax.rsqrt(ms_lo + RMS_EPS), lax.rsqrt(ms_hi + RMS_EPS))

    def q_part():
        for j in range(ATT_Q_W // LANES):
            cols = slice(j * LANES, (j + 1) * LANES)
            q_ref[:, cols] = (head_norm(z_ref[:, cols]) * vec_ref[V_QG:V_QG + 1, cols]).astype(BF16)

    def kv_part():
        k_ref[...] = head_norm(z_ref[:, OFF_K:OFF_V]) * vec_ref[V_KG:V_KG + 1, :ATT_KV_W]
        v_ref[...] = z_ref[:, OFF_V:OFF_GQ]
        gq_ref[...] = z_ref[:, OFF_GQ:OFF_GK] * (GLA_DK ** -0.5)

    def gk_part():
        gk_ref[...] = z_ref[:, OFF_GK:OFF_GV]

    def gv_part():
        gv_ref[...] = z_ref[:, OFF_GV:OFF_OG].astype(BF16)
        og_ref[...] = z_ref[:, OFF_OG:OFF_LOW]

    def gate_part():
        logit = (_dot(z_ref[:, OFF_LOW:PROJ_W].astype(BF16), wg2s_ref[...])
                 + vec_ref[V_BG:V_BG + 1, :GLA_K_W])
        log_sig = jnp.minimum(logit, 0.0) - jnp.log1p(jnp.exp(-jnp.abs(logit)))
        ld_ref[...] = log_sig * (1.0 / GATE_NORMALIZER)

    pieces = [q_part, kv_part, gk_part, gv_part, gate_part]
    for c in [len(pieces) - 1] + list(range(len(pieces) - 1)):
        cols = slice(c * PROJ_CHUNK, min((c + 1) * PROJ_CHUNK, PROJ_W))
        z_ref[:, cols] = _dot(h, wbf_ref[:, cols])
        pieces[c]()


def _attention_tile(blk, layer, sinks_ref, pf_ref, pb_ref, m_ref, kprev_ref, vprev_ref):
    q_ref, k_ref, v_ref = _proj_views(pf_ref, pb_ref)[:3]
    per_group = N_HEADS // N_KV_HEADS
    grows = per_group * BLOCK
    row = lax.broadcasted_iota(jnp.int32, (grows, BLOCK), 0) & (BLOCK - 1)
    col = lax.broadcasted_iota(jnp.int32, (grows, BLOCK), 1)
    own = col <= row
    kpos = jnp.where(own, blk * BLOCK, (blk - 1) * BLOCK) + col - PAD_LEN
    live = kpos >= 0
    low_half = lax.broadcasted_iota(jnp.int32, (BLOCK, LANES), 1) < HEAD_DIM
    head_of_row = lax.broadcasted_iota(jnp.int32, (grows, 1), 0) >> N_LEVELS
    sink_cols = []
    for g in range(N_KV_HEADS):
        sink = jnp.zeros((grows, 1), F32)
        for r in range(per_group):
            sink = jnp.where(head_of_row == r, sinks_ref[layer, g * per_group + r], sink)
        sink_cols.append(sink)

    units =[(b, g) for b in range(BATCH) for g in range(N_KV_HEADS)]
    rows = [slice(b * BLOCK, (b + 1) * BLOCK) for b in range(BATCH)]
    tiles = [range(g * per_group // 2, (g + 1) * per_group // 2) for g in range(N_KV_HEADS)]
    kdup = [_dup_halves(k_ref[rows[b], :].astype(BF16)) for b in range(BATCH)]
    vdup = [_dup_halves(v_ref[rows[b], :].astype(BF16)) for b in range(BATCH)]
    kk = [jnp.concatenate([kprev_ref[b * N_KV_HEADS + g], kdup[b][g]], axis=0) for b, g in units]
    vv = [jnp.concatenate([vprev_ref[b * N_KV_HEADS + g], vdup[b][g]], axis=0) for b, g in units]
    for b, g in units:
        kprev_ref[b * N_KV_HEADS + g] = kdup[b][g]
        vprev_ref[b * N_KV_HEADS + g] = vdup[b][g]

    def stacked_queries(b, g):
        pieces = []
        for j in tiles[g]:
            qt = q_ref[rows[b], j * LANES:(j + 1) * LANES]
            zero = jnp.zeros_like(qt)
            pieces += [jnp.where(low_half, qt, zero), jnp.where(low_half, zero, qt)]
        return jnp.concatenate(pieces, axis=0)

    def scores(u):
        b, g = units[u]
        return _dot_nt(stacked_queries(b, g), kk[u])

    def attend(u, s):
        b, g = units[u]
        s = jnp.where(live, jnp.where(own, s[:, BLOCK:], s[:, :BLOCK]), MASK_VALUE)
        m = jnp.maximum(jnp.max(s, axis=-1, keepdims=True), sink_cols[g])
        p = jnp.exp(s - m)
        denom = jnp.sum(p, axis=-1, keepdims=True) + jnp.exp(sink_cols[g] - m)
        p2 = jnp.concatenate([jnp.where(own, 0.0, p), jnp.where(own, p, 0.0)], axis=1).astype(BF16)
        o = _dot(p2, vv[u]) / denom
        for n, j in enumerate(tiles[g]):
            o_lo = o[2 * n * BLOCK:(2 * n + 1) * BLOCK]
            o_hi = o[(2 * n + 1) * BLOCK:(2 * n + 2) * BLOCK]
            m_ref[rows[b], j * LANES:(j + 1) * LANES] = jnp.where(low_half, o_lo, o_hi).astype(BF16)

    pending = [scores(u) for u in range(SCORE_LOOKAHEAD)]
    for u in range(len(units)):
        if u + SCORE_LOOKAHEAD < len(units):
            pending.append(scores(u + SCORE_LOOKAHEAD))
        attend(u, pending.pop(0))


def _gla_decays(pf_ref, tri_ref, lev_ref):
    ld_ref = pf_ref.at[:, PF_LD:PF_OG]
    ld_all = jnp.concatenate([ld_ref[b * BLOCK:(b + 1) * BLOCK, :] for b in range(BATCH)], axis=1)
    hi, lo = _split(ld_all)
    tri = tri_ref[...]
    g_cum_all = _dot(tri, hi) + _dot(tri, lo)
    lev = lev_ref[...]
    low_sums = _dot(lev, hi) + _dot(lev, lo)

    def level_decay(l, cols):
        if l < LOW_LEVELS:
            return jnp.exp(low_sums[l * BLOCK:(l + 1) * BLOCK, cols])
        half = 1 << l
        g = g_cum_all[:, cols]
        pieces = []
        for p in range(BLOCK // (2 * half)):
            mid = g[p * 2 * half + half - 1:p * 2 * half + half, :]
            pieces += [mid - g[p * 2 * half:p * 2 * half + half], g[p * 2 * half + half:(p + 1) * 2 * half] - mid]
        return jnp.exp(jnp.concatenate(pieces, axis=0))

    g_last_all = jnp.broadcast_to(g_cum_all[BLOCK - 1:BLOCK, :], g_cum_all.shape)
    decay_col_all = jnp.exp(jnp.concatenate(
        [g_last_all[:, c * LANES:(c + 1) * LANES].T for c in range(BATCH * GLA_K_W // LANES)], axis=0))
    return g_cum_all, level_decay, decay_col_all


def _gla_tile(blk, decays, pf_ref, pb_ref, vec_ref, plev_ref, m_ref, state_ref, sbd_ref):
    g_cum_all, level_decay, decay_col_all = decays
    gq_ref, gk_ref, _, gv_ref, og_ref = _proj_views(pf_ref, pb_ref)[3:]
    rpos = blk * BLOCK + lax.broadcasted_iota(jnp.int32, (BLOCK, 1), 0) - PAD_LEN
    valid = (rpos >= 0).astype(F32)
    pair_level = plev_ref[...]
    gn = vec_ref[V_GN:V_GN + 1, :GLA_DV]
    streams = range(BATCH)
    rows = [slice(b * BLOCK, (b + 1) * BLOCK) for b in streams]
    cols = [slice(b * GLA_K_W, (b + 1) * GLA_K_W) for b in streams]
    gq = [gq_ref[rows[b], :] for b in streams]
    gk = [gk_ref[rows[b], :] * valid for b in streams]
    gv = [gv_ref[rows[b], :] for b in streams]
    g_cum = [g_cum_all[:, cols[b]] for b in streams]
    a = _gla_intra(gq, gk, lambda b, l: level_decay(l, cols[b]), range(N_LEVELS), pair_level)

    pairs = range(GLA_HEADS // 2)
    q_dec = [(gq[b] * jnp.exp(g_cum[b])).astype(BF16) for b in streams]
    o = [jnp.concatenate([_dot(q_dec[b][:, i * LANES:(i + 1) * LANES], sbd_ref[2 * b + i]) for i in pairs], axis=1)
         for b in streams]
    o = [o[b] + jnp.concatenate(
        [_dot(a[b][h].astype(BF16), gv[b][:, h * GLA_DV:(h + 1) * GLA_DV]) for h in range(GLA_HEADS)], axis=1)
        for b in streams]
    for b in streams:
        for h, gh in enumerate(_gla_merge(o[b], og_ref[rows[b], :], gn)):
            m_ref[rows[b], ATT_Q_W + h * GLA_DV:ATT_Q_W + (h + 1) * GLA_DV] = gh.astype(BF16)

    k_dec = [(gk[b] * jnp.exp(g_cum[b][BLOCK - 1:BLOCK, :] - g_cum[b])).astype(BF16) for b in streams]
    kv = [[_dot_tn(k_dec[b][:, i * LANES:(i + 1) * LANES], gv[b][:, 2 * i * GLA_DV:(2 * i + 2) * GLA_DV])
           for i in pairs] for b in streams]
    for b in streams:
        new_state = decay_col_all[cols[b], :] * state_ref[b] + jnp.concatenate(
            [kv[b][h // 2][(h % 2) * GLA_DK:(h % 2 + 1) * GLA_DK, (h % 2) * GLA_DV:(h % 2 + 1) * GLA_DV]
             for h in range(GLA_HEADS)], axis=0)
        state_ref[b] = new_state
        for h in range(GLA_HEADS):
            sbd_ref[2 * b + h // 2, (h % 2) * GLA_DK:(h % 2 + 1) * GLA_DK,
                    (h % 2) * GLA_DV:(h % 2 + 1) * GLA_DV] = new_state[h * GLA_DK:(h + 1) * GLA_DK].astype(BF16)


def _front_kernel(*refs, layer, n_src, chained):
    sinks_ref, refs = refs[0], refs[1:]
    if chained:
        refs = refs[3:]
    x_refs, refs = refs[:n_src], refs[n_src:]
    vec_ref, w_ref, wg2_ref, tri_ref, lev_ref, plev_ref = refs[:6]
    ffn_f32_refs, refs = refs[6:10], refs[10:]
    if n_src > 1:
        xo_ref, refs = refs[0], refs[1:]
    (m_ref, pfs_ref, pbs_ref, s_out_ref, k_out_ref, v_out_ref) = refs[:6]
    ffn_bf16_refs, refs = refs[6:10], refs[10:]
    wbf_ref, wg2s_ref, z_ref, pf_ref, pb_ref, kprev_ref, vprev_ref, state_ref, sbd_ref = refs
    i = pl.program_id(0)
    t = i - N_PREP

    @pl.when(i < N_PREP)
    def _():
        _convert_proj_weights(i, w_ref, wg2_ref, wbf_ref, wg2s_ref)

    @pl.when((t >= 1) & (t <= N_CONV))
    def _():
        for src, dst in zip(ffn_f32_refs, ffn_bf16_refs):
            dst[...] = src[0].astype(BF16)

    @pl.when(i == 0)
    def _():
        kprev_ref[...] = jnp.zeros_like(kprev_ref)
        vprev_ref[...] = jnp.zeros_like(vprev_ref)
        state_ref[...] = jnp.zeros_like(state_ref)
        sbd_ref[...] = jnp.zeros_like(sbd_ref)

    def load_x():
        if n_src == 1:
            return x_refs[0][...]
        x = jnp.where(t == 0, x_refs[2][...],
                      jnp.where(t < N_BLOCKS, x_refs[0][...].reshape(ROW_TILE, D_MODEL), x_refs[1][...]))
        xo_ref[...] = x
        return x

    @pl.when((t >= 0) & (t < N_BLOCKS))
    def _():
        _project_tile(load_x(), vec_ref, wbf_ref, wg2s_ref, z_ref, pf_ref, pb_ref)
        _attention_tile(t, layer, sinks_ref, pf_ref, pb_ref, m_ref, kprev_ref, vprev_ref)
        _gla_tile(t, _gla_decays(pf_ref, tri_ref, lev_ref), pf_ref, pb_ref, vec_ref, plev_ref, m_ref,
                  state_ref, sbd_ref)

    @pl.when(t == N_BLOCKS - 1)
    def _():
        own = 0 if chained else layer
        if not chained:
            for ref in (s_out_ref, k_out_ref, v_out_ref):
                ref[...] = jnp.zeros_like(ref)
        s_out_ref[own] = state_ref[...]
        for b in range(BATCH):
            k_out_ref[own, b] = pf_ref[b * BLOCK:(b + 1) * BLOCK, PF_K:PF_V].T
            v_out_ref[own, b] = pf_ref[b * BLOCK:(b + 1) * BLOCK, PF_V:PF_GQ].T

    @pl.when(t >= N_BLOCKS)
    def _():
        _project_tile(load_x(), vec_ref, wbf_ref, wg2s_ref, z_ref, pfs_ref, pbs_ref)
        m_ref[...] = jnp.zeros_like(m_ref)


def _front(layer, sinks, x_srcs, vecs, w_in, w_g2, ffn_f32, consts, prev):
    n_src = len(x_srcs)
    step = lambda i: i - N_PREP
    conv_of = lambda i: jnp.clip(step(i) - 1, 0, N_CONV - 1)
    tile_of = lambda i: jnp.where(step(i) <= 0, LEAD_TILE, jnp.minimum(step(i) - 1, LEAD_TILE - 1))
    sample_tile_of = lambda i: jnp.clip(step(i) - N_BLOCKS, 0, SAMPLE_TILES - 1)
    full = lambda a: pl.BlockSpec(a.shape, lambda i, *_: (0,) * a.ndim)
    if prev is None:
        per_batch = lambda r, c: pl.BlockSpec((DEPTH, BATCH, r, c), lambda i, *_: (0, 0, 0, 0))
    else:
        per_batch = lambda r, c: pl.BlockSpec((1, BATCH, r, c), lambda i, *_: (layer, 0, 0, 0))
    if n_src == 1:
        x_specs = [pl.BlockSpec((ROW_TILE, D_MODEL), lambda i, *_: (tile_of(i), 0))]
    else:
        x_specs = [
            pl.BlockSpec((BATCH, BLOCK, D_MODEL), lambda i, *_: (0, jnp.clip(step(i) - 1, 0, MAIN_TILES - 1), 0)),
            pl.BlockSpec((ROW_TILE, D_MODEL), lambda i, *_: (sample_tile_of(i), 0), pipeline_mode=pl.Buffered(1)),
            pl.BlockSpec((ROW_TILE, D_MODEL), lambda i, *_: (0, 0), pipeline_mode=pl.Buffered(1)),
        ]
    const_args = (consts["tri"], consts["lev_low"], consts["pair_level"])
    chained = prev is not None
    prev_args = tuple(prev) if chained else ()
    in_specs = ([pl.BlockSpec(memory_space=pl.ANY)] * len(prev_args) + x_specs + [
        pl.BlockSpec((V_ROWS, D_MODEL), lambda i, *_: (layer, 0)),
        pl.BlockSpec((1, IN_W, W_ROWS), lambda i, *_: (layer, 0, jnp.minimum(i, N_PREP - 1))),
        pl.BlockSpec((1, GATE_RANK, GLA_K_W), lambda i, *_: (layer, 0, 0)),
    ] + [full(a) for a in const_args] + [
        pl.BlockSpec((1, a.shape[1] // N_CONV, a.shape[2]), lambda i, *_: (layer, conv_of(i), 0)) for a in ffn_f32])
    out_specs = [pl.BlockSpec((ROW_TILE, D_MIX), lambda i, *_: (tile_of(i), 0)),
                 pl.BlockSpec((ROW_TILE, PF_W), lambda i, *_: (sample_tile_of(i), 0)),
                 pl.BlockSpec((ROW_TILE, PB_W), lambda i, *_: (sample_tile_of(i), 0)),
                 per_batch(GLA_K_W, GLA_DV), per_batch(BLOCK, ATT_KV_W), per_batch(BLOCK, ATT_KV_W)]
    out_specs += [pl.BlockSpec((a.shape[1] // N_CONV, a.shape[2]), lambda i, *_: (conv_of(i), 0)) for a in ffn_f32]
    out_shape = [jax.ShapeDtypeStruct((TOTAL_ROWS, D_MIX), BF16),
                 jax.ShapeDtypeStruct((SAMPLE_ROWS, PF_W), F32),
                 jax.ShapeDtypeStruct((SAMPLE_ROWS, PB_W), BF16),
                 jax.ShapeDtypeStruct((DEPTH, BATCH, GLA_K_W, GLA_DV), F32),
                 jax.ShapeDtypeStruct((DEPTH, BATCH, BLOCK, ATT_KV_W), F32),
                 jax.ShapeDtypeStruct((DEPTH, BATCH, BLOCK, ATT_KV_W), F32)]
    out_shape += [jax.ShapeDtypeStruct(a.shape[1:], BF16) for a in ffn_f32]
    n_lead_out = 0
    if n_src > 1:
        out_specs = [pl.BlockSpec((ROW_TILE, D_MODEL), lambda i, *_: (tile_of(i), 0))] + out_specs
        out_shape = [jax.ShapeDtypeStruct((TOTAL_ROWS, D_MODEL), F32)] + out_shape
        n_lead_out = 1
    grid_spec = pltpu.PrefetchScalarGridSpec(
        num_scalar_prefetch=1,
        grid=(N_PREP + N_BLOCKS + SAMPLE_TILES,),
        in_specs=in_specs,
        out_specs=out_specs,
        scratch_shapes=[pltpu.VMEM((D_MODEL, PROJ_W), BF16), pltpu.VMEM((LANES, GLA_K_W), BF16),
                        pltpu.VMEM((ROW_TILE, PROJ_W), F32),
                        pltpu.VMEM((ROW_TILE, PF_W), F32), pltpu.VMEM((ROW_TILE, PB_W), BF16),
                        pltpu.VMEM((BATCH * N_KV_HEADS, BLOCK, ATT_KV_W), BF16),
                        pltpu.VMEM((BATCH * N_KV_HEADS, BLOCK, ATT_KV_W), BF16),
                        pltpu.VMEM((BATCH, GLA_K_W, GLA_DV), F32),
                        pltpu.VMEM((BATCH * GLA_HEADS // 2, 2 * GLA_DK, 2 * GLA_DV), BF16)],
    )
    res = pl.pallas_call(
        functools.partial(_front_kernel, layer=layer, n_src=n_src, chained=chained),
        grid_spec=grid_spec,
        out_shape=out_shape,
        input_output_aliases={1 + n: n_lead_out + 3 + n for n in range(len(prev_args))},
        compiler_params=pltpu.CompilerParams(dimension_semantics=("arbitrary",),
                                             vmem_limit_bytes=VMEM_LIMIT),
        name="front",
    )(sinks, *prev_args, *x_srcs, vecs, w_in, w_g2, *const_args, *ffn_f32)
    x_all = res[0] if n_src > 1 else x_srcs[0]
    merged, pfs, pbs, ps, pk, pv = res[n_lead_out:n_lead_out + 6]
    return x_all, merged, (pfs, pbs), (ps, pk, pv), tuple(res[n_lead_out + 6:])


def _mix_sample_kernel(*refs, layer, n_alias):
    sinks_ref, refs = refs[0], refs[n_alias + 1:]
    (pf_ref, pb_ref, ck_ref, cv_ref, st_ref,
     vec_ref, tri_ref, sones_ref, lev_ref, ones_ref, plev_ref,
     m_ref, ck_out_ref, cv_out_ref, st_out_ref) = refs
    q_ref, k_ref, v_ref, gq_ref, gk_ref, ld_ref, gv_ref, og_ref = _proj_views(pf_ref, pb_ref)
    gq = gq_ref[...]
    gk = gk_ref[...]
    gv = gv_ref[...]
    gvf = gv.astype(F32)
    hi, lo = _split(ld_ref[...])
    tri = tri_ref[...]
    g_cum = _dot(tri, hi) + _dot(tri, lo)
    sones = sones_ref[...]
    g_tot = _dot(sones, hi) + _dot(sones, lo)
    lev = lev_ref[...]
    level_sums = _dot(lev, hi) + _dot(lev, lo)
    a = _gla_intra([gq], [gk], lambda b, l: jnp.exp(level_sums[l * BLOCK:(l + 1) * BLOCK]),
                   range(LOW_LEVELS), plev_ref[...])[0]
    o_intra = jnp.concatenate(
        [_dot(a[h].astype(BF16), gv[:, h * GLA_DV:(h + 1) * GLA_DV]) for h in range(GLA_HEADS)], axis=1)
    q_dec = gq * jnp.exp(g_cum)
    k_dec = gk * jnp.exp(g_tot - g_cum)
    hm8 = _head_masks(DEC_SEQ)
    ones8 = ones_ref[...]
    hi_f, lo_f = hi.astype(F32), lo.astype(F32)

    qf = q_ref[...].astype(F32)
    kf = k_ref[...]
    vf = v_ref[...]
    low8 = lax.broadcasted_iota(jnp.int32, (DEC_SEQ, LANES), 1) < HEAD_DIM
    nkeys = WINDOW + DEC_SEQ
    srow = lax.broadcasted_iota(jnp.int32, (N_HEADS * DEC_SEQ, nkeys), 0)
    scol = lax.broadcasted_iota(jnp.int32, (N_HEADS * DEC_SEQ, nkeys), 1)
    t_of_row = srow & (DEC_SEQ - 1)
    amask = ((scol < WINDOW) & (scol > t_of_row)) | ((scol >= WINDOW) & (scol - WINDOW <= t_of_row))
    rid = lax.broadcasted_iota(jnp.int32, (N_HEADS * DEC_SEQ, 1), 0) >> LOW_LEVELS
    sink_col = jnp.zeros((N_HEADS * DEC_SEQ, 1), F32)
    for i in range(N_HEADS):
        sink_col = jnp.where(rid == i, sinks_ref[layer, i], sink_col)

    seqs = range(SEQ_GROUP)
    rows = [slice(b * DEC_SEQ, (b + 1) * DEC_SEQ) for b in seqs]

    def stacked_queries(b):
        pieces = []
        for j in range(N_HEADS // 2):
            g = (2 * j) // (N_HEADS // N_KV_HEADS)
            qt = qf[rows[b], j * LANES:(j + 1) * LANES]
            swapped = pltpu.roll(qt, HEAD_DIM, axis=1)
            own_lanes = low8 if g == 0 else ~low8
            even, odd = (qt, swapped) if g == 0 else (swapped, qt)
            pieces += [jnp.where(own_lanes, even, 0.0), jnp.where(own_lanes, odd, 0.0)]
        return jnp.concatenate(pieces, axis=0).astype(BF16)

    lane = lax.broadcasted_iota(jnp.int32, (WINDOW, WINDOW), 1)
    old_lanes = lane < WINDOW - DEC_SEQ

    def shifted_buffer(buf, new_rows):
        wide = jnp.concatenate([jnp.zeros((WINDOW - DEC_SEQ, LANES), F32), new_rows], axis=0)
        return jnp.where(old_lanes, pltpu.roll(buf, WINDOW - DEC_SEQ, axis=1), wide.T)

    qp = [stacked_queries(b) for b in seqs]
    ck = [ck_ref[0, b] for b in seqs]
    cv = [cv_ref[0, b] for b in seqs]
    for b in seqs:
        ck_out_ref[0, b] = shifted_buffer(ck[b], kf[rows[b]])
        cv_out_ref[0, b] = shifted_buffer(cv[b], vf[rows[b]])
    s = [jnp.concatenate([_dot(qp[b], ck[b].astype(BF16)), _dot_nt(qp[b], kf[rows[b]].astype(BF16))], axis=1)
         for b in seqs]
    s = [jnp.where(amask, s[b], MASK_VALUE) for b in seqs]
    m = [jnp.maximum(jnp.max(s[b], axis=-1, keepdims=True), sink_col) for b in seqs]
    p = [jnp.exp(s[b] - m[b]) for b in seqs]
    denom = [jnp.sum(p[b], axis=-1, keepdims=True) + jnp.exp(sink_col - m[b]) for b in seqs]
    pb = [p[b].astype(BF16) for b in seqs]
    ob = [(_dot_nt(pb[b][:, :WINDOW], cv[b].astype(BF16))
           + _dot(pb[b][:, WINDOW:], vf[rows[b]].astype(BF16))) / denom[b] for b in seqs]

    def head_tiles(o):
        tiles = []
        for j in range(N_HEADS // 2):
            g = (2 * j) // (N_HEADS // N_KV_HEADS)
            even = o[(2 * j) * DEC_SEQ:(2 * j + 1) * DEC_SEQ]
            odd = o[(2 * j + 1) * DEC_SEQ:(2 * j + 2) * DEC_SEQ]
            if g == 0:
                tiles.append(jnp.where(low8, even, pltpu.roll(odd, HEAD_DIM, axis=1)))
            else:
                tiles.append(jnp.where(low8, pltpu.roll(even, HEAD_DIM, axis=1), odd))
        return jnp.concatenate(tiles, axis=1)

    att_rows = [head_tiles(ob[b]) for b in seqs]

    head_stack = lambda x: jnp.concatenate([jnp.where(mk, x, 0.0) for mk in hm8], axis=0).astype(BF16)
    state = [st_ref[0, b] for b in seqs]
    oi = [_dot(head_stack(q_dec[rows[b]]), state[b].astype(BF16)) for b in seqs]
    inter_rows = [jnp.concatenate([oi[b][h * DEC_SEQ:(h + 1) * DEC_SEQ] for h in range(GLA_HEADS)], axis=1)
                  for b in seqs]
    vstack = [jnp.concatenate([gvf[rows[b]][:, h * GLA_DV:(h + 1) * GLA_DV] for h in range(GLA_HEADS)],
                              axis=0).astype(BF16) for b in seqs]
    kv = [_dot_tn(head_stack(k_dec[rows[b]]), vstack[b]) for b in seqs]
    decay_col = [jnp.exp(_dot_tn(hi_f[rows[b]], ones8) + _dot_tn(lo_f[rows[b]], ones8)) for b in seqs]
    for b in seqs:
        st_out_ref[0, b] = decay_col[b] * state[b] + kv[b]

    m_ref[:, :ATT_Q_W] = jnp.concatenate(att_rows, axis=0).astype(BF16)
    o = o_intra + jnp.concatenate(inter_rows, axis=0)
    gn = vec_ref[V_GN:V_GN + 1, :GLA_DV]
    for h, gh in enumerate(_gla_merge(o, og_ref[...], gn)):
        m_ref[:, ATT_Q_W + h * GLA_DV:ATT_Q_W + (h + 1) * GLA_DV] = gh.astype(BF16)


def _mix_sample(layer, sinks, merged, proj, cache_k, cache_v, state, vecs, consts, prev):
    tok = lambda w_: pl.BlockSpec((BLOCK, w_), lambda i, *_: (i, 0))
    merged_blk = pl.BlockSpec((BLOCK, D_MIX), lambda i, *_: (MAIN_ROWS // BLOCK + i, 0))
    full = lambda a: pl.BlockSpec(a.shape, lambda i, *_: (0,) * a.ndim)
    seq = lambda a: pl.BlockSpec((1, SEQ_GROUP) + a.shape[2:], lambda i, *_: (layer, i, 0, 0))
    ones8 = jnp.ones((DEC_SEQ, LANES), F32)
    const_args = (consts["seq_tri"], consts["seq_ones"], consts["lev_low"], ones8, consts["pair_level"])
    seq_args = (cache_k, cache_v, state)
    alias_args = (merged,) + (tuple(prev) if prev is not None else ())
    grid_spec = pltpu.PrefetchScalarGridSpec(
        num_scalar_prefetch=1,
        grid=(DEC_BATCH // SEQ_GROUP,),
        in_specs=([pl.BlockSpec(memory_space=pl.ANY)] * len(alias_args)
                  + [tok(a.shape[1]) for a in proj] + [seq(a) for a in seq_args]
                  + [pl.BlockSpec((V_ROWS, D_MODEL), lambda i, *_: (layer, 0))]
                  + [full(a) for a in const_args]),
        out_specs=[merged_blk] + [seq(a) for a in seq_args],
    )
    return pl.pallas_call(
        functools.partial(_mix_sample_kernel, layer=layer, n_alias=len(alias_args)),
        grid_spec=grid_spec,
        out_shape=[jax.ShapeDtypeStruct(merged.shape, merged.dtype)]
        + [jax.ShapeDtypeStruct(a.shape, a.dtype) for a in seq_args],
        input_output_aliases={1 + n: n for n in range(len(alias_args))},
        compiler_params=pltpu.CompilerParams(dimension_semantics=("arbitrary",),
                                             vmem_limit_bytes=VMEM_LIMIT),
        name="mix_sample",
    )(sinks, *alias_args, *proj, *seq_args, vecs, *const_args)


def _out_ffn_kernel(*refs, last):
    x_ref, m_ref, vec_ref, wo_ref, wg_ref, wu_ref, wd_ref = refs[:7]
    out_refs = refs[7:9] if last else refs[7:8]
    act_ref = refs[-1]
    x1 = x_ref[...] + _dot(m_ref[...], wo_ref[...])
    ms = jnp.mean(x1 * x1, axis=-1, keepdims=True)
    h = (x1 * lax.rsqrt(ms + RMS_EPS) * vec_ref[V_NORM2:V_NORM2 + 1, :]).astype(BF16)
    for c in range(D_FFN // FFN_CHUNK):
        cols = slice(c * FFN_CHUNK, (c + 1) * FFN_CHUNK)
        gate = _dot(h, wg_ref[:, cols])
        up = _dot(h, wu_ref[:, cols])
        act_ref[:, cols] = (gate * _sigmoid(gate) * up).astype(BF16)
    y = x1 + _dot(act_ref[...], wd_ref[...])
    if last:
        out_refs[0][...] = y.reshape(BATCH, BLOCK, D_MODEL)

        @pl.when(pl.program_id(0) < SAMPLE_TILES)
        def _():
            out_refs[1][...] = out_refs[0][...].reshape(ROW_TILE, D_MODEL)
    else:
        out_refs[0][...] = y


def _out_ffn(layer, x_all, merged, vecs, ffn_weights, last):
    resident = lambda a: pl.BlockSpec(a.shape, lambda i: (0, 0), pipeline_mode=pl.Buffered(1))
    if last:
        n_tiles = MAIN_TILES + SAMPLE_TILES
        tile_of = lambda i: jnp.where(i < SAMPLE_TILES, MAIN_TILES + i, i - SAMPLE_TILES)
        out_specs = [pl.BlockSpec((BATCH, BLOCK, D_MODEL), lambda i: (0, jnp.maximum(i - SAMPLE_TILES, 0), 0)),
                     pl.BlockSpec((ROW_TILE, D_MODEL), lambda i: (jnp.minimum(i, SAMPLE_TILES - 1), 0))]
        out_shape = [jax.ShapeDtypeStruct((BATCH, SEQ, D_MODEL), F32),
                     jax.ShapeDtypeStruct((SAMPLE_ROWS, D_MODEL), F32)]
    else:
        n_tiles = N_TILES
        tile_of = lambda i: i
        out_specs = [pl.BlockSpec((ROW_TILE, D_MODEL), lambda i: (i, 0))]
        out_shape = [jax.ShapeDtypeStruct((TOTAL_ROWS, D_MODEL), F32)]
    tile = lambda w_: pl.BlockSpec((ROW_TILE, w_), lambda i: (tile_of(i), 0))
    return pl.pallas_call(
        functools.partial(_out_ffn_kernel, last=last),
        grid=(n_tiles,),
        in_specs=[tile(D_MODEL), tile(D_MIX), pl.BlockSpec((V_ROWS, D_MODEL), lambda i: (layer, 0))]
        + [resident(w) for w in ffn_weights],
        out_specs=out_specs,
        out_shape=out_shape,
        scratch_shapes=[pltpu.VMEM((ROW_TILE, D_FFN), BF16)],
        compiler_params=pltpu.CompilerParams(dimension_semantics=("arbitrary",),
                                             vmem_limit_bytes=VMEM_LIMIT),
        name="out_ffn",
    )(x_all, merged, vecs, *ffn_weights)


def _vector_slab(norm1, norm2, q_norm, k_norm, b_g, gla_norm):
    pad = lambda a: jnp.pad(a.astype(F32), ((0, 0), (0, D_MODEL - a.shape[1])))
    rows = [norm1.astype(F32), norm2.astype(F32), pad(jnp.tile(q_norm, (1, N_HEADS)) * ATT_SCALE),
            pad(jnp.tile(k_norm, (1, N_KV_HEADS))), pad(b_g), pad(gla_norm)]
    rows += [jnp.zeros((DEPTH, D_MODEL), F32)] * (V_ROWS - len(rows))
    return jnp.stack(rows, axis=1).reshape(DEPTH * V_ROWS, D_MODEL)


def kernel(x_prompt, x_sample, cache_k, cache_v, state_gla, meta, norm1, w_in, q_norm, k_norm, sinks,
           w_g2, b_g, gla_norm, w_o, norm2, w_gate, w_up, w_down):
    consts = _constants()
    dt = x_prompt.dtype
    vecs = _vector_slab(norm1, norm2, q_norm, k_norm, b_g, gla_norm)
    sinks = sinks.astype(F32)
    lead = jnp.tile(jnp.concatenate([jnp.zeros((PAD_LEN, D_MODEL), dt), meta.astype(dt)], axis=0), (BATCH, 1))
    x_srcs = (x_prompt, x_sample.reshape(SAMPLE_ROWS, D_MODEL), lead)
    to_feature_major = lambda c: c.transpose(0, 1, 3, 4, 2).reshape(DEPTH, DEC_BATCH, ATT_KV_W, WINDOW)
    ck_in, cv_in = to_feature_major(cache_k), to_feature_major(cache_v)
    st_in = state_gla.reshape(DEPTH, DEC_BATCH, GLA_K_W, GLA_DV)
    w_in_t = jnp.swapaxes(w_in, 1, 2)

    prompt_outs, sample_outs = None, None
    for l in range(DEPTH):
        x_all, merged, proj_sample, prompt_outs, ffn_bf16 = _front(
            l, sinks, x_srcs, vecs, w_in_t, w_g2, (w_o, w_gate, w_up, w_down), consts, prompt_outs)
        merged, *sample_outs = _mix_sample(l, sinks, merged, proj_sample, ck_in, cv_in, st_in, vecs, consts,
                                           sample_outs)
        x_srcs = tuple(_out_ffn(l, x_all, merged, vecs, ffn_bf16, last=l == DEPTH - 1))

    y_main, y_sample = x_srcs
    ps, pk, pv = prompt_outs
    sk, sv, ss = sample_outs
    kv5 = lambda a, n: a.reshape(DEPTH, n, N_KV_HEADS, HEAD_DIM, WINDOW).transpose(0, 1, 4, 2, 3)
    st5 = lambda a, n: a.reshape(DEPTH, n, GLA_HEADS, GLA_DK, GLA_DV)
    return (y_main, y_sample.reshape(DEC_BATCH, DEC_SEQ, D_MODEL),
            kv5(pk, BATCH), kv5(pv, BATCH), st5(ps, BATCH), kv5(sk, DEC_BATCH), kv5(sv, DEC_BATCH),
            st5(ss, DEC_BATCH))
```

```python
import functools

import jax
import jax.numpy as jnp
import numpy as np
from jax import lax
from jax.experimental import pallas as pl
from jax.experimental.pallas import tpu as pltpu

F32 = jnp.float32
BF16 = jnp.bfloat16

D_MODEL = 1024
BATCH = 4
SEQ = 4096
DEPTH = 2
DEC_BATCH = 128
DEC_SEQ = 8
N_META = 16
WINDOW = 128
BLOCK = 128
PAD_LEN = BLOCK - N_META
N_HEADS = 8
N_KV_HEADS = 2
HEAD_DIM = 64
ATT_SCALE = HEAD_DIM ** -0.5
ATT_Q_W = N_HEADS * HEAD_DIM
ATT_KV_W = N_KV_HEADS * HEAD_DIM
GLA_HEADS = 4
GLA_DK = 64
GLA_DV = 128
GLA_K_W = GLA_HEADS * GLA_DK
GLA_V_W = GLA_HEADS * GLA_DV
GATE_RANK = 16
GATE_NORMALIZER = 16.0
D_MIX = ATT_Q_W + GLA_V_W
D_FFN = 2816
IN_W = 2320
RMS_EPS = 1e-6
MASK_VALUE = -1e30

LANES = 128
N_BLOCKS = 1 + SEQ // BLOCK
MAIN_ROWS = BATCH * SEQ
SAMPLE_ROWS = DEC_BATCH * DEC_SEQ
LEAD_ROWS = BATCH * BLOCK
TOTAL_ROWS = MAIN_ROWS + SAMPLE_ROWS + LEAD_ROWS
ROW_TILE = 512
MAIN_TILES = MAIN_ROWS // ROW_TILE
SAMPLE_TILES = SAMPLE_ROWS // ROW_TILE
N_TILES = TOTAL_ROWS // ROW_TILE
LEAD_TILE = N_TILES - 1
SEQ_GROUP = BLOCK // DEC_SEQ
LAST_TILES_PER_STEP = 2
N_LEVELS = 7
LOW_LEVELS = 3
LOG_DK = 6
N_PREP = 8
W_ROWS = D_MODEL // N_PREP
N_CONV = 16

SRC_LOW, SRC_OG = 1792, 1808
OFF_Q, OFF_K, OFF_V = 0, 512, 640
OFF_GQ, OFF_GK, OFF_GV, OFF_OG, OFF_LOW = 768, 1024, 1280, 1792, 2304
PROJ_W = OFF_LOW + LANES
PROJ_CHUNK = 512
FFN_CHUNK = 256
SCORE_LOOKAHEAD = 1
VMEM_LIMIT = 58 * 1024 * 1024

V_NORM1, V_NORM2, V_QG, V_KG, V_BG, V_GN, V_ROWS = 0, 1, 2, 3, 4, 5, 8

PF_K, PF_V, PF_GQ, PF_GK, PF_LD, PF_OG, PF_W = 0, 128, 256, 512, 768, 1024, 1536
PB_Q, PB_GV, PB_W = 0, 512, 1024


def _proj_views(pf_ref, pb_ref):
    f = lambda a, b: pf_ref.at[:, a:b]
    return (pb_ref.at[:, PB_Q:PB_GV], f(PF_K, PF_V), f(PF_V, PF_GQ), f(PF_GQ, PF_GK), f(PF_GK, PF_LD),
            f(PF_LD, PF_OG), pb_ref.at[:, PB_GV:PB_W], f(PF_OG, PF_W))


def _dot(a, b):
    return jnp.dot(a, b, preferred_element_type=F32)


def _dot_nt(a, b):
    return lax.dot_general(a, b, (((1,), (1,)), ((), ())), preferred_element_type=F32)


def _dot_tn(a, b):
    return lax.dot_general(a, b, (((0,), (0,)), ((), ())), preferred_element_type=F32)


def _split(x):
    hi = x.astype(BF16)
    lo = (x - hi.astype(F32)).astype(BF16)
    return hi, lo


def _sigmoid(x):
    return 1.0 / (1.0 + jnp.exp(-x))


def _level_matrix(levels, n=BLOCK):
    out = np.zeros((len(levels) * n, n), np.float32)
    for i, l in enumerate(levels):
        size = 2 << l
        for t in range(n):
            mid = (t // size) * size + size // 2 - 1
            if (t >> l) & 1:
                out[i * n + t, mid + 1:t + 1] = 1.0
            else:
                out[i * n + t, t + 1:mid + 1] = 1.0
    return out


def _constants():
    r = np.arange(BLOCK)
    tri = (r[None, :] <= r[:, None]).astype(np.float32)
    same_seq = (r[None, :] // DEC_SEQ) == (r[:, None] // DEC_SEQ)
    diff = np.maximum(r[:, None] ^ r[None, :], 1)
    pair_level = np.where(r[None, :] < r[:, None], np.floor(np.log2(diff)).astype(np.int32),
                          np.where(r[None, :] == r[:, None], N_LEVELS, N_LEVELS + 1)).astype(np.int32)
    return dict(
        tri=jnp.asarray(tri, BF16),
        seq_tri=jnp.asarray(tri * same_seq, BF16),
        seq_ones=jnp.asarray(same_seq.astype(np.float32), BF16),
        lev_low=jnp.asarray(_level_matrix(range(LOW_LEVELS)), BF16),
        pair_level=jnp.asarray(pair_level),
    )


def _head_masks(rows):
    lane = lax.broadcasted_iota(jnp.int32, (rows, GLA_K_W), 1)
    return [(lane >> LOG_DK) == h for h in range(GLA_HEADS)]


def _gla_intra(gq, gk, level_decay, levels, pair_level):
    rows = gq[0].shape[0]
    streams = range(len(gq))
    low_half = lax.broadcasted_iota(jnp.int32, (rows, LANES), 1) < GLA_DK

    def pair_products(qh, kh):
        qh, kh = qh.astype(BF16), kh.astype(BF16)
        out = []
        for pair in range(GLA_HEADS // 2):
            kl = kh[:, pair * LANES:(pair + 1) * LANES]
            zero = jnp.zeros_like(kl)
            stacked = jnp.concatenate([jnp.where(low_half, kl, zero), jnp.where(low_half, zero, kl)], axis=0)
            p = _dot_nt(qh[:, pair * LANES:(pair + 1) * LANES], stacked)
            out += [p[:, :rows], p[:, rows:]]
        return out

    row = lax.broadcasted_iota(jnp.int32, (rows, 1), 0)

    def upper_q_lower_k(b, l):
        size = 1 << l
        if size % 8:
            return jnp.where(((row >> l) & 1) == 1, gq[b], gk[b])
        return jnp.concatenate([(gq[b] if i & 1 else gk[b])[i * size:(i + 1) * size]
                                for i in range(rows // size)], axis=0)

    def diagonal(b):
        out = []
        for pair in range(GLA_HEADS // 2):
            prod = (gq[b] * gk[b])[:, pair * LANES:(pair + 1) * LANES]
            out += [jnp.sum(jnp.where(low_half, prod, 0.0), axis=-1, keepdims=True),
                    jnp.sum(jnp.where(low_half, 0.0, prod), axis=-1, keepdims=True)]
        return out

    def blocks(v, size, parity):
        return jnp.concatenate([v[i * size:(i + 1) * size] for i in range(parity, rows // size, 2)], axis=0)

    def interleave(lower, upper, size):
        pieces = []
        for i in range(rows // (2 * size)):
            pieces += [lower[i * size:(i + 1) * size], upper[i * size:(i + 1) * size]]
        return jnp.concatenate(pieces, axis=0)

    on_diag = pair_level == N_LEVELS
    a = [[jnp.where(on_diag, d, 0.0).astype(BF16) for d in diagonal(b)] for b in streams]
    for l in levels:
        at_level = pair_level == l
        size = 1 << l
        for b in streams:
            x = upper_q_lower_k(b, l) * level_decay(b, l)
            if size % 16:
                a[b] = [jnp.where(at_level, p.astype(BF16), a_h) for p, a_h in zip(pair_products(x, x), a[b])]
            else:
                mask_up = blocks(pair_level, size, 1) == l
                p_up = pair_products(blocks(x, size, 1), x)
                a[b] = [interleave(blocks(a_h, size, 0),
                                   jnp.where(mask_up, p.astype(BF16), blocks(a_h, size, 1)), size)
                        for p, a_h in zip(p_up, a[b])]
    return a


def _gla_merge(o, og, gn):
    outs = []
    for h in range(GLA_HEADS):
        oh = o[:, h * GLA_DV:(h + 1) * GLA_DV]
        gh = og[:, h * GLA_DV:(h + 1) * GLA_DV]
        ms = jnp.mean(oh * oh, axis=-1, keepdims=True)
        outs.append(oh * lax.rsqrt(ms + RMS_EPS) * gn * (gh * _sigmoid(gh)))
    return outs


def _dup_halves(x):
    low = lax.broadcasted_iota(jnp.int32, x.shape, 1) < HEAD_DIM
    rolled = pltpu.roll(x, HEAD_DIM, axis=1)
    return jnp.where(low, x, rolled), jnp.where(low, rolled, x)


def _convert_proj_weights(i, w_ref, wg2_ref, wbf_ref, wg2s_ref):
    rows = pl.ds(pl.multiple_of(i * W_ROWS, W_ROWS), W_ROWS)

    def put(dst, src, n=LANES):
        tile = w_ref[0, src:src + n, :]
        if n < LANES:
            tile = jnp.concatenate([tile, jnp.zeros((LANES - n, W_ROWS), F32)], axis=0)
        wbf_ref[rows, dst:dst + LANES] = tile.T.astype(BF16)

    for c in range(SRC_LOW // LANES):
        put(c * LANES, c * LANES)
    for c in range((IN_W - SRC_OG) // LANES):
        put(OFF_OG + c * LANES, SRC_OG + c * LANES)
    put(OFF_LOW, SRC_LOW, GATE_RANK)

    @pl.when(i == 0)
    def _():
        wg2s_ref[...] = jnp.concatenate(
            [wg2_ref[0], jnp.zeros((LANES - GATE_RANK, GLA_K_W), F32)], axis=0).astype(BF16)


def _project_tile(x, vec_ref, wbf_ref, wg2s_ref, z_ref, pf_ref, pb_ref):
    q_ref, k_ref, v_ref, gq_ref, gk_ref, ld_ref, gv_ref, og_ref = _proj_views(pf_ref, pb_ref)
    ms = jnp.mean(x * x, axis=-1, keepdims=True)
    h = (x * lax.rsqrt(ms + RMS_EPS) * vec_ref[V_NORM1:V_NORM1 + 1, :]).astype(BF16)

    low_half = lax.broadcasted_iota(jnp.int32, (ROW_TILE, LANES), 1) < HEAD_DIM

    def head_norm(t):
        t2 = t * t
        ms_lo = jnp.sum(jnp.where(low_half, t2, 0.0), axis=-1, keepdims=True) * (1.0 / HEAD_DIM)
        ms_hi = jnp.sum(jnp.where(low_half, 0.0, t2), axis=-1, keepdims=True) * (1.0 / HEAD_DIM)
        return t * jnp.where(low_half, lax.rsqrt(ms_lo + RMS_EPS), lax.rsqrt(ms_hi + RMS_EPS))

    def q_part():
        for j in range(ATT_Q_W // LANES):
            cols = slice(j * LANES, (j + 1) * LANES)
            q_ref[:, cols] = (head_norm(z_ref[:, cols]) * vec_ref[V_QG:V_QG + 1, cols]).astype(BF16)

    def kv_part():
        k_ref[...] = head_norm(z_ref[:, OFF_K:OFF_V]) * vec_ref[V_KG:V_KG + 1, :ATT_KV_W]
        v_ref[...] = z_ref[:, OFF_V:OFF_GQ]
        gq_ref[...] = z_ref[:, OFF_GQ:OFF_GK] * (GLA_DK ** -0.5)

    def gk_part():
        gk_ref[...] = z_ref[:, OFF_GK:OFF_GV]

    def gv_part():
        gv_ref[...] = z_ref[:, OFF_GV:OFF_OG].astype(BF16)
        og_ref[...] = z_ref[:, OFF_OG:OFF_LOW]

    def gate_part():
        logit = (_dot(z_ref[:, OFF_LOW:PROJ_W].astype(BF16), wg2s_ref[...])
                 + vec_ref[V_BG:V_BG + 1, :GLA_K_W])
        log_sig = jnp.minimum(logit, 0.0) - jnp.log1p(jnp.exp(-jnp.abs(logit)))
        ld_ref[...] = log_sig * (1.0 / GATE_NORMALIZER)

    pieces = [q_part, kv_part, gk_part, gv_part, gate_part]
    for c in [len(pieces) - 1] + list(range(len(pieces) - 1)):
        cols = slice(c * PROJ_CHUNK, min((c + 1) * PROJ_CHUNK, PROJ_W))
        z_ref[:, cols] = _dot(h, wbf_ref[:, cols])
        pieces[c]()


def _attention_tile(blk, layer, sinks_ref, pf_ref, pb_ref, m_ref, kprev_ref, vprev_ref):
    q_ref, k_ref, v_ref = _proj_views(pf_ref, pb_ref)[:3]
    per_group = N_HEADS // N_KV_HEADS
    grows = per_group * BLOCK
    row = lax.broadcasted_iota(jnp.int32, (grows, BLOCK), 0) & (BLOCK - 1)
    col = lax.broadcasted_iota(jnp.int32, (grows, BLOCK), 1)
    own = col <= row
    kpos = jnp.where(own, blk * BLOCK, (blk - 1) * BLOCK) + col - PAD_LEN
    live = kpos >= 0
    low_half = lax.broadcasted_iota(jnp.int32, (BLOCK, LANES), 1) < HEAD_DIM
    head_of_row = lax.broadcasted_iota(jnp.int32, (grows, 1), 0) >> N_LEVELS
    sink_cols = []
    for g in range(N_KV_HEADS):
        sink = jnp.zeros((grows, 1), F32)
        for r in range(per_group):
            sink = jnp.where(head_of_row == r, sinks_ref[layer, g * per_group + r], sink)
        sink_cols.append(sink)

    units =[(b, g) for b in range(BATCH) for g in range(N_KV_HEADS)]
    rows = [slice(b * BLOCK, (b + 1) * BLOCK) for b in range(BATCH)]
    tiles = [range(g * per_group // 2, (g + 1) * per_group // 2) for g in range(N_KV_HEADS)]
    kdup = [_dup_halves(k_ref[rows[b], :].astype(BF16)) for b in range(BATCH)]
    vdup = [_dup_halves(v_ref[rows[b], :].astype(BF16)) for b in range(BATCH)]
    kk = [jnp.concatenate([kprev_ref[b * N_KV_HEADS + g], kdup[b][g]], axis=0) for b, g in units]
    vv = [jnp.concatenate([vprev_ref[b * N_KV_HEADS + g], vdup[b][g]], axis=0) for b, g in units]
    for b, g in units:
        kprev_ref[b * N_KV_HEADS + g] = kdup[b][g]
        vprev_ref[b * N_KV_HEADS + g] = vdup[b][g]

    def stacked_queries(b, g):
        pieces = []
        for j in tiles[g]:
            qt = q_ref[rows[b], j * LANES:(j + 1) * LANES]
            zero = jnp.zeros_like(qt)
            pieces += [jnp.where(low_half, qt, zero), jnp.where(low_half, zero, qt)]
        return jnp.concatenate(pieces, axis=0)

    def scores(u):
        b, g = units[u]
        return _dot_nt(stacked_queries(b, g), kk[u])

    def attend(u, s):
        b, g = units[u]
        s = jnp.where(live, jnp.where(own, s[:, BLOCK:], s[:, :BLOCK]), MASK_VALUE)
        m = jnp.maximum(jnp.max(s, axis=-1, keepdims=True), sink_cols[g])
        p = jnp.exp(s - m)
        denom = jnp.sum(p, axis=-1, keepdims=True) + jnp.exp(sink_cols[g] - m)
        p2 = jnp.concatenate([jnp.where(own, 0.0, p), jnp.where(own, p, 0.0)], axis=1).astype(BF16)
        o = _dot(p2, vv[u]) / denom
        for n, j in enumerate(tiles[g]):
            o_lo = o[2 * n * BLOCK:(2 * n + 1) * BLOCK]
            o_hi = o[(2 * n + 1) * BLOCK:(2 * n + 2) * BLOCK]
            m_ref[rows[b], j * LANES:(j + 1) * LANES] = jnp.where(low_half, o_lo, o_hi).astype(BF16)

    pending = [scores(u) for u in range(SCORE_LOOKAHEAD)]
    for u in range(len(units)):
        if u + SCORE_LOOKAHEAD < len(units):
            pending.append(scores(u + SCORE_LOOKAHEAD))
        attend(u, pending.pop(0))


def _gla_decays(pf_ref, tri_ref, lev_ref):
    ld_ref = pf_ref.at[:, PF_LD:PF_OG]
    ld_all = jnp.concatenate([ld_ref[b * BLOCK:(b + 1) * BLOCK, :] for b in range(BATCH)], axis=1)
    hi, lo = _split(ld_all)
    tri = tri_ref[...]
    g_cum_all = _dot(tri, hi) + _dot(tri, lo)
    lev = lev_ref[...]
    low_sums = _dot(lev, hi) + _dot(lev, lo)

    def level_decay(l, cols):
        if l < LOW_LEVELS:
            return jnp.exp(low_sums[l * BLOCK:(l + 1) * BLOCK, cols])
        half = 1 << l
        g = g_cum_all[:, cols]
        pieces = []
        for p in range(BLOCK // (2 * half)):
            mid = g[p * 2 * half + half - 1:p * 2 * half + half, :]
            pieces += [mid - g[p * 2 * half:p * 2 * half + half], g[p * 2 * half + half:(p + 1) * 2 * half] - mid]
        return jnp.exp(jnp.concatenate(pieces, axis=0))

    g_last_all = jnp.broadcast_to(g_cum_all[BLOCK - 1:BLOCK, :], g_cum_all.shape)
    decay_col_all = jnp.exp(jnp.concatenate(
        [g_last_all[:, c * LANES:(c + 1) * LANES].T for c in range(BATCH * GLA_K_W // LANES)], axis=0))
    return g_cum_all, level_decay, decay_col_all


def _gla_tile(blk, decays, pf_ref, pb_ref, vec_ref, plev_ref, m_ref, state_ref, sbd_ref):
    g_cum_all, level_decay, decay_col_all = decays
    gq_ref, gk_ref, _, gv_ref, og_ref = _proj_views(pf_ref, pb_ref)[3:]
    rpos = blk * BLOCK + lax.broadcasted_iota(jnp.int32, (BLOCK, 1), 0) - PAD_LEN
    valid = (rpos >= 0).astype(F32)
    pair_level = plev_ref[...]
    gn = vec_ref[V_GN:V_GN + 1, :GLA_DV]
    streams = range(BATCH)
    rows = [slice(b * BLOCK, (b + 1) * BLOCK) for b in streams]
    cols = [slice(b * GLA_K_W, (b + 1) * GLA_K_W) for b in streams]
    gq = [gq_ref[rows[b], :] for b in streams]
    gk = [gk_ref[rows[b], :] * valid for b in streams]
    gv = [gv_ref[rows[b], :] for b in streams]
    g_cum = [g_cum_all[:, cols[b]] for b in streams]
    a = _gla_intra(gq, gk, lambda b, l: level_decay(l, cols[b]), range(N_LEVELS), pair_level)

    pairs = range(GLA_HEADS // 2)
    q_dec = [(gq[b] * jnp.exp(g_cum[b])).astype(BF16) for b in streams]
    o = [jnp.concatenate([_dot(q_dec[b][:, i * LANES:(i + 1) * LANES], sbd_ref[2 * b + i]) for i in pairs], axis=1)
         for b in streams]
    o = [o[b] + jnp.concatenate(
        [_dot(a[b][h].astype(BF16), gv[b][:, h * GLA_DV:(h + 1) * GLA_DV]) for h in range(GLA_HEADS)], axis=1)
        for b in streams]
    for b in streams:
        for h, gh in enumerate(_gla_merge(o[b], og_ref[rows[b], :], gn)):
            m_ref[rows[b], ATT_Q_W + h * GLA_DV:ATT_Q_W + (h + 1) * GLA_DV] = gh.astype(BF16)

    k_dec = [(gk[b] * jnp.exp(g_cum[b][BLOCK - 1:BLOCK, :] - g_cum[b])).astype(BF16) for b in streams]
    kv = [[_dot_tn(k_dec[b][:, i * LANES:(i + 1) * LANES], gv[b][:, 2 * i * GLA_DV:(2 * i + 2) * GLA_DV])
           for i in pairs] for b in streams]
    for b in streams:
        new_state = decay_col_all[cols[b], :] * state_ref[b] + jnp.concatenate(
            [kv[b][h // 2][(h % 2) * GLA_DK:(h % 2 + 1) * GLA_DK, (h % 2) * GLA_DV:(h % 2 + 1) * GLA_DV]
             for h in range(GLA_HEADS)], axis=0)
        state_ref[b] = new_state
        for h in range(GLA_HEADS):
            sbd_ref[2 * b + h // 2, (h % 2) * GLA_DK:(h % 2 + 1) * GLA_DK,
                    (h % 2) * GLA_DV:(h % 2 + 1) * GLA_DV] = new_state[h * GLA_DK:(h + 1) * GLA_DK].astype(BF16)


def _front_kernel(*refs, layer, n_src, chained):
    sinks_ref, refs = refs[0], refs[1:]
    if chained:
        refs = refs[3:]
    x_refs, refs = refs[:n_src], refs[n_src:]
    vec_ref, w_ref, wg2_ref, tri_ref, lev_ref, plev_ref = refs[:6]
    ffn_f32_refs, refs = refs[6:10], refs[10:]
    if n_src > 1:
        xo_ref, refs = refs[0], refs[1:]
    (m_ref, pfs_ref, pbs_ref, s_out_ref, k_out_ref, v_out_ref) = refs[:6]
    ffn_bf16_refs, refs = refs[6:10], refs[10:]
    wbf_ref, wg2s_ref, z_ref, pf_ref, pb_ref, kprev_ref, vprev_ref, state_ref, sbd_ref = refs
    i = pl.program_id(0)
    t = i - N_PREP

    @pl.when(i < N_PREP)
    def _():
        _convert_proj_weights(i, w_ref, wg2_ref, wbf_ref, wg2s_ref)

    @pl.when((t >= 1) & (t <= N_CONV))
    def _():
        for src, dst in zip(ffn_f32_refs, ffn_bf16_refs):
            dst[...] = src[0].astype(BF16)

    @pl.when(i == 0)
    def _():
        kprev_ref[...] = jnp.zeros_like(kprev_ref)
        vprev_ref[...] = jnp.zeros_like(vprev_ref)
        state_ref[...] = jnp.zeros_like(state_ref)
        sbd_ref[...] = jnp.zeros_like(sbd_ref)

    def load_x():
        if n_src == 1:
            return x_refs[0][...]
        x = jnp.where(t == 0, x_refs[2][...],
                      jnp.where(t < N_BLOCKS, x_refs[0][...].reshape(ROW_TILE, D_MODEL), x_refs[1][...]))
        xo_ref[...] = x
        return x

    @pl.when((t >= 0) & (t < N_BLOCKS))
    def _():
        _project_tile(load_x(), vec_ref, wbf_ref, wg2s_ref, z_ref, pf_ref, pb_ref)
        _attention_tile(t, layer, sinks_ref, pf_ref, pb_ref, m_ref, kprev_ref, vprev_ref)
        _gla_tile(t, _gla_decays(pf_ref, tri_ref, lev_ref), pf_ref, pb_ref, vec_ref, plev_ref, m_ref,
                  state_ref, sbd_ref)

    @pl.when(t == N_BLOCKS - 1)
    def _():
        own = 0 if chained else layer
        if not chained:
            for ref in (s_out_ref, k_out_ref, v_out_ref):
                ref[...] = jnp.zeros_like(ref)
        s_out_ref[own] = state_ref[...]
        for b in range(BATCH):
            k_out_ref[own, b] = pf_ref[b * BLOCK:(b + 1) * BLOCK, PF_K:PF_V].T
            v_out_ref[own, b] = pf_ref[b * BLOCK:(b + 1) * BLOCK, PF_V:PF_GQ].T

    @pl.when(t >= N_BLOCKS)
    def _():
        _project_tile(load_x(), vec_ref, wbf_ref, wg2s_ref, z_ref, pfs_ref, pbs_ref)
        m_ref[...] = jnp.zeros_like(m_ref)


def _front(layer, sinks, x_srcs, vecs, w_in, w_g2, ffn_f32, consts, prev):
    n_src = len(x_srcs)
    step = lambda i: i - N_PREP
    conv_of = lambda i: jnp.clip(step(i) - 1, 0, N_CONV - 1)
    tile_of = lambda i: jnp.where(step(i) <= 0, LEAD_TILE, jnp.minimum(step(i) - 1, LEAD_TILE - 1))
    sample_tile_of = lambda i: jnp.clip(step(i) - N_BLOCKS, 0, SAMPLE_TILES - 1)
    full = lambda a: pl.BlockSpec(a.shape, lambda i, *_: (0,) * a.ndim)
    if prev is None:
        per_batch = lambda r, c: pl.BlockSpec((DEPTH, BATCH, r, c), lambda i, *_: (0, 0, 0, 0))
    else:
        per_batch = lambda r, c: pl.BlockSpec((1, BATCH, r, c), lambda i, *_: (layer, 0, 0, 0))
    if n_src == 1:
        x_specs = [pl.BlockSpec((ROW_TILE, D_MODEL), lambda i, *_: (tile_of(i), 0))]
    else:
        x_specs = [
            pl.BlockSpec((BATCH, BLOCK, D_MODEL), lambda i, *_: (0, jnp.clip(step(i) - 1, 0, MAIN_TILES - 1), 0)),
            pl.BlockSpec((ROW_TILE, D_MODEL), lambda i, *_: (sample_tile_of(i), 0), pipeline_mode=pl.Buffered(1)),
            pl.BlockSpec((ROW_TILE, D_MODEL), lambda i, *_: (0, 0), pipeline_mode=pl.Buffered(1)),
        ]
    const_args = (consts["tri"], consts["lev_low"], consts["pair_level"])
    chained = prev is not None
    prev_args = tuple(prev) if chained else ()
    in_specs = ([pl.BlockSpec(memory_space=pl.ANY)] * len(prev_args) + x_specs + [
        pl.BlockSpec((V_ROWS, D_MODEL), lambda i, *_: (layer, 0)),
        pl.BlockSpec((1, IN_W, W_ROWS), lambda i, *_: (layer, 0, jnp.minimum(i, N_PREP - 1))),
        pl.BlockSpec((1, GATE_RANK, GLA_K_W), lambda i, *_: (layer, 0, 0)),
    ] + [full(a) for a in const_args] + [
        pl.BlockSpec((1, a.shape[1] // N_CONV, a.shape[2]), lambda i, *_: (layer, conv_of(i), 0)) for a in ffn_f32])
    out_specs = [pl.BlockSpec((ROW_TILE, D_MIX), lambda i, *_: (tile_of(i), 0)),
                 pl.BlockSpec((ROW_TILE, PF_W), lambda i, *_: (sample_tile_of(i), 0)),
                 pl.BlockSpec((ROW_TILE, PB_W), lambda i, *_: (sample_tile_of(i), 0)),
                 per_batch(GLA_K_W, GLA_DV), per_batch(BLOCK, ATT_KV_W), per_batch(BLOCK, ATT_KV_W)]
    out_specs += [pl.BlockSpec((a.shape[1] // N_CONV, a.shape[2]), lambda i, *_: (conv_of(i), 0)) for a in ffn_f32]
    out_shape = [jax.ShapeDtypeStruct((TOTAL_ROWS, D_MIX), BF16),
                 jax.ShapeDtypeStruct((SAMPLE_ROWS, PF_W), F32),
                 jax.ShapeDtypeStruct((SAMPLE_ROWS, PB_W), BF16),
                 jax.ShapeDtypeStruct((DEPTH, BATCH, GLA_K_W, GLA_DV), F32),
                 jax.ShapeDtypeStruct((DEPTH, BATCH, BLOCK, ATT_KV_W), F32),
                 jax.ShapeDtypeStruct((DEPTH, BATCH, BLOCK, ATT_KV_W), F32)]
    out_shape += [jax.ShapeDtypeStruct(a.shape[1:], BF16) for a in ffn_f32]
    n_lead_out = 0
    if n_src > 1:
        out_specs = [pl.BlockSpec((ROW_TILE, D_MODEL), lambda i, *_: (tile_of(i), 0))] + out_specs
        out_shape = [jax.ShapeDtypeStruct((TOTAL_ROWS, D_MODEL), F32)] + out_shape
        n_lead_out = 1
    grid_spec = pltpu.PrefetchScalarGridSpec(
        num_scalar_prefetch=1,
        grid=(N_PREP + N_BLOCKS + SAMPLE_TILES,),
        in_specs=in_specs,
        out_specs=out_specs,
        scratch_shapes=[pltpu.VMEM((D_MODEL, PROJ_W), BF16), pltpu.VMEM((LANES, GLA_K_W), BF16),
                        pltpu.VMEM((ROW_TILE, PROJ_W), F32),
                        pltpu.VMEM((ROW_TILE, PF_W), F32), pltpu.VMEM((ROW_TILE, PB_W), BF16),
                        pltpu.VMEM((BATCH * N_KV_HEADS, BLOCK, ATT_KV_W), BF16),
                        pltpu.VMEM((BATCH * N_KV_HEADS, BLOCK, ATT_KV_W), BF16),
                        pltpu.VMEM((BATCH, GLA_K_W, GLA_DV), F32),
                        pltpu.VMEM((BATCH * GLA_HEADS // 2, 2 * GLA_DK, 2 * GLA_DV), BF16)],
    )
    res = pl.pallas_call(
        functools.partial(_front_kernel, layer=layer, n_src=n_src, chained=chained),
        grid_spec=grid_spec,
        out_shape=out_shape,
        input_output_aliases={1 + n: n_lead_out + 3 + n for n in range(len(prev_args))},
        compiler_params=pltpu.CompilerParams(dimension_semantics=("arbitrary",),
                                             vmem_limit_bytes=VMEM_LIMIT),
        name="front",
    )(sinks, *prev_args, *x_srcs, vecs, w_in, w_g2, *const_args, *ffn_f32)
    x_all = res[0] if n_src > 1 else x_srcs[0]
    merged, pfs, pbs, ps, pk, pv = res[n_lead_out:n_lead_out + 6]
    return x_all, merged, (pfs, pbs), (ps, pk, pv), tuple(res[n_lead_out + 6:])


def _mix_sample_kernel(*refs, layer, n_alias):
    sinks_ref, refs = refs[0], refs[n_alias + 1:]
    (pf_ref, pb_ref, ck_ref, cv_ref, st_ref,
     vec_ref, tri_ref, sones_ref, lev_ref, ones_ref, plev_ref,
     m_ref, ck_out_ref, cv_out_ref, st_out_ref) = refs
    q_ref, k_ref, v_ref, gq_ref, gk_ref, ld_ref, gv_ref, og_ref = _proj_views(pf_ref, pb_ref)
    gq = gq_ref[...]
    gk = gk_ref[...]
    gv = gv_ref[...]
    gvf = gv.astype(F32)
    hi, lo = _split(ld_ref[...])
    tri = tri_ref[...]
    g_cum = _dot(tri, hi) + _dot(tri, lo)
    sones = sones_ref[...]
    g_tot = _dot(sones, hi) + _dot(sones, lo)
    lev = lev_ref[...]
    level_sums = _dot(lev, hi) + _dot(lev, lo)
    a = _gla_intra([gq], [gk], lambda b, l: jnp.exp(level_sums[l * BLOCK:(l + 1) * BLOCK]),
                   range(LOW_LEVELS), plev_ref[...])[0]
    o_intra = jnp.concatenate(
        [_dot(a[h].astype(BF16), gv[:, h * GLA_DV:(h + 1) * GLA_DV]) for h in range(GLA_HEADS)], axis=1)
    q_dec = gq * jnp.exp(g_cum)
    k_dec = gk * jnp.exp(g_tot - g_cum)
    hm8 = _head_masks(DEC_SEQ)
    ones8 = ones_ref[...]
    hi_f, lo_f = hi.astype(F32), lo.astype(F32)

    qf = q_ref[...].astype(F32)
    kf = k_ref[...]
    vf = v_ref[...]
    low8 = lax.broadcasted_iota(jnp.int32, (DEC_SEQ, LANES), 1) < HEAD_DIM
    nkeys = WINDOW + DEC_SEQ
    srow = lax.broadcasted_iota(jnp.int32, (N_HEADS * DEC_SEQ, nkeys), 0)
    scol = lax.broadcasted_iota(jnp.int32, (N_HEADS * DEC_SEQ, nkeys), 1)
    t_of_row = srow & (DEC_SEQ - 1)
    amask = ((scol < WINDOW) & (scol > t_of_row)) | ((scol >= WINDOW) & (scol - WINDOW <= t_of_row))
    rid = lax.broadcasted_iota(jnp.int32, (N_HEADS * DEC_SEQ, 1), 0) >> LOW_LEVELS
    sink_col = jnp.zeros((N_HEADS * DEC_SEQ, 1), F32)
    for i in range(N_HEADS):
        sink_col = jnp.where(rid == i, sinks_ref[layer, i], sink_col)

    seqs = range(SEQ_GROUP)
    rows = [slice(b * DEC_SEQ, (b + 1) * DEC_SEQ) for b in seqs]

    def stacked_queries(b):
        pieces = []
        for j in range(N_HEADS // 2):
            g = (2 * j) // (N_HEADS // N_KV_HEADS)
            qt = qf[rows[b], j * LANES:(j + 1) * LANES]
            swapped = pltpu.roll(qt, HEAD_DIM, axis=1)
            own_lanes = low8 if g == 0 else ~low8
            even, odd = (qt, swapped) if g == 0 else (swapped, qt)
            pieces += [jnp.where(own_lanes, even, 0.0), jnp.where(own_lanes, odd, 0.0)]
        return jnp.concatenate(pieces, axis=0).astype(BF16)

    lane = lax.broadcasted_iota(jnp.int32, (WINDOW, WINDOW), 1)
    old_lanes = lane < WINDOW - DEC_SEQ

    def shifted_buffer(buf, new_rows):
        wide = jnp.concatenate([jnp.zeros((WINDOW - DEC_SEQ, LANES), F32), new_rows], axis=0)
        return jnp.where(old_lanes, pltpu.roll(buf, WINDOW - DEC_SEQ, axis=1), wide.T)

    qp = [stacked_queries(b) for b in seqs]
    ck = [ck_ref[0, b] for b in seqs]
    cv = [cv_ref[0, b] for b in seqs]
    for b in seqs:
        ck_out_ref[0, b] = shifted_buffer(ck[b], kf[rows[b]])
        cv_out_ref[0, b] = shifted_buffer(cv[b], vf[rows[b]])
    s = [jnp.concatenate([_dot(qp[b], ck[b].astype(BF16)), _dot_nt(qp[b], kf[rows[b]].astype(BF16))], axis=1)
         for b in seqs]
    s = [jnp.where(amask, s[b], MASK_VALUE) for b in seqs]
    m = [jnp.maximum(jnp.max(s[b], axis=-1, keepdims=True), sink_col) for b in seqs]
    p = [jnp.exp(s[b] - m[b]) for b in seqs]
    denom = [jnp.sum(p[b], axis=-1, keepdims=True) + jnp.exp(sink_col - m[b]) for b in seqs]
    pb = [p[b].astype(BF16) for b in seqs]
    ob = [(_dot_nt(pb[b][:, :WINDOW], cv[b].astype(BF16))
           + _dot(pb[b][:, WINDOW:], vf[rows[b]].astype(BF16))) / denom[b] for b in seqs]

    def head_tiles(o):
        tiles = []
        for j in range(N_HEADS // 2):
            g = (2 * j) // (N_HEADS // N_KV_HEADS)
            even = o[(2 * j) * DEC_SEQ:(2 * j + 1) * DEC_SEQ]
            odd = o[(2 * j + 1) * DEC_SEQ:(2 * j + 2) * DEC_SEQ]
            if g == 0:
                tiles.append(jnp.where(low8, even, pltpu.roll(odd, HEAD_DIM, axis=1)))
            else:
                tiles.append(jnp.where(low8, pltpu.roll(even, HEAD_DIM, axis=1), odd))
        return jnp.concatenate(tiles, axis=1)

    att_rows = [head_tiles(ob[b]) for b in seqs]

    head_stack = lambda x: jnp.concatenate([jnp.where(mk, x, 0.0) for mk in hm8], axis=0).astype(BF16)
    state = [st_ref[0, b] for b in seqs]
    oi = [_dot(head_stack(q_dec[rows[b]]), state[b].astype(BF16)) for b in seqs]
    inter_rows = [jnp.concatenate([oi[b][h * DEC_SEQ:(h + 1) * DEC_SEQ] for h in range(GLA_HEADS)], axis=1)
                  for b in seqs]
    vstack = [jnp.concatenate([gvf[rows[b]][:, h * GLA_DV:(h + 1) * GLA_DV] for h in range(GLA_HEADS)],
                              axis=0).astype(BF16) for b in seqs]
    kv = [_dot_tn(head_stack(k_dec[rows[b]]), vstack[b]) for b in seqs]
    decay_col = [jnp.exp(_dot_tn(hi_f[rows[b]], ones8) + _dot_tn(lo_f[rows[b]], ones8)) for b in seqs]
    for b in seqs:
        st_out_ref[0, b] = decay_col[b] * state[b] + kv[b]

    m_ref[:, :ATT_Q_W] = jnp.concatenate(att_rows, axis=0).astype(BF16)
    o = o_intra + jnp.concatenate(inter_rows, axis=0)
    gn = vec_ref[V_GN:V_GN + 1, :GLA_DV]
    for h, gh in enumerate(_gla_merge(o, og_ref[...], gn)):
        m_ref[:, ATT_Q_W + h * GLA_DV:ATT_Q_W + (h + 1) * GLA_DV] = gh.astype(BF16)


def _mix_sample(layer, sinks, merged, proj, cache_k, cache_v, state, vecs, consts, prev):
    tok = lambda w_: pl.BlockSpec((BLOCK, w_), lambda i, *_: (i, 0))
    merged_blk = pl.BlockSpec((BLOCK, D_MIX), lambda i, *_: (MAIN_ROWS // BLOCK + i, 0))
    full = lambda a: pl.BlockSpec(a.shape, lambda i, *_: (0,) * a.ndim)
    seq = lambda a: pl.BlockSpec((1, SEQ_GROUP) + a.shape[2:], lambda i, *_: (layer, i, 0, 0))
    ones8 = jnp.ones((DEC_SEQ, LANES), F32)
    const_args = (consts["seq_tri"], consts["seq_ones"], consts["lev_low"], ones8, consts["pair_level"])
    seq_args = (cache_k, cache_v, state)
    alias_args = (merged,) + (tuple(prev) if prev is not None else ())
    grid_spec = pltpu.PrefetchScalarGridSpec(
        num_scalar_prefetch=1,
        grid=(DEC_BATCH // SEQ_GROUP,),
        in_specs=([pl.BlockSpec(memory_space=pl.ANY)] * len(alias_args)
                  + [tok(a.shape[1]) for a in proj] + [seq(a) for a in seq_args]
                  + [pl.BlockSpec((V_ROWS, D_MODEL), lambda i, *_: (layer, 0))]
                  + [full(a) for a in const_args]),
        out_specs=[merged_blk] + [seq(a) for a in seq_args],
    )
    return pl.pallas_call(
        functools.partial(_mix_sample_kernel, layer=layer, n_alias=len(alias_args)),
        grid_spec=grid_spec,
        out_shape=[jax.ShapeDtypeStruct(merged.shape, merged.dtype)]
        + [jax.ShapeDtypeStruct(a.shape, a.dtype) for a in seq_args],
        input_output_aliases={1 + n: n for n in range(len(alias_args))},
        compiler_params=pltpu.CompilerParams(dimension_semantics=("arbitrary",),
                                             vmem_limit_bytes=VMEM_LIMIT),
        name="mix_sample",
    )(sinks, *alias_args, *proj, *seq_args, vecs, *const_args)


def _out_ffn_kernel(*refs, last):
    x_ref, m_ref, vec_ref, wo_ref, wg_ref, wu_ref, wd_ref = refs[:7]
    out_refs = refs[7:9] if last else refs[7:8]
    act_ref = refs[-1]
    x1 = x_ref[...] + _dot(m_ref[...], wo_ref[...])
    ms = jnp.mean(x1 * x1, axis=-1, keepdims=True)
    h = (x1 * lax.rsqrt(ms + RMS_EPS) * vec_ref[V_NORM2:V_NORM2 + 1, :]).astype(BF16)
    for c in range(D_FFN // FFN_CHUNK):
        cols = slice(c * FFN_CHUNK, (c + 1) * FFN_CHUNK)
        gate = _dot(h, wg_ref[:, cols])
        up = _dot(h, wu_ref[:, cols])
        act_ref[:, cols] = (gate * _sigmoid(gate) * up).astype(BF16)
    y = x1 + _dot(act_ref[...], wd_ref[...])
    if last:
        for j in range(LAST_TILES_PER_STEP):
            out_refs[0][:, j * BLOCK:(j + 1) * BLOCK, :] = (
                y[j * ROW_TILE:(j + 1) * ROW_TILE].reshape(BATCH, BLOCK, D_MODEL))

        @pl.when(pl.program_id(0) == 0)
        def _():
            for j in range(LAST_TILES_PER_STEP):
                out_refs[1][j * ROW_TILE:(j + 1) * ROW_TILE, :] = (
                    out_refs[0][:, j * BLOCK:(j + 1) * BLOCK, :].reshape(ROW_TILE, D_MODEL))
    else:
        out_refs[0][...] = y


def _out_ffn(layer, x_all, merged, vecs, ffn_weights, last):
    resident = lambda a: pl.BlockSpec(a.shape, lambda i: (0, 0), pipeline_mode=pl.Buffered(1))
    if last:
        rows = LAST_TILES_PER_STEP * ROW_TILE
        main_steps = MAIN_ROWS // rows
        n_tiles = main_steps + 1
        tile_of = lambda i: jnp.where(i == 0, main_steps, i - 1)
        out_specs = [pl.BlockSpec((BATCH, LAST_TILES_PER_STEP * BLOCK, D_MODEL),
                                  lambda i: (0, jnp.maximum(i - 1, 0), 0)),
                     pl.BlockSpec((SAMPLE_ROWS, D_MODEL), lambda i: (0, 0))]
        out_shape = [jax.ShapeDtypeStruct((BATCH, SEQ, D_MODEL), F32),
                     jax.ShapeDtypeStruct((SAMPLE_ROWS, D_MODEL), F32)]
    else:
        rows = ROW_TILE
        n_tiles = N_TILES
        tile_of = lambda i: i
        out_specs = [pl.BlockSpec((ROW_TILE, D_MODEL), lambda i: (i, 0))]
        out_shape = [jax.ShapeDtypeStruct((TOTAL_ROWS, D_MODEL), F32)]
    tile = lambda w_: pl.BlockSpec((rows, w_), lambda i: (tile_of(i), 0))
    return pl.pallas_call(
        functools.partial(_out_ffn_kernel, last=last),
        grid=(n_tiles,),
        in_specs=[tile(D_MODEL), tile(D_MIX), pl.BlockSpec((V_ROWS, D_MODEL), lambda i: (layer, 0))]
        + [resident(w) for w in ffn_weights],
        out_specs=out_specs,
        out_shape=out_shape,
        scratch_shapes=[pltpu.VMEM((rows, D_FFN), BF16)],
        compiler_params=pltpu.CompilerParams(dimension_semantics=("arbitrary",),
                                             vmem_limit_bytes=VMEM_LIMIT),
        name="out_ffn",
    )(x_all, merged, vecs, *ffn_weights)


def _vector_slab(norm1, norm2, q_norm, k_norm, b_g, gla_norm):
    pad = lambda a: jnp.pad(a.astype(F32), ((0, 0), (0, D_MODEL - a.shape[1])))
    rows = [norm1.astype(F32), norm2.astype(F32), pad(jnp.tile(q_norm, (1, N_HEADS)) * ATT_SCALE),
            pad(jnp.tile(k_norm, (1, N_KV_HEADS))), pad(b_g), pad(gla_norm)]
    rows += [jnp.zeros((DEPTH, D_MODEL), F32)] * (V_ROWS - len(rows))
    return jnp.stack(rows, axis=1).reshape(DEPTH * V_ROWS, D_MODEL)


def kernel(x_prompt, x_sample, cache_k, cache_v, state_gla, meta, norm1, w_in, q_norm, k_norm, sinks,
           w_g2, b_g, gla_norm, w_o, norm2, w_gate, w_up, w_down):
    consts = _constants()
    dt = x_prompt.dtype
    vecs = _vector_slab(norm1, norm2, q_norm, k_norm, b_g, gla_norm)
    sinks = sinks.astype(F32)
    lead = jnp.tile(jnp.concatenate([jnp.zeros((PAD_LEN, D_MODEL), dt), meta.astype(dt)], axis=0), (BATCH, 1))
    x_srcs = (x_prompt, x_sample.reshape(SAMPLE_ROWS, D_MODEL), lead)
    to_feature_major = lambda c: c.transpose(0, 1, 3, 4, 2).reshape(DEPTH, DEC_BATCH, ATT_KV_W, WINDOW)
    ck_in, cv_in = to_feature_major(cache_k), to_feature_major(cache_v)
    st_in = state_gla.reshape(DEPTH, DEC_BATCH, GLA_K_W, GLA_DV)
    w_in_t = jnp.swapaxes(w_in, 1, 2)

    prompt_outs, sample_outs = None, None
    for l in range(DEPTH):
        x_all, merged, proj_sample, prompt_outs, ffn_bf16 = _front(
            l, sinks, x_srcs, vecs, w_in_t, w_g2, (w_o, w_gate, w_up, w_down), consts, prompt_outs)
        merged, *sample_outs = _mix_sample(l, sinks, merged, proj_sample, ck_in, cv_in, st_in, vecs, consts,
                                           sample_outs)
        x_srcs = tuple(_out_ffn(l, x_all, merged, vecs, ffn_bf16, last=l == DEPTH - 1))

    y_main, y_sample = x_srcs
    ps, pk, pv = prompt_outs
    sk, sv, ss = sample_outs
    kv5 = lambda a, n: a.reshape(DEPTH, n, N_KV_HEADS, HEAD_DIM, WINDOW).transpose(0, 1, 4, 2, 3)
    st5 = lambda a, n: a.reshape(DEPTH, n, GLA_HEADS, GLA_DK, GLA_DV)
    return (y_main, y_sample.reshape(DEC_BATCH, DEC_SEQ, D_MODEL),
            kv5(pk, BATCH), kv5(pv, BATCH), st5(ps, BATCH), kv5(sk, DEC_BATCH), kv5(sv, DEC_BATCH),
            st5(ss, DEC_BATCH))
```
